```python
import math
import jax, jax.numpy as jnp
from jax import lax
import numpy as np

D_MODEL = 1024
BATCH = 8
SEQ = 2048
DEPTH = 2

GRID_W = 64
CTX_LEN = 256
N_MIXERS = 2
N_HYENA = (DEPTH + 1) // 2
N_RGLRU = DEPTH // 2
D_FF = 2816
N_MOD = 9
MACARON = 0.5
NORM_EPS = 1e-6
POS_BASE = 10000.0
HY_WIDTH = D_MODEL
HY_SHORT = 3
HY_EMB = 33
HY_BANDS = (HY_EMB - 1) // 2
HY_FILTER_HIDDEN = 64
HY_FAST_DECAY = 0.3
HY_SLOW_DECAY = 1.5
HY_DECAY_TARGET = 1e-2
RG_WIDTH = D_MODEL
RG_HEADS = 4
RG_BLOCK = RG_WIDTH // RG_HEADS
RG_CONV = 4
RG_C = 8.0

kernel_name = "hybrid_hyena_rglru_prefix_dit"

F32 = jnp.float32


def rmsnorm(x, g):
    xf = x.astype(F32)
    y = xf * lax.rsqrt(jnp.mean(xf * xf, axis=-1, keepdims=True) + NORM_EPS)
    return (y * g.astype(F32)).astype(x.dtype)


def modulate(x, shift, scale):
    return x * (1 + scale) + shift


def dwconv(u, w, b):
    K = w.shape[0]
    L = u.shape[1]
    lo = (K - 1) // 2
    hi = K - 1 - lo
    up = jnp.pad(u, ((0, 0), (lo, hi), (0, 0)))
    out = b
    for k in range(K):
        out = out + up[:, k:k + L] * w[k]
    return out


def grid_pos_embed(L):
    rows = L // GRID_W
    row = jnp.repeat(jnp.arange(rows, dtype=F32), GRID_W)
    col = jnp.tile(jnp.arange(GRID_W, dtype=F32), rows)
    quarter = D_MODEL // 4
    omega = POS_BASE ** (-jnp.arange(quarter, dtype=F32) / quarter)
    def emb(p):
        ang = p[:, None] * omega[None]
        return jnp.concatenate([jnp.sin(ang), jnp.cos(ang)], axis=-1)
    return jnp.concatenate([emb(row), emb(col)], axis=-1)


def ffn_sublayer(x, shift, scale, gate, g_pre, g_post, w1, w3, w2):
    h = modulate(rmsnorm(x, g_pre), shift, scale)
    y = (jax.nn.silu(h @ w1) * (h @ w3)) @ w2
    return x + MACARON * gate * rmsnorm(y, g_post)


def hyena_filters(L, fw0, fb0, fw1, fb1, fw2, fb2, freq, fwout):
    t = jnp.linspace(0.0, 1.0, L, dtype=F32)[:, None]
    w = 2.0 * math.pi * jnp.arange(L, dtype=F32)[:, None] / L
    f = jnp.linspace(1e-4, HY_BANDS - 1, HY_BANDS, dtype=F32)[None]
    phase = f * w
    z = jnp.concatenate([t, jnp.cos(phase), -jnp.sin(phase)], axis=-1)
    hdn = jnp.sin(freq[0] * (z @ fw0 + fb0))
    hdn = jnp.sin(freq[1] * (hdn @ fw1 + fb1))
    hdn = jnp.sin(freq[2] * (hdn @ fw2 + fb2))
    filt = (hdn @ fwout).astype(F32)
    max_decay = math.log(HY_DECAY_TARGET) / HY_FAST_DECAY
    min_decay = math.log(HY_DECAY_TARGET) / HY_SLOW_DECAY
    deltas = jnp.abs(jnp.linspace(min_decay, max_decay, HY_WIDTH, dtype=F32))
    decay = jnp.exp(-t * deltas[None])
    return filt[:, :HY_WIDTH] * decay, filt[:, HY_WIDTH:] * decay


def bidir_long_conv(u, h_fwd, h_bwd, bias):
    L = u.shape[1]
    filt_circ = jnp.concatenate([h_fwd, jnp.zeros_like(h_fwd[:1]), h_bwd[:0:-1]], axis=0)
    k_f = jnp.fft.rfft(filt_circ, n=2 * L, axis=0)
    uf = u.astype(F32)
    u_f = jnp.fft.rfft(uf, n=2 * L, axis=1)
    y = jnp.fft.irfft(u_f * k_f[None], n=2 * L, axis=1)[:, :L]
    return (y + uf * bias.astype(F32)).astype(u.dtype)


def hyena_mixer(h, w_in, b_in, conv_w, conv_b, fw0, fb0, fw1, fb1, fw2, fb2, freq, fwout,
                filt_bias, w_out, b_out):
    L = h.shape[1]
    u = dwconv(h @ w_in + b_in, conv_w, conv_b)
    x0, x1, v = jnp.split(u, 3, axis=-1)
    h_fwd, h_bwd = hyena_filters(L, fw0, fb0, fw1, fb1, fw2, fb2, freq, fwout)
    y = x0 * bidir_long_conv(x1 * v, h_fwd, h_bwd, filt_bias)
    return y @ w_out + b_out


def rglru_coeffs(xc, wa, ba, wi, bi, lam):
    B, L, R = xc.shape
    xh = xc.reshape(B, L, RG_HEADS, RG_BLOCK)
    r = jax.nn.sigmoid((jnp.einsum('blhi,hij->blhj', xh, wa).reshape(B, L, R) + ba).astype(F32))
    i = jax.nn.sigmoid((jnp.einsum('blhi,hij->blhj', xh, wi).reshape(B, L, R) + bi).astype(F32))
    log_a = -RG_C * r * jax.nn.softplus(-lam.astype(F32))
    a = jnp.exp(log_a)
    b = jnp.sqrt(-jnp.expm1(2.0 * log_a)) * i * xc.astype(F32)
    return a, b


def linear_scan(a, b, h0, reverse):
    if h0 is not None:
        first = -1 if reverse else 0
        b = b.at[:, first].add(a[:, first] * h0)
    def combine(left, right):
        a_l, b_l = left
        a_r, b_r = right
        return a_r * a_l, a_r * b_l + b_r
    _, hs = lax.associative_scan(combine, (a, b), axis=1, reverse=reverse)
    return hs


def rglru_mixer(h_c, h_l, ctx_out, w_in, b_in, conv_w, conv_b, wa, ba, wi, bi, lam, w_out, b_out):
    def project(h):
        u = h @ w_in + b_in
        gate_br, rec_br = jnp.split(u, 2, axis=-1)
        return gate_br, dwconv(rec_br, conv_w, conv_b)
    g_c, xc_c = project(h_c)
    g_l, xc_l = project(h_l)
    y_l = 0.0
    y_c = 0.0
    for d, rev in enumerate((False, True)):
        a_c, b_c = rglru_coeffs(xc_c, wa[d], ba[d], wi[d], bi[d], lam[d])
        hs_c = linear_scan(a_c, b_c, None, rev)
        h_end = hs_c[:, 0] if rev else hs_c[:, -1]
        a_l, b_l = rglru_coeffs(xc_l, wa[d], ba[d], wi[d], bi[d], lam[d])
        y_l = y_l + linear_scan(a_l, b_l, h_end, rev)
        if ctx_out:
            y_c = y_c + hs_c
    out_l = ((y_l * jax.nn.gelu(g_l)) @ w_out + b_out).astype(h_l.dtype)
    out_c = ((y_c * jax.nn.gelu(g_c)) @ w_out + b_out).astype(h_c.dtype) if ctx_out else None
    return out_l, out_c


def setup_inputs(seed: int = 0) -> dict:
    key = jax.random.key(seed)
    ks = iter(jax.random.split(key, 64))
    def nrm(shape, scale):
        return jax.random.normal(next(ks), shape, F32) * scale
    D, F, R = D_MODEL, D_FF, RG_WIDTH
    H3 = 3 * HY_WIDTH
    FO = HY_FILTER_HIDDEN
    a_init = jax.random.uniform(next(ks), (N_RGLRU, 2, R), F32, 0.9, 0.999) ** (1.0 / RG_C)
    return {
        "x": nrm((BATCH, SEQ, D), 1.0),
        "c": nrm((BATCH, D), 1.0),
        "ctx": nrm((BATCH, CTX_LEN, D), 1.0),
        "c_ctx": nrm((D,), 1.0),
        "ada_w": nrm((DEPTH, D, N_MOD * D), D ** -0.5),
        "ada_b": nrm((DEPTH, N_MOD * D), 0.01),
        "norm_g": 1.0 + nrm((DEPTH, 6, D), 0.05),
        "ffn_w1": nrm((DEPTH, 2, D, F), D ** -0.5),
        "ffn_w3": nrm((DEPTH, 2, D, F), D ** -0.5),
        "ffn_w2": nrm((DEPTH, 2, F, D), F ** -0.5),
        "hy_w_in": nrm((N_HYENA, D, H3), D ** -0.5),
        "hy_b_in": nrm((N_HYENA, H3), 0.01),
        "hy_conv_w": nrm((N_HYENA, HY_SHORT, H3), HY_SHORT ** -0.5),
        "hy_conv_b": nrm((N_HYENA, H3), 0.01),
        "hy_fw0": nrm((N_HYENA, HY_EMB, FO), HY_EMB ** -0.5),
        "hy_fb0": nrm((N_HYENA, FO), 0.1),
        "hy_fw1": nrm((N_HYENA, FO, FO), FO ** -0.5),
        "hy_fb1": nrm((N_HYENA, FO), 0.1),
        "hy_fw2": nrm((N_HYENA, FO, FO), FO ** -0.5),
        "hy_fb2": nrm((N_HYENA, FO), 0.1),
        "hy_freq": 1.0 + nrm((N_HYENA, 3, FO), 0.05),
        "hy_fwout": nrm((N_HYENA, FO, 2 * HY_WIDTH), FO ** -0.5),
        "hy_filt_bias": nrm((N_HYENA, HY_WIDTH), 1.0),
        "hy_w_out": nrm((N_HYENA, HY_WIDTH, D), HY_WIDTH ** -0.5),
        "hy_b_out": nrm((N_HYENA, D), 0.01),
        "rg_w_in": nrm((N_RGLRU, D, 2 * R), D ** -0.5),
        "rg_b_in": nrm((N_RGLRU, 2 * R), 0.01),
        "rg_conv_w": nrm((N_RGLRU, RG_CONV, R), RG_CONV ** -0.5),
        "rg_conv_b": nrm((N_RGLRU, R), 0.01),
        "rg_wa": nrm((N_RGLRU, 2, RG_HEADS, RG_BLOCK, RG_BLOCK), RG_BLOCK ** -0.5),
        "rg_ba": nrm((N_RGLRU, 2, R), 0.01),
        "rg_wi": nrm((N_RGLRU, 2, RG_HEADS, RG_BLOCK, RG_BLOCK), RG_BLOCK ** -0.5),
        "rg_bi": nrm((N_RGLRU, 2, R), 0.01),
        "rg_lam": jnp.log(a_init) - jnp.log1p(-a_init),
        "rg_w_out": nrm((N_RGLRU, R, D), R ** -0.5),
        "rg_b_out": nrm((N_RGLRU, D), 0.01),
    }


def reference(x, c, ctx, c_ctx, ada_w, ada_b, norm_g, ffn_w1, ffn_w3, ffn_w2,
              hy_w_in, hy_b_in, hy_conv_w, hy_conv_b, hy_fw0, hy_fb0, hy_fw1, hy_fb1,
              hy_fw2, hy_fb2, hy_freq, hy_fwout, hy_filt_bias, hy_w_out, hy_b_out,
              rg_w_in, rg_b_in, rg_conv_w, rg_conv_b, rg_wa, rg_ba, rg_wi, rg_bi, rg_lam,
              rg_w_out, rg_b_out):
    L = x.shape[1]
    x = x + grid_pos_embed(L).astype(x.dtype)[None]
    s = ctx
    for i in range(DEPTH):
        kind = i % N_MIXERS
        j = i // N_MIXERS
        last = i == DEPTH - 1
        ctx_out = not last
        ctx_in = ctx_out or kind == 1
        g = norm_g[i]
        mod_l = jnp.split((jax.nn.silu(c) @ ada_w[i] + ada_b[i])[:, None, :], N_MOD, axis=-1)
        x = ffn_sublayer(x, mod_l[0], mod_l[1], mod_l[2], g[0], g[1],
                         ffn_w1[i, 0], ffn_w3[i, 0], ffn_w2[i, 0])
        if ctx_in:
            mod_c = jnp.split(jax.nn.silu(c_ctx) @ ada_w[i] + ada_b[i], N_MOD, axis=-1)
            s = ffn_sublayer(s, mod_c[0], mod_c[1], mod_c[2], g[0], g[1],
                             ffn_w1[i, 0], ffn_w3[i, 0], ffn_w2[i, 0])
            h_c = modulate(rmsnorm(s, g[2]), mod_c[3], mod_c[4])
        h_l = modulate(rmsnorm(x, g[2]), mod_l[3], mod_l[4])
        if kind == 0:
            hp = (hy_w_in[j], hy_b_in[j], hy_conv_w[j], hy_conv_b[j], hy_fw0[j], hy_fb0[j],
                  hy_fw1[j], hy_fb1[j], hy_fw2[j], hy_fb2[j], hy_freq[j], hy_fwout[j],
                  hy_filt_bias[j], hy_w_out[j], hy_b_out[j])
            y_l = hyena_mixer(h_l, *hp)
            y_c = hyena_mixer(h_c, *hp) if ctx_out else None
        else:
            y_l, y_c = rglru_mixer(h_c, h_l, ctx_out, rg_w_in[j], rg_b_in[j], rg_conv_w[j],
                                   rg_conv_b[j], rg_wa[j], rg_ba[j], rg_wi[j], rg_bi[j],
                                   rg_lam[j], rg_w_out[j], rg_b_out[j])
        x = x + mod_l[5] * rmsnorm(y_l, g[3])
        x = ffn_sublayer(x, mod_l[6], mod_l[7], mod_l[8], g[4], g[5],
                         ffn_w1[i, 1], ffn_w3[i, 1], ffn_w2[i, 1])
        if ctx_out:
            s = s + mod_c[5] * rmsnorm(y_c, g[3])
            s = ffn_sublayer(s, mod_c[6], mod_c[7], mod_c[8], g[4], g[5],
                             ffn_w1[i, 1], ffn_w3[i, 1], ffn_w2[i, 1])
    return x
```

```python
import functools
import math

import jax
import jax.numpy as jnp
import numpy as np
from jax import lax
from jax.experimental import pallas as pl
from jax.experimental.pallas import tpu as pltpu

F32 = jnp.float32
BF16 = jnp.bfloat16

D = 1024
B = 8
L = 2048
CTX = 256
DEPTH = 2
GRID_W = 64
D_FF = 2816
N_MOD = 9
MACARON = 0.5
NORM_EPS = 1e-6
POS_BASE = 10000.0
HY_EMB = 33
HY_BANDS = 16
HY_HID = 64
HY_FAST_DECAY = 0.3
HY_SLOW_DECAY = 1.5
HY_DECAY_TARGET = 1e-2
RG_HEADS = 4
RG_BLOCK = D // RG_HEADS
RG_C = 8.0

ROWS_LAT = L * B
ROWS_CTX = CTX * B
ROWS = ROWS_LAT + ROWS_CTX
TM = 512
NT_LAT = ROWS_LAT // TM
NT_CTX = ROWS_CTX // TM
NT = NT_LAT + NT_CTX
F_CHUNKS = ((0, 1536), (1536, D_FF))
TC = 256
VMEM_LIMIT = 56 * 1024 * 1024


def _cparams(sem):
    return pltpu.CompilerParams(dimension_semantics=sem, vmem_limit_bytes=VMEM_LIMIT)


def _resident(shape):
    nd = len(shape)
    return pl.BlockSpec(shape, lambda *_: (0,) * nd, pipeline_mode=pl.Buffered(1))


def _split(a):
    hi = a.astype(BF16)
    lo = (a - hi.astype(F32)).astype(BF16)
    return hi, lo


def _dot(a, b):
    return jnp.dot(a, b, preferred_element_type=F32)


def _dot3(a, b):
    ah, al = _split(a)
    bh, bl = _split(b)
    return _dot(ah, bh) + _dot(ah, bl) + _dot(al, bh)


def _rmsnorm(x, g):
    ms = jnp.mean(x * x, axis=-1, keepdims=True)
    return x * lax.rsqrt(ms + NORM_EPS) * g


def _mod(mod_ref, k):
    return mod_ref[:, k * D:(k + 1) * D]


def _per_batch(x, fn):
    rows = x.shape[0]
    return fn(x.reshape(rows // B, B, x.shape[1])).reshape(rows, x.shape[1])


def _modulate(xn, shift8, scale8):
    return _per_batch(xn, lambda v: v * (1.0 + scale8)[None] + shift8[None])


def _gated(z, gate8):
    return _per_batch(z, lambda v: v * gate8[None])


def _sigmoid(x):
    return jax.nn.sigmoid(x)


def _ffn(x, mod_ref, k0, g_pre, g_post, w1_ref, w3_ref, w2_ref):
    h = _modulate(_rmsnorm(x, g_pre), _mod(mod_ref, k0), _mod(mod_ref, k0 + 1)).astype(BF16)
    y = None
    for c0, c1 in F_CHUNKS:
        a = _dot(h, w1_ref[:, c0:c1])
        b = _dot(h, w3_ref[:, c0:c1])
        act = (a * _sigmoid(a) * b).astype(BF16)
        part = _dot(act, w2_ref[c0:c1, :])
        y = part if y is None else y + part
    return x + MACARON * _gated(_rmsnorm(y, g_post), _mod(mod_ref, k0 + 2))


def _mods_kernel(cc_ref, w_ref, b_ref, o_ref):
    a = cc_ref[...]
    a = a * _sigmoid(a)
    o_ref[...] = _dot3(a, w_ref[...]) + b_ref[...]


def _mods(cc, ada_w, ada_b):
    tn = 1024
    n = N_MOD * D
    return pl.pallas_call(
        _mods_kernel,
        grid=(DEPTH, n // tn),
        in_specs=[
            pl.BlockSpec((2 * B, D), lambda i, j: (0, 0)),
            pl.BlockSpec((None, D, tn), lambda i, j: (i, 0, j)),
            pl.BlockSpec((None, 1, tn), lambda i, j: (i, 0, j)),
        ],
        out_specs=pl.BlockSpec((None, 2 * B, tn), lambda i, j: (i, 0, j)),
        out_shape=jax.ShapeDtypeStruct((DEPTH, 2 * B, n), F32),
        compiler_params=_cparams(("arbitrary", "arbitrary")),
        name="ada_mods",
    )(cc, ada_w, ada_b.reshape(DEPTH, 1, n))


def _mod_spec():
    return pl.BlockSpec((None, B, N_MOD * D), lambda t: (jnp.where(t < NT_LAT, 0, 1), 0, 0))


def _row_spec(width=D):
    return pl.BlockSpec((TM, width), lambda t: (t, 0))


def _ffn_weight_specs():
    return [_resident((D, D_FF)), _resident((D, D_FF)), _resident((D_FF, D))]


def _ffn_pos_kernel(x_ref, pos_ref, mod_ref, g_ref, w1, w3, w2, o_ref):
    x = x_ref[...] + pos_ref[...]
    o_ref[...] = _ffn(x, mod_ref, 0, g_ref[0:1], g_ref[1:2], w1, w3, w2)


def _ffn_kernel(x_ref, mod_ref, g_ref, w1, w3, w2, o_ref):
    o_ref[...] = _ffn(x_ref[...], mod_ref, 0, g_ref[0:1], g_ref[1:2], w1, w3, w2)


def _ffn_first(x, pos, mods, g, w1, w3, w2):
    return pl.pallas_call(
        _ffn_pos_kernel,
        grid=(NT,),
        in_specs=[_row_spec(), _row_spec(), _mod_spec(), _resident((6, D))] + _ffn_weight_specs(),
        out_specs=_row_spec(),
        out_shape=jax.ShapeDtypeStruct((ROWS, D), F32),
        compiler_params=_cparams(("arbitrary",)),
        name="ffn_pos",
    )(x, pos, mods, g, w1, w3, w2)


def _ffn_plain(x, mods, g, w1, w3, w2):
    return pl.pallas_call(
        _ffn_kernel,
        grid=(NT,),
        in_specs=[_row_spec(), _mod_spec(), _resident((6, D))] + _ffn_weight_specs(),
        out_specs=_row_spec(),
        out_shape=jax.ShapeDtypeStruct((ROWS, D), F32),
        compiler_params=_cparams(("arbitrary",)),
        name="ffn_pre",
    )(x, mods, g, w1, w3, w2)


def _mix_ffn_kernel(y_ref, x_ref, mod_ref, g_ref, wo_ref, bo_ref, w1, w3, w2, o_ref):
    z = _dot(y_ref[...].astype(BF16), wo_ref[...]) + bo_ref[...]
    x = x_ref[...] + _gated(_rmsnorm(z, g_ref[3:4]), _mod(mod_ref, 5))
    o_ref[...] = _ffn(x, mod_ref, 6, g_ref[4:5], g_ref[5:6], w1, w3, w2)


def _mix_ffn(y, x, mods, g, w_out, b_out, w1, w3, w2, n_tiles):
    return pl.pallas_call(
        _mix_ffn_kernel,
        grid=(n_tiles,),
        in_specs=[_row_spec(), _row_spec(), _mod_spec(), _resident((6, D)),
                  _resident((D, D)), _resident((1, D))] + _ffn_weight_specs(),
        out_specs=_row_spec(),
        out_shape=jax.ShapeDtypeStruct((n_tiles * TM, D), F32),
        compiler_params=_cparams(("arbitrary",)),
        name="mix_ffn",
    )(y, x, mods, g, w_out, b_out, w1, w3, w2)


def _seq_first(t):
    return jnp.logical_or(t == 0, t == NT_LAT)


def _seq_last(t):
    return jnp.logical_or(t == NT_LAT - 1, t == NT - 1)


def _halo_specs(order, lo_rows, hi_rows):
    nlo = ROWS // lo_rows
    nhi = ROWS // hi_rows
    return [
        pl.BlockSpec((lo_rows, D), lambda s: (jnp.maximum(order(s) * (TM // lo_rows) - 1, 0), 0)),
        pl.BlockSpec((TM, D), lambda s: (order(s), 0)),
        pl.BlockSpec((hi_rows, D),
                     lambda s: (jnp.minimum((order(s) + 1) * (TM // hi_rows), nhi - 1), 0)),
    ]


def _hy_in_kernel(xp_ref, x_ref, xn_ref, mod_ref, g_ref, w_ref, b_ref, cw_ref, cb_ref,
                  x0_ref, p_ref, u_scr):
    t = pl.program_id(0)
    xa = jnp.concatenate([xp_ref[...], x_ref[...], xn_ref[...]], axis=0)
    h = _modulate(_rmsnorm(xa, g_ref[2:3]), _mod(mod_ref, 3), _mod(mod_ref, 4)).astype(BF16)
    u_scr[...] = _dot(h, w_ref[...]) + b_ref[...]

    @pl.when(_seq_first(t))
    def _():
        u_scr[0:B, :] = jnp.zeros((B, 3 * D), F32)

    @pl.when(_seq_last(t))
    def _():
        u_scr[TM + B:TM + 2 * B, :] = jnp.zeros((B, 3 * D), F32)

    cv = (cb_ref[...] + cw_ref[0:1] * u_scr[0:TM, :] + cw_ref[1:2] * u_scr[B:TM + B, :]
          + cw_ref[2:3] * u_scr[2 * B:TM + 2 * B, :])
    x0_ref[...] = cv[:, 0:D]
    p_ref[...] = cv[:, D:2 * D] * cv[:, 2 * D:3 * D]


def _hy_in(x, mods, g, w_in, b_in, conv_w, conv_b):
    return pl.pallas_call(
        _hy_in_kernel,
        grid=(NT,),
        in_specs=_halo_specs(lambda s: s, B, B) + [
            _mod_spec(), _resident((6, D)), _resident((D, 3 * D)), _resident((1, 3 * D)),
            _resident((3, 3 * D)), _resident((1, 3 * D))],
        out_specs=[_row_spec(), _row_spec()],
        out_shape=[jax.ShapeDtypeStruct((ROWS, D), F32)] * 2,
        scratch_shapes=[pltpu.VMEM((TM + 2 * B, 3 * D), F32)],
        compiler_params=_cparams(("arbitrary",)),
        name="hyena_in",
    )(x, x, x, mods, g, w_in, b_in, conv_w, conv_b)


def _filter_kernel(z_ref, fw0, fb0, fw1, fb1, fw2, fb2, freq, fwout, deltas, hf_ref, hb_ref):
    z = z_ref[...]
    h = jnp.sin(freq[0:1] * (_dot3(z, fw0[...]) + fb0[...]))
    h = jnp.sin(freq[1:2] * (_dot3(h, fw1[...]) + fb1[...]))
    h = jnp.sin(freq[2:3] * (_dot3(h, fw2[...]) + fb2[...]))
    filt = _dot3(h, fwout[...])
    decay = jnp.exp(-z[:, 0:1] * deltas[...])
    hf_ref[...] = filt[:, 0:D] * decay
    hb_ref[...] = filt[:, D:2 * D] * decay


def _filters(n, fw0, fb0, fw1, fb1, fw2, fb2, freq, fwout):
    t = jnp.linspace(0.0, 1.0, n, dtype=F32)[:, None]
    w = 2.0 * math.pi * jnp.arange(n, dtype=F32)[:, None] / n
    f = jnp.linspace(1e-4, HY_BANDS - 1, HY_BANDS, dtype=F32)[None]
    phase = f * w
    z = jnp.concatenate([t, jnp.cos(phase), -jnp.sin(phase)], axis=-1)
    zp = jnp.zeros((n, 128), F32).at[:, :HY_EMB].set(z)
    fw0p = jnp.zeros((128, HY_HID), F32).at[:HY_EMB].set(fw0)
    max_decay = math.log(HY_DECAY_TARGET) / HY_FAST_DECAY
    min_decay = math.log(HY_DECAY_TARGET) / HY_SLOW_DECAY
    deltas = jnp.abs(jnp.linspace(min_decay, max_decay, D, dtype=F32))[None]
    tl = 256
    row = lambda i: (i, 0)
    return pl.pallas_call(
        _filter_kernel,
        grid=(n // tl,),
        in_specs=[pl.BlockSpec((tl, 128), row), _resident((128, HY_HID)), _resident((1, HY_HID)),
                  _resident((HY_HID, HY_HID)), _resident((1, HY_HID)),
                  _resident((HY_HID, HY_HID)), _resident((1, HY_HID)),
                  _resident((3, HY_HID)), _resident((HY_HID, 2 * D)), _resident((1, D))],
        out_specs=[pl.BlockSpec((tl, D), row)] * 2,
        out_shape=[jax.ShapeDtypeStruct((n, D), F32)] * 2,
        compiler_params=_cparams(("arbitrary",)),
        name="hyena_filter",
    )(zp, fw0p, fb0[None], fw1, fb1[None], fw2, fb2[None], freq, fwout, deltas)


def _dft_tables(n):
    k = jnp.arange(n, dtype=jnp.int32)
    r = (k[:, None] * k[None, :]) % (2 * n)
    ang = r.astype(F32) * (math.pi / n)
    return jnp.cos(ang), jnp.sin(ang)


def _alt_sign(rows, cols):
    r = lax.broadcasted_iota(jnp.int32, (rows, cols), 0)
    return (1 - 2 * (r & 1)).astype(F32)


def _spectrum_kernel(hf_ref, hb_ref, ch_ref, cl_ref, sh_ref, sl_ref, kr_ref, ki_ref, kn_ref, *, n):
    hf = hf_ref[...]
    row = lax.broadcasted_iota(jnp.int32, hf.shape, 0)
    hb = jnp.where(row == 0, 0.0, hb_ref[...])
    even = hf + hb
    odd = hb - hf
    eh, el = _split(even)
    oh, ol = _split(odd)
    kr = _dot(ch_ref[...], eh) + _dot(ch_ref[...], el) + _dot(cl_ref[...], eh)
    ki = _dot(sh_ref[...], oh) + _dot(sh_ref[...], ol) + _dot(sl_ref[...], oh)
    tk = kr.shape[0]
    k = lax.broadcasted_iota(jnp.int32, kr.shape, 0) + pl.program_id(1) * tk
    scale = jnp.where(k == 0, 0.5 / n, 1.0 / n)
    kr_ref[...] = kr * scale
    ki_ref[...] = ki * scale
    nyq = jnp.sum(even * _alt_sign(*hf.shape), axis=0, keepdims=True) * (0.5 / n)
    kn_ref[...] = jnp.broadcast_to(nyq, kn_ref.shape)


def _spectrum(hf, hb, tables):
    n = hf.shape[0]
    tk = min(n, 512)
    col = lambda j, k: (0, j)
    tab = pl.BlockSpec((tk, n), lambda j, k: (k, 0))
    out = pl.BlockSpec((tk, TC), lambda j, k: (k, j))
    return pl.pallas_call(
        functools.partial(_spectrum_kernel, n=n),
        grid=(D // TC, n // tk),
        in_specs=[pl.BlockSpec((n, TC), col)] * 2 + [tab] * 4,
        out_specs=[out, out, pl.BlockSpec((B, TC), col)],
        out_shape=[jax.ShapeDtypeStruct((n, D), F32)] * 2 + [jax.ShapeDtypeStruct((B, D), F32)],
        compiler_params=_cparams(("arbitrary", "arbitrary")),
        name="hyena_spectrum",
    )(hf, hb, *tables)


def _long_conv_kernel(p_ref, x0_ref, kr_ref, ki_ref, kn_ref, bias_ref, c_ref, s_ref, o_ref,
                      pb_scr, yr_scr, ys_scr, *, n, tk):
    pb_scr[...] = p_ref[...].astype(BF16)

    def chunk(k):
        return pl.ds(pl.multiple_of(k * tk, tk), tk)

    def forward(k, carry):
        rows = chunk(k)
        xr = _dot(c_ref[rows, :], pb_scr[...])
        xs = _dot(s_ref[rows, :], pb_scr[...])
        kr = kr_ref[rows, :]
        ki = ki_ref[rows, :]
        yr_scr[rows, :] = (xr * kr + xs * ki).astype(BF16)
        ys_scr[rows, :] = (xs * kr - xr * ki).astype(BF16)
        return carry

    lax.fori_loop(0, n // tk, forward, 0)
    nyq = jnp.sum(p_ref[...] * _alt_sign(n, TC), axis=0, keepdims=True) * kn_ref[0:1, :]
    alt_nyq = _alt_sign(tk, TC) * nyq

    def inverse(k, carry):
        rows = chunk(k)
        y = _dot(c_ref[rows, :], yr_scr[...]) + _dot(s_ref[rows, :], ys_scr[...]) + alt_nyq
        o_ref[rows, :] = x0_ref[rows, :] * (y + p_ref[rows, :] * bias_ref[...])
        return carry

    lax.fori_loop(0, n // tk, inverse, 0)


def _long_conv(p2, x02, kr, ki, kn, bias, ctab, stab, n, row_block):
    nc = D // TC
    tk = min(n, 512)
    col = lambda j: (row_block, j)
    ch = lambda j: (0, j % nc)
    return pl.pallas_call(
        functools.partial(_long_conv_kernel, n=n, tk=tk),
        grid=(B * D // TC,),
        scratch_shapes=[pltpu.VMEM((n, TC), BF16)] * 3,
        in_specs=[pl.BlockSpec((n, TC), col), pl.BlockSpec((n, TC), col),
                  pl.BlockSpec((n, TC), ch), pl.BlockSpec((n, TC), ch),
                  pl.BlockSpec((B, TC), ch), pl.BlockSpec((1, TC), ch),
                  _resident((n, n)), _resident((n, n))],
        out_specs=pl.BlockSpec((n, TC), lambda j: (0, j)),
        out_shape=jax.ShapeDtypeStruct((n, B * D), F32),
        compiler_params=_cparams(("arbitrary",)),
        name="hyena_long_conv",
    )(p2, x02, kr, ki, kn, bias, ctab, stab)


def _gelu_tanh(x):
    return x * (0.5 * (1.0 + jnp.tanh(math.sqrt(2.0 / math.pi) * (x + 0.044715 * (x * x * x)))))


def _rg_coeffs(xc, wa_ref, ba_ref, wi_ref, bi_ref, lam_ref, a_scr, b_scr):
    xb = xc.astype(BF16)
    lam = lam_ref[...]
    softplus_neg = jnp.maximum(-lam, 0.0) + jnp.log1p(jnp.exp(-jnp.abs(lam)))
    for hd in range(RG_HEADS):
        sl = slice(hd * RG_BLOCK, (hd + 1) * RG_BLOCK)
        r = _sigmoid(_dot(xb[:, sl], wa_ref[hd]) + ba_ref[:, sl])
        i = _sigmoid(_dot(xb[:, sl], wi_ref[hd]) + bi_ref[:, sl])
        a = jnp.exp(-RG_C * r * softplus_neg[:, sl])
        a_scr[:, sl] = a
        b_scr[:, sl] = jnp.sqrt((1.0 - a) * (1.0 + a)) * i * xc[:, sl]


def _scan_tile(a_scr, b_scr, h_scr, emit, reverse):
    steps = TM // B

    def body(k, h):
        t = steps - 1 - k if reverse else k
        r0 = pl.multiple_of(t * B, B)
        h = a_scr[pl.ds(r0, B), :] * h + b_scr[pl.ds(r0, B), :]
        emit(r0, h)
        return h

    h_scr[...] = lax.fori_loop(0, steps, body, h_scr[...], unroll=8)


def _rg_fwd_order(s):
    return jnp.where(s < NT_CTX, NT_LAT + s, s - NT_CTX)


def _rg_bwd_order(s):
    return NT - 1 - s


def _rg_in_kernel(xp_ref, x_ref, xn_ref, mod_ref, g_ref, wg_ref, wr_ref, bg_ref, br_ref,
                  cw_ref, cb_ref, wa_ref, ba_ref, wi_ref, bi_ref, lam_ref,
                  xc_ref, gg_ref, hs_ref, u_scr, a_scr, b_scr, h_scr):
    s = pl.program_id(0)
    t = _rg_fwd_order(s)
    xa = jnp.concatenate([xp_ref[...], x_ref[...], xn_ref[...]], axis=0)
    h = _modulate(_rmsnorm(xa, g_ref[2:3]), _mod(mod_ref, 3), _mod(mod_ref, 4)).astype(BF16)
    gg_ref[...] = _gelu_tanh(_dot(h[B:TM + B], wg_ref[...]) + bg_ref[...])
    u_scr[...] = _dot(h, wr_ref[...]) + br_ref[...]

    @pl.when(_seq_first(t))
    def _():
        u_scr[0:B, :] = jnp.zeros((B, D), F32)

    @pl.when(_seq_last(t))
    def _():
        u_scr[TM + B:TM + 3 * B, :] = jnp.zeros((2 * B, D), F32)

    xc = cb_ref[...]
    for k in range(4):
        xc = xc + cw_ref[k:k + 1] * u_scr[k * B:k * B + TM, :]
    xc_ref[...] = xc
    _rg_coeffs(xc, wa_ref, ba_ref, wi_ref, bi_ref, lam_ref, a_scr, b_scr)

    @pl.when(s == 0)
    def _():
        h_scr[...] = jnp.zeros((B, D), F32)

    def emit(r0, hv):
        hs_ref[pl.ds(r0, B), :] = hv

    _scan_tile(a_scr, b_scr, h_scr, emit, reverse=False)


def _rg_in(x, mods, g, w_gate, w_rec, b_gate, b_rec, conv_w, conv_b, wa, ba, wi, bi, lam):
    order = _rg_fwd_order
    mod_spec = pl.BlockSpec((None, B, N_MOD * D),
                            lambda s: (jnp.where(order(s) < NT_LAT, 0, 1), 0, 0))
    out_spec = pl.BlockSpec((TM, D), lambda s: (order(s), 0))
    gate_w = _resident((RG_HEADS, RG_BLOCK, RG_BLOCK))
    return pl.pallas_call(
        _rg_in_kernel,
        grid=(NT,),
        in_specs=_halo_specs(order, B, 2 * B) + [
            mod_spec, _resident((6, D)), _resident((D, D)), _resident((D, D)),
            _resident((1, D)), _resident((1, D)), _resident((4, D)), _resident((1, D)),
            gate_w, _resident((1, D)), gate_w, _resident((1, D)), _resident((1, D))],
        out_specs=[out_spec] * 3,
        out_shape=[jax.ShapeDtypeStruct((ROWS, D), F32)] * 3,
        scratch_shapes=[pltpu.VMEM((TM + 3 * B, D), F32), pltpu.VMEM((TM, D), F32),
                        pltpu.VMEM((TM, D), F32), pltpu.VMEM((B, D), F32)],
        compiler_params=_cparams(("arbitrary",)),
        name="rglru_in_fwd_scan",
    )(x, x, x, mods, g, w_gate, w_rec, b_gate, b_rec, conv_w, conv_b, wa, ba, wi, bi, lam)


def _rg_out_kernel(xc_ref, gg_ref, hs_ref, wa_ref, ba_ref, wi_ref, bi_ref, lam_ref,
                   y_ref, a_scr, b_scr, h_scr):
    s = pl.program_id(0)
    _rg_coeffs(xc_ref[...], wa_ref, ba_ref, wi_ref, bi_ref, lam_ref, a_scr, b_scr)

    @pl.when(s == 0)
    def _():
        h_scr[...] = jnp.zeros((B, D), F32)

    def emit(r0, hv):
        rows = pl.ds(r0, B)
        y_ref[rows, :] = (hs_ref[rows, :] + hv) * gg_ref[rows, :]

    _scan_tile(a_scr, b_scr, h_scr, emit, reverse=True)


def _rg_out(xc, gg, hs, wa, ba, wi, bi, lam):
    order = _rg_bwd_order
    in_spec = pl.BlockSpec((TM, D), lambda s: (order(s), 0))
    gate_w = _resident((RG_HEADS, RG_BLOCK, RG_BLOCK))
    return pl.pallas_call(
        _rg_out_kernel,
        grid=(NT,),
        in_specs=[in_spec] * 3 + [gate_w, _resident((1, D)), gate_w, _resident((1, D)),
                                  _resident((1, D))],
        out_specs=in_spec,
        out_shape=jax.ShapeDtypeStruct((ROWS, D), F32),
        scratch_shapes=[pltpu.VMEM((TM, D), F32), pltpu.VMEM((TM, D), F32),
                        pltpu.VMEM((B, D), F32)],
        compiler_params=_cparams(("arbitrary",)),
        name="rglru_bwd_scan",
    )(xc, gg, hs, wa, ba, wi, bi, lam)


def _pos_rows():
    rows = L // GRID_W
    row = jnp.repeat(jnp.arange(rows, dtype=F32), GRID_W)
    col = jnp.tile(jnp.arange(GRID_W, dtype=F32), rows)
    quarter = D // 4
    omega = POS_BASE ** (-jnp.arange(quarter, dtype=F32) / quarter)

    def emb(q):
        ang = q[:, None] * omega[None]
        return jnp.concatenate([jnp.sin(ang), jnp.cos(ang)], axis=-1)

    pos = jnp.concatenate([emb(row), emb(col)], axis=-1)
    pos = jnp.broadcast_to(pos[:, None, :], (L, B, D)).reshape(ROWS_LAT, D)
    return jnp.concatenate([pos, jnp.zeros((ROWS_CTX, D), F32)], axis=0)


def kernel(x, c, ctx, c_ctx, ada_w, ada_b, norm_g, ffn_w1, ffn_w3, ffn_w2, hy_w_in, hy_b_in, hy_conv_w, hy_conv_b, hy_fw0, hy_fb0, hy_fw1, hy_fb1, hy_fw2, hy_fb2, hy_freq, hy_fwout, hy_filt_bias, hy_w_out, hy_b_out, rg_w_in, rg_b_in, rg_conv_w, rg_conv_b, rg_wa, rg_ba, rg_wi, rg_bi, rg_lam, rg_w_out, rg_b_out):
    xs = jnp.concatenate([jnp.transpose(x, (1, 0, 2)).reshape(ROWS_LAT, D),
                          jnp.transpose(ctx, (1, 0, 2)).reshape(ROWS_CTX, D)], axis=0)
    cc = jnp.concatenate([c, jnp.broadcast_to(c_ctx[None], (B, D))], axis=0)
    mods = _mods(cc, ada_w, ada_b).reshape(DEPTH, 2, B, N_MOD * D)
    w1 = ffn_w1.astype(BF16)
    w3 = ffn_w3.astype(BF16)
    w2 = ffn_w2.astype(BF16)

    g = norm_g[0]
    xs = _ffn_first(xs, _pos_rows(), mods[0], g, w1[0, 0], w3[0, 0], w2[0, 0])
    x0, p = _hy_in(xs, mods[0], g, hy_w_in[0].astype(BF16), hy_b_in[0][None],
                   hy_conv_w[0], hy_conv_b[0][None])
    fparams = (hy_fw0[0], hy_fb0[0], hy_fw1[0], hy_fb1[0], hy_fw2[0], hy_fb2[0],
               hy_freq[0], hy_fwout[0])
    x02 = x0.reshape(ROWS // B, B * D)
    p2 = p.reshape(ROWS // B, B * D)
    bias = hy_filt_bias[0][None]
    ys = []
    for n, row_block in ((L, 0), (CTX, L // CTX)):
        hf, hb = _filters(n, *fparams)
        ctab, stab = _dft_tables(n)
        chi, clo = _split(ctab)
        shi, slo = _split(stab)
        kr, ki, kn = _spectrum(hf, hb, (chi, clo, shi, slo))
        ys.append(_long_conv(p2, x02, kr, ki, kn, bias, chi, shi, n, row_block))
    y = jnp.concatenate(ys, axis=0).reshape(ROWS, D)
    xs = _mix_ffn(y, xs, mods[0], g, hy_w_out[0].astype(BF16), hy_b_out[0][None],
                  w1[0, 1], w3[0, 1], w2[0, 1], NT)

    g = norm_g[1]
    xs = _ffn_plain(xs, mods[1], g, w1[1, 0], w3[1, 0], w2[1, 0])
    w_in = rg_w_in[0].astype(BF16)
    b_in = rg_b_in[0][None]
    wa = rg_wa[0].astype(BF16)
    wi = rg_wi[0].astype(BF16)
    xc, gg, hs = _rg_in(xs, mods[1], g, w_in[:, :D], w_in[:, D:], b_in[:, :D], b_in[:, D:],
                        rg_conv_w[0], rg_conv_b[0][None], wa[0], rg_ba[0, 0][None],
                        wi[0], rg_bi[0, 0][None], rg_lam[0, 0][None])
    y = _rg_out(xc, gg, hs, wa[1], rg_ba[0, 1][None], wi[1], rg_bi[0, 1][None], rg_lam[0, 1][None])
    out = _mix_ffn(y, xs, mods[1], g, rg_w_out[0].astype(BF16), rg_b_out[0][None],
                   w1[1, 1], w3[1, 1], w2[1, 1], NT_LAT)
    return jnp.transpose(out.reshape(L, B, D), (1, 0, 2))
```

```python
import functools
import math

import jax
import jax.numpy as jnp
import numpy as np
from jax import lax
from jax.experimental import pallas as pl
from jax.experimental.pallas import tpu as pltpu

F32 = jnp.float32
BF16 = jnp.bfloat16

D = 1024
B = 8
LANES = 128
L = 2048
CTX = 256
DEPTH = 2
GRID_W = 64
D_FF = 2816
N_MOD = 9
MACARON = 0.5
NORM_EPS = 1e-6
POS_BASE = 10000.0
HY_EMB = 33
HY_BANDS = 16
HY_HID = 64
HY_FAST_DECAY = 0.3
HY_SLOW_DECAY = 1.5
HY_DECAY_TARGET = 1e-2
RG_HEADS = 4
RG_BLOCK = D // RG_HEADS
RG_C = 8.0

ROWS_LAT = L * B
ROWS_CTX = CTX * B
ROWS = ROWS_LAT + ROWS_CTX
TM = 512
TS = TM // B
NT_LAT = ROWS_LAT // TM
NT_CTX = ROWS_CTX // TM
NT = NT_LAT + NT_CTX
F_CHUNKS = ((0, 1536), (1536, D_FF))
TC = 256
VMEM_LIMIT = 56 * 1024 * 1024


def _cparams(sem):
    return pltpu.CompilerParams(dimension_semantics=sem, vmem_limit_bytes=VMEM_LIMIT)


def _resident(shape):
    nd = len(shape)
    return pl.BlockSpec(shape, lambda *_: (0,) * nd, pipeline_mode=pl.Buffered(1))


def _split(a):
    hi = a.astype(BF16)
    lo = (a - hi.astype(F32)).astype(BF16)
    return hi, lo


def _dot(a, b):
    return jnp.dot(a, b, preferred_element_type=F32)


def _dot3(a, b):
    ah, al = _split(a)
    bh, bl = _split(b)
    return _dot(ah, bh) + _dot(ah, bl) + _dot(al, bh)


def _rmsnorm(x, g):
    ms = jnp.mean(x * x, axis=-1, keepdims=True)
    return x * lax.rsqrt(ms + NORM_EPS) * g


def _mod(mod_ref, k):
    return mod_ref[:, k * D:(k + 1) * D]


def _per_batch(x, fn):
    rows = x.shape[0]
    return fn(x.reshape(rows // B, B, x.shape[1])).reshape(rows, x.shape[1])


def _modulate(xn, shift8, scale8):
    return _per_batch(xn, lambda v: v * (1.0 + scale8)[None] + shift8[None])


def _gated(z, gate8):
    return _per_batch(z, lambda v: v * gate8[None])


def _sigmoid(x):
    return jax.nn.sigmoid(x)


def _ffn(x, mod_ref, k0, g_pre, g_post, w1_ref, w3_ref, w2_ref):
    h = _modulate(_rmsnorm(x, g_pre), _mod(mod_ref, k0), _mod(mod_ref, k0 + 1)).astype(BF16)
    y = None
    for c0, c1 in F_CHUNKS:
        a = _dot(h, w1_ref[:, c0:c1])
        b = _dot(h, w3_ref[:, c0:c1])
        act = (a * _sigmoid(a) * b).astype(BF16)
        part = _dot(act, w2_ref[c0:c1, :])
        y = part if y is None else y + part
    return x + MACARON * _gated(_rmsnorm(y, g_post), _mod(mod_ref, k0 + 2))


def _mods_kernel(cc_ref, w_ref, b_ref, o_ref):
    a = cc_ref[...]
    a = a * _sigmoid(a)
    o_ref[...] = _dot3(a, w_ref[...]) + b_ref[...]


def _mods(cc, ada_w, ada_b):
    tn = 1024
    n = N_MOD * D
    return pl.pallas_call(
        _mods_kernel,
        grid=(DEPTH, n // tn),
        in_specs=[
            pl.BlockSpec((2 * B, D), lambda i, j: (0, 0)),
            pl.BlockSpec((None, D, tn), lambda i, j: (i, 0, j)),
            pl.BlockSpec((None, 1, tn), lambda i, j: (i, 0, j)),
        ],
        out_specs=pl.BlockSpec((None, 2 * B, tn), lambda i, j: (i, 0, j)),
        out_shape=jax.ShapeDtypeStruct((DEPTH, 2 * B, n), F32),
        compiler_params=_cparams(("arbitrary", "arbitrary")),
        name="ada_mods",
    )(cc, ada_w, ada_b.reshape(DEPTH, 1, n))


def _mod_spec():
    return pl.BlockSpec((None, B, N_MOD * D), lambda t: (jnp.where(t < NT_LAT, 0, 1), 0, 0))


def _row_spec(width=D):
    return pl.BlockSpec((TM, width), lambda t: (t, 0))


def _wide_spec():
    return pl.BlockSpec((TS, B * D), lambda t: (t, 0))


def _ffn_weight_specs(layer, half):
    pick = lambda *_: (layer, half, 0, 0)
    up = pl.BlockSpec((None, None, D, D_FF), pick, pipeline_mode=pl.Buffered(1))
    down = pl.BlockSpec((None, None, D_FF, D), pick, pipeline_mode=pl.Buffered(1))
    return [up, up, down]


def _rows_of_batch(b):
    return pl.ds(b, TS, stride=B)


_ROWS_SCRATCH = pltpu.VMEM((D // LANES, TM, LANES), F32)


def _put_batch(rows_scr, b, val):
    for j in range(D // LANES):
        rows_scr[j, _rows_of_batch(b), :] = val[:, j * LANES:(j + 1) * LANES]


def _get_batch(rows_scr, b):
    return jnp.concatenate([rows_scr[j, _rows_of_batch(b), :] for j in range(D // LANES)], axis=1)


def _put_rows(rows_scr, val):
    for j in range(D // LANES):
        rows_scr[j] = val[:, j * LANES:(j + 1) * LANES]


def _get_rows(rows_scr):
    return jnp.concatenate([rows_scr[j] for j in range(D // LANES)], axis=1)


def _ffn_first_kernel(x_ref, ctx_ref, pos_ref, mod_ref, g_ref, w1, w3, w2, o_ref, rows_scr):
    t = pl.program_id(0)

    @pl.when(t < NT_LAT)
    def _():
        for b in range(B):
            _put_batch(rows_scr, b, x_ref[b] + pos_ref[...])

    @pl.when(t >= NT_LAT)
    def _():
        for b in range(B):
            _put_batch(rows_scr, b, ctx_ref[b])

    o_ref[...] = _ffn(_get_rows(rows_scr), mod_ref, 0, g_ref[0:1], g_ref[1:2], w1, w3, w2)


def _ffn_kernel(x_ref, mod_ref, g_ref, w1, w3, w2, o_ref):
    o_ref[...] = _ffn(x_ref[...], mod_ref, 0, g_ref[0:1], g_ref[1:2], w1, w3, w2)


def _ffn_first(x, ctx, pos, mods, g, w1, w3, w2):
    lat = lambda t: jnp.minimum(t, NT_LAT - 1)
    return pl.pallas_call(
        _ffn_first_kernel,
        grid=(NT,),
        in_specs=[pl.BlockSpec((B, TS, D), lambda t: (0, lat(t), 0)),
                  pl.BlockSpec((B, TS, D), lambda t: (0, jnp.maximum(t - NT_LAT, 0), 0)),
                  pl.BlockSpec((TS, D), lambda t: (lat(t), 0)),
                  _mod_spec(), _resident((6, D))] + _ffn_weight_specs(0, 0),
        out_specs=_row_spec(),
        out_shape=jax.ShapeDtypeStruct((ROWS, D), F32),
        scratch_shapes=[_ROWS_SCRATCH],
        compiler_params=_cparams(("arbitrary",)),
        name="ffn_first",
    )(x, ctx, pos, mods, g, w1, w3, w2)


def _ffn_plain(x, mods, g, w1, w3, w2, layer):
    return pl.pallas_call(
        _ffn_kernel,
        grid=(NT,),
        in_specs=[_row_spec(), _mod_spec(), _resident((6, D))] + _ffn_weight_specs(layer, 0),
        out_specs=_row_spec(),
        out_shape=jax.ShapeDtypeStruct((ROWS, D), F32),
        compiler_params=_cparams(("arbitrary",)),
        name="ffn_pre",
    )(x, mods, g, w1, w3, w2)


def _mix_ffn_kernel(y_ref, x_ref, mod_ref, g_ref, wo_ref, bo_ref, w1, w3, w2, o_ref, rows_scr,
                    *, wide_in, batch_major_out):
    if wide_in:
        for b in range(B):
            _put_batch(rows_scr, b, y_ref[:, b * D:(b + 1) * D])
        y = _get_rows(rows_scr)
    else:
        y = y_ref[...]
    z = _dot(y.astype(BF16), wo_ref[...]) + bo_ref[...]
    x = x_ref[...] + _gated(_rmsnorm(z, g_ref[3:4]), _mod(mod_ref, 5))
    res = _ffn(x, mod_ref, 6, g_ref[4:5], g_ref[5:6], w1, w3, w2)
    if batch_major_out:
        _put_rows(rows_scr, res)
        for b in range(B):
            o_ref[b] = _get_batch(rows_scr, b)
    else:
        o_ref[...] = res


def _mix_ffn(y, x, mods, g, w_out, b_out, w1, w3, w2, layer, wide_in, batch_major_out):
    n_tiles = NT_LAT if batch_major_out else NT
    if batch_major_out:
        out_spec = pl.BlockSpec((B, TS, D), lambda t: (0, t, 0))
        out_shape = jax.ShapeDtypeStruct((B, L, D), F32)
    else:
        out_spec = _row_spec()
        out_shape = jax.ShapeDtypeStruct((ROWS, D), F32)
    return pl.pallas_call(
        functools.partial(_mix_ffn_kernel, wide_in=wide_in, batch_major_out=batch_major_out),
        grid=(n_tiles,),
        in_specs=[_wide_spec() if wide_in else _row_spec(), _row_spec(), _mod_spec(),
                  _resident((6, D)), _resident((D, D)), _resident((1, D))]
        + _ffn_weight_specs(layer, 1),
        out_specs=out_spec,
        out_shape=out_shape,
        scratch_shapes=[_ROWS_SCRATCH],
        compiler_params=_cparams(("arbitrary",)),
        name="mix_ffn",
    )(y, x, mods, g, w_out, b_out, w1, w3, w2)


def _seq_first(t):
    return jnp.logical_or(t == 0, t == NT_LAT)


def _seq_last(t):
    return jnp.logical_or(t == NT_LAT - 1, t == NT - 1)


def _halo_specs(order, lo_rows, hi_rows):
    nlo = ROWS // lo_rows
    nhi = ROWS // hi_rows
    return [
        pl.BlockSpec((lo_rows, D), lambda s: (jnp.maximum(order(s) * (TM // lo_rows) - 1, 0), 0)),
        pl.BlockSpec((TM, D), lambda s: (order(s), 0)),
        pl.BlockSpec((hi_rows, D),
                     lambda s: (jnp.minimum((order(s) + 1) * (TM // hi_rows), nhi - 1), 0)),
    ]


def _hy_in_kernel(xp_ref, x_ref, xn_ref, mod_ref, g_ref, w_ref, b_ref, cw_ref, cb_ref,
                  x0_ref, p_ref, u_scr, rows_scr):
    t = pl.program_id(0)
    xa = jnp.concatenate([xp_ref[...], x_ref[...], xn_ref[...]], axis=0)
    h = _modulate(_rmsnorm(xa, g_ref[2:3]), _mod(mod_ref, 3), _mod(mod_ref, 4)).astype(BF16)
    u_scr[...] = _dot(h, w_ref[...]) + b_ref[...]

    @pl.when(_seq_first(t))
    def _():
        u_scr[0:B, :] = jnp.zeros((B, 3 * D), F32)

    @pl.when(_seq_last(t))
    def _():
        u_scr[TM + B:TM + 2 * B, :] = jnp.zeros((B, 3 * D), F32)

    cv = (cb_ref[...] + cw_ref[0:1] * u_scr[0:TM, :] + cw_ref[1:2] * u_scr[B:TM + B, :]
          + cw_ref[2:3] * u_scr[2 * B:TM + 2 * B, :])
    for out_ref, val in ((x0_ref, cv[:, 0:D]), (p_ref, cv[:, D:2 * D] * cv[:, 2 * D:3 * D])):
        _put_rows(rows_scr, val)
        for b in range(B):
            out_ref[:, b * D:(b + 1) * D] = _get_batch(rows_scr, b)


def _hy_in(x, mods, g, w_in, b_in, conv_w, conv_b):
    return pl.pallas_call(
        _hy_in_kernel,
        grid=(NT,),
        in_specs=_halo_specs(lambda s: s, B, B) + [
            _mod_spec(), _resident((6, D)), _resident((D, 3 * D)), _resident((1, 3 * D)),
            _resident((3, 3 * D)), _resident((1, 3 * D))],
        out_specs=[_wide_spec(), _wide_spec()],
        out_shape=[jax.ShapeDtypeStruct((ROWS // B, B * D), F32)] * 2,
        scratch_shapes=[pltpu.VMEM((TM + 2 * B, 3 * D), F32), _ROWS_SCRATCH],
        compiler_params=_cparams(("arbitrary",)),
        name="hyena_in",
    )(x, x, x, mods, g, w_in, b_in, conv_w, conv_b)


def _filter_kernel(z_ref, fw0, fb0, fw1, fb1, fw2, fb2, freq, fwout, deltas, hf_ref, hb_ref):
    z = z_ref[...]
    h = jnp.sin(freq[0:1] * (_dot3(z, fw0[...]) + fb0[...]))
    h = jnp.sin(freq[1:2] * (_dot3(h, fw1[...]) + fb1[...]))
    h = jnp.sin(freq[2:3] * (_dot3(h, fw2[...]) + fb2[...]))
    filt = _dot3(h, fwout[...])
    decay = jnp.exp(-z[:, 0:1] * deltas[...])
    hf_ref[...] = filt[:, 0:D] * decay
    hb_ref[...] = filt[:, D:2 * D] * decay


def _filters(n, fw0, fb0, fw1, fb1, fw2, fb2, freq, fwout):
    t = jnp.linspace(0.0, 1.0, n, dtype=F32)[:, None]
    w = 2.0 * math.pi * jnp.arange(n, dtype=F32)[:, None] / n
    f = jnp.linspace(1e-4, HY_BANDS - 1, HY_BANDS, dtype=F32)[None]
    phase = f * w
    z = jnp.concatenate([t, jnp.cos(phase), -jnp.sin(phase)], axis=-1)
    zp = jnp.zeros((n, 128), F32).at[:, :HY_EMB].set(z)
    fw0p = jnp.zeros((128, HY_HID), F32).at[:HY_EMB].set(fw0)
    max_decay = math.log(HY_DECAY_TARGET) / HY_FAST_DECAY
    min_decay = math.log(HY_DECAY_TARGET) / HY_SLOW_DECAY
    deltas = jnp.abs(jnp.linspace(min_decay, max_decay, D, dtype=F32))[None]
    tl = 256
    row = lambda i: (i, 0)
    return pl.pallas_call(
        _filter_kernel,
        grid=(n // tl,),
        in_specs=[pl.BlockSpec((tl, 128), row), _resident((128, HY_HID)), _resident((1, HY_HID)),
                  _resident((HY_HID, HY_HID)), _resident((1, HY_HID)),
                  _resident((HY_HID, HY_HID)), _resident((1, HY_HID)),
                  _resident((3, HY_HID)), _resident((HY_HID, 2 * D)), _resident((1, D))],
        out_specs=[pl.BlockSpec((tl, D), row)] * 2,
        out_shape=[jax.ShapeDtypeStruct((n, D), F32)] * 2,
        compiler_params=_cparams(("arbitrary",)),
        name="hyena_filter",
    )(zp, fw0p, fb0[None], fw1, fb1[None], fw2, fb2[None], freq, fwout, deltas)


def _dft_tables(n):
    q = 1 << (int(math.log2(n)) // 2 + 1)
    m = jnp.arange(n, dtype=jnp.int32)[None, :]

    def thin(k):
        ang = ((k[:, None] * m) % (2 * n)).astype(F32) * (math.pi / n)
        return jnp.cos(ang)[:, None, :], jnp.sin(ang)[:, None, :]

    ca, sa = thin(q * jnp.arange(n // q, dtype=jnp.int32))
    cb, sb = thin(jnp.arange(q, dtype=jnp.int32))
    cb, sb = cb[:, 0][None], sb[:, 0][None]
    return (ca * cb - sa * sb).reshape(n, n), (sa * cb + ca * sb).reshape(n, n)


def _alt_sign(rows, cols):
    r = lax.broadcasted_iota(jnp.int32, (rows, cols), 0)
    return (1 - 2 * (r & 1)).astype(F32)


def _spectrum_kernel(hf_ref, hb_ref, ch_ref, cl_ref, sh_ref, sl_ref, kr_ref, ki_ref, kn_ref, *, n):
    hf = hf_ref[...]
    row = lax.broadcasted_iota(jnp.int32, hf.shape, 0)
    hb = jnp.where(row == 0, 0.0, hb_ref[...])
    even = hf + hb
    odd = hb - hf
    eh, el = _split(even)
    oh, ol = _split(odd)
    kr = _dot(ch_ref[...], eh) + _dot(ch_ref[...], el) + _dot(cl_ref[...], eh)
    ki = _dot(sh_ref[...], oh) + _dot(sh_ref[...], ol) + _dot(sl_ref[...], oh)
    tk = kr.shape[0]
    k = lax.broadcasted_iota(jnp.int32, kr.shape, 0) + pl.program_id(1) * tk
    scale = jnp.where(k == 0, 0.5 / n, 1.0 / n)
    kr_ref[...] = kr * scale
    ki_ref[...] = ki * scale
    nyq = jnp.sum(even * _alt_sign(*hf.shape), axis=0, keepdims=True) * (0.5 / n)
    kn_ref[...] = jnp.broadcast_to(nyq, kn_ref.shape)


def _spectrum(hf, hb, tables):
    n = hf.shape[0]
    tk = min(n, 512)
    col = lambda j, k: (0, j)
    tab = pl.BlockSpec((tk, n), lambda j, k: (k, 0))
    out = pl.BlockSpec((tk, TC), lambda j, k: (k, j))
    return pl.pallas_call(
        functools.partial(_spectrum_kernel, n=n),
        grid=(D // TC, n // tk),
        in_specs=[pl.BlockSpec((n, TC), col)] * 2 + [tab] * 4,
        out_specs=[out, out, pl.BlockSpec((B, TC), col)],
        out_shape=[jax.ShapeDtypeStruct((n, D), F32)] * 2 + [jax.ShapeDtypeStruct((B, D), F32)],
        compiler_params=_cparams(("arbitrary", "arbitrary")),
        name="hyena_spectrum",
    )(hf, hb, *tables)


def _long_conv_kernel(p_ref, x0_ref, kr_ref, ki_ref, kn_ref, bias_ref, c_ref, s_ref, o_ref,
                      pb_scr, yr_scr, ys_scr, *, n, tk):
    pb_scr[...] = p_ref[...].astype(BF16)

    def chunk(k):
        return pl.ds(pl.multiple_of(k * tk, tk), tk)

    def forward(k, carry):
        rows = chunk(k)
        xr = _dot(c_ref[rows, :], pb_scr[...])
        xs = _dot(s_ref[rows, :], pb_scr[...])
        kr = kr_ref[rows, :]
        ki = ki_ref[rows, :]
        yr_scr[rows, :] = (xr * kr + xs * ki).astype(BF16)
        ys_scr[rows, :] = (xs * kr - xr * ki).astype(BF16)
        return carry

    lax.fori_loop(0, n // tk, forward, 0)
    nyq = jnp.sum(p_ref[...] * _alt_sign(n, TC), axis=0, keepdims=True) * kn_ref[0:1, :]
    alt_nyq = _alt_sign(tk, TC) * nyq

    def inverse(k, carry):
        rows = chunk(k)
        y = _dot(c_ref[rows, :], yr_scr[...]) + _dot(s_ref[rows, :], ys_scr[...]) + alt_nyq
        o_ref[rows, :] = x0_ref[rows, :] * (y + p_ref[rows, :] * bias_ref[...])
        return carry

    lax.fori_loop(0, n // tk, inverse, 0)


def _long_conv_into_kernel(y_prev_ref, *refs, n, tk):
    del y_prev_ref
    _long_conv_kernel(*refs, n=n, tk=tk)


def _long_conv(p2, x02, kr, ki, kn, bias, ctab, stab, n, row_block, y_prev=None):
    nc = D // TC
    tk = min(n, 512)
    col = lambda j: (row_block, j)
    ch = lambda j: (0, j % nc)
    in_specs = [pl.BlockSpec((n, TC), col), pl.BlockSpec((n, TC), col),
                pl.BlockSpec((n, TC), ch), pl.BlockSpec((n, TC), ch),
                pl.BlockSpec((B, TC), ch), pl.BlockSpec((1, TC), ch),
                _resident((n, n)), _resident((n, n))]
    args = (p2, x02, kr, ki, kn, bias, ctab, stab)
    if y_prev is None:
        body, aliases = _long_conv_kernel, {}
    else:
        body, aliases = _long_conv_into_kernel, {0: 0}
        in_specs = [pl.BlockSpec(memory_space=pl.ANY)] + in_specs
        args = (y_prev,) + args
    return pl.pallas_call(
        functools.partial(body, n=n, tk=tk),
        grid=(B * D // TC,),
        scratch_shapes=[pltpu.VMEM((n, TC), BF16)] * 3,
        in_specs=in_specs,
        out_specs=pl.BlockSpec((n, TC), col),
        out_shape=jax.ShapeDtypeStruct((ROWS // B, B * D), F32),
        input_output_aliases=aliases,
        compiler_params=_cparams(("arbitrary",)),
        name="hyena_long_conv",
    )(*args)


def _gelu_tanh(x):
    return x * (0.5 * (1.0 + jnp.tanh(math.sqrt(2.0 / math.pi) * (x + 0.044715 * (x * x * x)))))


def _rg_coeffs(xc, wa_ref, ba_ref, wi_ref, bi_ref, lam_ref, a_scr, b_scr):
    xb = xc.astype(BF16)
    lam = lam_ref[...]
    softplus_neg = jnp.maximum(-lam, 0.0) + jnp.log1p(jnp.exp(-jnp.abs(lam)))
    for hd in range(RG_HEADS):
        sl = slice(hd * RG_BLOCK, (hd + 1) * RG_BLOCK)
        r = _sigmoid(_dot(xb[:, sl], wa_ref[hd]) + ba_ref[:, sl])
        i = _sigmoid(_dot(xb[:, sl], wi_ref[hd]) + bi_ref[:, sl])
        a = jnp.exp(-RG_C * r * softplus_neg[:, sl])
        a_scr[:, sl] = a
        b_scr[:, sl] = jnp.sqrt((1.0 - a) * (1.0 + a)) * i * xc[:, sl]


def _scan_tile(a_scr, b_scr, h_scr, emit, reverse):
    steps = TM // B

    def body(k, h):
        t = steps - 1 - k if reverse else k
        r0 = pl.multiple_of(t * B, B)
        h = a_scr[pl.ds(r0, B), :] * h + b_scr[pl.ds(r0, B), :]
        emit(r0, h)
        return h

    h_scr[...] = lax.fori_loop(0, steps, body, h_scr[...], unroll=8)


def _rg_fwd_order(s):
    return jnp.where(s < NT_CTX, NT_LAT + s, s - NT_CTX)


def _rg_bwd_order(s):
    return NT - 1 - s


def _rg_in_kernel(xp_ref, x_ref, xn_ref, mod_ref, g_ref, wg_ref, wr_ref, bg_ref, br_ref,
                  cw_ref, cb_ref, wa_ref, ba_ref, wi_ref, bi_ref, lam_ref,
                  xc_ref, gg_ref, hs_ref, u_scr, a_scr, b_scr, h_scr):
    s = pl.program_id(0)
    t = _rg_fwd_order(s)
    xa = jnp.concatenate([xp_ref[...], x_ref[...], xn_ref[...]], axis=0)
    h = _modulate(_rmsnorm(xa, g_ref[2:3]), _mod(mod_ref, 3), _mod(mod_ref, 4)).astype(BF16)
    gg_ref[...] = _gelu_tanh(_dot(h[B:TM + B], wg_ref[...]) + bg_ref[...])
    u_scr[...] = _dot(h, wr_ref[...]) + br_ref[...]

    @pl.when(_seq_first(t))
    def _():
        u_scr[0:B, :] = jnp.zeros((B, D), F32)

    @pl.when(_seq_last(t))
    def _():
        u_scr[TM + B:TM + 3 * B, :] = jnp.zeros((2 * B, D), F32)

    xc = cb_ref[...]
    for k in range(4):
        xc = xc + cw_ref[k:k + 1] * u_scr[k * B:k * B + TM, :]
    xc_ref[...] = xc
    _rg_coeffs(xc, wa_ref, ba_ref, wi_ref, bi_ref, lam_ref, a_scr, b_scr)

    @pl.when(s == 0)
    def _():
        h_scr[...] = jnp.zeros((B, D), F32)

    def emit(r0, hv):
        hs_ref[pl.ds(r0, B), :] = hv

    _scan_tile(a_scr, b_scr, h_scr, emit, reverse=False)


def _rg_in(x, mods, g, w_in, b_in, conv_w, conv_b, wa, ba, wi, bi, lam):
    order = _rg_fwd_order
    mod_spec = pl.BlockSpec((None, B, N_MOD * D),
                            lambda s: (jnp.where(order(s) < NT_LAT, 0, 1), 0, 0))
    out_spec = pl.BlockSpec((TM, D), lambda s: (order(s), 0))
    gate_w = _resident((RG_HEADS, RG_BLOCK, RG_BLOCK))
    half = lambda rows, j: pl.BlockSpec((rows, D), lambda s: (0, j), pipeline_mode=pl.Buffered(1))
    return pl.pallas_call(
        _rg_in_kernel,
        grid=(NT,),
        in_specs=_halo_specs(order, B, 2 * B) + [
            mod_spec, _resident((6, D)), half(D, 0), half(D, 1),
            half(1, 0), half(1, 1), _resident((4, D)), _resident((1, D)),
            gate_w, _resident((1, D)), gate_w, _resident((1, D)), _resident((1, D))],
        out_specs=[out_spec] * 3,
        out_shape=[jax.ShapeDtypeStruct((ROWS, D), F32)] * 3,
        scratch_shapes=[pltpu.VMEM((TM + 3 * B, D), F32), pltpu.VMEM((TM, D), F32),
                        pltpu.VMEM((TM, D), F32), pltpu.VMEM((B, D), F32)],
        compiler_params=_cparams(("arbitrary",)),
        name="rglru_in_fwd_scan",
    )(x, x, x, mods, g, w_in, w_in, b_in, b_in, conv_w, conv_b, wa, ba, wi, bi, lam)


def _rg_out_kernel(xc_ref, gg_ref, hs_ref, wa_ref, ba_ref, wi_ref, bi_ref, lam_ref,
                   y_ref, a_scr, b_scr, h_scr):
    s = pl.program_id(0)
    _rg_coeffs(xc_ref[...], wa_ref, ba_ref, wi_ref, bi_ref, lam_ref, a_scr, b_scr)

    @pl.when(s == 0)
    def _():
        h_scr[...] = jnp.zeros((B, D), F32)

    def emit(r0, hv):
        rows = pl.ds(r0, B)
        y_ref[rows, :] = (hs_ref[rows, :] + hv) * gg_ref[rows, :]

    _scan_tile(a_scr, b_scr, h_scr, emit, reverse=True)


def _rg_out(xc, gg, hs, wa, ba, wi, bi, lam):
    order = _rg_bwd_order
    in_spec = pl.BlockSpec((TM, D), lambda s: (order(s), 0))
    gate_w = _resident((RG_HEADS, RG_BLOCK, RG_BLOCK))
    return pl.pallas_call(
        _rg_out_kernel,
        grid=(NT,),
        in_specs=[in_spec] * 3 + [gate_w, _resident((1, D)), gate_w, _resident((1, D)),
                                  _resident((1, D))],
        out_specs=in_spec,
        out_shape=jax.ShapeDtypeStruct((ROWS, D), F32),
        scratch_shapes=[pltpu.VMEM((TM, D), F32), pltpu.VMEM((TM, D), F32),
                        pltpu.VMEM((B, D), F32)],
        compiler_params=_cparams(("arbitrary",)),
        name="rglru_bwd_scan",
    )(xc, gg, hs, wa, ba, wi, bi, lam)


def _grid_pos():
    rows = L // GRID_W
    row = jnp.repeat(jnp.arange(rows, dtype=F32), GRID_W)
    col = jnp.tile(jnp.arange(GRID_W, dtype=F32), rows)
    quarter = D // 4
    omega = POS_BASE ** (-jnp.arange(quarter, dtype=F32) / quarter)

    def emb(q):
        ang = q[:, None] * omega[None]
        return jnp.concatenate([jnp.sin(ang), jnp.cos(ang)], axis=-1)

    return jnp.concatenate([emb(row), emb(col)], axis=-1)


def kernel(x, c, ctx, c_ctx, ada_w, ada_b, norm_g, ffn_w1, ffn_w3, ffn_w2, hy_w_in, hy_b_in, hy_conv_w, hy_conv_b, hy_fw0, hy_fb0, hy_fw1, hy_fb1, hy_fw2, hy_fb2, hy_freq, hy_fwout, hy_filt_bias, hy_w_out, hy_b_out, rg_w_in, rg_b_in, rg_conv_w, rg_conv_b, rg_wa, rg_ba, rg_wi, rg_bi, rg_lam, rg_w_out, rg_b_out):
    cc = jnp.concatenate([c, jnp.broadcast_to(c_ctx[None], (B, D))], axis=0)
    mods = _mods(cc, ada_w, ada_b).reshape(DEPTH, 2, B, N_MOD * D)
    w1 = ffn_w1.astype(BF16)
    w3 = ffn_w3.astype(BF16)
    w2 = ffn_w2.astype(BF16)

    g = norm_g[0]
    xs = _ffn_first(x, ctx, _grid_pos(), mods[0], g, w1, w3, w2)
    x0, p = _hy_in(xs, mods[0], g, hy_w_in[0].astype(BF16), hy_b_in[0][None],
                   hy_conv_w[0], hy_conv_b[0][None])
    fparams = (hy_fw0[0], hy_fb0[0], hy_fw1[0], hy_fb1[0], hy_fw2[0], hy_fb2[0],
               hy_freq[0], hy_fwout[0])
    bias = hy_filt_bias[0][None]
    y = None
    for n, row_block in ((CTX, L // CTX), (L, 0)):
        hf, hb = _filters(n, *fparams)
        ctab, stab = _dft_tables(n)
        chi, clo = _split(ctab)
        shi, slo = _split(stab)
        kr, ki, kn = _spectrum(hf, hb, (chi, clo, shi, slo))
        y = _long_conv(p, x0, kr, ki, kn, bias, chi, shi, n, row_block, y_prev=y)
    xs = _mix_ffn(y, xs, mods[0], g, hy_w_out[0].astype(BF16), hy_b_out[0][None], w1, w3, w2,
                  layer=0, wide_in=True, batch_major_out=False)

    g = norm_g[1]
    xs = _ffn_plain(xs, mods[1], g, w1, w3, w2, layer=1)
    wa = rg_wa[0].astype(BF16)
    wi = rg_wi[0].astype(BF16)
    xc, gg, hs = _rg_in(xs, mods[1], g, rg_w_in[0].astype(BF16), rg_b_in[0][None],
                        rg_conv_w[0], rg_conv_b[0][None], wa[0], rg_ba[0, 0][None],
                        wi[0], rg_bi[0, 0][None], rg_lam[0, 0][None])
    y = _rg_out(xc, gg, hs, wa[1], rg_ba[0, 1][None], wi[1], rg_bi[0, 1][None], rg_lam[0, 1][None])
    return _mix_ffn(y, xs, mods[1], g, rg_w_out[0].astype(BF16), rg_b_out[0][None], w1, w3, w2,
                    layer=1, wide_in=False, batch_major_out=True)
```

```python
import functools
import math

import jax
import jax.numpy as jnp
import numpy as np
from jax import lax
from jax.experimental import pallas as pl
from jax.experimental.pallas import tpu as pltpu

F32 = jnp.float32
BF16 = jnp.bfloat16

D = 1024
B = 8
LANES = 128
MXU_N = 256
L = 2048
CTX = 256
DEPTH = 2
GRID_W = 64
D_FF = 2816
N_MOD = 9
MACARON = 0.5
NORM_EPS = 1e-6
POS_BASE = 10000.0
HY_EMB = 33
HY_BANDS = 16
HY_HID = 64
HY_FAST_DECAY = 0.3
HY_SLOW_DECAY = 1.5
HY_DECAY_TARGET = 1e-2
RG_HEADS = 4
RG_BLOCK = D // RG_HEADS
RG_C = 8.0

ROWS_LAT = L * B
ROWS_CTX = CTX * B
ROWS = ROWS_LAT + ROWS_CTX
TM = 512
TS = TM // B
NT_LAT = ROWS_LAT // TM
NT_CTX = ROWS_CTX // TM
NT = NT_LAT + NT_CTX
F_CHUNKS = ((0, 1536), (1536, D_FF))
TC = 256
VMEM_LIMIT = 56 * 1024 * 1024


def _cparams(sem):
    return pltpu.CompilerParams(dimension_semantics=sem, vmem_limit_bytes=VMEM_LIMIT)


def _resident(shape):
    nd = len(shape)
    return pl.BlockSpec(shape, lambda *_: (0,) * nd, pipeline_mode=pl.Buffered(1))


def _split(a):
    hi = a.astype(BF16)
    lo = (a - hi.astype(F32)).astype(BF16)
    return hi, lo


def _dot(a, b):
    return jnp.dot(a, b, preferred_element_type=F32)


def _dot3(a, b):
    ah, al = _split(a)
    bh, bl = _split(b)
    return _dot(ah, bh) + _dot(ah, bl) + _dot(al, bh)


def _rmsnorm(x, g):
    ms = jnp.mean(x * x, axis=-1, keepdims=True)
    return x * lax.rsqrt(ms + NORM_EPS) * g


def _mod(mod_ref, k):
    return mod_ref[:, k * D:(k + 1) * D]


def _per_batch(x, fn):
    rows = x.shape[0]
    return fn(x.reshape(rows // B, B, x.shape[1])).reshape(rows, x.shape[1])


def _modulate(xn, shift8, scale8):
    return _per_batch(xn, lambda v: v * (1.0 + scale8)[None] + shift8[None])


def _gated(z, gate8):
    return _per_batch(z, lambda v: v * gate8[None])


def _sigmoid(x):
    return jax.nn.sigmoid(x)


def _ffn(x, mod_ref, k0, g_pre, g_post, w1_ref, w3_ref, w2_ref):
    h = _modulate(_rmsnorm(x, g_pre), _mod(mod_ref, k0), _mod(mod_ref, k0 + 1)).astype(BF16)
    y = None
    for c0, c1 in F_CHUNKS:
        a = _dot(h, w1_ref[:, c0:c1])
        b = _dot(h, w3_ref[:, c0:c1])
        act = (a * _sigmoid(a) * b).astype(BF16)
        part = _dot(act, w2_ref[c0:c1, :])
        y = part if y is None else y + part
    return x + MACARON * _gated(_rmsnorm(y, g_post), _mod(mod_ref, k0 + 2))


def _mods_kernel(cc_ref, w_ref, b_ref, o_ref):
    a = cc_ref[...]
    a = a * _sigmoid(a)
    o_ref[...] = _dot3(a, w_ref[...]) + b_ref[...]


def _mods(cc, ada_w, ada_b):
    tn = 1024
    n = N_MOD * D
    return pl.pallas_call(
        _mods_kernel,
        grid=(DEPTH, n // tn),
        in_specs=[
            pl.BlockSpec((2 * B, D), lambda i, j: (0, 0)),
            pl.BlockSpec((None, D, tn), lambda i, j: (i, 0, j)),
            pl.BlockSpec((None, 1, tn), lambda i, j: (i, 0, j)),
        ],
        out_specs=pl.BlockSpec((None, 2 * B, tn), lambda i, j: (i, 0, j)),
        out_shape=jax.ShapeDtypeStruct((DEPTH, 2 * B, n), F32),
        compiler_params=_cparams(("arbitrary", "arbitrary")),
        name="ada_mods",
    )(cc, ada_w, ada_b.reshape(DEPTH, 1, n))


def _mod_spec():
    return pl.BlockSpec((None, B, N_MOD * D), lambda t: (jnp.where(t < NT_LAT, 0, 1), 0, 0))


def _row_spec(width=D):
    return pl.BlockSpec((TM, width), lambda t: (t, 0))


def _wide_spec():
    return pl.BlockSpec((TS, B * D), lambda t: (t, 0))


def _ffn_weight_specs(layer, half):
    pick = lambda *_: (layer, half, 0, 0)
    up = pl.BlockSpec((None, None, D, D_FF), pick, pipeline_mode=pl.Buffered(1))
    down = pl.BlockSpec((None, None, D_FF, D), pick, pipeline_mode=pl.Buffered(1))
    return [up, up, down]


def _rows_of_batch(b):
    return pl.ds(b, TS, stride=B)


_ROWS_SCRATCH = pltpu.VMEM((D // LANES, TM, LANES), F32)


def _put_batch(rows_scr, b, val):
    for j in range(D // LANES):
        rows_scr[j, _rows_of_batch(b), :] = val[:, j * LANES:(j + 1) * LANES]


def _get_batch(rows_scr, b):
    return jnp.concatenate([rows_scr[j, _rows_of_batch(b), :] for j in range(D // LANES)], axis=1)


def _put_rows(rows_scr, val):
    for j in range(D // LANES):
        rows_scr[j] = val[:, j * LANES:(j + 1) * LANES]


def _get_rows(rows_scr):
    return jnp.concatenate([rows_scr[j] for j in range(D // LANES)], axis=1)


def _ffn_first_kernel(x_ref, ctx_ref, pos_ref, mod_ref, g_ref, w1, w3, w2, o_ref, rows_scr):
    t = pl.program_id(0)

    @pl.when(t < NT_LAT)
    def _():
        for b in range(B):
            _put_batch(rows_scr, b, x_ref[b] + pos_ref[...])

    @pl.when(t >= NT_LAT)
    def _():
        for b in range(B):
            _put_batch(rows_scr, b, ctx_ref[b])

    o_ref[...] = _ffn(_get_rows(rows_scr), mod_ref, 0, g_ref[0:1], g_ref[1:2], w1, w3, w2)


def _ffn_kernel(x_ref, mod_ref, g_ref, w1, w3, w2, o_ref):
    o_ref[...] = _ffn(x_ref[...], mod_ref, 0, g_ref[0:1], g_ref[1:2], w1, w3, w2)


def _ffn_first(x, ctx, pos, mods, g, w1, w3, w2):
    lat = lambda t: jnp.minimum(t, NT_LAT - 1)
    return pl.pallas_call(
        _ffn_first_kernel,
        grid=(NT,),
        in_specs=[pl.BlockSpec((B, TS, D), lambda t: (0, lat(t), 0)),
                  pl.BlockSpec((B, TS, D), lambda t: (0, jnp.maximum(t - NT_LAT, 0), 0)),
                  pl.BlockSpec((TS, D), lambda t: (lat(t), 0)),
                  _mod_spec(), _resident((6, D))] + _ffn_weight_specs(0, 0),
        out_specs=_row_spec(),
        out_shape=jax.ShapeDtypeStruct((ROWS, D), F32),
        scratch_shapes=[_ROWS_SCRATCH],
        compiler_params=_cparams(("arbitrary",)),
        name="ffn_first",
    )(x, ctx, pos, mods, g, w1, w3, w2)


def _ffn_plain(x, mods, g, w1, w3, w2, layer):
    return pl.pallas_call(
        _ffn_kernel,
        grid=(NT,),
        in_specs=[_row_spec(), _mod_spec(), _resident((6, D))] + _ffn_weight_specs(layer, 0),
        out_specs=_row_spec(),
        out_shape=jax.ShapeDtypeStruct((ROWS, D), F32),
        compiler_params=_cparams(("arbitrary",)),
        name="ffn_pre",
    )(x, mods, g, w1, w3, w2)


def _mix_ffn_kernel(y_ref, x_ref, mod_ref, g_ref, wo_ref, bo_ref, w1, w3, w2, o_ref, rows_scr,
                    *, wide_in, batch_major_out):
    if wide_in:
        for b in range(B):
            _put_batch(rows_scr, b, y_ref[:, b * D:(b + 1) * D])
        y = _get_rows(rows_scr)
    else:
        y = y_ref[...]
    z = _dot(y.astype(BF16), wo_ref[...]) + bo_ref[...]
    x = x_ref[...] + _gated(_rmsnorm(z, g_ref[3:4]), _mod(mod_ref, 5))
    res = _ffn(x, mod_ref, 6, g_ref[4:5], g_ref[5:6], w1, w3, w2)
    if batch_major_out:
        _put_rows(rows_scr, res)
        for b in range(B):
            o_ref[b] = _get_batch(rows_scr, b)
    else:
        o_ref[...] = res


def _mix_ffn(y, x, mods, g, w_out, b_out, w1, w3, w2, layer, wide_in, batch_major_out):
    n_tiles = NT_LAT if batch_major_out else NT
    if batch_major_out:
        out_spec = pl.BlockSpec((B, TS, D), lambda t: (0, t, 0))
        out_shape = jax.ShapeDtypeStruct((B, L, D), F32)
    else:
        out_spec = _row_spec()
        out_shape = jax.ShapeDtypeStruct((ROWS, D), F32)
    return pl.pallas_call(
        functools.partial(_mix_ffn_kernel, wide_in=wide_in, batch_major_out=batch_major_out),
        grid=(n_tiles,),
        in_specs=[_wide_spec() if wide_in else _row_spec(), _row_spec(), _mod_spec(),
                  _resident((6, D)), _resident((D, D)), _resident((1, D))]
        + _ffn_weight_specs(layer, 1),
        out_specs=out_spec,
        out_shape=out_shape,
        scratch_shapes=[_ROWS_SCRATCH],
        compiler_params=_cparams(("arbitrary",)),
        name="mix_ffn",
    )(y, x, mods, g, w_out, b_out, w1, w3, w2)


def _seq_first(t):
    return jnp.logical_or(t == 0, t == NT_LAT)


def _seq_last(t):
    return jnp.logical_or(t == NT_LAT - 1, t == NT - 1)


def _halo_specs(order, lo_rows, hi_rows):
    nlo = ROWS // lo_rows
    nhi = ROWS // hi_rows
    return [
        pl.BlockSpec((lo_rows, D), lambda s: (jnp.maximum(order(s) * (TM // lo_rows) - 1, 0), 0)),
        pl.BlockSpec((TM, D), lambda s: (order(s), 0)),
        pl.BlockSpec((hi_rows, D),
                     lambda s: (jnp.minimum((order(s) + 1) * (TM // hi_rows), nhi - 1), 0)),
    ]


def _hy_in_kernel(xp_ref, x_ref, xn_ref, mod_ref, g_ref, w_ref, b_ref, cw_ref, cb_ref,
                  x0_ref, p_ref, x0_scr, p_scr):
    t = pl.program_id(0)
    xa = jnp.concatenate([xp_ref[...], x_ref[...], xn_ref[...]], axis=0)
    h = _modulate(_rmsnorm(xa, g_ref[2:3]), _mod(mod_ref, 3), _mod(mod_ref, 4)).astype(BF16)
    keep_lo = jnp.where(_seq_first(t), 0.0, 1.0).astype(F32)
    keep_hi = jnp.where(_seq_last(t), 0.0, 1.0).astype(F32)

    def conv_cols(c0):
        cols = slice(c0, c0 + MXU_N)
        u = _dot(h, w_ref[:, cols]) + b_ref[:, cols]
        lo = jnp.concatenate([u[0:B] * keep_lo, u[B:TM]], axis=0)
        hi = jnp.concatenate([u[2 * B:TM + B], u[TM + B:TM + 2 * B] * keep_hi], axis=0)
        return (cb_ref[:, cols] + cw_ref[0:1, cols] * lo + cw_ref[1:2, cols] * u[B:TM + B]
                + cw_ref[2:3, cols] * hi)

    slabs = MXU_N // LANES
    for c in range(D // MXU_N):
        c0 = c * MXU_N
        vals = (conv_cols(c0), conv_cols(D + c0) * conv_cols(2 * D + c0))
        for out_ref, scr, val in zip((x0_ref, p_ref), (x0_scr, p_scr), vals):
            for j in range(slabs):
                scr[c * slabs + j] = val[:, j * LANES:(j + 1) * LANES]
            for b in range(B):
                out_ref[:, b * D + c0:b * D + c0 + MXU_N] = jnp.concatenate(
                    [scr[c * slabs + j, _rows_of_batch(b), :] for j in range(slabs)], axis=1)


def _hy_in(x, mods, g, w_in, b_in, conv_w, conv_b):
    return pl.pallas_call(
        _hy_in_kernel,
        grid=(NT,),
        in_specs=_halo_specs(lambda s: s, B, B) + [
            _mod_spec(), _resident((6, D)), _resident((D, 3 * D)), _resident((1, 3 * D)),
            _resident((3, 3 * D)), _resident((1, 3 * D))],
        out_specs=[_wide_spec(), _wide_spec()],
        out_shape=[jax.ShapeDtypeStruct((ROWS // B, B * D), F32)] * 2,
        scratch_shapes=[_ROWS_SCRATCH, _ROWS_SCRATCH],
        compiler_params=_cparams(("arbitrary",)),
        name="hyena_in",
    )(x, x, x, mods, g, w_in, b_in, conv_w, conv_b)


def _filter_kernel(z_ref, fw0, fb0, fw1, fb1, fw2, fb2, freq, fwout, deltas, hf_ref, hb_ref):
    z = z_ref[...]
    h = jnp.sin(freq[0:1] * (_dot3(z, fw0[...]) + fb0[...]))
    h = jnp.sin(freq[1:2] * (_dot3(h, fw1[...]) + fb1[...]))
    h = jnp.sin(freq[2:3] * (_dot3(h, fw2[...]) + fb2[...]))
    filt = _dot3(h, fwout[...])
    decay = jnp.exp(-z[:, 0:1] * deltas[...])
    hf_ref[...] = filt[:, 0:D] * decay
    hb_ref[...] = filt[:, D:2 * D] * decay


def _filters(n, fw0, fb0, fw1, fb1, fw2, fb2, freq, fwout):
    t = jnp.linspace(0.0, 1.0, n, dtype=F32)[:, None]
    w = 2.0 * math.pi * jnp.arange(n, dtype=F32)[:, None] / n
    f = jnp.linspace(1e-4, HY_BANDS - 1, HY_BANDS, dtype=F32)[None]
    phase = f * w
    z = jnp.concatenate([t, jnp.cos(phase), -jnp.sin(phase)], axis=-1)
    zp = jnp.zeros((n, 128), F32).at[:, :HY_EMB].set(z)
    fw0p = jnp.zeros((128, HY_HID), F32).at[:HY_EMB].set(fw0)
    max_decay = math.log(HY_DECAY_TARGET) / HY_FAST_DECAY
    min_decay = math.log(HY_DECAY_TARGET) / HY_SLOW_DECAY
    deltas = jnp.abs(jnp.linspace(min_decay, max_decay, D, dtype=F32))[None]
    tl = 256
    row = lambda i: (i, 0)
    return pl.pallas_call(
        _filter_kernel,
        grid=(n // tl,),
        in_specs=[pl.BlockSpec((tl, 128), row), _resident((128, HY_HID)), _resident((1, HY_HID)),
                  _resident((HY_HID, HY_HID)), _resident((1, HY_HID)),
                  _resident((HY_HID, HY_HID)), _resident((1, HY_HID)),
                  _resident((3, HY_HID)), _resident((HY_HID, 2 * D)), _resident((1, D))],
        out_specs=[pl.BlockSpec((tl, D), row)] * 2,
        out_shape=[jax.ShapeDtypeStruct((n, D), F32)] * 2,
        compiler_params=_cparams(("arbitrary",)),
        name="hyena_filter",
    )(zp, fw0p, fb0[None], fw1, fb1[None], fw2, fb2[None], freq, fwout, deltas)


def _dft_tables(n):
    q = 1 << (int(math.log2(n)) // 2 + 1)
    m = jnp.arange(n, dtype=jnp.int32)[None, :]

    def thin(k):
        ang = ((k[:, None] * m) % (2 * n)).astype(F32) * (math.pi / n)
        return jnp.cos(ang)[:, None, :], jnp.sin(ang)[:, None, :]

    ca, sa = thin(q * jnp.arange(n // q, dtype=jnp.int32))
    cb, sb = thin(jnp.arange(q, dtype=jnp.int32))
    cb, sb = cb[:, 0][None], sb[:, 0][None]
    return (ca * cb - sa * sb).reshape(n, n), (sa * cb + ca * sb).reshape(n, n)


def _alt_sign(rows, cols):
    r = lax.broadcasted_iota(jnp.int32, (rows, cols), 0)
    return (1 - 2 * (r & 1)).astype(F32)


def _spectrum_kernel(hf_ref, hb_ref, ch_ref, cl_ref, sh_ref, sl_ref, kr_ref, ki_ref, kn_ref, *, n):
    hf = hf_ref[...]
    row = lax.broadcasted_iota(jnp.int32, hf.shape, 0)
    hb = jnp.where(row == 0, 0.0, hb_ref[...])
    even = hf + hb
    odd = hb - hf
    eh, el = _split(even)
    oh, ol = _split(odd)
    kr = _dot(ch_ref[...], eh) + _dot(ch_ref[...], el) + _dot(cl_ref[...], eh)
    ki = _dot(sh_ref[...], oh) + _dot(sh_ref[...], ol) + _dot(sl_ref[...], oh)
    tk = kr.shape[0]
    k = lax.broadcasted_iota(jnp.int32, kr.shape, 0) + pl.program_id(1) * tk
    scale = jnp.where(k == 0, 0.5 / n, 1.0 / n)
    kr_ref[...] = kr * scale
    ki_ref[...] = ki * scale
    nyq = jnp.sum(even * _alt_sign(*hf.shape), axis=0, keepdims=True) * (0.5 / n)
    kn_ref[...] = jnp.broadcast_to(nyq, kn_ref.shape)


def _spectrum(hf, hb, tables):
    n = hf.shape[0]
    tk = min(n, 512)
    col = lambda j, k: (0, j)
    tab = pl.BlockSpec((tk, n), lambda j, k: (k, 0))
    out = pl.BlockSpec((tk, TC), lambda j, k: (k, j))
    return pl.pallas_call(
        functools.partial(_spectrum_kernel, n=n),
        grid=(D // TC, n // tk),
        in_specs=[pl.BlockSpec((n, TC), col)] * 2 + [tab] * 4,
        out_specs=[out, out, pl.BlockSpec((B, TC), col)],
        out_shape=[jax.ShapeDtypeStruct((n, D), F32)] * 2 + [jax.ShapeDtypeStruct((B, D), F32)],
        compiler_params=_cparams(("arbitrary", "arbitrary")),
        name="hyena_spectrum",
    )(hf, hb, *tables)


def _long_conv_kernel(p_ref, x0_ref, kr_ref, ki_ref, kn_ref, bias_ref, c_ref, s_ref, o_ref,
                      pb_scr, yr_scr, ys_scr, *, n, tk):
    pb_scr[...] = p_ref[...].astype(BF16)

    chunks = [slice(k * tk, (k + 1) * tk) for k in range(n // tk)]
    for rows in chunks:
        xr = _dot(c_ref[rows, :], pb_scr[...])
        xs = _dot(s_ref[rows, :], pb_scr[...])
        kr = kr_ref[rows, :]
        ki = ki_ref[rows, :]
        yr_scr[rows, :] = (xr * kr + xs * ki).astype(BF16)
        ys_scr[rows, :] = (xs * kr - xr * ki).astype(BF16)
    tc = p_ref.shape[1]
    nyq = jnp.sum(p_ref[...] * _alt_sign(n, tc), axis=0, keepdims=True) * kn_ref[0:1, :]
    alt_nyq = _alt_sign(tk, tc) * nyq
    for rows in chunks:
        y = _dot(c_ref[rows, :], yr_scr[...]) + _dot(s_ref[rows, :], ys_scr[...]) + alt_nyq
        o_ref[rows, :] = x0_ref[rows, :] * (y + p_ref[rows, :] * bias_ref[...])


def _long_conv_into_kernel(y_prev_ref, *refs, n, tk):
    del y_prev_ref
    _long_conv_kernel(*refs, n=n, tk=tk)


def _long_conv(p2, x02, kr, ki, kn, bias, ctab, stab, n, row_block, y_prev=None):
    tc = TC if n == L else D
    nc = D // tc
    tk = min(n, 512)
    col = lambda j: (row_block, j)
    ch = lambda j: (0, j % nc)
    in_specs = [pl.BlockSpec((n, tc), col), pl.BlockSpec((n, tc), col),
                pl.BlockSpec((n, tc), ch), pl.BlockSpec((n, tc), ch),
                pl.BlockSpec((B, tc), ch), pl.BlockSpec((1, tc), ch),
                _resident((n, n)), _resident((n, n))]
    args = (p2, x02, kr, ki, kn, bias, ctab, stab)
    if y_prev is None:
        body, aliases = _long_conv_kernel, {}
    else:
        body, aliases = _long_conv_into_kernel, {0: 0}
        in_specs = [pl.BlockSpec(memory_space=pl.ANY)] + in_specs
        args = (y_prev,) + args
    return pl.pallas_call(
        functools.partial(body, n=n, tk=tk),
        grid=(B * D // tc,),
        scratch_shapes=[pltpu.VMEM((n, tc), BF16)] * 3,
        in_specs=in_specs,
        out_specs=pl.BlockSpec((n, tc), col),
        out_shape=jax.ShapeDtypeStruct((ROWS // B, B * D), F32),
        input_output_aliases=aliases,
        compiler_params=_cparams(("arbitrary",)),
        name="hyena_long_conv",
    )(*args)


def _gelu_tanh(x):
    return x * (0.5 * (1.0 + jnp.tanh(math.sqrt(2.0 / math.pi) * (x + 0.044715 * (x * x * x)))))


def _rg_coeffs(xc, hd, wa_ref, ba_ref, wi_ref, bi_ref, lam_ref, a_scr, b_scr):
    sl = slice(hd * RG_BLOCK, (hd + 1) * RG_BLOCK)
    xb = xc.astype(BF16)
    lam = lam_ref[:, sl]
    softplus_neg = jnp.maximum(-lam, 0.0) + jnp.log1p(jnp.exp(-jnp.abs(lam)))
    r = _sigmoid(_dot(xb, wa_ref[hd]) + ba_ref[:, sl])
    i = _sigmoid(_dot(xb, wi_ref[hd]) + bi_ref[:, sl])
    a = jnp.exp(-RG_C * r * softplus_neg)
    a_scr[:, sl] = a
    b_scr[:, sl] = jnp.sqrt((1.0 - a) * (1.0 + a)) * i * xc


def _scan_tile(a_scr, b_scr, h_scr, emit, reverse):
    steps = TM // B

    def body(k, h):
        t = steps - 1 - k if reverse else k
        r0 = pl.multiple_of(t * B, B)
        h = a_scr[pl.ds(r0, B), :] * h + b_scr[pl.ds(r0, B), :]
        emit(r0, h)
        return h

    h_scr[...] = lax.fori_loop(0, steps, body, h_scr[...], unroll=8)


def _rg_fwd_order(s):
    return jnp.where(s < NT_CTX, NT_LAT + s, s - NT_CTX)


def _rg_bwd_order(s):
    return NT - 1 - s


def _rg_in_kernel(xp_ref, x_ref, xn_ref, mod_ref, g_ref, wg_ref, wr_ref, bg_ref, br_ref,
                  cw_ref, cb_ref, wa_ref, ba_ref, wi_ref, bi_ref, lam_ref,
                  xc_ref, gg_ref, hs_ref, a_scr, b_scr, h_scr):
    s = pl.program_id(0)
    t = _rg_fwd_order(s)
    xa = jnp.concatenate([xp_ref[...], x_ref[...], xn_ref[...]], axis=0)
    h = _modulate(_rmsnorm(xa, g_ref[2:3]), _mod(mod_ref, 3), _mod(mod_ref, 4)).astype(BF16)
    keep_lo = jnp.where(_seq_first(t), 0.0, 1.0).astype(F32)
    keep_hi = jnp.where(_seq_last(t), 0.0, 1.0).astype(F32)
    for hd in range(RG_HEADS):
        sl = slice(hd * RG_BLOCK, (hd + 1) * RG_BLOCK)
        gg_ref[:, sl] = _gelu_tanh(_dot(h[B:TM + B], wg_ref[:, sl]) + bg_ref[:, sl])
        u = _dot(h, wr_ref[:, sl]) + br_ref[:, sl]
        taps = (jnp.concatenate([u[0:B] * keep_lo, u[B:TM]], axis=0),
                u[B:TM + B],
                jnp.concatenate([u[2 * B:TM + B], u[TM + B:TM + 2 * B] * keep_hi], axis=0),
                jnp.concatenate([u[3 * B:TM + B], u[TM + B:TM + 3 * B] * keep_hi], axis=0))
        xc = cb_ref[:, sl]
        for k, tap in enumerate(taps):
            xc = xc + cw_ref[k:k + 1, sl] * tap
        xc_ref[:, sl] = xc
        _rg_coeffs(xc, hd, wa_ref, ba_ref, wi_ref, bi_ref, lam_ref, a_scr, b_scr)

    @pl.when(s == 0)
    def _():
        h_scr[...] = jnp.zeros((B, D), F32)

    def emit(r0, hv):
        hs_ref[pl.ds(r0, B), :] = hv

    _scan_tile(a_scr, b_scr, h_scr, emit, reverse=False)


def _rg_in(x, mods, g, w_in, b_in, conv_w, conv_b, wa, ba, wi, bi, lam):
    order = _rg_fwd_order
    mod_spec = pl.BlockSpec((None, B, N_MOD * D),
                            lambda s: (jnp.where(order(s) < NT_LAT, 0, 1), 0, 0))
    out_spec = pl.BlockSpec((TM, D), lambda s: (order(s), 0))
    gate_w = _resident((RG_HEADS, RG_BLOCK, RG_BLOCK))
    half = lambda rows, j: pl.BlockSpec((rows, D), lambda s: (0, j), pipeline_mode=pl.Buffered(1))
    return pl.pallas_call(
        _rg_in_kernel,
        grid=(NT,),
        in_specs=_halo_specs(order, B, 2 * B) + [
            mod_spec, _resident((6, D)), half(D, 0), half(D, 1),
            half(1, 0), half(1, 1), _resident((4, D)), _resident((1, D)),
            gate_w, _resident((1, D)), gate_w, _resident((1, D)), _resident((1, D))],
        out_specs=[out_spec] * 3,
        out_shape=[jax.ShapeDtypeStruct((ROWS, D), F32)] * 3,
        scratch_shapes=[pltpu.VMEM((TM, D), F32), pltpu.VMEM((TM, D), F32),
                        pltpu.VMEM((B, D), F32)],
        compiler_params=_cparams(("arbitrary",)),
        name="rglru_in_fwd_scan",
    )(x, x, x, mods, g, w_in, w_in, b_in, b_in, conv_w, conv_b, wa, ba, wi, bi, lam)


def _rg_out_kernel(xc_ref, gg_ref, hs_ref, wa_ref, ba_ref, wi_ref, bi_ref, lam_ref,
                   y_ref, a_scr, b_scr, h_scr):
    s = pl.program_id(0)
    for hd in range(RG_HEADS):
        xc = xc_ref[:, hd * RG_BLOCK:(hd + 1) * RG_BLOCK]
        _rg_coeffs(xc, hd, wa_ref, ba_ref, wi_ref, bi_ref, lam_ref, a_scr, b_scr)

    @pl.when(s == 0)
    def _():
        h_scr[...] = jnp.zeros((B, D), F32)

    def emit(r0, hv):
        rows = pl.ds(r0, B)
        y_ref[rows, :] = (hs_ref[rows, :] + hv) * gg_ref[rows, :]

    _scan_tile(a_scr, b_scr, h_scr, emit, reverse=True)


def _rg_out(xc, gg, hs, wa, ba, wi, bi, lam):
    order = _rg_bwd_order
    in_spec = pl.BlockSpec((TM, D), lambda s: (order(s), 0))
    gate_w = _resident((RG_HEADS, RG_BLOCK, RG_BLOCK))
    return pl.pallas_call(
        _rg_out_kernel,
        grid=(NT,),
        in_specs=[in_spec] * 3 + [gate_w, _resident((1, D)), gate_w, _resident((1, D)),
                                  _resident((1, D))],
        out_specs=in_spec,
        out_shape=jax.ShapeDtypeStruct((ROWS, D), F32),
        scratch_shapes=[pltpu.VMEM((TM, D), F32), pltpu.VMEM((TM, D), F32),
                        pltpu.VMEM((B, D), F32)],
        compiler_params=_cparams(("arbitrary",)),
        name="rglru_bwd_scan",
    )(xc, gg, hs, wa, ba, wi, bi, lam)


def _grid_pos():
    rows = L // GRID_W
    row = jnp.repeat(jnp.arange(rows, dtype=F32), GRID_W)
    col = jnp.tile(jnp.arange(GRID_W, dtype=F32), rows)
    quarter = D // 4
    omega = POS_BASE ** (-jnp.arange(quarter, dtype=F32) / quarter)

    def emb(q):
        ang = q[:, None] * omega[None]
        return jnp.concatenate([jnp.sin(ang), jnp.cos(ang)], axis=-1)

    return jnp.concatenate([emb(row), emb(col)], axis=-1)


def kernel(x, c, ctx, c_ctx, ada_w, ada_b, norm_g, ffn_w1, ffn_w3, ffn_w2, hy_w_in, hy_b_in, hy_conv_w, hy_conv_b, hy_fw0, hy_fb0, hy_fw1, hy_fb1, hy_fw2, hy_fb2, hy_freq, hy_fwout, hy_filt_bias, hy_w_out, hy_b_out, rg_w_in, rg_b_in, rg_conv_w, rg_conv_b, rg_wa, rg_ba, rg_wi, rg_bi, rg_lam, rg_w_out, rg_b_out):
    cc = jnp.concatenate([c, jnp.broadcast_to(c_ctx[None], (B, D))], axis=0)
    mods = _mods(cc, ada_w, ada_b).reshape(DEPTH, 2, B, N_MOD * D)
    w1 = ffn_w1.astype(BF16)
    w3 = ffn_w3.astype(BF16)
    w2 = ffn_w2.astype(BF16)

    g = norm_g[0]
    xs = _ffn_first(x, ctx, _grid_pos(), mods[0], g, w1, w3, w2)
    x0, p = _hy_in(xs, mods[0], g, hy_w_in[0].astype(BF16), hy_b_in[0][None],
                   hy_conv_w[0], hy_conv_b[0][None])
    fparams = (hy_fw0[0], hy_fb0[0], hy_fw1[0], hy_fb1[0], hy_fw2[0], hy_fb2[0],
               hy_freq[0], hy_fwout[0])
    bias = hy_filt_bias[0][None]
    y = None
    for n, row_block in ((CTX, L // CTX), (L, 0)):
        hf, hb = _filters(n, *fparams)
        ctab, stab = _dft_tables(n)
        chi, clo = _split(ctab)
        shi, slo = _split(stab)
        kr, ki, kn = _spectrum(hf, hb, (chi, clo, shi, slo))
        y = _long_conv(p, x0, kr, ki, kn, bias, chi, shi, n, row_block, y_prev=y)
    xs = _mix_ffn(y, xs, mods[0], g, hy_w_out[0].astype(BF16), hy_b_out[0][None], w1, w3, w2,
                  layer=0, wide_in=True, batch_major_out=False)

    g = norm_g[1]
    xs = _ffn_plain(xs, mods[1], g, w1, w3, w2, layer=1)
    wa = rg_wa[0].astype(BF16)
    wi = rg_wi[0].astype(BF16)
    xc, gg, hs = _rg_in(xs, mods[1], g, rg_w_in[0].astype(BF16), rg_b_in[0][None],
                        rg_conv_w[0], rg_conv_b[0][None], wa[0], rg_ba[0, 0][None],
                        wi[0], rg_bi[0, 0][None], rg_lam[0, 0][None])
    y = _rg_out(xc, gg, hs, wa[1], rg_ba[0, 1][None], wi[1], rg_bi[0, 1][None], rg_lam[0, 1][None])
    return _mix_ffn(y, xs, mods[1], g, rg_w_out[0].astype(BF16), rg_b_out[0][None], w1, w3, w2,
                    layer=1, wide_in=False, batch_major_out=True)
```

```python
import functools
import math

import jax
import jax.numpy as jnp
import numpy as np
from jax import lax
from jax.experimental import pallas as pl
from jax.experimental.pallas import tpu as pltpu

F32 = jnp.float32
BF16 = jnp.bfloat16

D = 1024
B = 8
LANES = 128
MXU_N = 256
L = 2048
CTX = 256
DEPTH = 2
GRID_W = 64
D_FF = 2816
N_MOD = 9
MACARON = 0.5
NORM_EPS = 1e-6
POS_BASE = 10000.0
HY_EMB = 33
HY_BANDS = 16
HY_HID = 64
HY_FAST_DECAY = 0.3
HY_SLOW_DECAY = 1.5
HY_DECAY_TARGET = 1e-2
RG_HEADS = 4
RG_BLOCK = D // RG_HEADS
RG_C = 8.0

ROWS_LAT = L * B
ROWS_CTX = CTX * B
ROWS = ROWS_LAT + ROWS_CTX
TM = 512
TS = TM // B
NT_LAT = ROWS_LAT // TM
NT_CTX = ROWS_CTX // TM
NT = NT_LAT + NT_CTX
F_CHUNKS = ((0, 1536), (1536, D_FF))
F_QUARTERS = ((0, 768), (768, 1536), (1536, 2304), (2304, D_FF))
TC = 256
VMEM_LIMIT = 56 * 1024 * 1024


def _cparams(sem):
    return pltpu.CompilerParams(dimension_semantics=sem, vmem_limit_bytes=VMEM_LIMIT)


def _resident(shape):
    nd = len(shape)
    return pl.BlockSpec(shape, lambda *_: (0,) * nd, pipeline_mode=pl.Buffered(1))


def _split(a):
    hi = a.astype(BF16)
    lo = (a - hi.astype(F32)).astype(BF16)
    return hi, lo


def _dot(a, b):
    return jnp.dot(a, b, preferred_element_type=F32)


def _dot3(a, b):
    ah, al = _split(a)
    bh, bl = _split(b)
    return _dot(ah, bh) + _dot(ah, bl) + _dot(al, bh)


def _rmsnorm(x, g):
    ms = jnp.mean(x * x, axis=-1, keepdims=True)
    return x * lax.rsqrt(ms + NORM_EPS) * g


def _mod(mod_ref, k):
    return mod_ref[:, k * D:(k + 1) * D]


def _per_batch(x, fn):
    rows = x.shape[0]
    return fn(x.reshape(rows // B, B, x.shape[1])).reshape(rows, x.shape[1])


def _modulate(xn, shift8, scale8):
    return _per_batch(xn, lambda v: v * (1.0 + scale8)[None] + shift8[None])


def _gated(z, gate8):
    return _per_batch(z, lambda v: v * gate8[None])


def _sigmoid(x):
    return jax.nn.sigmoid(x)


def _ffn(x, mod_ref, k0, g_pre, g_post, w1_ref, w3_ref, w2_ref, chunks=F_CHUNKS, side_work=()):
    h = _modulate(_rmsnorm(x, g_pre), _mod(mod_ref, k0), _mod(mod_ref, k0 + 1)).astype(BF16)
    y = None
    for idx, (c0, c1) in enumerate(chunks):
        a = _dot(h, w1_ref[:, c0:c1])
        b = _dot(h, w3_ref[:, c0:c1])
        if idx < len(side_work):
            side_work[idx]()
        act = (a * _sigmoid(a) * b).astype(BF16)
        part = _dot(act, w2_ref[c0:c1, :])
        y = part if y is None else y + part
    return x + MACARON * _gated(_rmsnorm(y, g_post), _mod(mod_ref, k0 + 2))


def _mods_kernel(cc_ref, w_ref, b_ref, o_ref):
    a = cc_ref[...]
    a = a * _sigmoid(a)
    o_ref[...] = _dot3(a, w_ref[...]) + b_ref[...]


def _mods(cc, ada_w, ada_b):
    tn = 1024
    n = N_MOD * D
    return pl.pallas_call(
        _mods_kernel,
        grid=(DEPTH, n // tn),
        in_specs=[
            pl.BlockSpec((2 * B, D), lambda i, j: (0, 0)),
            pl.BlockSpec((None, D, tn), lambda i, j: (i, 0, j)),
            pl.BlockSpec((None, 1, tn), lambda i, j: (i, 0, j)),
        ],
        out_specs=pl.BlockSpec((None, 2 * B, tn), lambda i, j: (i, 0, j)),
        out_shape=jax.ShapeDtypeStruct((DEPTH, 2 * B, n), F32),
        compiler_params=_cparams(("arbitrary", "arbitrary")),
        name="ada_mods",
    )(cc, ada_w, ada_b.reshape(DEPTH, 1, n))


def _mod_spec():
    return pl.BlockSpec((None, B, N_MOD * D), lambda t: (jnp.where(t < NT_LAT, 0, 1), 0, 0))


def _row_spec(width=D):
    return pl.BlockSpec((TM, width), lambda t: (t, 0))


def _wide_spec():
    return pl.BlockSpec((TS, B * D), lambda t: (t, 0))


def _ffn_weight_specs(layer, half):
    pick = lambda *_: (layer, half, 0, 0)
    up = pl.BlockSpec((None, None, D, D_FF), pick, pipeline_mode=pl.Buffered(1))
    down = pl.BlockSpec((None, None, D_FF, D), pick, pipeline_mode=pl.Buffered(1))
    return [up, up, down]


def _rows_of_batch(b):
    return pl.ds(b, TS, stride=B)


_ROWS_SCRATCH = pltpu.VMEM((D // LANES, TM, LANES), F32)


def _put_batch(rows_scr, b, val):
    for j in range(D // LANES):
        rows_scr[j, _rows_of_batch(b), :] = val[:, j * LANES:(j + 1) * LANES]


def _get_batch(rows_scr, b):
    return jnp.concatenate([rows_scr[j, _rows_of_batch(b), :] for j in range(D // LANES)], axis=1)


def _put_rows(rows_scr, val):
    for j in range(D // LANES):
        rows_scr[j] = val[:, j * LANES:(j + 1) * LANES]


def _get_rows(rows_scr):
    return jnp.concatenate([rows_scr[j] for j in range(D // LANES)], axis=1)


def _ffn_first_kernel(x_ref, ctx_ref, pos_ref, mod_ref, g_ref, w1, w3, w2, o_ref, rows_scr):
    t = pl.program_id(0)

    @pl.when(t < NT_LAT)
    def _():
        for b in range(B):
            _put_batch(rows_scr, b, x_ref[b] + pos_ref[...])

    @pl.when(t >= NT_LAT)
    def _():
        for b in range(B):
            _put_batch(rows_scr, b, ctx_ref[b])

    o_ref[...] = _ffn(_get_rows(rows_scr), mod_ref, 0, g_ref[0:1], g_ref[1:2], w1, w3, w2)


def _ffn_kernel(x_ref, mod_ref, g_ref, w1, w3, w2, o_ref):
    for r in range(x_ref.shape[0] // TM):
        rows = slice(r * TM, (r + 1) * TM)
        o_ref[rows, :] = _ffn(x_ref[rows, :], mod_ref, 0, g_ref[0:1], g_ref[1:2], w1, w3, w2)


def _ffn_first(x, ctx, pos, mods, g, w1, w3, w2):
    lat = lambda t: jnp.minimum(t, NT_LAT - 1)
    return pl.pallas_call(
        _ffn_first_kernel,
        grid=(NT,),
        in_specs=[pl.BlockSpec((B, TS, D), lambda t: (0, lat(t), 0)),
                  pl.BlockSpec((B, TS, D), lambda t: (0, jnp.maximum(t - NT_LAT, 0), 0)),
                  pl.BlockSpec((TS, D), lambda t: (lat(t), 0)),
                  _mod_spec(), _resident((6, D))] + _ffn_weight_specs(0, 0),
        out_specs=_row_spec(),
        out_shape=jax.ShapeDtypeStruct((ROWS, D), F32),
        scratch_shapes=[_ROWS_SCRATCH],
        compiler_params=_cparams(("arbitrary",)),
        name="ffn_first",
    )(x, ctx, pos, mods, g, w1, w3, w2)


def _ffn_plain(x, mods, g, w1, w3, w2, layer):
    return pl.pallas_call(
        _ffn_kernel,
        grid=(NT // 2,),
        in_specs=[pl.BlockSpec((2 * TM, D), lambda t: (t, 0)),
                  pl.BlockSpec((None, B, N_MOD * D), lambda t: (jnp.where(t < NT_LAT // 2, 0, 1), 0, 0)),
                  _resident((6, D))] + _ffn_weight_specs(layer, 0),
        out_specs=pl.BlockSpec((2 * TM, D), lambda t: (t, 0)),
        out_shape=jax.ShapeDtypeStruct((ROWS, D), F32),
        compiler_params=_cparams(("arbitrary",)),
        name="ffn_pre",
    )(x, mods, g, w1, w3, w2)


def _mix_ffn_core(y, x, mod_ref, g_ref, wo_ref, bo_ref, w1, w3, w2, **ffn_kwargs):
    z = _dot(y.astype(BF16), wo_ref[...]) + bo_ref[...]
    x = x + _gated(_rmsnorm(z, g_ref[3:4]), _mod(mod_ref, 5))
    return _ffn(x, mod_ref, 6, g_ref[4:5], g_ref[5:6], w1, w3, w2, **ffn_kwargs)


def _hy_out_ffn_kernel(ylat_ref, yctx_ref, x_ref, mod_ref, g_ref, wo_ref, bo_ref, w1, w3, w2,
                       o_ref, rows_scr):
    t = pl.program_id(0)

    @pl.when(t < NT_LAT)
    def _():
        for b in range(B):
            _put_batch(rows_scr, b, ylat_ref[:, b * D:(b + 1) * D])

    @pl.when(t >= NT_LAT)
    def _():
        for b in range(B):
            _put_batch(rows_scr, b, yctx_ref[:, b * D:(b + 1) * D])

    o_ref[...] = _mix_ffn_core(_get_rows(rows_scr), x_ref[...], mod_ref, g_ref, wo_ref, bo_ref,
                               w1, w3, w2)


def _hy_out_ffn(y_lat, y_ctx, x, mods, g, w_out, b_out, w1, w3, w2):
    return pl.pallas_call(
        _hy_out_ffn_kernel,
        grid=(NT,),
        in_specs=[pl.BlockSpec((TS, B * D), lambda t: (jnp.minimum(t, NT_LAT - 1), 0)),
                  pl.BlockSpec((TS, B * D), lambda t: (jnp.maximum(t - NT_LAT, 0), 0)),
                  _row_spec(), _mod_spec(), _resident((6, D)), _resident((D, D)),
                  _resident((1, D))] + _ffn_weight_specs(0, 1),
        out_specs=_row_spec(),
        out_shape=jax.ShapeDtypeStruct((ROWS, D), F32),
        scratch_shapes=[_ROWS_SCRATCH],
        compiler_params=_cparams(("arbitrary",)),
        name="hyena_out_ffn",
    )(y_lat, y_ctx, x, mods, g, w_out, b_out, w1, w3, w2)


def _seq_first(t):
    return jnp.logical_or(t == 0, t == NT_LAT)


def _seq_last(t):
    return jnp.logical_or(t == NT_LAT - 1, t == NT - 1)


def _halo_specs(order, lo_rows, hi_rows):
    nlo = ROWS // lo_rows
    nhi = ROWS // hi_rows
    return [
        pl.BlockSpec((lo_rows, D), lambda s: (jnp.maximum(order(s) * (TM // lo_rows) - 1, 0), 0)),
        pl.BlockSpec((TM, D), lambda s: (order(s), 0)),
        pl.BlockSpec((hi_rows, D),
                     lambda s: (jnp.minimum((order(s) + 1) * (TM // hi_rows), nhi - 1), 0)),
    ]


def _hy_in_kernel(xp_ref, x_ref, xn_ref, mod_ref, g_ref, w_ref, b_ref, cw_ref, cb_ref,
                  x0_ref, p_ref, x0_scr, p_scr):
    t = pl.program_id(0)
    xa = jnp.concatenate([xp_ref[...], x_ref[...], xn_ref[...]], axis=0)
    h = _modulate(_rmsnorm(xa, g_ref[2:3]), _mod(mod_ref, 3), _mod(mod_ref, 4)).astype(BF16)
    keep_lo = jnp.where(_seq_first(t), 0.0, 1.0).astype(F32)
    keep_hi = jnp.where(_seq_last(t), 0.0, 1.0).astype(F32)

    def conv_cols(c0):
        cols = slice(c0, c0 + MXU_N)
        u = _dot(h, w_ref[:, cols]) + b_ref[:, cols]
        lo = jnp.concatenate([u[0:B] * keep_lo, u[B:TM]], axis=0)
        hi = jnp.concatenate([u[2 * B:TM + B], u[TM + B:TM + 2 * B] * keep_hi], axis=0)
        return (cb_ref[:, cols] + cw_ref[0:1, cols] * lo + cw_ref[1:2, cols] * u[B:TM + B]
                + cw_ref[2:3, cols] * hi)

    slabs = MXU_N // LANES
    for c in range(D // MXU_N):
        c0 = c * MXU_N
        vals = (conv_cols(c0), conv_cols(D + c0) * conv_cols(2 * D + c0))
        for out_ref, scr, val in zip((x0_ref, p_ref), (x0_scr, p_scr), vals):
            for j in range(slabs):
                scr[c * slabs + j] = val[:, j * LANES:(j + 1) * LANES]
            for b in range(B):
                out_ref[:, b * D + c0:b * D + c0 + MXU_N] = jnp.concatenate(
                    [scr[c * slabs + j, _rows_of_batch(b), :] for j in range(slabs)], axis=1)


def _hy_in(x, mods, g, w_in, b_in, conv_w, conv_b):
    return pl.pallas_call(
        _hy_in_kernel,
        grid=(NT,),
        in_specs=_halo_specs(lambda s: s, B, B) + [
            _mod_spec(), _resident((6, D)), _resident((D, 3 * D)), _resident((1, 3 * D)),
            _resident((3, 3 * D)), _resident((1, 3 * D))],
        out_specs=[_wide_spec(), _wide_spec()],
        out_shape=[jax.ShapeDtypeStruct((ROWS // B, B * D), F32)] * 2,
        scratch_shapes=[_ROWS_SCRATCH, _ROWS_SCRATCH],
        compiler_params=_cparams(("arbitrary",)),
        name="hyena_in",
    )(x, x, x, mods, g, w_in, b_in, conv_w, conv_b)


def _filter_kernel(z_ref, fw0, fb0, fw1, fb1, fw2, fb2, freq, fwout, deltas, hf_ref, hb_ref):
    z = z_ref[...]
    h = jnp.sin(freq[0:1] * (_dot3(z, fw0[...]) + fb0[...]))
    h = jnp.sin(freq[1:2] * (_dot3(h, fw1[...]) + fb1[...]))
    h = jnp.sin(freq[2:3] * (_dot3(h, fw2[...]) + fb2[...]))
    filt = _dot3(h, fwout[...])
    decay = jnp.exp(-z[:, 0:1] * deltas[...])
    hf_ref[...] = filt[:, 0:D] * decay
    hb_ref[...] = filt[:, D:2 * D] * decay


def _filters(n, fw0, fb0, fw1, fb1, fw2, fb2, freq, fwout):
    t = jnp.linspace(0.0, 1.0, n, dtype=F32)[:, None]
    w = 2.0 * math.pi * jnp.arange(n, dtype=F32)[:, None] / n
    f = jnp.linspace(1e-4, HY_BANDS - 1, HY_BANDS, dtype=F32)[None]
    phase = f * w
    z = jnp.concatenate([t, jnp.cos(phase), -jnp.sin(phase)], axis=-1)
    zp = jnp.zeros((n, 128), F32).at[:, :HY_EMB].set(z)
    fw0p = jnp.zeros((128, HY_HID), F32).at[:HY_EMB].set(fw0)
    max_decay = math.log(HY_DECAY_TARGET) / HY_FAST_DECAY
    min_decay = math.log(HY_DECAY_TARGET) / HY_SLOW_DECAY
    deltas = jnp.abs(jnp.linspace(min_decay, max_decay, D, dtype=F32))[None]
    tl = 256
    row = lambda i: (i, 0)
    return pl.pallas_call(
        _filter_kernel,
        grid=(n // tl,),
        in_specs=[pl.BlockSpec((tl, 128), row), _resident((128, HY_HID)), _resident((1, HY_HID)),
                  _resident((HY_HID, HY_HID)), _resident((1, HY_HID)),
                  _resident((HY_HID, HY_HID)), _resident((1, HY_HID)),
                  _resident((3, HY_HID)), _resident((HY_HID, 2 * D)), _resident((1, D))],
        out_specs=[pl.BlockSpec((tl, D), row)] * 2,
        out_shape=[jax.ShapeDtypeStruct((n, D), F32)] * 2,
        compiler_params=_cparams(("arbitrary",)),
        name="hyena_filter",
    )(zp, fw0p, fb0[None], fw1, fb1[None], fw2, fb2[None], freq, fwout, deltas)


def _dft_tables(n):
    q = 1 << (int(math.log2(n)) // 2 + 1)
    m = jnp.arange(n, dtype=jnp.int32)[None, :]

    def thin(k):
        ang = ((k[:, None] * m) % (2 * n)).astype(F32) * (math.pi / n)
        return jnp.cos(ang)[:, None, :], jnp.sin(ang)[:, None, :]

    ca, sa = thin(q * jnp.arange(n // q, dtype=jnp.int32))
    cb, sb = thin(jnp.arange(q, dtype=jnp.int32))
    cb, sb = cb[:, 0][None], sb[:, 0][None]
    return (ca * cb - sa * sb).reshape(n, n), (sa * cb + ca * sb).reshape(n, n)


def _alt_sign(rows, cols):
    r = lax.broadcasted_iota(jnp.int32, (rows, cols), 0)
    return (1 - 2 * (r & 1)).astype(F32)


def _spectrum_kernel(hf_ref, hb_ref, c_ref, s_ref, kr_ref, ki_ref, kn_ref, *, n, tk):
    hf = hf_ref[...]
    row = lax.broadcasted_iota(jnp.int32, hf.shape, 0)
    hb = jnp.where(row == 0, 0.0, hb_ref[...])
    even = hf + hb
    eb = even.astype(BF16)
    ob = (hb - hf).astype(BF16)
    first = jnp.where(lax.broadcasted_iota(jnp.int32, (tk, hf.shape[1]), 0) == 0, 0.5 / n, 1.0 / n)
    for k in range(n // tk):
        rows = slice(k * tk, (k + 1) * tk)
        scale = first if k == 0 else 1.0 / n
        kr_ref[rows, :] = _dot(c_ref[rows, :], eb) * scale
        ki_ref[rows, :] = _dot(s_ref[rows, :], ob) * scale
    nyq = jnp.sum(even * _alt_sign(*hf.shape), axis=0, keepdims=True) * (0.5 / n)
    kn_ref[...] = jnp.broadcast_to(nyq, kn_ref.shape)


def _spectrum(hf, hb, ctab, stab):
    n = hf.shape[0]
    col = lambda j: (0, j)
    return pl.pallas_call(
        functools.partial(_spectrum_kernel, n=n, tk=min(n, 512)),
        grid=(D // TC,),
        in_specs=[pl.BlockSpec((n, TC), col)] * 2 + [_resident((n, n))] * 2,
        out_specs=[pl.BlockSpec((n, TC), col)] * 2 + [pl.BlockSpec((B, TC), col)],
        out_shape=[jax.ShapeDtypeStruct((n, D), F32)] * 2 + [jax.ShapeDtypeStruct((B, D), F32)],
        compiler_params=_cparams(("arbitrary",)),
        name="hyena_spectrum",
    )(hf, hb, ctab, stab)


def _long_conv_kernel(p_ref, x0_ref, kr_ref, ki_ref, kn_ref, bias_ref, c_ref, s_ref, o_ref,
                      pb_scr, yr_scr, ys_scr, *, n, tk):
    pb_scr[...] = p_ref[...].astype(BF16)

    chunks = [slice(k * tk, (k + 1) * tk) for k in range(n // tk)]
    for rows in chunks:
        xr = _dot(c_ref[rows, :], pb_scr[...])
        xs = _dot(s_ref[rows, :], pb_scr[...])
        kr = kr_ref[rows, :]
        ki = ki_ref[rows, :]
        yr_scr[rows, :] = (xr * kr + xs * ki).astype(BF16)
        ys_scr[rows, :] = (xs * kr - xr * ki).astype(BF16)
    tc = p_ref.shape[1]
    nyq = jnp.sum(p_ref[...] * _alt_sign(n, tc), axis=0, keepdims=True) * kn_ref[0:1, :]
    alt_nyq = _alt_sign(tk, tc) * nyq
    for rows in chunks:
        y = _dot(c_ref[rows, :], yr_scr[...]) + _dot(s_ref[rows, :], ys_scr[...]) + alt_nyq
        o_ref[rows, :] = x0_ref[rows, :] * (y + p_ref[rows, :] * bias_ref[...])


def _long_conv(p2, x02, kr, ki, kn, bias, ctab, stab, n, row_block):
    tc = TC if n == L else D
    nc = D // tc
    tk = min(n, 512)
    col = lambda j: (row_block, j)
    ch = lambda j: (0, j % nc)
    return pl.pallas_call(
        functools.partial(_long_conv_kernel, n=n, tk=tk),
        grid=(B * D // tc,),
        scratch_shapes=[pltpu.VMEM((n, tc), BF16)] * 3,
        in_specs=[pl.BlockSpec((n, tc), col), pl.BlockSpec((n, tc), col),
                  pl.BlockSpec((n, tc), ch), pl.BlockSpec((n, tc), ch),
                  pl.BlockSpec((B, tc), ch), pl.BlockSpec((1, tc), ch),
                  _resident((n, n)), _resident((n, n))],
        out_specs=pl.BlockSpec((n, tc), lambda j: (0, j)),
        out_shape=jax.ShapeDtypeStruct((n, B * D), F32),
        compiler_params=_cparams(("arbitrary",)),
        name="hyena_long_conv",
    )(p2, x02, kr, ki, kn, bias, ctab, stab)


def _gelu_tanh(x):
    return x * (0.5 * (1.0 + jnp.tanh(math.sqrt(2.0 / math.pi) * (x + 0.044715 * (x * x * x)))))


def _rg_coeffs(xc, hd, wa_ref, ba_ref, wi_ref, bi_ref, lam_ref, a_scr, b_scr):
    sl = slice(hd * RG_BLOCK, (hd + 1) * RG_BLOCK)
    xb = xc.astype(BF16)
    lam = lam_ref[:, sl]
    softplus_neg = jnp.maximum(-lam, 0.0) + jnp.log1p(jnp.exp(-jnp.abs(lam)))
    r = _sigmoid(_dot(xb, wa_ref[hd]) + ba_ref[:, sl])
    i = _sigmoid(_dot(xb, wi_ref[hd]) + bi_ref[:, sl])
    a = jnp.exp(-RG_C * r * softplus_neg)
    a_scr[:, sl] = a
    b_scr[:, sl] = jnp.sqrt((1.0 - a) * (1.0 + a)) * i * xc


def _scan_tile(a_scr, b_scr, h_scr, emit, reverse):
    steps = TM // B

    def body(k, h):
        t = steps - 1 - k if reverse else k
        r0 = pl.multiple_of(t * B, B)
        h = a_scr[pl.ds(r0, B), :] * h + b_scr[pl.ds(r0, B), :]
        emit(r0, h)
        return h

    h_scr[...] = lax.fori_loop(0, steps, body, h_scr[...], unroll=8)


def _rg_fwd_order(s):
    return jnp.where(s < NT_CTX, NT_LAT + s, s - NT_CTX)


def _rg_in_kernel(xp_ref, x_ref, xn_ref, mod_ref, g_ref, wg_ref, wr_ref, bg_ref, br_ref,
                  cw_ref, cb_ref, wa_ref, ba_ref, wi_ref, bi_ref, lam_ref,
                  xc_ref, gate_ref, hs_ref, a_scr, b_scr, h_scr):
    s = pl.program_id(0)
    t = _rg_fwd_order(s)
    xa = jnp.concatenate([xp_ref[...], x_ref[...], xn_ref[...]], axis=0)
    h = _modulate(_rmsnorm(xa, g_ref[2:3]), _mod(mod_ref, 3), _mod(mod_ref, 4)).astype(BF16)
    keep_lo = jnp.where(_seq_first(t), 0.0, 1.0).astype(F32)
    keep_hi = jnp.where(_seq_last(t), 0.0, 1.0).astype(F32)
    for hd in range(RG_HEADS):
        sl = slice(hd * RG_BLOCK, (hd + 1) * RG_BLOCK)
        gate_ref[:, sl] = (_dot(h[B:TM + B], wg_ref[:, sl]) + bg_ref[:, sl]).astype(BF16)
        u = _dot(h, wr_ref[:, sl]) + br_ref[:, sl]
        taps = (jnp.concatenate([u[0:B] * keep_lo, u[B:TM]], axis=0),
                u[B:TM + B],
                jnp.concatenate([u[2 * B:TM + B], u[TM + B:TM + 2 * B] * keep_hi], axis=0),
                jnp.concatenate([u[3 * B:TM + B], u[TM + B:TM + 3 * B] * keep_hi], axis=0))
        xc = cb_ref[:, sl]
        for k, tap in enumerate(taps):
            xc = xc + cw_ref[k:k + 1, sl] * tap
        xc_ref[:, sl] = xc
        _rg_coeffs(xc, hd, wa_ref, ba_ref, wi_ref, bi_ref, lam_ref, a_scr, b_scr)

    @pl.when(s == 0)
    def _():
        h_scr[...] = jnp.zeros((B, D), F32)

    def emit(r0, hv):
        hs_ref[pl.ds(r0, B), :] = hv

    _scan_tile(a_scr, b_scr, h_scr, emit, reverse=False)


def _rg_in(x, mods, g, w_in, b_in, conv_w, conv_b, wa, ba, wi, bi, lam):
    order = _rg_fwd_order
    mod_spec = pl.BlockSpec((None, B, N_MOD * D),
                            lambda s: (jnp.where(order(s) < NT_LAT, 0, 1), 0, 0))
    out_spec = pl.BlockSpec((TM, D), lambda s: (order(s), 0))
    gate_w = _resident((RG_HEADS, RG_BLOCK, RG_BLOCK))
    half = lambda rows, j: pl.BlockSpec((rows, D), lambda s: (0, j), pipeline_mode=pl.Buffered(1))
    return pl.pallas_call(
        _rg_in_kernel,
        grid=(NT,),
        in_specs=_halo_specs(order, B, 2 * B) + [
            mod_spec, _resident((6, D)), half(D, 0), half(D, 1),
            half(1, 0), half(1, 1), _resident((4, D)), _resident((1, D)),
            gate_w, _resident((1, D)), gate_w, _resident((1, D)), _resident((1, D))],
        out_specs=[out_spec] * 3,
        out_shape=[jax.ShapeDtypeStruct((ROWS, D), F32), jax.ShapeDtypeStruct((ROWS, D), BF16),
                   jax.ShapeDtypeStruct((ROWS, D), F32)],
        scratch_shapes=[pltpu.VMEM((TM, D), F32), pltpu.VMEM((TM, D), F32),
                        pltpu.VMEM((B, D), F32)],
        compiler_params=_cparams(("arbitrary",)),
        name="rglru_in_fwd_scan",
    )(x, x, x, mods, g, w_in, w_in, b_in, b_in, conv_w, conv_b, wa, ba, wi, bi, lam)


def _rg_tile_coeffs(xc_ref, wa_ref, ba_ref, wi_ref, bi_ref, lam_ref, a_scr, b_scr):
    for hd in range(RG_HEADS):
        xc = xc_ref[:, hd * RG_BLOCK:(hd + 1) * RG_BLOCK]
        _rg_coeffs(xc, hd, wa_ref, ba_ref, wi_ref, bi_ref, lam_ref, a_scr, b_scr)


def _rg_ctx_bwd_kernel(xc_ref, wa_ref, ba_ref, wi_ref, bi_ref, lam_ref, h_ref, a_scr, b_scr, h_scr):
    _rg_tile_coeffs(xc_ref, wa_ref, ba_ref, wi_ref, bi_ref, lam_ref, a_scr, b_scr)

    @pl.when(pl.program_id(0) == 0)
    def _():
        h_scr[...] = jnp.zeros((B, D), F32)

    _scan_tile(a_scr, b_scr, h_scr, lambda r0, hv: None, reverse=True)
    h_ref[...] = h_scr[...]


def _rg_gate_specs():
    gate_w = _resident((RG_HEADS, RG_BLOCK, RG_BLOCK))
    return [gate_w, _resident((1, D)), gate_w, _resident((1, D)), _resident((1, D))]


def _rg_ctx_bwd(xc, wa, ba, wi, bi, lam):
    return pl.pallas_call(
        _rg_ctx_bwd_kernel,
        grid=(NT_CTX,),
        in_specs=[pl.BlockSpec((TM, D), lambda s: (NT - 1 - s, 0))] + _rg_gate_specs(),
        out_specs=pl.BlockSpec((B, D), lambda s: (0, 0)),
        out_shape=jax.ShapeDtypeStruct((B, D), F32),
        scratch_shapes=[pltpu.VMEM((TM, D), F32), pltpu.VMEM((TM, D), F32),
                        pltpu.VMEM((B, D), F32)],
        compiler_params=_cparams(("arbitrary",)),
        name="rglru_ctx_bwd_scan",
    )(xc, wa, ba, wi, bi, lam)


def _rg_out_ffn_kernel(xc_ref, gate_ref, hs_ref, x_ref, h0_ref, mod_ref, g_ref,
                       wa_ref, ba_ref, wi_ref, bi_ref, lam_ref, wo_ref, bo_ref, w1, w3, w2,
                       o_ref, a_scr, b_scr, hb_scr, h_scr, rows_scr):
    s = pl.program_id(0)

    def head_gates(hd):
        xc = xc_ref[:, hd * RG_BLOCK:(hd + 1) * RG_BLOCK]
        _rg_coeffs(xc, hd, wa_ref, ba_ref, wi_ref, bi_ref, lam_ref, a_scr, b_scr)

    @pl.when(s == 0)
    def _():
        h_scr[...] = h0_ref[...]
        for hd in range(RG_HEADS):
            head_gates(hd)

    @pl.when(s > 0)
    def _():
        y = (hs_ref[...] + hb_scr[...]) * _gelu_tanh(gate_ref[...].astype(F32))
        res = _mix_ffn_core(
            y, x_ref[...], mod_ref, g_ref, wo_ref, bo_ref, w1, w3, w2, chunks=F_QUARTERS,
            side_work=[functools.partial(head_gates, hd) for hd in range(RG_HEADS)])
        _put_rows(rows_scr, res)
        for b in range(B):
            o_ref[b] = _get_batch(rows_scr, b)

    @pl.when(s < NT_LAT)
    def _():
        def emit(r0, hv):
            hb_scr[pl.ds(r0, B), :] = hv

        _scan_tile(a_scr, b_scr, h_scr, emit, reverse=True)


def _rg_out_ffn(xc, gate, hs, x, h0, mods, g, wa, ba, wi, bi, lam, w_out, b_out, w1, w3, w2):
    scan_tile = lambda s: (jnp.maximum(NT_LAT - 1 - s, 0), 0)
    out_tile = lambda s: jnp.minimum(NT_LAT - s, NT_LAT - 1)
    prev = pl.BlockSpec((TM, D), lambda s: (out_tile(s), 0))
    return pl.pallas_call(
        _rg_out_ffn_kernel,
        grid=(NT_LAT + 1,),
        in_specs=[pl.BlockSpec((TM, D), scan_tile), prev, prev, prev, _resident((B, D)),
                  pl.BlockSpec((None, B, N_MOD * D), lambda s: (0, 0, 0)), _resident((6, D))]
        + _rg_gate_specs() + [_resident((D, D)), _resident((1, D))] + _ffn_weight_specs(1, 1),
        out_specs=pl.BlockSpec((B, TS, D), lambda s: (0, out_tile(s), 0)),
        out_shape=jax.ShapeDtypeStruct((B, L, D), F32),
        scratch_shapes=[pltpu.VMEM((TM, D), F32), pltpu.VMEM((TM, D), F32),
                        pltpu.VMEM((TM, D), F32), pltpu.VMEM((B, D), F32), _ROWS_SCRATCH],
        compiler_params=_cparams(("arbitrary",)),
        name="rglru_out_ffn",
    )(xc, gate, hs, x, h0, mods, g, wa, ba, wi, bi, lam, w_out, b_out, w1, w3, w2)


def _grid_pos():
    rows = L // GRID_W
    row = jnp.repeat(jnp.arange(rows, dtype=F32), GRID_W)
    col = jnp.tile(jnp.arange(GRID_W, dtype=F32), rows)
    quarter = D // 4
    omega = POS_BASE ** (-jnp.arange(quarter, dtype=F32) / quarter)

    def emb(q):
        ang = q[:, None] * omega[None]
        return jnp.concatenate([jnp.sin(ang), jnp.cos(ang)], axis=-1)

    return jnp.concatenate([emb(row), emb(col)], axis=-1)


def kernel(x, c, ctx, c_ctx, ada_w, ada_b, norm_g, ffn_w1, ffn_w3, ffn_w2, hy_w_in, hy_b_in, hy_conv_w, hy_conv_b, hy_fw0, hy_fb0, hy_fw1, hy_fb1, hy_fw2, hy_fb2, hy_freq, hy_fwout, hy_filt_bias, hy_w_out, hy_b_out, rg_w_in, rg_b_in, rg_conv_w, rg_conv_b, rg_wa, rg_ba, rg_wi, rg_bi, rg_lam, rg_w_out, rg_b_out):
    cc = jnp.concatenate([c, jnp.broadcast_to(c_ctx[None], (B, D))], axis=0)
    mods = _mods(cc, ada_w, ada_b).reshape(DEPTH, 2, B, N_MOD * D)
    w1 = ffn_w1.astype(BF16)
    w3 = ffn_w3.astype(BF16)
    w2 = ffn_w2.astype(BF16)

    g = norm_g[0]
    xs = _ffn_first(x, ctx, _grid_pos(), mods[0], g, w1, w3, w2)
    x0, p = _hy_in(xs, mods[0], g, hy_w_in[0].astype(BF16), hy_b_in[0][None],
                   hy_conv_w[0], hy_conv_b[0][None])
    fparams = (hy_fw0[0], hy_fb0[0], hy_fw1[0], hy_fb1[0], hy_fw2[0], hy_fb2[0],
               hy_freq[0], hy_fwout[0])
    bias = hy_filt_bias[0][None]
    ys = []
    for n, row_block in ((L, 0), (CTX, L // CTX)):
        hf, hb = _filters(n, *fparams)
        ctab, stab = (tab.astype(BF16) for tab in _dft_tables(n))
        kr, ki, kn = _spectrum(hf, hb, ctab, stab)
        ys.append(_long_conv(p, x0, kr, ki, kn, bias, ctab, stab, n, row_block))
    xs = _hy_out_ffn(ys[0], ys[1], xs, mods[0], g, hy_w_out[0].astype(BF16), hy_b_out[0][None],
                     w1, w3, w2)

    g = norm_g[1]
    xs = _ffn_plain(xs, mods[1], g, w1, w3, w2, layer=1)
    wa = rg_wa[0].astype(BF16)
    wi = rg_wi[0].astype(BF16)
    xc, gate, hs = _rg_in(xs, mods[1], g, rg_w_in[0].astype(BF16), rg_b_in[0][None],
                          rg_conv_w[0], rg_conv_b[0][None], wa[0], rg_ba[0, 0][None],
                          wi[0], rg_bi[0, 0][None], rg_lam[0, 0][None])
    bwd = (wa[1], rg_ba[0, 1][None], wi[1], rg_bi[0, 1][None], rg_lam[0, 1][None])
    h_ctx = _rg_ctx_bwd(xc, *bwd)
    return _rg_out_ffn(xc, gate, hs, xs, h_ctx, mods[1], g, *bwd,
                       rg_w_out[0].astype(BF16), rg_b_out[0][None], w1, w3, w2)
```

```python
import functools
import math

import jax
import jax.numpy as jnp
import numpy as np
from jax import lax
from jax.experimental import pallas as pl
from jax.experimental.pallas import tpu as pltpu

F32 = jnp.float32
BF16 = jnp.bfloat16

D = 1024
B = 8
LANES = 128
MXU_N = 256
L = 2048
CTX = 256
DEPTH = 2
GRID_W = 64
D_FF = 2816
N_MOD = 9
MACARON = 0.5
NORM_EPS = 1e-6
POS_BASE = 10000.0
HY_EMB = 33
HY_BANDS = 16
HY_HID = 64
HY_FAST_DECAY = 0.3
HY_SLOW_DECAY = 1.5
HY_DECAY_TARGET = 1e-2
RG_HEADS = 4
RG_BLOCK = D // RG_HEADS
RG_C = 8.0

ROWS_LAT = L * B
ROWS_CTX = CTX * B
ROWS = ROWS_LAT + ROWS_CTX
TM = 512
TS = TM // B
NT_LAT = ROWS_LAT // TM
NT_CTX = ROWS_CTX // TM
NT = NT_LAT + NT_CTX
F_CHUNKS = ((0, 1536), (1536, D_FF))
F_QUARTERS = ((0, 768), (768, 1536), (1536, 2304), (2304, D_FF))
TC = 256
VMEM_LIMIT = 56 * 1024 * 1024


def _cparams(sem):
    return pltpu.CompilerParams(dimension_semantics=sem, vmem_limit_bytes=VMEM_LIMIT)


def _resident(shape):
    nd = len(shape)
    return pl.BlockSpec(shape, lambda *_: (0,) * nd, pipeline_mode=pl.Buffered(1))


def _split(a):
    hi = a.astype(BF16)
    lo = (a - hi.astype(F32)).astype(BF16)
    return hi, lo


def _dot(a, b):
    return jnp.dot(a, b, preferred_element_type=F32)


def _dot3(a, b):
    ah, al = _split(a)
    bh, bl = _split(b)
    return _dot(ah, bh) + _dot(ah, bl) + _dot(al, bh)


def _rmsnorm(x, g):
    ms = jnp.mean(x * x, axis=-1, keepdims=True)
    return x * lax.rsqrt(ms + NORM_EPS) * g


def _mod(mod_ref, k):
    return mod_ref[:, k * D:(k + 1) * D]


def _per_batch(x, fn):
    rows = x.shape[0]
    return fn(x.reshape(rows // B, B, x.shape[1])).reshape(rows, x.shape[1])


def _modulate(xn, shift8, scale8):
    return _per_batch(xn, lambda v: v * (1.0 + scale8)[None] + shift8[None])


def _gated(z, gate8):
    return _per_batch(z, lambda v: v * gate8[None])


def _sigmoid(x):
    return jax.nn.sigmoid(x)


def _ffn(x, mod_ref, k0, g_pre, g_post, w1_ref, w3_ref, w2_ref, chunks=F_CHUNKS, side_work=()):
    h = _modulate(_rmsnorm(x, g_pre), _mod(mod_ref, k0), _mod(mod_ref, k0 + 1)).astype(BF16)
    y = None
    for idx, (c0, c1) in enumerate(chunks):
        a = _dot(h, w1_ref[:, c0:c1])
        b = _dot(h, w3_ref[:, c0:c1])
        if idx < len(side_work):
            side_work[idx]()
        act = (a * _sigmoid(a) * b).astype(BF16)
        part = _dot(act, w2_ref[c0:c1, :])
        y = part if y is None else y + part
    return x + MACARON * _gated(_rmsnorm(y, g_post), _mod(mod_ref, k0 + 2))


def _mods_kernel(cc_ref, w_ref, b_ref, o_ref):
    a = cc_ref[...]
    a = a * _sigmoid(a)
    o_ref[...] = _dot3(a, w_ref[...]) + b_ref[...]


def _mods(cc, ada_w, ada_b):
    tn = 1024
    n = N_MOD * D
    return pl.pallas_call(
        _mods_kernel,
        grid=(DEPTH, n // tn),
        in_specs=[
            pl.BlockSpec((2 * B, D), lambda i, j: (0, 0)),
            pl.BlockSpec((None, D, tn), lambda i, j: (i, 0, j)),
            pl.BlockSpec((None, 1, tn), lambda i, j: (i, 0, j)),
        ],
        out_specs=pl.BlockSpec((None, 2 * B, tn), lambda i, j: (i, 0, j)),
        out_shape=jax.ShapeDtypeStruct((DEPTH, 2 * B, n), F32),
        compiler_params=_cparams(("arbitrary", "arbitrary")),
        name="ada_mods",
    )(cc, ada_w, ada_b.reshape(DEPTH, 1, n))


def _mod_spec():
    return pl.BlockSpec((None, B, N_MOD * D), lambda t: (jnp.where(t < NT_LAT, 0, 1), 0, 0))


def _row_spec(width=D):
    return pl.BlockSpec((TM, width), lambda t: (t, 0))


def _wide_spec():
    return pl.BlockSpec((TS, B * D), lambda t: (t, 0))


def _ffn_weight_specs(layer, half):
    pick = lambda *_: (layer, half, 0, 0)
    up = pl.BlockSpec((None, None, D, D_FF), pick, pipeline_mode=pl.Buffered(1))
    down = pl.BlockSpec((None, None, D_FF, D), pick, pipeline_mode=pl.Buffered(1))
    return [up, up, down]


def _rows_of_batch(b):
    return pl.ds(b, TS, stride=B)


_ROWS_SCRATCH = pltpu.VMEM((D // LANES, TM, LANES), F32)


def _put_batch(rows_scr, b, val):
    for j in range(D // LANES):
        rows_scr[j, _rows_of_batch(b), :] = val[:, j * LANES:(j + 1) * LANES]


def _get_batch(rows_scr, b):
    return jnp.concatenate([rows_scr[j, _rows_of_batch(b), :] for j in range(D // LANES)], axis=1)


def _put_rows(rows_scr, val):
    for j in range(D // LANES):
        rows_scr[j] = val[:, j * LANES:(j + 1) * LANES]


def _get_rows(rows_scr):
    return jnp.concatenate([rows_scr[j] for j in range(D // LANES)], axis=1)


def _ffn_first_kernel(x_ref, ctx_ref, pos_ref, mod_ref, g_ref, w1, w3, w2, o_ref, rows_scr):
    is_latent = pl.program_id(0) < NT_LAT
    for b in range(B):
        _put_batch(rows_scr, b, jnp.where(is_latent, x_ref[b] + pos_ref[...], ctx_ref[b]))
    o_ref[...] = _ffn(_get_rows(rows_scr), mod_ref, 0, g_ref[0:1], g_ref[1:2], w1, w3, w2)


def _ffn_kernel(x_ref, mod_ref, g_ref, w1, w3, w2, o_ref):
    for r in range(x_ref.shape[0] // TM):
        rows = slice(r * TM, (r + 1) * TM)
        o_ref[rows, :] = _ffn(x_ref[rows, :], mod_ref, 0, g_ref[0:1], g_ref[1:2], w1, w3, w2)


def _ffn_first(x, ctx, pos, mods, g, w1, w3, w2):
    lat = lambda t: jnp.minimum(t, NT_LAT - 1)
    return pl.pallas_call(
        _ffn_first_kernel,
        grid=(NT,),
        in_specs=[pl.BlockSpec((B, TS, D), lambda t: (0, lat(t), 0)),
                  pl.BlockSpec((B, TS, D), lambda t: (0, jnp.maximum(t - NT_LAT, 0), 0)),
                  pl.BlockSpec((TS, D), lambda t: (lat(t), 0)),
                  _mod_spec(), _resident((6, D))] + _ffn_weight_specs(0, 0),
        out_specs=_row_spec(),
        out_shape=jax.ShapeDtypeStruct((ROWS, D), F32),
        scratch_shapes=[_ROWS_SCRATCH],
        compiler_params=_cparams(("arbitrary",)),
        name="ffn_first",
    )(x, ctx, pos, mods, g, w1, w3, w2)


def _ffn_plain(x, mods, g, w1, w3, w2, layer):
    return pl.pallas_call(
        _ffn_kernel,
        grid=(NT // 2,),
        in_specs=[pl.BlockSpec((2 * TM, D), lambda t: (t, 0)),
                  pl.BlockSpec((None, B, N_MOD * D),
                               lambda t: (jnp.where(t < NT_LAT // 2, 0, 1), 0, 0)),
                  _resident((6, D))] + _ffn_weight_specs(layer, 0),
        out_specs=pl.BlockSpec((2 * TM, D), lambda t: (t, 0)),
        out_shape=jax.ShapeDtypeStruct((ROWS, D), F32),
        compiler_params=_cparams(("arbitrary",)),
        name="ffn_pre",
    )(x, mods, g, w1, w3, w2)


def _mix_ffn_core(y, x, mod_ref, g_ref, wo_ref, bo_ref, w1, w3, w2, **ffn_kwargs):
    z = _dot(y.astype(BF16), wo_ref[...]) + bo_ref[...]
    x = x + _gated(_rmsnorm(z, g_ref[3:4]), _mod(mod_ref, 5))
    return _ffn(x, mod_ref, 6, g_ref[4:5], g_ref[5:6], w1, w3, w2, **ffn_kwargs)


def _hy_out_ffn_kernel(y_ref, x_ref, mod_ref, g_ref, wo_ref, bo_ref, w1, w3, w2, o_ref, rows_scr):
    for b in range(B):
        _put_batch(rows_scr, b, y_ref[:, b * D:(b + 1) * D])
    o_ref[...] = _mix_ffn_core(_get_rows(rows_scr), x_ref[...], mod_ref, g_ref, wo_ref, bo_ref,
                               w1, w3, w2)


def _hy_out_ffn(y, x, mods, g, w_out, b_out, w1, w3, w2):
    return pl.pallas_call(
        _hy_out_ffn_kernel,
        grid=(NT,),
        in_specs=[_wide_spec(), _row_spec(), _mod_spec(), _resident((6, D)), _resident((D, D)),
                  _resident((1, D))] + _ffn_weight_specs(0, 1),
        out_specs=_row_spec(),
        out_shape=jax.ShapeDtypeStruct((ROWS, D), F32),
        scratch_shapes=[_ROWS_SCRATCH],
        compiler_params=_cparams(("arbitrary",)),
        name="hyena_out_ffn",
    )(y, x, mods, g, w_out, b_out, w1, w3, w2)


def _seq_first(t):
    return jnp.logical_or(t == 0, t == NT_LAT)


def _seq_last(t):
    return jnp.logical_or(t == NT_LAT - 1, t == NT - 1)


def _halo_specs(order, lo_rows, hi_rows):
    nlo = ROWS // lo_rows
    nhi = ROWS // hi_rows
    return [
        pl.BlockSpec((lo_rows, D), lambda s: (jnp.maximum(order(s) * (TM // lo_rows) - 1, 0), 0)),
        pl.BlockSpec((TM, D), lambda s: (order(s), 0)),
        pl.BlockSpec((hi_rows, D),
                     lambda s: (jnp.minimum((order(s) + 1) * (TM // hi_rows), nhi - 1), 0)),
    ]


def _hy_in_kernel(xp_ref, x_ref, xn_ref, mod_ref, g_ref, w_ref, b_ref, cw_ref, cb_ref,
                  x0_ref, p_ref, x0_scr, p_scr):
    t = pl.program_id(0)
    xa = jnp.concatenate([xp_ref[...], x_ref[...], xn_ref[...]], axis=0)
    h = _modulate(_rmsnorm(xa, g_ref[2:3]), _mod(mod_ref, 3), _mod(mod_ref, 4)).astype(BF16)
    keep_lo = jnp.where(_seq_first(t), 0.0, 1.0).astype(F32)
    keep_hi = jnp.where(_seq_last(t), 0.0, 1.0).astype(F32)

    group = 3 * MXU_N
    slabs = MXU_N // LANES
    for c in range(D // MXU_N):
        cols = slice(c * group, (c + 1) * group)
        u = _dot(h, w_ref[:, cols]) + b_ref[:, cols]
        lo = jnp.concatenate([u[0:B] * keep_lo, u[B:TM]], axis=0)
        hi = jnp.concatenate([u[2 * B:TM + B], u[TM + B:TM + 2 * B] * keep_hi], axis=0)
        cv = (cb_ref[:, cols] + cw_ref[0:1, cols] * lo + cw_ref[1:2, cols] * u[B:TM + B]
              + cw_ref[2:3, cols] * hi)
        vals = (cv[:, 0:MXU_N], cv[:, MXU_N:2 * MXU_N] * cv[:, 2 * MXU_N:3 * MXU_N])
        c0 = c * MXU_N
        for out_ref, scr, val in zip((x0_ref, p_ref), (x0_scr, p_scr), vals):
            for j in range(slabs):
                scr[c * slabs + j] = val[:, j * LANES:(j + 1) * LANES]
            for b in range(B):
                out_ref[:, b * D + c0:b * D + c0 + MXU_N] = jnp.concatenate(
                    [scr[c * slabs + j, _rows_of_batch(b), :] for j in range(slabs)], axis=1)


def _group_channels(a, parts):
    lead = a.shape[:-1]
    a = a.reshape(*lead, parts, D // MXU_N, MXU_N)
    return jnp.swapaxes(a, -3, -2).reshape(*lead, parts * D)


def _hy_in(x, mods, g, w_in, b_in, conv_w, conv_b):
    return pl.pallas_call(
        _hy_in_kernel,
        grid=(NT,),
        in_specs=_halo_specs(lambda s: s, B, B) + [
            _mod_spec(), _resident((6, D)), _resident((D, 3 * D)), _resident((1, 3 * D)),
            _resident((3, 3 * D)), _resident((1, 3 * D))],
        out_specs=[_wide_spec(), _wide_spec()],
        out_shape=[jax.ShapeDtypeStruct((ROWS // B, B * D), F32)] * 2,
        scratch_shapes=[_ROWS_SCRATCH, _ROWS_SCRATCH],
        compiler_params=_cparams(("arbitrary",)),
        name="hyena_in",
    )(x, x, x, mods, g, w_in, b_in, conv_w, conv_b)


def _filter_kernel(z_ref, fw0, fb0, fw1, fb1, fw2, fb2, freq, fwout, deltas, hf_ref, hb_ref):
    z = z_ref[...]
    h = jnp.sin(freq[0:1] * (_dot3(z, fw0[...]) + fb0[...]))
    h = jnp.sin(freq[1:2] * (_dot3(h, fw1[...]) + fb1[...]))
    h = jnp.sin(freq[2:3] * (_dot3(h, fw2[...]) + fb2[...]))
    filt = _dot3(h, fwout[...])
    decay = jnp.exp(-z[:, 0:1] * deltas[...])
    hf_ref[...] = filt[:, 0:D] * decay
    hb_ref[...] = filt[:, D:2 * D] * decay


def _filters(n, fw0, fb0, fw1, fb1, fw2, fb2, freq, fwout):
    t = np.linspace(0.0, 1.0, n)[:, None]
    bands = np.linspace(1e-4, HY_BANDS - 1, HY_BANDS)[None]
    phase = bands * (2.0 * math.pi * np.arange(n)[:, None] / n)
    zp = np.zeros((n, LANES), np.float32)
    zp[:, :HY_EMB] = np.concatenate([t, np.cos(phase), -np.sin(phase)], axis=-1)
    fw0p = jnp.zeros((LANES, HY_HID), F32).at[:HY_EMB].set(fw0)
    max_decay = math.log(HY_DECAY_TARGET) / HY_FAST_DECAY
    min_decay = math.log(HY_DECAY_TARGET) / HY_SLOW_DECAY
    deltas = np.abs(np.linspace(min_decay, max_decay, D))[None].astype(np.float32)
    tl = 256
    row = lambda i: (i, 0)
    return pl.pallas_call(
        _filter_kernel,
        grid=(n // tl,),
        in_specs=[pl.BlockSpec((tl, LANES), row), _resident((LANES, HY_HID)), _resident((1, HY_HID)),
                  _resident((HY_HID, HY_HID)), _resident((1, HY_HID)),
                  _resident((HY_HID, HY_HID)), _resident((1, HY_HID)),
                  _resident((3, HY_HID)), _resident((HY_HID, 2 * D)), _resident((1, D))],
        out_specs=[pl.BlockSpec((tl, D), row)] * 2,
        out_shape=[jax.ShapeDtypeStruct((n, D), F32)] * 2,
        compiler_params=_cparams(("arbitrary",)),
        name="hyena_filter",
    )(zp, fw0p, fb0[None], fw1, fb1[None], fw2, fb2[None], freq, fwout, deltas)


def _dft_tables(n):
    q = 1 << (int(math.log2(n)) // 2 + 1)
    m = np.arange(n, dtype=np.int64)[None, :]

    def thin(k):
        ang = ((k[:, None] * m) % (2 * n)) * (math.pi / n)
        return jnp.asarray(np.cos(ang), F32), jnp.asarray(np.sin(ang), F32)

    ca, sa = (v[:, None, :] for v in thin(q * np.arange(n // q, dtype=np.int64)))
    cb, sb = (v[None, :, :] for v in thin(np.arange(q, dtype=np.int64)))
    return (ca * cb - sa * sb).reshape(n, n), (sa * cb + ca * sb).reshape(n, n)


def _alt_sign(rows, cols):
    r = lax.broadcasted_iota(jnp.int32, (rows, cols), 0)
    return (1 - 2 * (r & 1)).astype(F32)


def _spectrum_kernel(hf_ref, hb_ref, c_ref, s_ref, kr_ref, ki_ref, kn_ref, *, n, tk):
    hf = hf_ref[...]
    row = lax.broadcasted_iota(jnp.int32, hf.shape, 0)
    hb = jnp.where(row == 0, 0.0, hb_ref[...])
    even = hf + hb
    eb = even.astype(BF16)
    ob = (hb - hf).astype(BF16)
    first = jnp.where(lax.broadcasted_iota(jnp.int32, (tk, hf.shape[1]), 0) == 0, 0.5 / n, 1.0 / n)
    for k in range(n // tk):
        rows = slice(k * tk, (k + 1) * tk)
        scale = first if k == 0 else 1.0 / n
        kr_ref[rows, :] = _dot(c_ref[rows, :], eb) * scale
        ki_ref[rows, :] = _dot(s_ref[rows, :], ob) * scale
    nyq = jnp.sum(even * _alt_sign(*hf.shape), axis=0, keepdims=True) * (0.5 / n)
    kn_ref[...] = jnp.broadcast_to(nyq, kn_ref.shape)


def _spectrum(hf, hb, ctab, stab):
    n = hf.shape[0]
    col = lambda j: (0, j)
    return pl.pallas_call(
        functools.partial(_spectrum_kernel, n=n, tk=min(n, 512)),
        grid=(D // TC,),
        in_specs=[pl.BlockSpec((n, TC), col)] * 2 + [_resident((n, n))] * 2,
        out_specs=[pl.BlockSpec((n, TC), col)] * 2 + [pl.BlockSpec((B, TC), col)],
        out_shape=[jax.ShapeDtypeStruct((n, D), F32)] * 2 + [jax.ShapeDtypeStruct((B, D), F32)],
        compiler_params=_cparams(("arbitrary",)),
        name="hyena_spectrum",
    )(hf, hb, ctab, stab)


def _long_conv_rows(r0, n, p_ref, x0_ref, spec, bias_ref, o_ref, pb_scr, yr_scr, ys_scr):
    kr_ref, ki_ref, kn_ref, c_ref, s_ref = spec
    tk = min(n, 512)
    tc = p_ref.shape[1]
    seq = slice(r0, r0 + n)
    chunks = [slice(k * tk, (k + 1) * tk) for k in range(n // tk)]
    pb_scr[seq, :] = p_ref[seq, :].astype(BF16)
    for rows in chunks:
        dst = slice(r0 + rows.start, r0 + rows.stop)
        xr = _dot(c_ref[rows, :], pb_scr[seq, :])
        xs = _dot(s_ref[rows, :], pb_scr[seq, :])
        kr = kr_ref[rows, :]
        ki = ki_ref[rows, :]
        yr_scr[dst, :] = (xr * kr + xs * ki).astype(BF16)
        ys_scr[dst, :] = (xs * kr - xr * ki).astype(BF16)
    nyq = jnp.sum(p_ref[seq, :] * _alt_sign(n, tc), axis=0, keepdims=True) * kn_ref[0:1, :]
    alt_nyq = _alt_sign(tk, tc) * nyq
    for rows in chunks:
        dst = slice(r0 + rows.start, r0 + rows.stop)
        y = _dot(c_ref[rows, :], yr_scr[seq, :]) + _dot(s_ref[rows, :], ys_scr[seq, :]) + alt_nyq
        o_ref[dst, :] = x0_ref[dst, :] * (y + p_ref[dst, :] * bias_ref[...])


def _long_conv_kernel(p_ref, x0_ref, bias_ref, *refs):
    lat, ctx, (o_ref, pb_scr, yr_scr, ys_scr) = refs[0:5], refs[5:10], refs[10:]
    _long_conv_rows(0, L, p_ref, x0_ref, lat, bias_ref, o_ref, pb_scr, yr_scr, ys_scr)
    _long_conv_rows(L, CTX, p_ref, x0_ref, ctx, bias_ref, o_ref, pb_scr, yr_scr, ys_scr)


def _long_conv(p2, x02, bias, spec_lat, spec_ctx):
    nc = D // TC
    strip = pl.BlockSpec((ROWS // B, TC), lambda j: (0, j))
    ch = lambda j: (0, j % nc)

    def spec_specs(n):
        return [pl.BlockSpec((n, TC), ch), pl.BlockSpec((n, TC), ch), pl.BlockSpec((B, TC), ch),
                _resident((n, n)), _resident((n, n))]

    return pl.pallas_call(
        _long_conv_kernel,
        grid=(B * D // TC,),
        scratch_shapes=[pltpu.VMEM((ROWS // B, TC), BF16)] * 3,
        in_specs=[strip, strip, pl.BlockSpec((1, TC), ch)] + spec_specs(L) + spec_specs(CTX),
        out_specs=strip,
        out_shape=jax.ShapeDtypeStruct((ROWS // B, B * D), F32),
        compiler_params=_cparams(("arbitrary",)),
        name="hyena_long_conv",
    )(p2, x02, bias, *spec_lat, *spec_ctx)


def _gelu_tanh(x):
    return x * (0.5 * (1.0 + jnp.tanh(math.sqrt(2.0 / math.pi) * (x + 0.044715 * (x * x * x)))))


def _rg_coeffs(xc, hd, wai_ref, bai_ref, lam_ref, a_scr, b_scr):
    sl = slice(hd * RG_BLOCK, (hd + 1) * RG_BLOCK)
    lam = lam_ref[:, sl]
    softplus_neg = jnp.maximum(-lam, 0.0) + jnp.log1p(jnp.exp(-jnp.abs(lam)))
    pre = _dot(xc.astype(BF16), wai_ref[hd]) + bai_ref[:, 2 * hd * RG_BLOCK:2 * (hd + 1) * RG_BLOCK]
    gates = _sigmoid(pre)
    a = jnp.exp(-RG_C * gates[:, 0:RG_BLOCK] * softplus_neg)
    a_scr[:, sl] = a
    b_scr[:, sl] = jnp.sqrt((1.0 - a) * (1.0 + a)) * gates[:, RG_BLOCK:2 * RG_BLOCK] * xc


def _scan_tile(a_scr, b_scr, h_scr, emit, reverse):
    steps = TM // B

    def body(k, h):
        t = steps - 1 - k if reverse else k
        r0 = pl.multiple_of(t * B, B)
        h = a_scr[pl.ds(r0, B), :] * h + b_scr[pl.ds(r0, B), :]
        emit(r0, h)
        return h

    h_scr[...] = lax.fori_loop(0, steps, body, h_scr[...], unroll=8)


def _rg_fwd_order(s):
    return jnp.where(s < NT_CTX, NT_LAT + s, s - NT_CTX)


def _rg_in_kernel(xp_ref, x_ref, xn_ref, mod_ref, g_ref, w_ref, b_ref, cw_ref, cb_ref,
                  wai_ref, bai_ref, lam_ref, xc_ref, gate_ref, hs_ref, a_scr, b_scr, h_scr):
    s = pl.program_id(0)
    t = _rg_fwd_order(s)
    xa = jnp.concatenate([xp_ref[...], x_ref[...], xn_ref[...]], axis=0)
    h = _modulate(_rmsnorm(xa, g_ref[2:3]), _mod(mod_ref, 3), _mod(mod_ref, 4)).astype(BF16)
    keep_lo = jnp.where(_seq_first(t), 0.0, 1.0).astype(F32)
    keep_hi = jnp.where(_seq_last(t), 0.0, 1.0).astype(F32)
    for hd in range(RG_HEADS):
        sl = slice(hd * RG_BLOCK, (hd + 1) * RG_BLOCK)
        both = slice(2 * hd * RG_BLOCK, 2 * (hd + 1) * RG_BLOCK)
        ug = _dot(h, w_ref[:, both]) + b_ref[:, both]
        gate_ref[:, sl] = ug[B:TM + B, 0:RG_BLOCK].astype(BF16)
        u = ug[:, RG_BLOCK:2 * RG_BLOCK]
        taps = (jnp.concatenate([u[0:B] * keep_lo, u[B:TM]], axis=0),
                u[B:TM + B],
                jnp.concatenate([u[2 * B:TM + B], u[TM + B:TM + 2 * B] * keep_hi], axis=0),
                jnp.concatenate([u[3 * B:TM + B], u[TM + B:TM + 3 * B] * keep_hi], axis=0))
        xc = cb_ref[:, sl]
        for k, tap in enumerate(taps):
            xc = xc + cw_ref[k:k + 1, sl] * tap
        xc_ref[:, sl] = xc
        _rg_coeffs(xc, hd, wai_ref, bai_ref, lam_ref, a_scr, b_scr)

    @pl.when(s == 0)
    def _():
        h_scr[...] = jnp.zeros((B, D), F32)

    def emit(r0, hv):
        hs_ref[pl.ds(r0, B), :] = hv

    _scan_tile(a_scr, b_scr, h_scr, emit, reverse=False)


def _rg_gate_specs():
    return [_resident((RG_HEADS, RG_BLOCK, 2 * RG_BLOCK)), _resident((1, 2 * D)), _resident((1, D))]


def _rg_in(x, mods, g, w_in, b_in, conv_w, conv_b, wai, bai, lam):
    order = _rg_fwd_order
    mod_spec = pl.BlockSpec((None, B, N_MOD * D),
                            lambda s: (jnp.where(order(s) < NT_LAT, 0, 1), 0, 0))
    out_spec = pl.BlockSpec((TM, D), lambda s: (order(s), 0))
    return pl.pallas_call(
        _rg_in_kernel,
        grid=(NT,),
        in_specs=_halo_specs(order, B, 2 * B) + [
            mod_spec, _resident((6, D)), _resident((D, 2 * D)), _resident((1, 2 * D)),
            _resident((4, D)), _resident((1, D))] + _rg_gate_specs(),
        out_specs=[out_spec] * 3,
        out_shape=[jax.ShapeDtypeStruct((ROWS, D), F32), jax.ShapeDtypeStruct((ROWS, D), BF16),
                   jax.ShapeDtypeStruct((ROWS, D), F32)],
        scratch_shapes=[pltpu.VMEM((TM, D), F32), pltpu.VMEM((TM, D), F32),
                        pltpu.VMEM((B, D), F32)],
        compiler_params=_cparams(("arbitrary",)),
        name="rglru_in_fwd_scan",
    )(x, x, x, mods, g, w_in, b_in, conv_w, conv_b, wai, bai, lam)


def _rg_tile_coeffs(xc_ref, wai_ref, bai_ref, lam_ref, a_scr, b_scr):
    for hd in range(RG_HEADS):
        xc = xc_ref[:, hd * RG_BLOCK:(hd + 1) * RG_BLOCK]
        _rg_coeffs(xc, hd, wai_ref, bai_ref, lam_ref, a_scr, b_scr)


def _rg_ctx_bwd_kernel(xc_ref, wai_ref, bai_ref, lam_ref, h_ref, a_scr, b_scr, h_scr):
    _rg_tile_coeffs(xc_ref, wai_ref, bai_ref, lam_ref, a_scr, b_scr)

    @pl.when(pl.program_id(0) == 0)
    def _():
        h_scr[...] = jnp.zeros((B, D), F32)

    _scan_tile(a_scr, b_scr, h_scr, lambda r0, hv: None, reverse=True)
    h_ref[...] = h_scr[...]


def _rg_ctx_bwd(xc, wai, bai, lam):
    return pl.pallas_call(
        _rg_ctx_bwd_kernel,
        grid=(NT_CTX,),
        in_specs=[pl.BlockSpec((TM, D), lambda s: (NT - 1 - s, 0))] + _rg_gate_specs(),
        out_specs=pl.BlockSpec((B, D), lambda s: (0, 0)),
        out_shape=jax.ShapeDtypeStruct((B, D), F32),
        scratch_shapes=[pltpu.VMEM((TM, D), F32), pltpu.VMEM((TM, D), F32),
                        pltpu.VMEM((B, D), F32)],
        compiler_params=_cparams(("arbitrary",)),
        name="rglru_ctx_bwd_scan",
    )(xc, wai, bai, lam)


def _rg_out_ffn_kernel(xc_ref, gate_ref, hs_ref, x_ref, h0_ref, mod_ref, g_ref,
                       wai_ref, bai_ref, lam_ref, wo_ref, bo_ref, w1, w3, w2,
                       o_ref, a_scr, b_scr, hb_scr, h_scr, rows_scr):
    s = pl.program_id(0)

    def head_gates(hd):
        xc = xc_ref[:, hd * RG_BLOCK:(hd + 1) * RG_BLOCK]
        _rg_coeffs(xc, hd, wai_ref, bai_ref, lam_ref, a_scr, b_scr)

    @pl.when(s == 0)
    def _():
        h_scr[...] = h0_ref[...]
        for hd in range(RG_HEADS):
            head_gates(hd)

    @pl.when(s > 0)
    def _():
        y = (hs_ref[...] + hb_scr[...]) * _gelu_tanh(gate_ref[...].astype(F32))
        res = _mix_ffn_core(
            y, x_ref[...], mod_ref, g_ref, wo_ref, bo_ref, w1, w3, w2, chunks=F_QUARTERS,
            side_work=[functools.partial(head_gates, hd) for hd in range(RG_HEADS)])
        _put_rows(rows_scr, res)
        for b in range(B):
            o_ref[b] = _get_batch(rows_scr, b)

    @pl.when(s < NT_LAT)
    def _():
        def emit(r0, hv):
            hb_scr[pl.ds(r0, B), :] = hv

        _scan_tile(a_scr, b_scr, h_scr, emit, reverse=True)


def _rg_out_ffn(xc, gate, hs, x, h0, mods, g, wai, bai, lam, w_out, b_out, w1, w3, w2):
    scan_tile = lambda s: (jnp.maximum(NT_LAT - 1 - s, 0), 0)
    out_tile = lambda s: jnp.minimum(NT_LAT - s, NT_LAT - 1)
    prev = pl.BlockSpec((TM, D), lambda s: (out_tile(s), 0))
    return pl.pallas_call(
        _rg_out_ffn_kernel,
        grid=(NT_LAT + 1,),
        in_specs=[pl.BlockSpec((TM, D), scan_tile), prev, prev, prev, _resident((B, D)),
                  pl.BlockSpec((None, B, N_MOD * D), lambda s: (0, 0, 0)), _resident((6, D))]
        + _rg_gate_specs() + [_resident((D, D)), _resident((1, D))] + _ffn_weight_specs(1, 1),
        out_specs=pl.BlockSpec((B, TS, D), lambda s: (0, out_tile(s), 0)),
        out_shape=jax.ShapeDtypeStruct((B, L, D), F32),
        scratch_shapes=[pltpu.VMEM((TM, D), F32), pltpu.VMEM((TM, D), F32),
                        pltpu.VMEM((TM, D), F32), pltpu.VMEM((B, D), F32), _ROWS_SCRATCH],
        compiler_params=_cparams(("arbitrary",)),
        name="rglru_out_ffn",
    )(xc, gate, hs, x, h0, mods, g, wai, bai, lam, w_out, b_out, w1, w3, w2)


def _grid_pos():
    rows = L // GRID_W
    quarter = D // 4
    omega = POS_BASE ** (-np.arange(quarter) / quarter)

    def emb(q):
        ang = q[:, None] * omega[None]
        return jnp.asarray(np.concatenate([np.sin(ang), np.cos(ang)], axis=-1), F32)

    row_code = jnp.repeat(emb(np.arange(rows)), GRID_W, axis=0)
    col_code = jnp.tile(emb(np.arange(GRID_W)), (rows, 1))
    return jnp.concatenate([row_code, col_code], axis=-1)


def kernel(x, c, ctx, c_ctx, ada_w, ada_b, norm_g, ffn_w1, ffn_w3, ffn_w2, hy_w_in, hy_b_in, hy_conv_w, hy_conv_b, hy_fw0, hy_fb0, hy_fw1, hy_fb1, hy_fw2, hy_fb2, hy_freq, hy_fwout, hy_filt_bias, hy_w_out, hy_b_out, rg_w_in, rg_b_in, rg_conv_w, rg_conv_b, rg_wa, rg_ba, rg_wi, rg_bi, rg_lam, rg_w_out, rg_b_out):
    cc = jnp.concatenate([c, jnp.broadcast_to(c_ctx[None], (B, D))], axis=0)
    mods = _mods(cc, ada_w, ada_b).reshape(DEPTH, 2, B, N_MOD * D)
    w1 = ffn_w1.astype(BF16)
    w3 = ffn_w3.astype(BF16)
    w2 = ffn_w2.astype(BF16)

    g = norm_g[0]
    xs = _ffn_first(x, ctx, _grid_pos(), mods[0], g, w1, w3, w2)
    x0, p = _hy_in(xs, mods[0], g, _group_channels(hy_w_in[0], 3).astype(BF16),
                   _group_channels(hy_b_in[0][None], 3), _group_channels(hy_conv_w[0], 3),
                   _group_channels(hy_conv_b[0][None], 3))
    fparams = (hy_fw0[0], hy_fb0[0], hy_fw1[0], hy_fb1[0], hy_fw2[0], hy_fb2[0],
               hy_freq[0], hy_fwout[0])
    specs = []
    for n in (L, CTX):
        hf, hb = _filters(n, *fparams)
        ctab, stab = (tab.astype(BF16) for tab in _dft_tables(n))
        specs.append(list(_spectrum(hf, hb, ctab, stab)) + [ctab, stab])
    y = _long_conv(p, x0, hy_filt_bias[0][None], *specs)
    xs = _hy_out_ffn(y, xs, mods[0], g, hy_w_out[0].astype(BF16), hy_b_out[0][None], w1, w3, w2)

    g = norm_g[1]
    xs = _ffn_plain(xs, mods[1], g, w1, w3, w2, layer=1)
    wai = jnp.concatenate([rg_wa[0], rg_wi[0]], axis=-1).astype(BF16)
    per_head = lambda v: v.reshape(2, RG_HEADS, RG_BLOCK)
    bai = jnp.concatenate([per_head(rg_ba[0]), per_head(rg_bi[0])], axis=-1).reshape(2, 1, 2 * D)
    lam = rg_lam[0][:, None, :]
    xc, gate, hs = _rg_in(xs, mods[1], g, _group_channels(rg_w_in[0], 2).astype(BF16),
                          _group_channels(rg_b_in[0][None], 2), rg_conv_w[0], rg_conv_b[0][None],
                          wai[0], bai[0], lam[0])
    h_ctx = _rg_ctx_bwd(xc, wai[1], bai[1], lam[1])
    return _rg_out_ffn(xc, gate, hs, xs, h_ctx, mods[1], g, wai[1], bai[1], lam[1],
                       rg_w_out[0].astype(BF16), rg_b_out[0][None], w1, w3, w2)
```

```python
import functools
import math

import jax
import jax.numpy as jnp
import numpy as np
from jax import lax
from jax.experimental import pallas as pl
from jax.experimental.pallas import tpu as pltpu

F32 = jnp.float32
BF16 = jnp.bfloat16

D = 1024
B = 8
LANES = 128
MXU_N = 256
L = 2048
CTX = 256
DEPTH = 2
GRID_W = 64
D_FF = 2816
N_MOD = 9
MACARON = 0.5
NORM_EPS = 1e-6
POS_BASE = 10000.0
HY_EMB = 33
HY_BANDS = 16
HY_HID = 64
HY_FAST_DECAY = 0.3
HY_SLOW_DECAY = 1.5
HY_DECAY_TARGET = 1e-2
RG_HEADS = 4
RG_BLOCK = D // RG_HEADS
RG_C = 8.0

ROWS_LAT = L * B
ROWS_CTX = CTX * B
ROWS = ROWS_LAT + ROWS_CTX
TM = 512
TS = TM // B
NT_LAT = ROWS_LAT // TM
NT_CTX = ROWS_CTX // TM
NT = NT_LAT + NT_CTX
F_CHUNKS = ((0, 1536), (1536, D_FF))
F_QUARTERS = ((0, 768), (768, 1536), (1536, 2304), (2304, D_FF))
TC = 256
VMEM_LIMIT = 58 * 1024 * 1024


def _cparams(sem):
    return pltpu.CompilerParams(dimension_semantics=sem, vmem_limit_bytes=VMEM_LIMIT)


def _resident(shape):
    nd = len(shape)
    return pl.BlockSpec(shape, lambda *_: (0,) * nd, pipeline_mode=pl.Buffered(1))


def _split(a):
    hi = a.astype(BF16)
    lo = (a - hi.astype(F32)).astype(BF16)
    return hi, lo


def _dot(a, b):
    return jnp.dot(a, b, preferred_element_type=F32)


def _dot3(a, b):
    ah, al = _split(a)
    bh, bl = _split(b)
    return _dot(ah, bh) + _dot(ah, bl) + _dot(al, bh)


def _rmsnorm(x, g):
    ms = jnp.mean(x * x, axis=-1, keepdims=True)
    return x * lax.rsqrt(ms + NORM_EPS) * g


def _mod(mod_ref, k):
    return mod_ref[:, k * D:(k + 1) * D]


def _per_batch(x, fn):
    rows = x.shape[0]
    return fn(x.reshape(rows // B, B, x.shape[1])).reshape(rows, x.shape[1])


def _modulate(xn, shift8, scale8):
    return _per_batch(xn, lambda v: v * (1.0 + scale8)[None] + shift8[None])


def _gated(z, gate8):
    return _per_batch(z, lambda v: v * gate8[None])


def _sigmoid(x):
    return jax.nn.sigmoid(x)


def _ffn(x, mod_ref, k0, g_pre, g_post, w1_ref, w3_ref, w2_ref, chunks=F_CHUNKS, side_work=()):
    h = _modulate(_rmsnorm(x, g_pre), _mod(mod_ref, k0), _mod(mod_ref, k0 + 1)).astype(BF16)
    y = None
    for idx, (c0, c1) in enumerate(chunks):
        a = _dot(h, w1_ref[:, c0:c1])
        b = _dot(h, w3_ref[:, c0:c1])
        if idx < len(side_work):
            side_work[idx]()
        act = (a * _sigmoid(a) * b).astype(BF16)
        part = _dot(act, w2_ref[c0:c1, :])
        y = part if y is None else y + part
    return x + MACARON * _gated(_rmsnorm(y, g_post), _mod(mod_ref, k0 + 2))


def _mods_kernel(cc_ref, w_ref, b_ref, o_ref):
    a = cc_ref[...]
    a = a * _sigmoid(a)
    o_ref[...] = _dot3(a, w_ref[...]) + b_ref[...]


def _mods(cc, ada_w, ada_b):
    tn = 1024
    n = N_MOD * D
    return pl.pallas_call(
        _mods_kernel,
        grid=(DEPTH, n // tn),
        in_specs=[
            pl.BlockSpec((2 * B, D), lambda i, j: (0, 0)),
            pl.BlockSpec((None, D, tn), lambda i, j: (i, 0, j)),
            pl.BlockSpec((None, 1, tn), lambda i, j: (i, 0, j)),
        ],
        out_specs=pl.BlockSpec((None, 2 * B, tn), lambda i, j: (i, 0, j)),
        out_shape=jax.ShapeDtypeStruct((DEPTH, 2 * B, n), F32),
        compiler_params=_cparams(("arbitrary", "arbitrary")),
        name="ada_mods",
    )(cc, ada_w, ada_b.reshape(DEPTH, 1, n))


def _mod_spec():
    return pl.BlockSpec((None, B, N_MOD * D), lambda t: (jnp.where(t < NT_LAT, 0, 1), 0, 0))


def _row_spec(width=D):
    return pl.BlockSpec((TM, width), lambda t: (t, 0))


def _wide_spec():
    return pl.BlockSpec((TS, B * D), lambda t: (t, 0))


W_ROWS = 128
_WEIGHT_SPECS = [pl.BlockSpec(memory_space=pl.ANY)] * 3
_WEIGHT_SCRATCH = [pltpu.VMEM((D, D_FF), BF16), pltpu.VMEM((D, D_FF), BF16),
                   pltpu.VMEM((D_FF, D), BF16), pltpu.VMEM((2, W_ROWS, D_FF), F32),
                   pltpu.SemaphoreType.DMA((2,))]


def _stage_weights(srcs, dsts, stage, sems):
    chunks = [(src, dst, r0) for src, dst in zip(srcs, dsts)
              for r0 in range(0, dst.shape[0], W_ROWS)]

    def copy(k):
        src, dst, r0 = chunks[k]
        window = stage.at[k % 2, :, pl.ds(0, dst.shape[1])]
        return pltpu.make_async_copy(src.at[pl.ds(r0, W_ROWS), :], window, sems.at[k % 2])

    copy(0).start()
    for k, (_, dst, r0) in enumerate(chunks):
        if k + 1 < len(chunks):
            copy(k + 1).start()
        copy(k).wait()
        dst[pl.ds(r0, W_ROWS), :] = stage[k % 2, :, pl.ds(0, dst.shape[1])].astype(BF16)


def _with_staged_weights(body, layer, half, n_in):
    def kernel(*refs):
        ins, rest = refs[:n_in], refs[n_in:]
        w1, w3, w2, stage, sems = rest[-5:]

        @pl.when(pl.program_id(0) == 0)
        def _():
            _stage_weights([w.at[layer, half] for w in ins[-3:]], (w1, w3, w2), stage, sems)

        body(*ins[:-3], w1, w3, w2, *rest[:-5])

    return kernel


def _rows_of_batch(b):
    return pl.ds(b, TS, stride=B)


_ROWS_SCRATCH = pltpu.VMEM((D // LANES, TM, LANES), F32)


def _put_batch(rows_scr, b, val):
    for j in range(D // LANES):
        rows_scr[j, _rows_of_batch(b), :] = val[:, j * LANES:(j + 1) * LANES]


def _get_batch(rows_scr, b):
    return jnp.concatenate([rows_scr[j, _rows_of_batch(b), :] for j in range(D // LANES)], axis=1)


def _put_rows(rows_scr, val):
    for j in range(D // LANES):
        rows_scr[j] = val[:, j * LANES:(j + 1) * LANES]


def _get_rows(rows_scr):
    return jnp.concatenate([rows_scr[j] for j in range(D // LANES)], axis=1)


def _ffn_first_kernel(x_ref, ctx_ref, pos_ref, mod_ref, g_ref, w1, w3, w2, o_ref, rows_scr):
    is_latent = pl.program_id(0) < NT_LAT
    for b in range(B):
        _put_batch(rows_scr, b, jnp.where(is_latent, x_ref[b] + pos_ref[...], ctx_ref[b]))
    o_ref[...] = _ffn(_get_rows(rows_scr), mod_ref, 0, g_ref[0:1], g_ref[1:2], w1, w3, w2)


def _ffn_kernel(x_ref, mod_ref, g_ref, w1, w3, w2, o_ref):
    for r in range(x_ref.shape[0] // TM):
        rows = slice(r * TM, (r + 1) * TM)
        o_ref[rows, :] = _ffn(x_ref[rows, :], mod_ref, 0, g_ref[0:1], g_ref[1:2], w1, w3, w2)


def _ffn_first(x, ctx, pos, mods, g, w1, w3, w2):
    lat = lambda t: jnp.minimum(t, NT_LAT - 1)
    return pl.pallas_call(
        _with_staged_weights(_ffn_first_kernel, 0, 0, n_in=8),
        grid=(NT,),
        in_specs=[pl.BlockSpec((B, TS, D), lambda t: (0, lat(t), 0)),
                  pl.BlockSpec((B, TS, D), lambda t: (0, jnp.maximum(t - NT_LAT, 0), 0)),
                  pl.BlockSpec((TS, D), lambda t: (lat(t), 0)),
                  _mod_spec(), _resident((6, D))] + _WEIGHT_SPECS,
        out_specs=_row_spec(),
        out_shape=jax.ShapeDtypeStruct((ROWS, D), F32),
        scratch_shapes=[_ROWS_SCRATCH] + _WEIGHT_SCRATCH,
        compiler_params=_cparams(("arbitrary",)),
        name="ffn_first",
    )(x, ctx, pos, mods, g, w1, w3, w2)


def _ffn_plain(x, mods, g, w1, w3, w2, layer):
    return pl.pallas_call(
        _with_staged_weights(_ffn_kernel, layer, 0, n_in=6),
        grid=(NT // 2,),
        in_specs=[pl.BlockSpec((2 * TM, D), lambda t: (t, 0)),
                  pl.BlockSpec((None, B, N_MOD * D),
                               lambda t: (jnp.where(t < NT_LAT // 2, 0, 1), 0, 0)),
                  _resident((6, D))] + _WEIGHT_SPECS,
        out_specs=pl.BlockSpec((2 * TM, D), lambda t: (t, 0)),
        out_shape=jax.ShapeDtypeStruct((ROWS, D), F32),
        scratch_shapes=_WEIGHT_SCRATCH,
        compiler_params=_cparams(("arbitrary",)),
        name="ffn_pre",
    )(x, mods, g, w1, w3, w2)


def _mix_ffn_core(y, x, mod_ref, g_ref, wo_ref, bo_ref, w1, w3, w2, **ffn_kwargs):
    z = _dot(y.astype(BF16), wo_ref[...]) + bo_ref[...]
    x = x + _gated(_rmsnorm(z, g_ref[3:4]), _mod(mod_ref, 5))
    return _ffn(x, mod_ref, 6, g_ref[4:5], g_ref[5:6], w1, w3, w2, **ffn_kwargs)


def _hy_out_ffn_kernel(y_ref, x_ref, mod_ref, g_ref, wo_ref, bo_ref, w1, w3, w2, o_ref, rows_scr):
    for b in range(B):
        _put_batch(rows_scr, b, y_ref[:, b * D:(b + 1) * D])
    o_ref[...] = _mix_ffn_core(_get_rows(rows_scr), x_ref[...], mod_ref, g_ref, wo_ref, bo_ref,
                               w1, w3, w2)


def _hy_out_ffn(y, x, mods, g, w_out, b_out, w1, w3, w2):
    return pl.pallas_call(
        _with_staged_weights(_hy_out_ffn_kernel, 0, 1, n_in=9),
        grid=(NT,),
        in_specs=[_wide_spec(), _row_spec(), _mod_spec(), _resident((6, D)), _resident((D, D)),
                  _resident((1, D))] + _WEIGHT_SPECS,
        out_specs=_row_spec(),
        out_shape=jax.ShapeDtypeStruct((ROWS, D), F32),
        scratch_shapes=[_ROWS_SCRATCH] + _WEIGHT_SCRATCH,
        compiler_params=_cparams(("arbitrary",)),
        name="hyena_out_ffn",
    )(y, x, mods, g, w_out, b_out, w1, w3, w2)


def _seq_first(t):
    return jnp.logical_or(t == 0, t == NT_LAT)


def _seq_last(t):
    return jnp.logical_or(t == NT_LAT - 1, t == NT - 1)


def _halo_specs(order, lo_rows, hi_rows):
    nlo = ROWS // lo_rows
    nhi = ROWS // hi_rows
    return [
        pl.BlockSpec((lo_rows, D), lambda s: (jnp.maximum(order(s) * (TM // lo_rows) - 1, 0), 0)),
        pl.BlockSpec((TM, D), lambda s: (order(s), 0)),
        pl.BlockSpec((hi_rows, D),
                     lambda s: (jnp.minimum((order(s) + 1) * (TM // hi_rows), nhi - 1), 0)),
    ]


def _hy_in_kernel(xp_ref, x_ref, xn_ref, mod_ref, g_ref, w_ref, b_ref, cw_ref, cb_ref,
                  x0_ref, p_ref, x0_scr, p_scr):
    t = pl.program_id(0)
    xa = jnp.concatenate([xp_ref[...], x_ref[...], xn_ref[...]], axis=0)
    h = _modulate(_rmsnorm(xa, g_ref[2:3]), _mod(mod_ref, 3), _mod(mod_ref, 4)).astype(BF16)
    keep_lo = jnp.where(_seq_first(t), 0.0, 1.0).astype(F32)
    keep_hi = jnp.where(_seq_last(t), 0.0, 1.0).astype(F32)

    def conv_cols(c0):
        cols = slice(c0, c0 + MXU_N)
        u = _dot(h, w_ref[:, cols]) + b_ref[:, cols]
        lo = jnp.concatenate([u[0:B] * keep_lo, u[B:TM]], axis=0)
        hi = jnp.concatenate([u[2 * B:TM + B], u[TM + B:TM + 2 * B] * keep_hi], axis=0)
        return (cb_ref[:, cols] + cw_ref[0:1, cols] * lo + cw_ref[1:2, cols] * u[B:TM + B]
                + cw_ref[2:3, cols] * hi)

    slabs = MXU_N // LANES
    for c in range(D // MXU_N):
        c0 = c * MXU_N
        vals = (conv_cols(c0), conv_cols(D + c0) * conv_cols(2 * D + c0))
        for out_ref, scr, val in zip((x0_ref, p_ref), (x0_scr, p_scr), vals):
            for j in range(slabs):
                scr[c * slabs + j] = val[:, j * LANES:(j + 1) * LANES]
            for b in range(B):
                out_ref[:, b * D + c0:b * D + c0 + MXU_N] = jnp.concatenate(
                    [scr[c * slabs + j, _rows_of_batch(b), :] for j in range(slabs)], axis=1)


def _hy_in(x, mods, g, w_in, b_in, conv_w, conv_b):
    return pl.pallas_call(
        _hy_in_kernel,
        grid=(NT,),
        in_specs=_halo_specs(lambda s: s, B, B) + [
            _mod_spec(), _resident((6, D)), _resident((D, 3 * D)), _resident((1, 3 * D)),
            _resident((3, 3 * D)), _resident((1, 3 * D))],
        out_specs=[_wide_spec(), _wide_spec()],
        out_shape=[jax.ShapeDtypeStruct((ROWS // B, B * D), F32)] * 2,
        scratch_shapes=[_ROWS_SCRATCH, _ROWS_SCRATCH],
        compiler_params=_cparams(("arbitrary",)),
        name="hyena_in",
    )(x, x, x, mods, g, w_in, b_in, conv_w, conv_b)


def _filter_kernel(z_ref, fw0, fb0, fw1, fb1, fw2, fb2, freq, fwout, deltas, hf_ref, hb_ref):
    z = z_ref[...]
    h = jnp.sin(freq[0:1] * (_dot3(z, fw0[...]) + fb0[...]))
    h = jnp.sin(freq[1:2] * (_dot3(h, fw1[...]) + fb1[...]))
    h = jnp.sin(freq[2:3] * (_dot3(h, fw2[...]) + fb2[...]))
    filt = _dot3(h, fwout[...])
    decay = jnp.exp(-z[:, 0:1] * deltas[...])
    hf_ref[...] = filt[:, 0:D] * decay
    hb_ref[...] = filt[:, D:2 * D] * decay


def _filters(n, fw0, fb0, fw1, fb1, fw2, fb2, freq, fwout):
    t = np.linspace(0.0, 1.0, n)[:, None]
    bands = np.linspace(1e-4, HY_BANDS - 1, HY_BANDS)[None]
    phase = bands * (2.0 * math.pi * np.arange(n)[:, None] / n)
    zp = np.zeros((n, LANES), np.float32)
    zp[:, :HY_EMB] = np.concatenate([t, np.cos(phase), -np.sin(phase)], axis=-1)
    fw0p = jnp.zeros((LANES, HY_HID), F32).at[:HY_EMB].set(fw0)
    max_decay = math.log(HY_DECAY_TARGET) / HY_FAST_DECAY
    min_decay = math.log(HY_DECAY_TARGET) / HY_SLOW_DECAY
    deltas = np.abs(np.linspace(min_decay, max_decay, D))[None].astype(np.float32)
    tl = 256
    row = lambda i: (i, 0)
    return pl.pallas_call(
        _filter_kernel,
        grid=(n // tl,),
        in_specs=[pl.BlockSpec((tl, LANES), row), _resident((LANES, HY_HID)), _resident((1, HY_HID)),
                  _resident((HY_HID, HY_HID)), _resident((1, HY_HID)),
                  _resident((HY_HID, HY_HID)), _resident((1, HY_HID)),
                  _resident((3, HY_HID)), _resident((HY_HID, 2 * D)), _resident((1, D))],
        out_specs=[pl.BlockSpec((tl, D), row)] * 2,
        out_shape=[jax.ShapeDtypeStruct((n, D), F32)] * 2,
        compiler_params=_cparams(("arbitrary",)),
        name="hyena_filter",
    )(zp, fw0p, fb0[None], fw1, fb1[None], fw2, fb2[None], freq, fwout, deltas)


def _dft_tables(n):
    q = 1 << (int(math.log2(n)) // 2 + 1)
    m = np.arange(n, dtype=np.int64)[None, :]

    def thin(k):
        ang = ((k[:, None] * m) % (2 * n)) * (math.pi / n)
        return jnp.asarray(np.cos(ang), F32), jnp.asarray(np.sin(ang), F32)

    ca, sa = (v[:, None, :] for v in thin(q * np.arange(n // q, dtype=np.int64)))
    cb, sb = (v[None, :, :] for v in thin(np.arange(q, dtype=np.int64)))
    return (ca * cb - sa * sb).reshape(n, n), (sa * cb + ca * sb).reshape(n, n)


def _alt_sign(rows, cols):
    r = lax.broadcasted_iota(jnp.int32, (rows, cols), 0)
    return (1 - 2 * (r & 1)).astype(F32)


def _spectrum_kernel(hf_ref, hb_ref, c_ref, s_ref, kr_ref, ki_ref, kn_ref, *, n, tk):
    hf = hf_ref[...]
    row = lax.broadcasted_iota(jnp.int32, hf.shape, 0)
    hb = jnp.where(row == 0, 0.0, hb_ref[...])
    even = hf + hb
    eb = even.astype(BF16)
    ob = (hb - hf).astype(BF16)
    first = jnp.where(lax.broadcasted_iota(jnp.int32, (tk, hf.shape[1]), 0) == 0, 0.5 / n, 1.0 / n)
    for k in range(n // tk):
        rows = slice(k * tk, (k + 1) * tk)
        scale = first if k == 0 else 1.0 / n
        kr_ref[rows, :] = _dot(c_ref[rows, :], eb) * scale
        ki_ref[rows, :] = _dot(s_ref[rows, :], ob) * scale
    nyq = jnp.sum(even * _alt_sign(*hf.shape), axis=0, keepdims=True) * (0.5 / n)
    kn_ref[...] = jnp.broadcast_to(nyq, kn_ref.shape)


def _spectrum(hf, hb, ctab, stab):
    n = hf.shape[0]
    col = lambda j: (0, j)
    return pl.pallas_call(
        functools.partial(_spectrum_kernel, n=n, tk=min(n, 512)),
        grid=(D // TC,),
        in_specs=[pl.BlockSpec((n, TC), col)] * 2 + [_resident((n, n))] * 2,
        out_specs=[pl.BlockSpec((n, TC), col)] * 2 + [pl.BlockSpec((B, TC), col)],
        out_shape=[jax.ShapeDtypeStruct((n, D), F32)] * 2 + [jax.ShapeDtypeStruct((B, D), F32)],
        compiler_params=_cparams(("arbitrary",)),
        name="hyena_spectrum",
    )(hf, hb, ctab, stab)


def _long_conv_rows(r0, n, p_ref, x0_ref, spec, bias_ref, o_ref, pb_scr, yr_scr, ys_scr):
    kr_ref, ki_ref, kn_ref, c_ref, s_ref = spec
    tk = min(n, 512)
    tc = p_ref.shape[1]
    seq = slice(r0, r0 + n)
    chunks = [slice(k * tk, (k + 1) * tk) for k in range(n // tk)]
    pb_scr[seq, :] = p_ref[seq, :].astype(BF16)
    for rows in chunks:
        dst = slice(r0 + rows.start, r0 + rows.stop)
        xr = _dot(c_ref[rows, :], pb_scr[seq, :])
        xs = _dot(s_ref[rows, :], pb_scr[seq, :])
        kr = kr_ref[rows, :]
        ki = ki_ref[rows, :]
        yr_scr[dst, :] = (xr * kr + xs * ki).astype(BF16)
        ys_scr[dst, :] = (xs * kr - xr * ki).astype(BF16)
    nyq = jnp.sum(p_ref[seq, :] * _alt_sign(n, tc), axis=0, keepdims=True) * kn_ref[0:1, :]
    alt_nyq = _alt_sign(tk, tc) * nyq
    for rows in chunks:
        dst = slice(r0 + rows.start, r0 + rows.stop)
        y = _dot(c_ref[rows, :], yr_scr[seq, :]) + _dot(s_ref[rows, :], ys_scr[seq, :]) + alt_nyq
        o_ref[dst, :] = x0_ref[dst, :] * (y + p_ref[dst, :] * bias_ref[...])


def _long_conv_kernel(p_ref, x0_ref, bias_ref, *refs):
    lat, ctx, (o_ref, pb_scr, yr_scr, ys_scr) = refs[0:5], refs[5:10], refs[10:]
    _long_conv_rows(0, L, p_ref, x0_ref, lat, bias_ref, o_ref, pb_scr, yr_scr, ys_scr)
    _long_conv_rows(L, CTX, p_ref, x0_ref, ctx, bias_ref, o_ref, pb_scr, yr_scr, ys_scr)


def _long_conv(p2, x02, bias, spec_lat, spec_ctx):
    nc = D // TC
    strip = pl.BlockSpec((ROWS // B, TC), lambda j: (0, j))
    ch = lambda j: (0, j % nc)

    def spec_specs(n):
        return [pl.BlockSpec((n, TC), ch), pl.BlockSpec((n, TC), ch), pl.BlockSpec((B, TC), ch),
                _resident((n, n)), _resident((n, n))]

    return pl.pallas_call(
        _long_conv_kernel,
        grid=(B * D // TC,),
        scratch_shapes=[pltpu.VMEM((ROWS // B, TC), BF16)] * 3,
        in_specs=[strip, strip, pl.BlockSpec((1, TC), ch)] + spec_specs(L) + spec_specs(CTX),
        out_specs=strip,
        out_shape=jax.ShapeDtypeStruct((ROWS // B, B * D), F32),
        compiler_params=_cparams(("arbitrary",)),
        name="hyena_long_conv",
    )(p2, x02, bias, *spec_lat, *spec_ctx)


def _gelu_tanh(x):
    return x * (0.5 * (1.0 + jnp.tanh(math.sqrt(2.0 / math.pi) * (x + 0.044715 * (x * x * x)))))


def _rg_coeffs(xc, hd, wai_ref, bai_ref, lam_ref, a_scr, b_scr):
    sl = slice(hd * RG_BLOCK, (hd + 1) * RG_BLOCK)
    lam = lam_ref[:, sl]
    softplus_neg = jnp.maximum(-lam, 0.0) + jnp.log1p(jnp.exp(-jnp.abs(lam)))
    pre = _dot(xc.astype(BF16), wai_ref[hd]) + bai_ref[:, 2 * hd * RG_BLOCK:2 * (hd + 1) * RG_BLOCK]
    gates = _sigmoid(pre)
    a = jnp.exp(-RG_C * gates[:, 0:RG_BLOCK] * softplus_neg)
    a_scr[:, sl] = a
    b_scr[:, sl] = jnp.sqrt((1.0 - a) * (1.0 + a)) * gates[:, RG_BLOCK:2 * RG_BLOCK] * xc


def _scan_tile(a_scr, b_scr, h_scr, emit, reverse):
    steps = TM // B

    def body(k, h):
        t = steps - 1 - k if reverse else k
        r0 = pl.multiple_of(t * B, B)
        h = a_scr[pl.ds(r0, B), :] * h + b_scr[pl.ds(r0, B), :]
        emit(r0, h)
        return h

    h_scr[...] = lax.fori_loop(0, steps, body, h_scr[...], unroll=8)


def _rg_fwd_order(s):
    return jnp.where(s < NT_CTX, NT_LAT + s, s - NT_CTX)


def _rg_in_kernel(xp_ref, x_ref, xn_ref, mod_ref, g_ref, w_ref, b_ref, cw_ref, cb_ref,
                  wai_ref, bai_ref, lam_ref, xc_ref, gate_ref, hs_ref, a_scr, b_scr, h_scr):
    s = pl.program_id(0)
    t = _rg_fwd_order(s)
    xa = jnp.concatenate([xp_ref[...], x_ref[...], xn_ref[...]], axis=0)
    h = _modulate(_rmsnorm(xa, g_ref[2:3]), _mod(mod_ref, 3), _mod(mod_ref, 4)).astype(BF16)
    keep_lo = jnp.where(_seq_first(t), 0.0, 1.0).astype(F32)
    keep_hi = jnp.where(_seq_last(t), 0.0, 1.0).astype(F32)
    for hd in range(RG_HEADS):
        sl = slice(hd * RG_BLOCK, (hd + 1) * RG_BLOCK)
        rec = slice(D + hd * RG_BLOCK, D + (hd + 1) * RG_BLOCK)
        gate_ref[:, sl] = (_dot(h[B:TM + B], w_ref[:, sl]) + b_ref[:, sl]).astype(BF16)
        u = _dot(h, w_ref[:, rec]) + b_ref[:, rec]
        taps = (jnp.concatenate([u[0:B] * keep_lo, u[B:TM]], axis=0),
                u[B:TM + B],
                jnp.concatenate([u[2 * B:TM + B], u[TM + B:TM + 2 * B] * keep_hi], axis=0),
                jnp.concatenate([u[3 * B:TM + B], u[TM + B:TM + 3 * B] * keep_hi], axis=0))
        xc = cb_ref[:, sl]
        for k, tap in enumerate(taps):
            xc = xc + cw_ref[k:k + 1, sl] * tap
        xc_ref[:, sl] = xc
        _rg_coeffs(xc, hd, wai_ref, bai_ref, lam_ref, a_scr, b_scr)

    @pl.when(s == 0)
    def _():
        h_scr[...] = jnp.zeros((B, D), F32)

    def emit(r0, hv):
        hs_ref[pl.ds(r0, B), :] = hv

    _scan_tile(a_scr, b_scr, h_scr, emit, reverse=False)


def _rg_gate_specs():
    return [_resident((RG_HEADS, RG_BLOCK, 2 * RG_BLOCK)), _resident((1, 2 * D)), _resident((1, D))]


def _rg_in(x, mods, g, w_in, b_in, conv_w, conv_b, wai, bai, lam):
    order = _rg_fwd_order
    mod_spec = pl.BlockSpec((None, B, N_MOD * D),
                            lambda s: (jnp.where(order(s) < NT_LAT, 0, 1), 0, 0))
    out_spec = pl.BlockSpec((TM, D), lambda s: (order(s), 0))
    return pl.pallas_call(
        _rg_in_kernel,
        grid=(NT,),
        in_specs=_halo_specs(order, B, 2 * B) + [
            mod_spec, _resident((6, D)), _resident((D, 2 * D)), _resident((1, 2 * D)),
            _resident((4, D)), _resident((1, D))] + _rg_gate_specs(),
        out_specs=[out_spec] * 3,
        out_shape=[jax.ShapeDtypeStruct((ROWS, D), F32), jax.ShapeDtypeStruct((ROWS, D), BF16),
                   jax.ShapeDtypeStruct((ROWS, D), F32)],
        scratch_shapes=[pltpu.VMEM((TM, D), F32), pltpu.VMEM((TM, D), F32),
                        pltpu.VMEM((B, D), F32)],
        compiler_params=_cparams(("arbitrary",)),
        name="rglru_in_fwd_scan",
    )(x, x, x, mods, g, w_in, b_in, conv_w, conv_b, wai, bai, lam)


def _rg_tile_coeffs(xc_ref, wai_ref, bai_ref, lam_ref, a_scr, b_scr):
    for hd in range(RG_HEADS):
        xc = xc_ref[:, hd * RG_BLOCK:(hd + 1) * RG_BLOCK]
        _rg_coeffs(xc, hd, wai_ref, bai_ref, lam_ref, a_scr, b_scr)


def _rg_ctx_bwd_kernel(xc_ref, wai_ref, bai_ref, lam_ref, h_ref, a_scr, b_scr, h_scr):
    _rg_tile_coeffs(xc_ref, wai_ref, bai_ref, lam_ref, a_scr, b_scr)

    @pl.when(pl.program_id(0) == 0)
    def _():
        h_scr[...] = jnp.zeros((B, D), F32)

    _scan_tile(a_scr, b_scr, h_scr, lambda r0, hv: None, reverse=True)
    h_ref[...] = h_scr[...]


def _rg_ctx_bwd(xc, wai, bai, lam):
    return pl.pallas_call(
        _rg_ctx_bwd_kernel,
        grid=(NT_CTX,),
        in_specs=[pl.BlockSpec((TM, D), lambda s: (NT - 1 - s, 0))] + _rg_gate_specs(),
        out_specs=pl.BlockSpec((B, D), lambda s: (0, 0)),
        out_shape=jax.ShapeDtypeStruct((B, D), F32),
        scratch_shapes=[pltpu.VMEM((TM, D), F32), pltpu.VMEM((TM, D), F32),
                        pltpu.VMEM((B, D), F32)],
        compiler_params=_cparams(("arbitrary",)),
        name="rglru_ctx_bwd_scan",
    )(xc, wai, bai, lam)


def _rg_out_ffn_kernel(xc_ref, gate_ref, hs_ref, x_ref, h0_ref, mod_ref, g_ref,
                       wai_ref, bai_ref, lam_ref, wo_ref, bo_ref, w1, w3, w2,
                       o_ref, a_scr, b_scr, hb_scr, h_scr, rows_scr):
    s = pl.program_id(0)

    def head_gates(hd):
        xc = xc_ref[:, hd * RG_BLOCK:(hd + 1) * RG_BLOCK]
        _rg_coeffs(xc, hd, wai_ref, bai_ref, lam_ref, a_scr, b_scr)

    @pl.when(s == 0)
    def _():
        h_scr[...] = h0_ref[...]
        for hd in range(RG_HEADS):
            head_gates(hd)

    @pl.when(s > 0)
    def _():
        y = (hs_ref[...] + hb_scr[...]) * _gelu_tanh(gate_ref[...].astype(F32))
        res = _mix_ffn_core(
            y, x_ref[...], mod_ref, g_ref, wo_ref, bo_ref, w1, w3, w2, chunks=F_QUARTERS,
            side_work=[functools.partial(head_gates, hd) for hd in range(RG_HEADS)])
        _put_rows(rows_scr, res)
        for b in range(B):
            o_ref[b] = _get_batch(rows_scr, b)

    @pl.when(s < NT_LAT)
    def _():
        def emit(r0, hv):
            hb_scr[pl.ds(r0, B), :] = hv

        _scan_tile(a_scr, b_scr, h_scr, emit, reverse=True)


def _rg_out_ffn(xc, gate, hs, x, h0, mods, g, wai, bai, lam, w_out, b_out, w1, w3, w2):
    scan_tile = lambda s: (jnp.maximum(NT_LAT - 1 - s, 0), 0)
    out_tile = lambda s: jnp.minimum(NT_LAT - s, NT_LAT - 1)
    prev = pl.BlockSpec((TM, D), lambda s: (out_tile(s), 0))
    return pl.pallas_call(
        _with_staged_weights(_rg_out_ffn_kernel, 1, 1, n_in=15),
        grid=(NT_LAT + 1,),
        in_specs=[pl.BlockSpec((TM, D), scan_tile), prev, prev, prev, _resident((B, D)),
                  pl.BlockSpec((None, B, N_MOD * D), lambda s: (0, 0, 0)), _resident((6, D))]
        + _rg_gate_specs() + [_resident((D, D)), _resident((1, D))] + _WEIGHT_SPECS,
        out_specs=pl.BlockSpec((B, TS, D), lambda s: (0, out_tile(s), 0)),
        out_shape=jax.ShapeDtypeStruct((B, L, D), F32),
        scratch_shapes=[pltpu.VMEM((TM, D), F32), pltpu.VMEM((TM, D), F32),
                        pltpu.VMEM((TM, D), F32), pltpu.VMEM((B, D), F32), _ROWS_SCRATCH]
        + _WEIGHT_SCRATCH,
        compiler_params=_cparams(("arbitrary",)),
        name="rglru_out_ffn",
    )(xc, gate, hs, x, h0, mods, g, wai, bai, lam, w_out, b_out, w1, w3, w2)


def _grid_pos():
    rows = L // GRID_W
    quarter = D // 4
    omega = POS_BASE ** (-np.arange(quarter) / quarter)

    def emb(q):
        ang = q[:, None] * omega[None]
        return jnp.asarray(np.concatenate([np.sin(ang), np.cos(ang)], axis=-1), F32)

    row_code = jnp.repeat(emb(np.arange(rows)), GRID_W, axis=0)
    col_code = jnp.tile(emb(np.arange(GRID_W)), (rows, 1))
    return jnp.concatenate([row_code, col_code], axis=-1)


def kernel(x, c, ctx, c_ctx, ada_w, ada_b, norm_g, ffn_w1, ffn_w3, ffn_w2, hy_w_in, hy_b_in, hy_conv_w, hy_conv_b, hy_fw0, hy_fb0, hy_fw1, hy_fb1, hy_fw2, hy_fb2, hy_freq, hy_fwout, hy_filt_bias, hy_w_out, hy_b_out, rg_w_in, rg_b_in, rg_conv_w, rg_conv_b, rg_wa, rg_ba, rg_wi, rg_bi, rg_lam, rg_w_out, rg_b_out):
    cc = jnp.concatenate([c, jnp.broadcast_to(c_ctx[None], (B, D))], axis=0)
    mods = _mods(cc, ada_w, ada_b).reshape(DEPTH, 2, B, N_MOD * D)
    w1, w3, w2 = ffn_w1, ffn_w3, ffn_w2

    g = norm_g[0]
    xs = _ffn_first(x, ctx, _grid_pos(), mods[0], g, w1, w3, w2)
    x0, p = _hy_in(xs, mods[0], g, hy_w_in[0].astype(BF16), hy_b_in[0][None],
                   hy_conv_w[0], hy_conv_b[0][None])
    fparams = (hy_fw0[0], hy_fb0[0], hy_fw1[0], hy_fb1[0], hy_fw2[0], hy_fb2[0],
               hy_freq[0], hy_fwout[0])
    specs = []
    for n in (L, CTX):
        hf, hb = _filters(n, *fparams)
        ctab, stab = (tab.astype(BF16) for tab in _dft_tables(n))
        specs.append(list(_spectrum(hf, hb, ctab, stab)) + [ctab, stab])
    y = _long_conv(p, x0, hy_filt_bias[0][None], *specs)
    xs = _hy_out_ffn(y, xs, mods[0], g, hy_w_out[0].astype(BF16), hy_b_out[0][None], w1, w3, w2)

    g = norm_g[1]
    xs = _ffn_plain(xs, mods[1], g, w1, w3, w2, layer=1)
    wai = jnp.concatenate([rg_wa[0], rg_wi[0]], axis=-1).astype(BF16)
    per_head = lambda v: v.reshape(2, RG_HEADS, RG_BLOCK)
    bai = jnp.concatenate([per_head(rg_ba[0]), per_head(rg_bi[0])], axis=-1).reshape(2, 1, 2 * D)
    lam = rg_lam[0][:, None, :]
    xc, gate, hs = _rg_in(xs, mods[1], g, rg_w_in[0].astype(BF16), rg_b_in[0][None],
                          rg_conv_w[0], rg_conv_b[0][None], wai[0], bai[0], lam[0])
    h_ctx = _rg_ctx_bwd(xc, wai[1], bai[1], lam[1])
    return _rg_out_ffn(xc, gate, hs, xs, h_ctx, mods[1], g, wai[1], bai[1], lam[1],
                       rg_w_out[0].astype(BF16), rg_b_out[0][None], w1, w3, w2)
```

```python
import functools
import math

import jax
import jax.numpy as jnp
import numpy as np
from jax import lax
from jax.experimental import pallas as pl
from jax.experimental.pallas import tpu as pltpu

F32 = jnp.float32
BF16 = jnp.bfloat16

D = 1024
B = 8
LANES = 128
MXU_N = 256
L = 2048
CTX = 256
DEPTH = 2
GRID_W = 64
D_FF = 2816
N_MOD = 9
MACARON = 0.5
NORM_EPS = 1e-6
POS_BASE = 10000.0
HY_EMB = 33
HY_BANDS = 16
HY_HID = 64
HY_FAST_DECAY = 0.3
HY_SLOW_DECAY = 1.5
HY_DECAY_TARGET = 1e-2
RG_HEADS = 4
RG_BLOCK = D // RG_HEADS
RG_C = 8.0

ROWS_LAT = L * B
ROWS_CTX = CTX * B
ROWS = ROWS_LAT + ROWS_CTX
TM = 512
TS = TM // B
NT_LAT = ROWS_LAT // TM
NT_CTX = ROWS_CTX // TM
NT = NT_LAT + NT_CTX
F_CHUNKS = ((0, 1536), (1536, D_FF))
F_QUARTERS = ((0, 768), (768, 1536), (1536, 2304), (2304, D_FF))
TC = 256
VMEM_LIMIT = 58 * 1024 * 1024


def _cparams(sem):
    return pltpu.CompilerParams(dimension_semantics=sem, vmem_limit_bytes=VMEM_LIMIT)


def _resident(shape):
    nd = len(shape)
    return pl.BlockSpec(shape, lambda *_: (0,) * nd, pipeline_mode=pl.Buffered(1))


def _split(a):
    hi = a.astype(BF16)
    lo = (a - hi.astype(F32)).astype(BF16)
    return hi, lo


def _dot(a, b):
    return jnp.dot(a, b, preferred_element_type=F32)


def _dot3(a, b):
    ah, al = _split(a)
    bh, bl = _split(b)
    return _dot(ah, bh) + _dot(ah, bl) + _dot(al, bh)


def _rmsnorm(x, g):
    ms = jnp.mean(x * x, axis=-1, keepdims=True)
    return x * lax.rsqrt(ms + NORM_EPS) * g


def _mod(mod_ref, k):
    return mod_ref[:, k * D:(k + 1) * D]


def _per_batch(x, fn):
    rows = x.shape[0]
    return fn(x.reshape(rows // B, B, x.shape[1])).reshape(rows, x.shape[1])


def _modulate(xn, shift8, scale8):
    return _per_batch(xn, lambda v: v * (1.0 + scale8)[None] + shift8[None])


def _gated(z, gate8):
    return _per_batch(z, lambda v: v * gate8[None])


def _sigmoid(x):
    return jax.nn.sigmoid(x)


def _ffn(x, mod_ref, k0, g_pre, g_post, w1_ref, w3_ref, w2_ref, chunks=F_CHUNKS, side_work=()):
    h = _modulate(_rmsnorm(x, g_pre), _mod(mod_ref, k0), _mod(mod_ref, k0 + 1)).astype(BF16)
    y = None
    for idx, (c0, c1) in enumerate(chunks):
        a = _dot(h, w1_ref[:, c0:c1])
        b = _dot(h, w3_ref[:, c0:c1])
        if idx < len(side_work):
            side_work[idx]()
        act = (a * _sigmoid(a) * b).astype(BF16)
        part = _dot(act, w2_ref[c0:c1, :])
        y = part if y is None else y + part
    return x + MACARON * _gated(_rmsnorm(y, g_post), _mod(mod_ref, k0 + 2))


def _mods_kernel(cc_ref, w_ref, b_ref, o_ref):
    a = cc_ref[...]
    a = a * _sigmoid(a)
    o_ref[...] = _dot3(a, w_ref[...]) + b_ref[...]


def _mods(cc, ada_w, ada_b):
    tn = 1024
    n = N_MOD * D
    return pl.pallas_call(
        _mods_kernel,
        grid=(DEPTH, n // tn),
        in_specs=[
            pl.BlockSpec((2 * B, D), lambda i, j: (0, 0)),
            pl.BlockSpec((None, D, tn), lambda i, j: (i, 0, j)),
            pl.BlockSpec((None, 1, tn), lambda i, j: (i, 0, j)),
        ],
        out_specs=pl.BlockSpec((None, 2 * B, tn), lambda i, j: (i, 0, j)),
        out_shape=jax.ShapeDtypeStruct((DEPTH, 2 * B, n), F32),
        compiler_params=_cparams(("arbitrary", "arbitrary")),
        name="ada_mods",
    )(cc, ada_w, ada_b.reshape(DEPTH, 1, n))


def _mod_spec():
    return pl.BlockSpec((None, B, N_MOD * D), lambda t: (jnp.where(t < NT_LAT, 0, 1), 0, 0))


def _row_spec(width=D):
    return pl.BlockSpec((TM, width), lambda t: (t, 0))


def _wide_spec():
    return pl.BlockSpec((TS, B * D), lambda t: (t, 0))


W_CHUNKS = 8
W_SLOTS = 3
_WEIGHT_SPECS = [pl.BlockSpec(memory_space=pl.ANY)] * 3
_WEIGHT_SCRATCH = [pltpu.VMEM((D, D_FF), BF16), pltpu.VMEM((D, D_FF), BF16),
                   pltpu.VMEM((D_FF, D), BF16)]


def _stage_weights(srcs, dsts):
    chunks = [(src, dst, r0, dst.shape[0] // W_CHUNKS) for src, dst in zip(srcs, dsts)
              for r0 in range(0, dst.shape[0], dst.shape[0] // W_CHUNKS)]

    def run(stage_up, stage_down, sems):
        def copy(k):
            src, dst, r0, rows = chunks[k]
            stage = stage_up if dst.shape[1] == D_FF else stage_down
            return pltpu.make_async_copy(src.at[pl.ds(r0, rows), :], stage.at[k % W_SLOTS],
                                         sems.at[k % W_SLOTS])

        for k in range(W_SLOTS - 1):
            copy(k).start()
        for k, (_, dst, r0, rows) in enumerate(chunks):
            if k + W_SLOTS - 1 < len(chunks):
                copy(k + W_SLOTS - 1).start()
            copy(k).wait()
            stage = stage_up if dst.shape[1] == D_FF else stage_down
            dst[pl.ds(r0, rows), :] = stage[k % W_SLOTS].astype(BF16)

    pl.run_scoped(run, pltpu.VMEM((W_SLOTS, D // W_CHUNKS, D_FF), F32),
                  pltpu.VMEM((W_SLOTS, D_FF // W_CHUNKS, D), F32),
                  pltpu.SemaphoreType.DMA((W_SLOTS,)))


def _with_staged_weights(body, layer, half, n_in):
    def kernel(*refs):
        ins, rest = refs[:n_in], refs[n_in:]
        w1, w3, w2 = rest[-3:]

        @pl.when(pl.program_id(0) == 0)
        def _():
            _stage_weights([w.at[layer, half] for w in ins[-3:]], (w1, w3, w2))

        body(*ins[:-3], w1, w3, w2, *rest[:-3])

    return kernel


def _rows_of_batch(b):
    return pl.ds(b, TS, stride=B)


_ROWS_SCRATCH = pltpu.VMEM((D // LANES, TM, LANES), F32)


def _put_batch(rows_scr, b, val):
    for j in range(D // LANES):
        rows_scr[j, _rows_of_batch(b), :] = val[:, j * LANES:(j + 1) * LANES]


def _get_batch(rows_scr, b):
    return jnp.concatenate([rows_scr[j, _rows_of_batch(b), :] for j in range(D // LANES)], axis=1)


def _put_rows(rows_scr, val):
    for j in range(D // LANES):
        rows_scr[j] = val[:, j * LANES:(j + 1) * LANES]


def _get_rows(rows_scr):
    return jnp.concatenate([rows_scr[j] for j in range(D // LANES)], axis=1)


def _ffn_first_kernel(x_ref, ctx_ref, pos_ref, mod_ref, g_ref, w1, w3, w2, o_ref, rows_scr):
    is_latent = pl.program_id(0) < NT_LAT
    for b in range(B):
        _put_batch(rows_scr, b, jnp.where(is_latent, x_ref[b] + pos_ref[...], ctx_ref[b]))
    o_ref[...] = _ffn(_get_rows(rows_scr), mod_ref, 0, g_ref[0:1], g_ref[1:2], w1, w3, w2)


def _ffn_kernel(x_ref, mod_ref, g_ref, w1, w3, w2, o_ref):
    for r in range(x_ref.shape[0] // TM):
        rows = slice(r * TM, (r + 1) * TM)
        o_ref[rows, :] = _ffn(x_ref[rows, :], mod_ref, 0, g_ref[0:1], g_ref[1:2], w1, w3, w2)


def _ffn_first(x, ctx, pos, mods, g, w1, w3, w2):
    lat = lambda t: jnp.minimum(t, NT_LAT - 1)
    return pl.pallas_call(
        _with_staged_weights(_ffn_first_kernel, 0, 0, n_in=8),
        grid=(NT,),
        in_specs=[pl.BlockSpec((B, TS, D), lambda t: (0, lat(t), 0)),
                  pl.BlockSpec((B, TS, D), lambda t: (0, jnp.maximum(t - NT_LAT, 0), 0)),
                  pl.BlockSpec((TS, D), lambda t: (lat(t), 0)),
                  _mod_spec(), _resident((6, D))] + _WEIGHT_SPECS,
        out_specs=_row_spec(),
        out_shape=jax.ShapeDtypeStruct((ROWS, D), F32),
        scratch_shapes=[_ROWS_SCRATCH] + _WEIGHT_SCRATCH,
        compiler_params=_cparams(("arbitrary",)),
        name="ffn_first",
    )(x, ctx, pos, mods, g, w1, w3, w2)


def _ffn_plain(x, mods, g, w1, w3, w2, layer):
    return pl.pallas_call(
        _with_staged_weights(_ffn_kernel, layer, 0, n_in=6),
        grid=(NT // 2,),
        in_specs=[pl.BlockSpec((2 * TM, D), lambda t: (t, 0)),
                  pl.BlockSpec((None, B, N_MOD * D),
                               lambda t: (jnp.where(t < NT_LAT // 2, 0, 1), 0, 0)),
                  _resident((6, D))] + _WEIGHT_SPECS,
        out_specs=pl.BlockSpec((2 * TM, D), lambda t: (t, 0)),
        out_shape=jax.ShapeDtypeStruct((ROWS, D), F32),
        scratch_shapes=_WEIGHT_SCRATCH,
        compiler_params=_cparams(("arbitrary",)),
        name="ffn_pre",
    )(x, mods, g, w1, w3, w2)


def _mix_ffn_core(y, x, mod_ref, g_ref, wo_ref, bo_ref, w1, w3, w2, **ffn_kwargs):
    z = _dot(y.astype(BF16), wo_ref[...]) + bo_ref[...]
    x = x + _gated(_rmsnorm(z, g_ref[3:4]), _mod(mod_ref, 5))
    return _ffn(x, mod_ref, 6, g_ref[4:5], g_ref[5:6], w1, w3, w2, **ffn_kwargs)


def _hy_out_ffn_kernel(y_ref, x_ref, mod_ref, g_ref, wo_ref, bo_ref, w1, w3, w2, o_ref, rows_scr):
    for b in range(B):
        _put_batch(rows_scr, b, y_ref[:, b * D:(b + 1) * D])
    o_ref[...] = _mix_ffn_core(_get_rows(rows_scr), x_ref[...], mod_ref, g_ref, wo_ref, bo_ref,
                               w1, w3, w2)


def _hy_out_ffn(y, x, mods, g, w_out, b_out, w1, w3, w2):
    return pl.pallas_call(
        _with_staged_weights(_hy_out_ffn_kernel, 0, 1, n_in=9),
        grid=(NT,),
        in_specs=[_wide_spec(), _row_spec(), _mod_spec(), _resident((6, D)), _resident((D, D)),
                  _resident((1, D))] + _WEIGHT_SPECS,
        out_specs=_row_spec(),
        out_shape=jax.ShapeDtypeStruct((ROWS, D), F32),
        scratch_shapes=[_ROWS_SCRATCH] + _WEIGHT_SCRATCH,
        compiler_params=_cparams(("arbitrary",)),
        name="hyena_out_ffn",
    )(y, x, mods, g, w_out, b_out, w1, w3, w2)


def _seq_first(t):
    return jnp.logical_or(t == 0, t == NT_LAT)


def _seq_last(t):
    return jnp.logical_or(t == NT_LAT - 1, t == NT - 1)


def _halo_specs(order, lo_rows, hi_rows):
    nlo = ROWS // lo_rows
    nhi = ROWS // hi_rows
    return [
        pl.BlockSpec((lo_rows, D), lambda s: (jnp.maximum(order(s) * (TM // lo_rows) - 1, 0), 0)),
        pl.BlockSpec((TM, D), lambda s: (order(s), 0)),
        pl.BlockSpec((hi_rows, D),
                     lambda s: (jnp.minimum((order(s) + 1) * (TM // hi_rows), nhi - 1), 0)),
    ]


def _hy_in_kernel(xp_ref, x_ref, xn_ref, mod_ref, g_ref, w_ref, b_ref, cw_ref, cb_ref,
                  x0_ref, p_ref, x0_scr, p_scr):
    t = pl.program_id(0)
    xa = jnp.concatenate([xp_ref[...], x_ref[...], xn_ref[...]], axis=0)
    h = _modulate(_rmsnorm(xa, g_ref[2:3]), _mod(mod_ref, 3), _mod(mod_ref, 4)).astype(BF16)
    keep_lo = jnp.where(_seq_first(t), 0.0, 1.0).astype(F32)
    keep_hi = jnp.where(_seq_last(t), 0.0, 1.0).astype(F32)

    def conv_cols(c0):
        cols = slice(c0, c0 + MXU_N)
        u = _dot(h, w_ref[:, cols]) + b_ref[:, cols]
        lo = jnp.concatenate([u[0:B] * keep_lo, u[B:TM]], axis=0)
        hi = jnp.concatenate([u[2 * B:TM + B], u[TM + B:TM + 2 * B] * keep_hi], axis=0)
        return (cb_ref[:, cols] + cw_ref[0:1, cols] * lo + cw_ref[1:2, cols] * u[B:TM + B]
                + cw_ref[2:3, cols] * hi)

    slabs = MXU_N // LANES
    for c in range(D // MXU_N):
        c0 = c * MXU_N
        vals = (conv_cols(c0), conv_cols(D + c0) * conv_cols(2 * D + c0))
        for out_ref, scr, val in zip((x0_ref, p_ref), (x0_scr, p_scr), vals):
            for j in range(slabs):
                scr[c * slabs + j] = val[:, j * LANES:(j + 1) * LANES]
            for b in range(B):
                out_ref[:, b * D + c0:b * D + c0 + MXU_N] = jnp.concatenate(
                    [scr[c * slabs + j, _rows_of_batch(b), :] for j in range(slabs)], axis=1)


def _hy_in(x, mods, g, w_in, b_in, conv_w, conv_b):
    return pl.pallas_call(
        _hy_in_kernel,
        grid=(NT,),
        in_specs=_halo_specs(lambda s: s, B, B) + [
            _mod_spec(), _resident((6, D)), _resident((D, 3 * D)), _resident((1, 3 * D)),
            _resident((3, 3 * D)), _resident((1, 3 * D))],
        out_specs=[_wide_spec(), _wide_spec()],
        out_shape=[jax.ShapeDtypeStruct((ROWS // B, B * D), F32)] * 2,
        scratch_shapes=[_ROWS_SCRATCH, _ROWS_SCRATCH],
        compiler_params=_cparams(("arbitrary",)),
        name="hyena_in",
    )(x, x, x, mods, g, w_in, b_in, conv_w, conv_b)


def _filter_kernel(z_ref, fw0, fb0, fw1, fb1, fw2, fb2, freq, fwout, deltas, hf_ref, hb_ref):
    z = z_ref[...]
    h = jnp.sin(freq[0:1] * (_dot3(z, fw0[...]) + fb0[...]))
    h = jnp.sin(freq[1:2] * (_dot3(h, fw1[...]) + fb1[...]))
    h = jnp.sin(freq[2:3] * (_dot3(h, fw2[...]) + fb2[...]))
    filt = _dot3(h, fwout[...])
    decay = jnp.exp(-z[:, 0:1] * deltas[...])
    hf_ref[...] = filt[:, 0:D] * decay
    hb_ref[...] = filt[:, D:2 * D] * decay


def _filters(n, fw0, fb0, fw1, fb1, fw2, fb2, freq, fwout):
    t = np.linspace(0.0, 1.0, n)[:, None]
    bands = np.linspace(1e-4, HY_BANDS - 1, HY_BANDS)[None]
    phase = bands * (2.0 * math.pi * np.arange(n)[:, None] / n)
    zp = np.zeros((n, LANES), np.float32)
    zp[:, :HY_EMB] = np.concatenate([t, np.cos(phase), -np.sin(phase)], axis=-1)
    fw0p = jnp.zeros((LANES, HY_HID), F32).at[:HY_EMB].set(fw0)
    max_decay = math.log(HY_DECAY_TARGET) / HY_FAST_DECAY
    min_decay = math.log(HY_DECAY_TARGET) / HY_SLOW_DECAY
    deltas = np.abs(np.linspace(min_decay, max_decay, D))[None].astype(np.float32)
    tl = 256
    row = lambda i: (i, 0)
    return pl.pallas_call(
        _filter_kernel,
        grid=(n // tl,),
        in_specs=[pl.BlockSpec((tl, LANES), row), _resident((LANES, HY_HID)), _resident((1, HY_HID)),
                  _resident((HY_HID, HY_HID)), _resident((1, HY_HID)),
                  _resident((HY_HID, HY_HID)), _resident((1, HY_HID)),
                  _resident((3, HY_HID)), _resident((HY_HID, 2 * D)), _resident((1, D))],
        out_specs=[pl.BlockSpec((tl, D), row)] * 2,
        out_shape=[jax.ShapeDtypeStruct((n, D), F32)] * 2,
        compiler_params=_cparams(("arbitrary",)),
        name="hyena_filter",
    )(zp, fw0p, fb0[None], fw1, fb1[None], fw2, fb2[None], freq, fwout, deltas)


def _dft_tables(n):
    q = 1 << (int(math.log2(n)) // 2 + 1)
    m = np.arange(n, dtype=np.int64)[None, :]

    def thin(k):
        ang = ((k[:, None] * m) % (2 * n)) * (math.pi / n)
        return jnp.asarray(np.cos(ang), F32), jnp.asarray(np.sin(ang), F32)

    ca, sa = (v[:, None, :] for v in thin(q * np.arange(n // q, dtype=np.int64)))
    cb, sb = (v[None, :, :] for v in thin(np.arange(q, dtype=np.int64)))
    return (ca * cb - sa * sb).reshape(n, n), (sa * cb + ca * sb).reshape(n, n)


def _alt_sign(rows, cols):
    r = lax.broadcasted_iota(jnp.int32, (rows, cols), 0)
    return (1 - 2 * (r & 1)).astype(F32)


def _spectrum_kernel(hf_ref, hb_ref, c_ref, s_ref, kr_ref, ki_ref, kn_ref, *, n, tk):
    hf = hf_ref[...]
    row = lax.broadcasted_iota(jnp.int32, hf.shape, 0)
    hb = jnp.where(row == 0, 0.0, hb_ref[...])
    even = hf + hb
    eb = even.astype(BF16)
    ob = (hb - hf).astype(BF16)
    first = jnp.where(lax.broadcasted_iota(jnp.int32, (tk, hf.shape[1]), 0) == 0, 0.5 / n, 1.0 / n)
    for k in range(n // tk):
        rows = slice(k * tk, (k + 1) * tk)
        scale = first if k == 0 else 1.0 / n
        kr_ref[rows, :] = _dot(c_ref[rows, :], eb) * scale
        ki_ref[rows, :] = _dot(s_ref[rows, :], ob) * scale
    nyq = jnp.sum(even * _alt_sign(*hf.shape), axis=0, keepdims=True) * (0.5 / n)
    kn_ref[...] = jnp.broadcast_to(nyq, kn_ref.shape)


def _spectrum(hf, hb, ctab, stab):
    n = hf.shape[0]
    col = lambda j: (0, j)
    return pl.pallas_call(
        functools.partial(_spectrum_kernel, n=n, tk=min(n, 512)),
        grid=(D // TC,),
        in_specs=[pl.BlockSpec((n, TC), col)] * 2 + [_resident((n, n))] * 2,
        out_specs=[pl.BlockSpec((n, TC), col)] * 2 + [pl.BlockSpec((B, TC), col)],
        out_shape=[jax.ShapeDtypeStruct((n, D), F32)] * 2 + [jax.ShapeDtypeStruct((B, D), F32)],
        compiler_params=_cparams(("arbitrary",)),
        name="hyena_spectrum",
    )(hf, hb, ctab, stab)


def _long_conv_rows(r0, n, p_ref, x0_ref, spec, bias_ref, o_ref, pb_scr, yr_scr, ys_scr):
    kr_ref, ki_ref, kn_ref, c_ref, s_ref = spec
    tk = min(n, 512)
    tc = p_ref.shape[1]
    seq = slice(r0, r0 + n)
    chunks = [slice(k * tk, (k + 1) * tk) for k in range(n // tk)]
    pb_scr[seq, :] = p_ref[seq, :].astype(BF16)
    for rows in chunks:
        dst = slice(r0 + rows.start, r0 + rows.stop)
        xr = _dot(c_ref[rows, :], pb_scr[seq, :])
        xs = _dot(s_ref[rows, :], pb_scr[seq, :])
        kr = kr_ref[rows, :]
        ki = ki_ref[rows, :]
        yr_scr[dst, :] = (xr * kr + xs * ki).astype(BF16)
        ys_scr[dst, :] = (xs * kr - xr * ki).astype(BF16)
    nyq = jnp.sum(p_ref[seq, :] * _alt_sign(n, tc), axis=0, keepdims=True) * kn_ref[0:1, :]
    alt_nyq = _alt_sign(tk, tc) * nyq
    for rows in chunks:
        dst = slice(r0 + rows.start, r0 + rows.stop)
        y = _dot(c_ref[rows, :], yr_scr[seq, :]) + _dot(s_ref[rows, :], ys_scr[seq, :]) + alt_nyq
        o_ref[dst, :] = x0_ref[dst, :] * (y + p_ref[dst, :] * bias_ref[...])


def _long_conv_kernel(p_ref, x0_ref, bias_ref, *refs):
    lat, ctx, (o_ref, pb_scr, yr_scr, ys_scr) = refs[0:5], refs[5:10], refs[10:]
    _long_conv_rows(0, L, p_ref, x0_ref, lat, bias_ref, o_ref, pb_scr, yr_scr, ys_scr)
    _long_conv_rows(L, CTX, p_ref, x0_ref, ctx, bias_ref, o_ref, pb_scr, yr_scr, ys_scr)


def _long_conv(p2, x02, bias, spec_lat, spec_ctx):
    nc = D // TC
    strip = pl.BlockSpec((ROWS // B, TC), lambda j: (0, j))
    ch = lambda j: (0, j % nc)

    def spec_specs(n):
        return [pl.BlockSpec((n, TC), ch), pl.BlockSpec((n, TC), ch), pl.BlockSpec((B, TC), ch),
                _resident((n, n)), _resident((n, n))]

    return pl.pallas_call(
        _long_conv_kernel,
        grid=(B * D // TC,),
        scratch_shapes=[pltpu.VMEM((ROWS // B, TC), BF16)] * 3,
        in_specs=[strip, strip, pl.BlockSpec((1, TC), ch)] + spec_specs(L) + spec_specs(CTX),
        out_specs=strip,
        out_shape=jax.ShapeDtypeStruct((ROWS // B, B * D), F32),
        compiler_params=_cparams(("arbitrary",)),
        name="hyena_long_conv",
    )(p2, x02, bias, *spec_lat, *spec_ctx)


def _gelu_tanh(x):
    return x * (0.5 * (1.0 + jnp.tanh(math.sqrt(2.0 / math.pi) * (x + 0.044715 * (x * x * x)))))


def _rg_coeffs(xc, hd, wai_ref, bai_ref, lam_ref, a_scr, b_scr):
    sl = slice(hd * RG_BLOCK, (hd + 1) * RG_BLOCK)
    lam = lam_ref[:, sl]
    softplus_neg = jnp.maximum(-lam, 0.0) + jnp.log1p(jnp.exp(-jnp.abs(lam)))
    pre = _dot(xc.astype(BF16), wai_ref[hd]) + bai_ref[:, 2 * hd * RG_BLOCK:2 * (hd + 1) * RG_BLOCK]
    gates = _sigmoid(pre)
    a = jnp.exp(-RG_C * gates[:, 0:RG_BLOCK] * softplus_neg)
    a_scr[:, sl] = a
    b_scr[:, sl] = jnp.sqrt((1.0 - a) * (1.0 + a)) * gates[:, RG_BLOCK:2 * RG_BLOCK] * xc


def _scan_tile(a_scr, b_scr, h_scr, emit, reverse):
    steps = TM // B

    def body(k, h):
        t = steps - 1 - k if reverse else k
        r0 = pl.multiple_of(t * B, B)
        h = a_scr[pl.ds(r0, B), :] * h + b_scr[pl.ds(r0, B), :]
        emit(r0, h)
        return h

    h_scr[...] = lax.fori_loop(0, steps, body, h_scr[...], unroll=8)


def _rg_fwd_order(s):
    return jnp.where(s < NT_CTX, NT_LAT + s, s - NT_CTX)


def _rg_in_kernel(xp_ref, x_ref, xn_ref, mod_ref, g_ref, w_ref, b_ref, cw_ref, cb_ref,
                  wai_ref, bai_ref, lam_ref, xc_ref, gate_ref, hs_ref, a_scr, b_scr, h_scr):
    s = pl.program_id(0)
    t = _rg_fwd_order(s)
    xa = jnp.concatenate([xp_ref[...], x_ref[...], xn_ref[...]], axis=0)
    h = _modulate(_rmsnorm(xa, g_ref[2:3]), _mod(mod_ref, 3), _mod(mod_ref, 4)).astype(BF16)
    keep_lo = jnp.where(_seq_first(t), 0.0, 1.0).astype(F32)
    keep_hi = jnp.where(_seq_last(t), 0.0, 1.0).astype(F32)
    for hd in range(RG_HEADS):
        sl = slice(hd * RG_BLOCK, (hd + 1) * RG_BLOCK)
        rec = slice(D + hd * RG_BLOCK, D + (hd + 1) * RG_BLOCK)
        gate_ref[:, sl] = (_dot(h[B:TM + B], w_ref[:, sl]) + b_ref[:, sl]).astype(BF16)
        u = _dot(h, w_ref[:, rec]) + b_ref[:, rec]
        taps = (jnp.concatenate([u[0:B] * keep_lo, u[B:TM]], axis=0),
                u[B:TM + B],
                jnp.concatenate([u[2 * B:TM + B], u[TM + B:TM + 2 * B] * keep_hi], axis=0),
                jnp.concatenate([u[3 * B:TM + B], u[TM + B:TM + 3 * B] * keep_hi], axis=0))
        xc = cb_ref[:, sl]
        for k, tap in enumerate(taps):
            xc = xc + cw_ref[k:k + 1, sl] * tap
        xc_ref[:, sl] = xc
        _rg_coeffs(xc, hd, wai_ref, bai_ref, lam_ref, a_scr, b_scr)

    @pl.when(s == 0)
    def _():
        h_scr[...] = jnp.zeros((B, D), F32)

    def emit(r0, hv):
        hs_ref[pl.ds(r0, B), :] = hv

    _scan_tile(a_scr, b_scr, h_scr, emit, reverse=False)


def _rg_gate_specs():
    return [_resident((RG_HEADS, RG_BLOCK, 2 * RG_BLOCK)), _resident((1, 2 * D)), _resident((1, D))]


def _rg_in(x, mods, g, w_in, b_in, conv_w, conv_b, wai, bai, lam):
    order = _rg_fwd_order
    mod_spec = pl.BlockSpec((None, B, N_MOD * D),
                            lambda s: (jnp.where(order(s) < NT_LAT, 0, 1), 0, 0))
    out_spec = pl.BlockSpec((TM, D), lambda s: (order(s), 0))
    return pl.pallas_call(
        _rg_in_kernel,
        grid=(NT,),
        in_specs=_halo_specs(order, B, 2 * B) + [
            mod_spec, _resident((6, D)), _resident((D, 2 * D)), _resident((1, 2 * D)),
            _resident((4, D)), _resident((1, D))] + _rg_gate_specs(),
        out_specs=[out_spec] * 3,
        out_shape=[jax.ShapeDtypeStruct((ROWS, D), F32), jax.ShapeDtypeStruct((ROWS, D), BF16),
                   jax.ShapeDtypeStruct((ROWS, D), F32)],
        scratch_shapes=[pltpu.VMEM((TM, D), F32), pltpu.VMEM((TM, D), F32),
                        pltpu.VMEM((B, D), F32)],
        compiler_params=_cparams(("arbitrary",)),
        name="rglru_in_fwd_scan",
    )(x, x, x, mods, g, w_in, b_in, conv_w, conv_b, wai, bai, lam)


def _rg_tile_coeffs(xc_ref, wai_ref, bai_ref, lam_ref, a_scr, b_scr):
    for hd in range(RG_HEADS):
        xc = xc_ref[:, hd * RG_BLOCK:(hd + 1) * RG_BLOCK]
        _rg_coeffs(xc, hd, wai_ref, bai_ref, lam_ref, a_scr, b_scr)


def _rg_ctx_bwd_kernel(xc_ref, wai_ref, bai_ref, lam_ref, h_ref, a_scr, b_scr, h_scr):
    _rg_tile_coeffs(xc_ref, wai_ref, bai_ref, lam_ref, a_scr, b_scr)

    @pl.when(pl.program_id(0) == 0)
    def _():
        h_scr[...] = jnp.zeros((B, D), F32)

    _scan_tile(a_scr, b_scr, h_scr, lambda r0, hv: None, reverse=True)
    h_ref[...] = h_scr[...]


def _rg_ctx_bwd(xc, wai, bai, lam):
    return pl.pallas_call(
        _rg_ctx_bwd_kernel,
        grid=(NT_CTX,),
        in_specs=[pl.BlockSpec((TM, D), lambda s: (NT - 1 - s, 0))] + _rg_gate_specs(),
        out_specs=pl.BlockSpec((B, D), lambda s: (0, 0)),
        out_shape=jax.ShapeDtypeStruct((B, D), F32),
        scratch_shapes=[pltpu.VMEM((TM, D), F32), pltpu.VMEM((TM, D), F32),
                        pltpu.VMEM((B, D), F32)],
        compiler_params=_cparams(("arbitrary",)),
        name="rglru_ctx_bwd_scan",
    )(xc, wai, bai, lam)


def _rg_out_ffn_kernel(xc_ref, gate_ref, hs_ref, x_ref, h0_ref, mod_ref, g_ref,
                       wai_ref, bai_ref, lam_ref, wo_ref, bo_ref, w1, w3, w2,
                       o_ref, a_scr, b_scr, hb_scr, h_scr, rows_scr):
    s = pl.program_id(0)

    def head_gates(hd):
        xc = xc_ref[:, hd * RG_BLOCK:(hd + 1) * RG_BLOCK]
        _rg_coeffs(xc, hd, wai_ref, bai_ref, lam_ref, a_scr, b_scr)

    @pl.when(s == 0)
    def _():
        h_scr[...] = h0_ref[...]
        for hd in range(RG_HEADS):
            head_gates(hd)

    @pl.when(s > 0)
    def _():
        y = (hs_ref[...] + hb_scr[...]) * _gelu_tanh(gate_ref[...].astype(F32))
        res = _mix_ffn_core(
            y, x_ref[...], mod_ref, g_ref, wo_ref, bo_ref, w1, w3, w2, chunks=F_QUARTERS,
            side_work=[functools.partial(head_gates, hd) for hd in range(RG_HEADS)])
        _put_rows(rows_scr, res)
        for b in range(B):
            o_ref[b] = _get_batch(rows_scr, b)

    @pl.when(s < NT_LAT)
    def _():
        def emit(r0, hv):
            hb_scr[pl.ds(r0, B), :] = hv

        _scan_tile(a_scr, b_scr, h_scr, emit, reverse=True)


def _rg_out_ffn(xc, gate, hs, x, h0, mods, g, wai, bai, lam, w_out, b_out, w1, w3, w2):
    scan_tile = lambda s: (jnp.maximum(NT_LAT - 1 - s, 0), 0)
    out_tile = lambda s: jnp.minimum(NT_LAT - s, NT_LAT - 1)
    prev = pl.BlockSpec((TM, D), lambda s: (out_tile(s), 0))
    return pl.pallas_call(
        _with_staged_weights(_rg_out_ffn_kernel, 1, 1, n_in=15),
        grid=(NT_LAT + 1,),
        in_specs=[pl.BlockSpec((TM, D), scan_tile), prev, prev, prev, _resident((B, D)),
                  pl.BlockSpec((None, B, N_MOD * D), lambda s: (0, 0, 0)), _resident((6, D))]
        + _rg_gate_specs() + [_resident((D, D)), _resident((1, D))] + _WEIGHT_SPECS,
        out_specs=pl.BlockSpec((B, TS, D), lambda s: (0, out_tile(s), 0)),
        out_shape=jax.ShapeDtypeStruct((B, L, D), F32),
        scratch_shapes=[pltpu.VMEM((TM, D), F32), pltpu.VMEM((TM, D), F32),
                        pltpu.VMEM((TM, D), F32), pltpu.VMEM((B, D), F32), _ROWS_SCRATCH]
        + _WEIGHT_SCRATCH,
        compiler_params=_cparams(("arbitrary",)),
        name="rglru_out_ffn",
    )(xc, gate, hs, x, h0, mods, g, wai, bai, lam, w_out, b_out, w1, w3, w2)


def _grid_pos():
    rows = L // GRID_W
    quarter = D // 4
    omega = POS_BASE ** (-np.arange(quarter) / quarter)

    def emb(q):
        ang = q[:, None] * omega[None]
        return jnp.asarray(np.concatenate([np.sin(ang), np.cos(ang)], axis=-1), F32)

    row_code = jnp.repeat(emb(np.arange(rows)), GRID_W, axis=0)
    col_code = jnp.tile(emb(np.arange(GRID_W)), (rows, 1))
    return jnp.concatenate([row_code, col_code], axis=-1)


def kernel(x, c, ctx, c_ctx, ada_w, ada_b, norm_g, ffn_w1, ffn_w3, ffn_w2, hy_w_in, hy_b_in, hy_conv_w, hy_conv_b, hy_fw0, hy_fb0, hy_fw1, hy_fb1, hy_fw2, hy_fb2, hy_freq, hy_fwout, hy_filt_bias, hy_w_out, hy_b_out, rg_w_in, rg_b_in, rg_conv_w, rg_conv_b, rg_wa, rg_ba, rg_wi, rg_bi, rg_lam, rg_w_out, rg_b_out):
    cc = jnp.concatenate([c, jnp.broadcast_to(c_ctx[None], (B, D))], axis=0)
    mods = _mods(cc, ada_w, ada_b).reshape(DEPTH, 2, B, N_MOD * D)
    w1, w3, w2 = ffn_w1, ffn_w3, ffn_w2

    g = norm_g[0]
    xs = _ffn_first(x, ctx, _grid_pos(), mods[0], g, w1, w3, w2)
    x0, p = _hy_in(xs, mods[0], g, hy_w_in[0].astype(BF16), hy_b_in[0][None],
                   hy_conv_w[0], hy_conv_b[0][None])
    fparams = (hy_fw0[0], hy_fb0[0], hy_fw1[0], hy_fb1[0], hy_fw2[0], hy_fb2[0],
               hy_freq[0], hy_fwout[0])
    specs = []
    for n in (L, CTX):
        hf, hb = _filters(n, *fparams)
        ctab, stab = (tab.astype(BF16) for tab in _dft_tables(n))
        specs.append(list(_spectrum(hf, hb, ctab, stab)) + [ctab, stab])
    y = _long_conv(p, x0, hy_filt_bias[0][None], *specs)
    xs = _hy_out_ffn(y, xs, mods[0], g, hy_w_out[0].astype(BF16), hy_b_out[0][None], w1, w3, w2)

    g = norm_g[1]
    xs = _ffn_plain(xs, mods[1], g, w1, w3, w2, layer=1)
    wai = jnp.concatenate([rg_wa[0], rg_wi[0]], axis=-1).astype(BF16)
    per_head = lambda v: v.reshape(2, RG_HEADS, RG_BLOCK)
    bai = jnp.concatenate([per_head(rg_ba[0]), per_head(rg_bi[0])], axis=-1).reshape(2, 1, 2 * D)
    lam = rg_lam[0][:, None, :]
    xc, gate, hs = _rg_in(xs, mods[1], g, rg_w_in[0].astype(BF16), rg_b_in[0][None],
                          rg_conv_w[0], rg_conv_b[0][None], wai[0], bai[0], lam[0])
    h_ctx = _rg_ctx_bwd(xc, wai[1], bai[1], lam[1])
    return _rg_out_ffn(xc, gate, hs, xs, h_ctx, mods[1], g, wai[1], bai[1], lam[1],
                       rg_w_out[0].astype(BF16), rg_b_out[0][None], w1, w3, w2)
```

```python
import functools
import math

import jax
import jax.numpy as jnp
import numpy as np
from jax import lax
from jax.experimental import pallas as pl
from jax.experimental.pallas import tpu as pltpu

F32 = jnp.float32
BF16 = jnp.bfloat16

D = 1024
B = 8
LANES = 128
MXU_N = 256
L = 2048
CTX = 256
DEPTH = 2
GRID_W = 64
D_FF = 2816
N_MOD = 9
MACARON = 0.5
NORM_EPS = 1e-6
POS_BASE = 10000.0
HY_EMB = 33
HY_BANDS = 16
HY_HID = 64
HY_FAST_DECAY = 0.3
HY_SLOW_DECAY = 1.5
HY_DECAY_TARGET = 1e-2
RG_HEADS = 4
RG_BLOCK = D // RG_HEADS
RG_C = 8.0

ROWS_LAT = L * B
ROWS_CTX = CTX * B
ROWS = ROWS_LAT + ROWS_CTX
TM = 512
TS = TM // B
NT_LAT = ROWS_LAT // TM
NT_CTX = ROWS_CTX // TM
NT = NT_LAT + NT_CTX
TI = 1024
NI_LAT = ROWS_LAT // TI
NI_CTX = ROWS_CTX // TI
NI = NI_LAT + NI_CTX
F_CHUNKS = ((0, 1536), (1536, D_FF))
F_QUARTERS = ((0, 768), (768, 1536), (1536, 2304), (2304, D_FF))
TC = 256
VMEM_LIMIT = 58 * 1024 * 1024


def _cparams(sem):
    return pltpu.CompilerParams(dimension_semantics=sem, vmem_limit_bytes=VMEM_LIMIT)


def _resident(shape):
    nd = len(shape)
    return pl.BlockSpec(shape, lambda *_: (0,) * nd, pipeline_mode=pl.Buffered(1))


def _split(a):
    hi = a.astype(BF16)
    lo = (a - hi.astype(F32)).astype(BF16)
    return hi, lo


def _dot(a, b):
    return jnp.dot(a, b, preferred_element_type=F32)


def _dot3(a, b):
    ah, al = _split(a)
    bh, bl = _split(b)
    return _dot(ah, bh) + _dot(ah, bl) + _dot(al, bh)


def _rmsnorm(x, g):
    ms = jnp.mean(x * x, axis=-1, keepdims=True)
    return x * lax.rsqrt(ms + NORM_EPS) * g


def _mod(mod_ref, k):
    return mod_ref[:, k * D:(k + 1) * D]


def _per_batch(x, fn):
    rows = x.shape[0]
    return fn(x.reshape(rows // B, B, x.shape[1])).reshape(rows, x.shape[1])


def _modulate(xn, shift8, scale8):
    return _per_batch(xn, lambda v: v * (1.0 + scale8)[None] + shift8[None])


def _gated(z, gate8):
    return _per_batch(z, lambda v: v * gate8[None])


def _sigmoid(x):
    return jax.nn.sigmoid(x)


def _ffn(x, mod_ref, k0, g_pre, g_post, w1_ref, w3_ref, w2_ref, chunks=F_CHUNKS, side_work=()):
    h = _modulate(_rmsnorm(x, g_pre), _mod(mod_ref, k0), _mod(mod_ref, k0 + 1)).astype(BF16)
    y = None
    for idx, (c0, c1) in enumerate(chunks):
        a = _dot(h, w1_ref[:, c0:c1])
        b = _dot(h, w3_ref[:, c0:c1])
        if idx < len(side_work):
            side_work[idx]()
        act = (a * _sigmoid(a) * b).astype(BF16)
        part = _dot(act, w2_ref[c0:c1, :])
        y = part if y is None else y + part
    return x + MACARON * _gated(_rmsnorm(y, g_post), _mod(mod_ref, k0 + 2))


def _mods_kernel(cc_ref, w_ref, b_ref, o_ref):
    a = cc_ref[...]
    a = a * _sigmoid(a)
    o_ref[...] = _dot3(a, w_ref[...]) + b_ref[...]


def _mods(cc, ada_w, ada_b):
    tn = 1024
    n = N_MOD * D
    return pl.pallas_call(
        _mods_kernel,
        grid=(DEPTH, n // tn),
        in_specs=[
            pl.BlockSpec((2 * B, D), lambda i, j: (0, 0)),
            pl.BlockSpec((None, D, tn), lambda i, j: (i, 0, j)),
            pl.BlockSpec((None, 1, tn), lambda i, j: (i, 0, j)),
        ],
        out_specs=pl.BlockSpec((None, 2 * B, tn), lambda i, j: (i, 0, j)),
        out_shape=jax.ShapeDtypeStruct((DEPTH, 2 * B, n), F32),
        compiler_params=_cparams(("arbitrary", "arbitrary")),
        name="ada_mods",
    )(cc, ada_w, ada_b.reshape(DEPTH, 1, n))


def _mod_spec(nt_lat=NT_LAT):
    return pl.BlockSpec((None, B, N_MOD * D), lambda t: (jnp.where(t < nt_lat, 0, 1), 0, 0))


def _row_spec(width=D):
    return pl.BlockSpec((TM, width), lambda t: (t, 0))


def _wide_spec():
    return pl.BlockSpec((TS, B * D), lambda t: (t, 0))


W_CHUNKS = 8
W_SLOTS = 3
_WEIGHT_SPECS = [pl.BlockSpec(memory_space=pl.ANY)] * 3
_WEIGHT_SCRATCH = [pltpu.VMEM((D, D_FF), BF16), pltpu.VMEM((D, D_FF), BF16),
                   pltpu.VMEM((D_FF, D), BF16)]


def _stage_weights(srcs, dsts):
    chunks = [(src, dst, r0, dst.shape[0] // W_CHUNKS) for src, dst in zip(srcs, dsts)
              for r0 in range(0, dst.shape[0], dst.shape[0] // W_CHUNKS)]

    def run(stage_up, stage_down, sems):
        def copy(k):
            src, dst, r0, rows = chunks[k]
            stage = stage_up if dst.shape[1] == D_FF else stage_down
            return pltpu.make_async_copy(src.at[pl.ds(r0, rows), :], stage.at[k % W_SLOTS],
                                         sems.at[k % W_SLOTS])

        for k in range(W_SLOTS - 1):
            copy(k).start()
        for k, (_, dst, r0, rows) in enumerate(chunks):
            if k + W_SLOTS - 1 < len(chunks):
                copy(k + W_SLOTS - 1).start()
            copy(k).wait()
            stage = stage_up if dst.shape[1] == D_FF else stage_down
            dst[pl.ds(r0, rows), :] = stage[k % W_SLOTS].astype(BF16)

    pl.run_scoped(run, pltpu.VMEM((W_SLOTS, D // W_CHUNKS, D_FF), F32),
                  pltpu.VMEM((W_SLOTS, D_FF // W_CHUNKS, D), F32),
                  pltpu.SemaphoreType.DMA((W_SLOTS,)))


def _with_staged_weights(body, layer, half, n_in):
    def kernel(*refs):
        ins, rest = refs[:n_in], refs[n_in:]
        w1, w3, w2 = rest[-3:]

        @pl.when(pl.program_id(0) == 0)
        def _():
            _stage_weights([w.at[layer, half] for w in ins[-3:]], (w1, w3, w2))

        body(*ins[:-3], w1, w3, w2, *rest[:-3])

    return kernel


def _rows_of_batch(b):
    return pl.ds(b, TS, stride=B)


_ROWS_SCRATCH = pltpu.VMEM((D // LANES, TM, LANES), F32)


def _put_batch(rows_scr, b, val):
    for j in range(D // LANES):
        rows_scr[j, _rows_of_batch(b), :] = val[:, j * LANES:(j + 1) * LANES]


def _get_batch(rows_scr, b):
    return jnp.concatenate([rows_scr[j, _rows_of_batch(b), :] for j in range(D // LANES)], axis=1)


def _put_rows(rows_scr, val):
    for j in range(D // LANES):
        rows_scr[j] = val[:, j * LANES:(j + 1) * LANES]


def _get_rows(rows_scr):
    return jnp.concatenate([rows_scr[j] for j in range(D // LANES)], axis=1)


def _ffn_first_kernel(x_ref, ctx_ref, pos_ref, mod_ref, g_ref, w1, w3, w2, o_ref, rows_scr):
    is_latent = pl.program_id(0) < NT_LAT
    for b in range(B):
        _put_batch(rows_scr, b, jnp.where(is_latent, x_ref[b] + pos_ref[...], ctx_ref[b]))
    o_ref[...] = _ffn(_get_rows(rows_scr), mod_ref, 0, g_ref[0:1], g_ref[1:2], w1, w3, w2)


def _ffn_kernel(x_ref, mod_ref, g_ref, w1, w3, w2, o_ref):
    for r in range(x_ref.shape[0] // TM):
        rows = slice(r * TM, (r + 1) * TM)
        o_ref[rows, :] = _ffn(x_ref[rows, :], mod_ref, 0, g_ref[0:1], g_ref[1:2], w1, w3, w2)


def _ffn_first(x, ctx, pos, mods, g, w1, w3, w2):
    lat = lambda t: jnp.minimum(t, NT_LAT - 1)
    return pl.pallas_call(
        _with_staged_weights(_ffn_first_kernel, 0, 0, n_in=8),
        grid=(NT,),
        in_specs=[pl.BlockSpec((B, TS, D), lambda t: (0, lat(t), 0)),
                  pl.BlockSpec((B, TS, D), lambda t: (0, jnp.maximum(t - NT_LAT, 0), 0)),
                  pl.BlockSpec((TS, D), lambda t: (lat(t), 0)),
                  _mod_spec(), _resident((6, D))] + _WEIGHT_SPECS,
        out_specs=_row_spec(),
        out_shape=jax.ShapeDtypeStruct((ROWS, D), F32),
        scratch_shapes=[_ROWS_SCRATCH] + _WEIGHT_SCRATCH,
        compiler_params=_cparams(("arbitrary",)),
        name="ffn_first",
    )(x, ctx, pos, mods, g, w1, w3, w2)


def _ffn_plain(x, mods, g, w1, w3, w2, layer):
    return pl.pallas_call(
        _with_staged_weights(_ffn_kernel, layer, 0, n_in=6),
        grid=(NT // 2,),
        in_specs=[pl.BlockSpec((2 * TM, D), lambda t: (t, 0)),
                  pl.BlockSpec((None, B, N_MOD * D),
                               lambda t: (jnp.where(t < NT_LAT // 2, 0, 1), 0, 0)),
                  _resident((6, D))] + _WEIGHT_SPECS,
        out_specs=pl.BlockSpec((2 * TM, D), lambda t: (t, 0)),
        out_shape=jax.ShapeDtypeStruct((ROWS, D), F32),
        scratch_shapes=_WEIGHT_SCRATCH,
        compiler_params=_cparams(("arbitrary",)),
        name="ffn_pre",
    )(x, mods, g, w1, w3, w2)


def _mix_ffn_core(y, x, mod_ref, g_ref, wo_ref, bo_ref, w1, w3, w2, **ffn_kwargs):
    z = _dot(y.astype(BF16), wo_ref[...]) + bo_ref[...]
    x = x + _gated(_rmsnorm(z, g_ref[3:4]), _mod(mod_ref, 5))
    return _ffn(x, mod_ref, 6, g_ref[4:5], g_ref[5:6], w1, w3, w2, **ffn_kwargs)


def _hy_out_ffn_kernel(y_ref, x_ref, mod_ref, g_ref, wo_ref, bo_ref, w1, w3, w2, o_ref, rows_scr):
    for b in range(B):
        _put_batch(rows_scr, b, y_ref[:, b * D:(b + 1) * D])
    o_ref[...] = _mix_ffn_core(_get_rows(rows_scr), x_ref[...], mod_ref, g_ref, wo_ref, bo_ref,
                               w1, w3, w2)


def _hy_out_ffn(y, x, mods, g, w_out, b_out, w1, w3, w2):
    return pl.pallas_call(
        _with_staged_weights(_hy_out_ffn_kernel, 0, 1, n_in=9),
        grid=(NT,),
        in_specs=[_wide_spec(), _row_spec(), _mod_spec(), _resident((6, D)), _resident((D, D)),
                  _resident((1, D))] + _WEIGHT_SPECS,
        out_specs=_row_spec(),
        out_shape=jax.ShapeDtypeStruct((ROWS, D), F32),
        scratch_shapes=[_ROWS_SCRATCH] + _WEIGHT_SCRATCH,
        compiler_params=_cparams(("arbitrary",)),
        name="hyena_out_ffn",
    )(y, x, mods, g, w_out, b_out, w1, w3, w2)


def _seq_first(t):
    return jnp.logical_or(t == 0, t == NI_LAT)


def _seq_last(t):
    return jnp.logical_or(t == NI_LAT - 1, t == NI - 1)


def _halo_specs(order, lo_rows, hi_rows):
    nlo = ROWS // lo_rows
    nhi = ROWS // hi_rows
    return [
        pl.BlockSpec((lo_rows, D), lambda s: (jnp.maximum(order(s) * (TI // lo_rows) - 1, 0), 0)),
        pl.BlockSpec((TI, D), lambda s: (order(s), 0)),
        pl.BlockSpec((hi_rows, D),
                     lambda s: (jnp.minimum((order(s) + 1) * (TI // hi_rows), nhi - 1), 0)),
    ]


def _hy_in_kernel(xp_ref, x_ref, xn_ref, mod_ref, g_ref, w_ref, b_ref, cw_ref, cb_ref,
                  x0_ref, p_ref, x0_scr, p_scr):
    t = pl.program_id(0)
    xa = jnp.concatenate([xp_ref[...], x_ref[...], xn_ref[...]], axis=0)
    h = _modulate(_rmsnorm(xa, g_ref[2:3]), _mod(mod_ref, 3), _mod(mod_ref, 4)).astype(BF16)
    keep_lo = jnp.where(_seq_first(t), 0.0, 1.0).astype(F32)
    keep_hi = jnp.where(_seq_last(t), 0.0, 1.0).astype(F32)

    def conv_cols(c0):
        cols = slice(c0, c0 + MXU_N)
        u = _dot(h, w_ref[:, cols]) + b_ref[:, cols]
        lo = jnp.concatenate([u[0:B] * keep_lo, u[B:TI]], axis=0)
        hi = jnp.concatenate([u[2 * B:TI + B], u[TI + B:TI + 2 * B] * keep_hi], axis=0)
        return (cb_ref[:, cols] + cw_ref[0:1, cols] * lo + cw_ref[1:2, cols] * u[B:TI + B]
                + cw_ref[2:3, cols] * hi)

    slabs = MXU_N // LANES
    for c in range(D // MXU_N):
        c0 = c * MXU_N
        vals = (conv_cols(c0), conv_cols(D + c0) * conv_cols(2 * D + c0))
        for out_ref, scr, val in zip((x0_ref, p_ref), (x0_scr, p_scr), vals):
            for j in range(slabs):
                scr[c * slabs + j] = val[:, j * LANES:(j + 1) * LANES]
            for b in range(B):
                out_ref[:, b * D + c0:b * D + c0 + MXU_N] = jnp.concatenate(
                    [scr[c * slabs + j, pl.ds(b, TI // B, stride=B), :] for j in range(slabs)], axis=1)


def _hy_in(x, mods, g, w_in, b_in, conv_w, conv_b):
    wide = pl.BlockSpec((TI // B, B * D), lambda t: (t, 0))
    rows_scratch = pltpu.VMEM((D // LANES, TI, LANES), F32)
    return pl.pallas_call(
        _hy_in_kernel,
        grid=(NI,),
        in_specs=_halo_specs(lambda s: s, B, B) + [
            _mod_spec(NI_LAT), _resident((6, D)), _resident((D, 3 * D)), _resident((1, 3 * D)),
            _resident((3, 3 * D)), _resident((1, 3 * D))],
        out_specs=[wide, wide],
        out_shape=[jax.ShapeDtypeStruct((ROWS // B, B * D), F32)] * 2,
        scratch_shapes=[rows_scratch, rows_scratch],
        compiler_params=_cparams(("arbitrary",)),
        name="hyena_in",
    )(x, x, x, mods, g, w_in, b_in, conv_w, conv_b)


def _filter_kernel(z_ref, fw0, fb0, fw1, fb1, fw2, fb2, freq, fwout, deltas, hf_ref, hb_ref):
    z = z_ref[...]
    h = jnp.sin(freq[0:1] * (_dot3(z, fw0[...]) + fb0[...]))
    h = jnp.sin(freq[1:2] * (_dot3(h, fw1[...]) + fb1[...]))
    h = jnp.sin(freq[2:3] * (_dot3(h, fw2[...]) + fb2[...]))
    filt = _dot3(h, fwout[...])
    decay = jnp.exp(-z[:, 0:1] * deltas[...])
    hf_ref[...] = filt[:, 0:D] * decay
    hb_ref[...] = filt[:, D:2 * D] * decay


def _filters(n, fw0, fb0, fw1, fb1, fw2, fb2, freq, fwout):
    t = np.linspace(0.0, 1.0, n)[:, None]
    bands = np.linspace(1e-4, HY_BANDS - 1, HY_BANDS)[None]
    phase = bands * (2.0 * math.pi * np.arange(n)[:, None] / n)
    zp = np.zeros((n, LANES), np.float32)
    zp[:, :HY_EMB] = np.concatenate([t, np.cos(phase), -np.sin(phase)], axis=-1)
    fw0p = jnp.zeros((LANES, HY_HID), F32).at[:HY_EMB].set(fw0)
    max_decay = math.log(HY_DECAY_TARGET) / HY_FAST_DECAY
    min_decay = math.log(HY_DECAY_TARGET) / HY_SLOW_DECAY
    deltas = np.abs(np.linspace(min_decay, max_decay, D))[None].astype(np.float32)
    tl = 256
    row = lambda i: (i, 0)
    return pl.pallas_call(
        _filter_kernel,
        grid=(n // tl,),
        in_specs=[pl.BlockSpec((tl, LANES), row), _resident((LANES, HY_HID)), _resident((1, HY_HID)),
                  _resident((HY_HID, HY_HID)), _resident((1, HY_HID)),
                  _resident((HY_HID, HY_HID)), _resident((1, HY_HID)),
                  _resident((3, HY_HID)), _resident((HY_HID, 2 * D)), _resident((1, D))],
        out_specs=[pl.BlockSpec((tl, D), row)] * 2,
        out_shape=[jax.ShapeDtypeStruct((n, D), F32)] * 2,
        compiler_params=_cparams(("arbitrary",)),
        name="hyena_filter",
    )(zp, fw0p, fb0[None], fw1, fb1[None], fw2, fb2[None], freq, fwout, deltas)


def _dft_tables(n):
    q = 1 << (int(math.log2(n)) // 2 + 1)
    m = np.arange(n, dtype=np.int64)[None, :]

    def thin(k):
        ang = ((k[:, None] * m) % (2 * n)) * (math.pi / n)
        return jnp.asarray(np.cos(ang), F32), jnp.asarray(np.sin(ang), F32)

    ca, sa = (v[:, None, :] for v in thin(q * np.arange(n // q, dtype=np.int64)))
    cb, sb = (v[None, :, :] for v in thin(np.arange(q, dtype=np.int64)))
    return (ca * cb - sa * sb).reshape(n, n), (sa * cb + ca * sb).reshape(n, n)


def _alt_sign(rows, cols):
    r = lax.broadcasted_iota(jnp.int32, (rows, cols), 0)
    return (1 - 2 * (r & 1)).astype(F32)


def _spectrum_kernel(hf_ref, hb_ref, c_ref, s_ref, kr_ref, ki_ref, kn_ref, *, n, tk):
    hf = hf_ref[...]
    row = lax.broadcasted_iota(jnp.int32, hf.shape, 0)
    hb = jnp.where(row == 0, 0.0, hb_ref[...])
    even = hf + hb
    eb = even.astype(BF16)
    ob = (hb - hf).astype(BF16)
    first = jnp.where(lax.broadcasted_iota(jnp.int32, (tk, hf.shape[1]), 0) == 0, 0.5 / n, 1.0 / n)
    for k in range(n // tk):
        rows = slice(k * tk, (k + 1) * tk)
        scale = first if k == 0 else 1.0 / n
        kr_ref[rows, :] = _dot(c_ref[rows, :], eb) * scale
        ki_ref[rows, :] = _dot(s_ref[rows, :], ob) * scale
    nyq = jnp.sum(even * _alt_sign(*hf.shape), axis=0, keepdims=True) * (0.5 / n)
    kn_ref[...] = jnp.broadcast_to(nyq, kn_ref.shape)


def _spectrum(hf, hb, ctab, stab):
    n = hf.shape[0]
    col = lambda j: (0, j)
    return pl.pallas_call(
        functools.partial(_spectrum_kernel, n=n, tk=min(n, 512)),
        grid=(D // TC,),
        in_specs=[pl.BlockSpec((n, TC), col)] * 2 + [_resident((n, n))] * 2,
        out_specs=[pl.BlockSpec((n, TC), col)] * 2 + [pl.BlockSpec((B, TC), col)],
        out_shape=[jax.ShapeDtypeStruct((n, D), F32)] * 2 + [jax.ShapeDtypeStruct((B, D), F32)],
        compiler_params=_cparams(("arbitrary",)),
        name="hyena_spectrum",
    )(hf, hb, ctab, stab)


def _long_conv_rows(r0, n, p_ref, x0_ref, spec, bias_ref, o_ref, pb_scr, yr_scr, ys_scr):
    kr_ref, ki_ref, kn_ref, c_ref, s_ref = spec
    tk = min(n, 512)
    tc = p_ref.shape[1]
    seq = slice(r0, r0 + n)
    chunks = [slice(k * tk, (k + 1) * tk) for k in range(n // tk)]
    pb_scr[seq, :] = p_ref[seq, :].astype(BF16)
    for rows in chunks:
        dst = slice(r0 + rows.start, r0 + rows.stop)
        xr = _dot(c_ref[rows, :], pb_scr[seq, :])
        xs = _dot(s_ref[rows, :], pb_scr[seq, :])
        kr = kr_ref[rows, :]
        ki = ki_ref[rows, :]
        yr_scr[dst, :] = (xr * kr + xs * ki).astype(BF16)
        ys_scr[dst, :] = (xs * kr - xr * ki).astype(BF16)
    nyq = jnp.sum(p_ref[seq, :] * _alt_sign(n, tc), axis=0, keepdims=True) * kn_ref[0:1, :]
    alt_nyq = _alt_sign(tk, tc) * nyq
    for rows in chunks:
        dst = slice(r0 + rows.start, r0 + rows.stop)
        y = _dot(c_ref[rows, :], yr_scr[seq, :]) + _dot(s_ref[rows, :], ys_scr[seq, :]) + alt_nyq
        o_ref[dst, :] = x0_ref[dst, :] * (y + p_ref[dst, :] * bias_ref[...])


def _long_conv_kernel(p_ref, x0_ref, bias_ref, *refs):
    lat, ctx, (o_ref, pb_scr, yr_scr, ys_scr) = refs[0:5], refs[5:10], refs[10:]
    _long_conv_rows(0, L, p_ref, x0_ref, lat, bias_ref, o_ref, pb_scr, yr_scr, ys_scr)
    _long_conv_rows(L, CTX, p_ref, x0_ref, ctx, bias_ref, o_ref, pb_scr, yr_scr, ys_scr)


def _long_conv(p2, x02, bias, spec_lat, spec_ctx):
    nc = D // TC
    strip = pl.BlockSpec((ROWS // B, TC), lambda j: (0, (j % B) * nc + j // B))
    ch = lambda j: (0, j // B)

    def spec_specs(n):
        return [pl.BlockSpec((n, TC), ch), pl.BlockSpec((n, TC), ch), pl.BlockSpec((B, TC), ch),
                _resident((n, n)), _resident((n, n))]

    return pl.pallas_call(
        _long_conv_kernel,
        grid=(B * D // TC,),
        scratch_shapes=[pltpu.VMEM((ROWS // B, TC), BF16)] * 3,
        in_specs=[strip, strip, pl.BlockSpec((1, TC), ch)] + spec_specs(L) + spec_specs(CTX),
        out_specs=strip,
        out_shape=jax.ShapeDtypeStruct((ROWS // B, B * D), F32),
        compiler_params=_cparams(("arbitrary",)),
        name="hyena_long_conv",
    )(p2, x02, bias, *spec_lat, *spec_ctx)


def _gelu_tanh(x):
    return x * (0.5 * (1.0 + jnp.tanh(math.sqrt(2.0 / math.pi) * (x + 0.044715 * (x * x * x)))))


def _rg_coeffs(xc, hd, wai_ref, bai_ref, lam_ref, a_scr, b_scr):
    sl = slice(hd * RG_BLOCK, (hd + 1) * RG_BLOCK)
    lam = lam_ref[:, sl]
    softplus_neg = jnp.maximum(-lam, 0.0) + jnp.log1p(jnp.exp(-jnp.abs(lam)))
    pre = _dot(xc.astype(BF16), wai_ref[hd]) + bai_ref[:, 2 * hd * RG_BLOCK:2 * (hd + 1) * RG_BLOCK]
    gates = _sigmoid(pre)
    a = jnp.exp(-RG_C * gates[:, 0:RG_BLOCK] * softplus_neg)
    a_scr[:, sl] = a
    b_scr[:, sl] = jnp.sqrt((1.0 - a) * (1.0 + a)) * gates[:, RG_BLOCK:2 * RG_BLOCK] * xc


def _scan_tile(a_scr, b_scr, h_scr, emit, reverse):
    steps = a_scr.shape[0] // B

    def body(k, h):
        t = steps - 1 - k if reverse else k
        r0 = pl.multiple_of(t * B, B)
        h = a_scr[pl.ds(r0, B), :] * h + b_scr[pl.ds(r0, B), :]
        emit(r0, h)
        return h

    h_scr[...] = lax.fori_loop(0, steps, body, h_scr[...], unroll=8)


def _rg_fwd_order(s):
    return jnp.where(s < NI_CTX, NI_LAT + s, s - NI_CTX)


def _rg_in_kernel(xp_ref, x_ref, xn_ref, mod_ref, g_ref, w_ref, b_ref, cw_ref, cb_ref,
                  wai_ref, bai_ref, lam_ref, xc_ref, gate_ref, hs_ref, a_scr, b_scr, h_scr):
    s = pl.program_id(0)
    t = _rg_fwd_order(s)
    xa = jnp.concatenate([xp_ref[...], x_ref[...], xn_ref[...]], axis=0)
    h = _modulate(_rmsnorm(xa, g_ref[2:3]), _mod(mod_ref, 3), _mod(mod_ref, 4)).astype(BF16)
    keep_lo = jnp.where(_seq_first(t), 0.0, 1.0).astype(F32)
    keep_hi = jnp.where(_seq_last(t), 0.0, 1.0).astype(F32)
    for hd in range(RG_HEADS):
        sl = slice(hd * RG_BLOCK, (hd + 1) * RG_BLOCK)
        rec = slice(D + hd * RG_BLOCK, D + (hd + 1) * RG_BLOCK)
        gate_ref[:, sl] = (_dot(h[B:TI + B], w_ref[:, sl]) + b_ref[:, sl]).astype(BF16)
        u = _dot(h, w_ref[:, rec]) + b_ref[:, rec]
        taps = (jnp.concatenate([u[0:B] * keep_lo, u[B:TI]], axis=0),
                u[B:TI + B],
                jnp.concatenate([u[2 * B:TI + B], u[TI + B:TI + 2 * B] * keep_hi], axis=0),
                jnp.concatenate([u[3 * B:TI + B], u[TI + B:TI + 3 * B] * keep_hi], axis=0))
        xc = cb_ref[:, sl]
        for k, tap in enumerate(taps):
            xc = xc + cw_ref[k:k + 1, sl] * tap
        xc_ref[:, sl] = xc
        _rg_coeffs(xc, hd, wai_ref, bai_ref, lam_ref, a_scr, b_scr)

    @pl.when(s == 0)
    def _():
        h_scr[...] = jnp.zeros((B, D), F32)

    def emit(r0, hv):
        hs_ref[pl.ds(r0, B), :] = hv

    _scan_tile(a_scr, b_scr, h_scr, emit, reverse=False)


def _rg_gate_specs():
    return [_resident((RG_HEADS, RG_BLOCK, 2 * RG_BLOCK)), _resident((1, 2 * D)), _resident((1, D))]


def _rg_in(x, mods, g, w_in, b_in, conv_w, conv_b, wai, bai, lam):
    order = _rg_fwd_order
    mod_spec = pl.BlockSpec((None, B, N_MOD * D),
                            lambda s: (jnp.where(order(s) < NI_LAT, 0, 1), 0, 0))
    out_spec = pl.BlockSpec((TI, D), lambda s: (order(s), 0))
    return pl.pallas_call(
        _rg_in_kernel,
        grid=(NI,),
        in_specs=_halo_specs(order, B, 2 * B) + [
            mod_spec, _resident((6, D)), _resident((D, 2 * D)), _resident((1, 2 * D)),
            _resident((4, D)), _resident((1, D))] + _rg_gate_specs(),
        out_specs=[out_spec] * 3,
        out_shape=[jax.ShapeDtypeStruct((ROWS, D), F32), jax.ShapeDtypeStruct((ROWS, D), BF16),
                   jax.ShapeDtypeStruct((ROWS, D), F32)],
        scratch_shapes=[pltpu.VMEM((TI, D), F32), pltpu.VMEM((TI, D), F32),
                        pltpu.VMEM((B, D), F32)],
        compiler_params=_cparams(("arbitrary",)),
        name="rglru_in_fwd_scan",
    )(x, x, x, mods, g, w_in, b_in, conv_w, conv_b, wai, bai, lam)


def _rg_tile_coeffs(xc_ref, wai_ref, bai_ref, lam_ref, a_scr, b_scr):
    for hd in range(RG_HEADS):
        xc = xc_ref[:, hd * RG_BLOCK:(hd + 1) * RG_BLOCK]
        _rg_coeffs(xc, hd, wai_ref, bai_ref, lam_ref, a_scr, b_scr)


def _rg_ctx_bwd_kernel(xc_ref, wai_ref, bai_ref, lam_ref, h_ref, a_scr, b_scr, h_scr):
    _rg_tile_coeffs(xc_ref, wai_ref, bai_ref, lam_ref, a_scr, b_scr)

    @pl.when(pl.program_id(0) == 0)
    def _():
        h_scr[...] = jnp.zeros((B, D), F32)

    _scan_tile(a_scr, b_scr, h_scr, lambda r0, hv: None, reverse=True)
    h_ref[...] = h_scr[...]


def _rg_ctx_bwd(xc, wai, bai, lam):
    return pl.pallas_call(
        _rg_ctx_bwd_kernel,
        grid=(NT_CTX,),
        in_specs=[pl.BlockSpec((TM, D), lambda s: (NT - 1 - s, 0))] + _rg_gate_specs(),
        out_specs=pl.BlockSpec((B, D), lambda s: (0, 0)),
        out_shape=jax.ShapeDtypeStruct((B, D), F32),
        scratch_shapes=[pltpu.VMEM((TM, D), F32), pltpu.VMEM((TM, D), F32),
                        pltpu.VMEM((B, D), F32)],
        compiler_params=_cparams(("arbitrary",)),
        name="rglru_ctx_bwd_scan",
    )(xc, wai, bai, lam)


def _rg_out_ffn_kernel(xc_ref, gate_ref, hs_ref, x_ref, h0_ref, mod_ref, g_ref,
                       wai_ref, bai_ref, lam_ref, wo_ref, bo_ref, w1, w3, w2,
                       o_ref, a_scr, b_scr, hb_scr, h_scr, rows_scr):
    s = pl.program_id(0)

    def head_gates(hd):
        xc = xc_ref[:, hd * RG_BLOCK:(hd + 1) * RG_BLOCK]
        _rg_coeffs(xc, hd, wai_ref, bai_ref, lam_ref, a_scr, b_scr)

    @pl.when(s == 0)
    def _():
        h_scr[...] = h0_ref[...]
        for hd in range(RG_HEADS):
            head_gates(hd)

    @pl.when(s > 0)
    def _():
        y = (hs_ref[...] + hb_scr[...]) * _gelu_tanh(gate_ref[...].astype(F32))
        res = _mix_ffn_core(
            y, x_ref[...], mod_ref, g_ref, wo_ref, bo_ref, w1, w3, w2, chunks=F_QUARTERS,
            side_work=[functools.partial(head_gates, hd) for hd in range(RG_HEADS)])
        _put_rows(rows_scr, res)
        for b in range(B):
            o_ref[b] = _get_batch(rows_scr, b)

    @pl.when(s < NT_LAT)
    def _():
        def emit(r0, hv):
            hb_scr[pl.ds(r0, B), :] = hv

        _scan_tile(a_scr, b_scr, h_scr, emit, reverse=True)


def _rg_out_ffn(xc, gate, hs, x, h0, mods, g, wai, bai, lam, w_out, b_out, w1, w3, w2):
    scan_tile = lambda s: (jnp.maximum(NT_LAT - 1 - s, 0), 0)
    out_tile = lambda s: jnp.minimum(NT_LAT - s, NT_LAT - 1)
    prev = pl.BlockSpec((TM, D), lambda s: (out_tile(s), 0))
    return pl.pallas_call(
        _with_staged_weights(_rg_out_ffn_kernel, 1, 1, n_in=15),
        grid=(NT_LAT + 1,),
        in_specs=[pl.BlockSpec((TM, D), scan_tile), prev, prev, prev, _resident((B, D)),
                  pl.BlockSpec((None, B, N_MOD * D), lambda s: (0, 0, 0)), _resident((6, D))]
        + _rg_gate_specs() + [_resident((D, D)), _resident((1, D))] + _WEIGHT_SPECS,
        out_specs=pl.BlockSpec((B, TS, D), lambda s: (0, out_tile(s), 0)),
        out_shape=jax.ShapeDtypeStruct((B, L, D), F32),
        scratch_shapes=[pltpu.VMEM((TM, D), F32), pltpu.VMEM((TM, D), F32),
                        pltpu.VMEM((TM, D), F32), pltpu.VMEM((B, D), F32), _ROWS_SCRATCH]
        + _WEIGHT_SCRATCH,
        compiler_params=_cparams(("arbitrary",)),
        name="rglru_out_ffn",
    )(xc, gate, hs, x, h0, mods, g, wai, bai, lam, w_out, b_out, w1, w3, w2)


def _grid_pos():
    rows = L // GRID_W
    quarter = D // 4
    omega = POS_BASE ** (-np.arange(quarter) / quarter)

    def emb(q):
        ang = q[:, None] * omega[None]
        return jnp.asarray(np.concatenate([np.sin(ang), np.cos(ang)], axis=-1), F32)

    row_code = jnp.repeat(emb(np.arange(rows)), GRID_W, axis=0)
    col_code = jnp.tile(emb(np.arange(GRID_W)), (rows, 1))
    return jnp.concatenate([row_code, col_code], axis=-1)


def kernel(x, c, ctx, c_ctx, ada_w, ada_b, norm_g, ffn_w1, ffn_w3, ffn_w2, hy_w_in, hy_b_in, hy_conv_w, hy_conv_b, hy_fw0, hy_fb0, hy_fw1, hy_fb1, hy_fw2, hy_fb2, hy_freq, hy_fwout, hy_filt_bias, hy_w_out, hy_b_out, rg_w_in, rg_b_in, rg_conv_w, rg_conv_b, rg_wa, rg_ba, rg_wi, rg_bi, rg_lam, rg_w_out, rg_b_out):
    cc = jnp.concatenate([c, jnp.broadcast_to(c_ctx[None], (B, D))], axis=0)
    mods = _mods(cc, ada_w, ada_b).reshape(DEPTH, 2, B, N_MOD * D)
    w1, w3, w2 = ffn_w1, ffn_w3, ffn_w2

    g = norm_g[0]
    xs = _ffn_first(x, ctx, _grid_pos(), mods[0], g, w1, w3, w2)
    x0, p = _hy_in(xs, mods[0], g, hy_w_in[0].astype(BF16), hy_b_in[0][None],
                   hy_conv_w[0], hy_conv_b[0][None])
    fparams = (hy_fw0[0], hy_fb0[0], hy_fw1[0], hy_fb1[0], hy_fw2[0], hy_fb2[0],
               hy_freq[0], hy_fwout[0])
    specs = []
    for n in (L, CTX):
        hf, hb = _filters(n, *fparams)
        ctab, stab = (tab.astype(BF16) for tab in _dft_tables(n))
        specs.append(list(_spectrum(hf, hb, ctab, stab)) + [ctab, stab])
    y = _long_conv(p, x0, hy_filt_bias[0][None], *specs)
    xs = _hy_out_ffn(y, xs, mods[0], g, hy_w_out[0].astype(BF16), hy_b_out[0][None], w1, w3, w2)

    g = norm_g[1]
    xs = _ffn_plain(xs, mods[1], g, w1, w3, w2, layer=1)
    wai = jnp.concatenate([rg_wa[0], rg_wi[0]], axis=-1).astype(BF16)
    per_head = lambda v: v.reshape(2, RG_HEADS, RG_BLOCK)
    bai = jnp.concatenate([per_head(rg_ba[0]), per_head(rg_bi[0])], axis=-1).reshape(2, 1, 2 * D)
    lam = rg_lam[0][:, None, :]
    xc, gate, hs = _rg_in(xs, mods[1], g, rg_w_in[0].astype(BF16), rg_b_in[0][None],
                          rg_conv_w[0], rg_conv_b[0][None], wai[0], bai[0], lam[0])
    h_ctx = _rg_ctx_bwd(xc, wai[1], bai[1], lam[1])
    return _rg_out_ffn(xc, gate, hs, xs, h_ctx, mods[1], g, wai[1], bai[1], lam[1],
                       rg_w_out[0].astype(BF16), rg_b_out[0][None], w1, w3, w2)
```

```python
import functools
import math

import jax
import jax.numpy as jnp
import numpy as np
from jax import lax
from jax.experimental import pallas as pl
from jax.experimental.pallas import tpu as pltpu

F32 = jnp.float32
BF16 = jnp.bfloat16

D = 1024
B = 8
LANES = 128
MXU_N = 256
L = 2048
CTX = 256
DEPTH = 2
GRID_W = 64
D_FF = 2816
N_MOD = 9
MACARON = 0.5
NORM_EPS = 1e-6
POS_BASE = 10000.0
HY_EMB = 33
HY_BANDS = 16
HY_HID = 64
HY_FAST_DECAY = 0.3
HY_SLOW_DECAY = 1.5
HY_DECAY_TARGET = 1e-2
RG_HEADS = 4
RG_BLOCK = D // RG_HEADS
RG_C = 8.0

ROWS_LAT = L * B
ROWS_CTX = CTX * B
ROWS = ROWS_LAT + ROWS_CTX
TM = 512
TS = TM // B
NT_LAT = ROWS_LAT // TM
NT_CTX = ROWS_CTX // TM
NT = NT_LAT + NT_CTX
TI = 1024
NI_LAT = ROWS_LAT // TI
NI_CTX = ROWS_CTX // TI
NI = NI_LAT + NI_CTX
F_CHUNKS = ((0, 1536), (1536, D_FF))
F_QUARTERS = ((0, 768), (768, 1536), (1536, 2304), (2304, D_FF))
TC = 256
VMEM_LIMIT = 58 * 1024 * 1024


def _cparams(sem):
    return pltpu.CompilerParams(dimension_semantics=sem, vmem_limit_bytes=VMEM_LIMIT)


def _resident(shape):
    nd = len(shape)
    return pl.BlockSpec(shape, lambda *_: (0,) * nd, pipeline_mode=pl.Buffered(1))


def _split(a):
    hi = a.astype(BF16)
    lo = (a - hi.astype(F32)).astype(BF16)
    return hi, lo


def _dot(a, b):
    return jnp.dot(a, b, preferred_element_type=F32)


def _dot3(a, b):
    ah, al = _split(a)
    bh, bl = _split(b)
    return _dot(ah, bh) + _dot(ah, bl) + _dot(al, bh)


def _rmsnorm(x, g):
    ms = jnp.mean(x * x, axis=-1, keepdims=True)
    return x * lax.rsqrt(ms + NORM_EPS) * g


def _mod(mod_ref, k):
    return mod_ref[:, k * D:(k + 1) * D]


def _per_batch(x, fn):
    rows = x.shape[0]
    return fn(x.reshape(rows // B, B, x.shape[1])).reshape(rows, x.shape[1])


def _modulate(xn, shift8, scale8):
    return _per_batch(xn, lambda v: v * (1.0 + scale8)[None] + shift8[None])


def _gated(z, gate8):
    return _per_batch(z, lambda v: v * gate8[None])


def _sigmoid(x):
    return jax.nn.sigmoid(x)


def _ffn(x, mod_ref, k0, g_pre, g_post, w1_ref, w3_ref, w2_ref, chunks=F_CHUNKS, side_work=()):
    h = _modulate(_rmsnorm(x, g_pre), _mod(mod_ref, k0), _mod(mod_ref, k0 + 1)).astype(BF16)
    y = None
    for idx, (c0, c1) in enumerate(chunks):
        a = _dot(h, w1_ref[:, c0:c1])
        b = _dot(h, w3_ref[:, c0:c1])
        if idx < len(side_work):
            side_work[idx]()
        act = (a * _sigmoid(a) * b).astype(BF16)
        part = _dot(act, w2_ref[c0:c1, :])
        y = part if y is None else y + part
    return x + MACARON * _gated(_rmsnorm(y, g_post), _mod(mod_ref, k0 + 2))


def _mods_kernel(cc_ref, w_ref, b_ref, o_ref):
    a = cc_ref[...]
    a = a * _sigmoid(a)
    o_ref[...] = _dot3(a, w_ref[...]) + b_ref[...]


def _mods(cc, ada_w, ada_b):
    tn = 1024
    n = N_MOD * D
    return pl.pallas_call(
        _mods_kernel,
        grid=(DEPTH, n // tn),
        in_specs=[
            pl.BlockSpec((2 * B, D), lambda i, j: (0, 0)),
            pl.BlockSpec((None, D, tn), lambda i, j: (i, 0, j)),
            pl.BlockSpec((None, 1, tn), lambda i, j: (i, 0, j)),
        ],
        out_specs=pl.BlockSpec((None, 2 * B, tn), lambda i, j: (i, 0, j)),
        out_shape=jax.ShapeDtypeStruct((DEPTH, 2 * B, n), F32),
        compiler_params=_cparams(("arbitrary", "arbitrary")),
        name="ada_mods",
    )(cc, ada_w, ada_b.reshape(DEPTH, 1, n))


def _mod_spec(nt_lat=NT_LAT):
    return pl.BlockSpec((None, B, N_MOD * D), lambda t: (jnp.where(t < nt_lat, 0, 1), 0, 0))


def _row_spec(width=D):
    return pl.BlockSpec((TM, width), lambda t: (t, 0))


def _wide_spec():
    return pl.BlockSpec((TS, B * D), lambda t: (t, 0))


W_CHUNKS = 8
W_SLOTS = 3
_WEIGHT_SPECS = [pl.BlockSpec(memory_space=pl.ANY)] * 3
_WEIGHT_SCRATCH = [pltpu.VMEM((D, D_FF), BF16), pltpu.VMEM((D, D_FF), BF16),
                   pltpu.VMEM((D_FF, D), BF16)]


def _stage_weights(srcs, dsts):
    chunks = [(src, dst, r0, dst.shape[0] // W_CHUNKS) for src, dst in zip(srcs, dsts)
              for r0 in range(0, dst.shape[0], dst.shape[0] // W_CHUNKS)]

    def run(stage_up, stage_down, sems):
        def copy(k):
            src, dst, r0, rows = chunks[k]
            stage = stage_up if dst.shape[1] == D_FF else stage_down
            return pltpu.make_async_copy(src.at[pl.ds(r0, rows), :], stage.at[k % W_SLOTS],
                                         sems.at[k % W_SLOTS])

        for k in range(W_SLOTS - 1):
            copy(k).start()
        for k, (_, dst, r0, rows) in enumerate(chunks):
            if k + W_SLOTS - 1 < len(chunks):
                copy(k + W_SLOTS - 1).start()
            copy(k).wait()
            stage = stage_up if dst.shape[1] == D_FF else stage_down
            dst[pl.ds(r0, rows), :] = stage[k % W_SLOTS].astype(BF16)

    pl.run_scoped(run, pltpu.VMEM((W_SLOTS, D // W_CHUNKS, D_FF), F32),
                  pltpu.VMEM((W_SLOTS, D_FF // W_CHUNKS, D), F32),
                  pltpu.SemaphoreType.DMA((W_SLOTS,)))


def _with_staged_weights(body, layer, half, n_in):
    def kernel(*refs):
        ins, rest = refs[:n_in], refs[n_in:]
        w1, w3, w2 = rest[-3:]

        @pl.when(pl.program_id(0) == 0)
        def _():
            _stage_weights([w.at[layer, half] for w in ins[-3:]], (w1, w3, w2))

        body(*ins[:-3], w1, w3, w2, *rest[:-3])

    return kernel


def _rows_of_batch(b):
    return pl.ds(b, TS, stride=B)


_ROWS_SCRATCH = pltpu.VMEM((D // LANES, TM, LANES), F32)


def _put_batch(rows_scr, b, val):
    for j in range(D // LANES):
        rows_scr[j, _rows_of_batch(b), :] = val[:, j * LANES:(j + 1) * LANES]


def _get_batch(rows_scr, b):
    return jnp.concatenate([rows_scr[j, _rows_of_batch(b), :] for j in range(D // LANES)], axis=1)


def _put_rows(rows_scr, val):
    for j in range(D // LANES):
        rows_scr[j] = val[:, j * LANES:(j + 1) * LANES]


def _get_rows(rows_scr):
    return jnp.concatenate([rows_scr[j] for j in range(D // LANES)], axis=1)


def _ffn_first_kernel(x_ref, ctx_ref, pos_ref, mod_ref, g_ref, w1, w3, w2, o_ref, rows_scr):
    is_latent = pl.program_id(0) < NT_LAT
    for b in range(B):
        _put_batch(rows_scr, b, jnp.where(is_latent, x_ref[b] + pos_ref[...], ctx_ref[b]))
    o_ref[...] = _ffn(_get_rows(rows_scr), mod_ref, 0, g_ref[0:1], g_ref[1:2], w1, w3, w2)


def _ffn_kernel(x_ref, mod_ref, g_ref, w1, w3, w2, o_ref):
    for r in range(x_ref.shape[0] // TM):
        rows = slice(r * TM, (r + 1) * TM)
        o_ref[rows, :] = _ffn(x_ref[rows, :], mod_ref, 0, g_ref[0:1], g_ref[1:2], w1, w3, w2)


def _ffn_first(x, ctx, pos, mods, g, w1, w3, w2):
    lat = lambda t: jnp.minimum(t, NT_LAT - 1)
    return pl.pallas_call(
        _with_staged_weights(_ffn_first_kernel, 0, 0, n_in=8),
        grid=(NT,),
        in_specs=[pl.BlockSpec((B, TS, D), lambda t: (0, lat(t), 0)),
                  pl.BlockSpec((B, TS, D), lambda t: (0, jnp.maximum(t - NT_LAT, 0), 0)),
                  pl.BlockSpec((TS, D), lambda t: (lat(t), 0)),
                  _mod_spec(), _resident((6, D))] + _WEIGHT_SPECS,
        out_specs=_row_spec(),
        out_shape=jax.ShapeDtypeStruct((ROWS, D), F32),
        scratch_shapes=[_ROWS_SCRATCH] + _WEIGHT_SCRATCH,
        compiler_params=_cparams(("arbitrary",)),
        name="ffn_first",
    )(x, ctx, pos, mods, g, w1, w3, w2)


def _ffn_plain(x, mods, g, w1, w3, w2, layer):
    return pl.pallas_call(
        _with_staged_weights(_ffn_kernel, layer, 0, n_in=6),
        grid=(NT // 2,),
        in_specs=[pl.BlockSpec((2 * TM, D), lambda t: (t, 0)),
                  pl.BlockSpec((None, B, N_MOD * D),
                               lambda t: (jnp.where(t < NT_LAT // 2, 0, 1), 0, 0)),
                  _resident((6, D))] + _WEIGHT_SPECS,
        out_specs=pl.BlockSpec((2 * TM, D), lambda t: (t, 0)),
        out_shape=jax.ShapeDtypeStruct((ROWS, D), F32),
        scratch_shapes=_WEIGHT_SCRATCH,
        compiler_params=_cparams(("arbitrary",)),
        name="ffn_pre",
    )(x, mods, g, w1, w3, w2)


def _mix_ffn_core(y, x, mod_ref, g_ref, wo_ref, bo_ref, w1, w3, w2, **ffn_kwargs):
    z = _dot(y.astype(BF16), wo_ref[...]) + bo_ref[...]
    x = x + _gated(_rmsnorm(z, g_ref[3:4]), _mod(mod_ref, 5))
    return _ffn(x, mod_ref, 6, g_ref[4:5], g_ref[5:6], w1, w3, w2, **ffn_kwargs)


def _hy_out_ffn_kernel(ylo_ref, yhi_ref, x_ref, mod_ref, g_ref, wo_ref, bo_ref, w1, w3, w2,
                       o_ref, rows_scr):
    halves = (ylo_ref, yhi_ref)
    for b in range(B):
        for j in range(D // LANES):
            col = (b * (D // TC) + j // 2) * LANES
            rows_scr[j, _rows_of_batch(b), :] = halves[j % 2][:, col:col + LANES]
    o_ref[...] = _mix_ffn_core(_get_rows(rows_scr), x_ref[...], mod_ref, g_ref, wo_ref, bo_ref,
                               w1, w3, w2)


def _hy_out_ffn(y_lo, y_hi, x, mods, g, w_out, b_out, w1, w3, w2):
    half = pl.BlockSpec((TS, B * D // 2), lambda t: (t, 0))
    return pl.pallas_call(
        _with_staged_weights(_hy_out_ffn_kernel, 0, 1, n_in=10),
        grid=(NT,),
        in_specs=[half, half, _row_spec(), _mod_spec(), _resident((6, D)), _resident((D, D)),
                  _resident((1, D))] + _WEIGHT_SPECS,
        out_specs=_row_spec(),
        out_shape=jax.ShapeDtypeStruct((ROWS, D), F32),
        scratch_shapes=[_ROWS_SCRATCH] + _WEIGHT_SCRATCH,
        compiler_params=_cparams(("arbitrary",)),
        name="hyena_out_ffn",
    )(y_lo, y_hi, x, mods, g, w_out, b_out, w1, w3, w2)


def _seq_first(t):
    return jnp.logical_or(t == 0, t == NI_LAT)


def _seq_last(t):
    return jnp.logical_or(t == NI_LAT - 1, t == NI - 1)


def _halo_specs(order, lo_rows, hi_rows):
    nlo = ROWS // lo_rows
    nhi = ROWS // hi_rows
    return [
        pl.BlockSpec((lo_rows, D), lambda s: (jnp.maximum(order(s) * (TI // lo_rows) - 1, 0), 0)),
        pl.BlockSpec((TI, D), lambda s: (order(s), 0)),
        pl.BlockSpec((hi_rows, D),
                     lambda s: (jnp.minimum((order(s) + 1) * (TI // hi_rows), nhi - 1), 0)),
    ]


def _hy_in_kernel(xp_ref, x_ref, xn_ref, mod_ref, g_ref, w_ref, b_ref, cw_ref, cb_ref,
                  x0_ref, p_ref, x0_scr, p_scr):
    t = pl.program_id(0)
    xa = jnp.concatenate([xp_ref[...], x_ref[...], xn_ref[...]], axis=0)
    h = _modulate(_rmsnorm(xa, g_ref[2:3]), _mod(mod_ref, 3), _mod(mod_ref, 4)).astype(BF16)
    keep_lo = jnp.where(_seq_first(t), 0.0, 1.0).astype(F32)
    keep_hi = jnp.where(_seq_last(t), 0.0, 1.0).astype(F32)

    def conv_cols(c0):
        cols = slice(c0, c0 + MXU_N)
        u = _dot(h, w_ref[:, cols]) + b_ref[:, cols]
        lo = jnp.concatenate([u[0:B] * keep_lo, u[B:TI]], axis=0)
        hi = jnp.concatenate([u[2 * B:TI + B], u[TI + B:TI + 2 * B] * keep_hi], axis=0)
        return (cb_ref[:, cols] + cw_ref[0:1, cols] * lo + cw_ref[1:2, cols] * u[B:TI + B]
                + cw_ref[2:3, cols] * hi)

    slabs = MXU_N // LANES
    for c in range(D // MXU_N):
        c0 = c * MXU_N
        vals = (conv_cols(c0), conv_cols(D + c0) * conv_cols(2 * D + c0))
        for out_ref, scr, val in zip((x0_ref, p_ref), (x0_scr, p_scr), vals):
            for j in range(slabs):
                scr[c * slabs + j] = val[:, j * LANES:(j + 1) * LANES]
            for b in range(B):
                out_ref[:, b * D + c0:b * D + c0 + MXU_N] = jnp.concatenate(
                    [scr[c * slabs + j, pl.ds(b, TI // B, stride=B), :] for j in range(slabs)], axis=1)


def _hy_in(x, mods, g, w_in, b_in, conv_w, conv_b):
    wide = pl.BlockSpec((TI // B, B * D), lambda t: (t, 0))
    rows_scratch = pltpu.VMEM((D // LANES, TI, LANES), F32)
    return pl.pallas_call(
        _hy_in_kernel,
        grid=(NI,),
        in_specs=_halo_specs(lambda s: s, B, B) + [
            _mod_spec(NI_LAT), _resident((6, D)), _resident((D, 3 * D)), _resident((1, 3 * D)),
            _resident((3, 3 * D)), _resident((1, 3 * D))],
        out_specs=[wide, wide],
        out_shape=[jax.ShapeDtypeStruct((ROWS // B, B * D), F32)] * 2,
        scratch_shapes=[rows_scratch, rows_scratch],
        compiler_params=_cparams(("arbitrary",)),
        name="hyena_in",
    )(x, x, x, mods, g, w_in, b_in, conv_w, conv_b)


def _filter_kernel(z_ref, fw0, fb0, fw1, fb1, fw2, fb2, freq, fwout, deltas, hf_ref, hb_ref):
    z = z_ref[...]
    h = jnp.sin(freq[0:1] * (_dot3(z, fw0[...]) + fb0[...]))
    h = jnp.sin(freq[1:2] * (_dot3(h, fw1[...]) + fb1[...]))
    h = jnp.sin(freq[2:3] * (_dot3(h, fw2[...]) + fb2[...]))
    filt = _dot3(h, fwout[...])
    decay = jnp.exp(-z[:, 0:1] * deltas[...])
    hf_ref[...] = filt[:, 0:D] * decay
    hb_ref[...] = filt[:, D:2 * D] * decay


def _filters(n, fw0, fb0, fw1, fb1, fw2, fb2, freq, fwout):
    t = np.linspace(0.0, 1.0, n)[:, None]
    bands = np.linspace(1e-4, HY_BANDS - 1, HY_BANDS)[None]
    phase = bands * (2.0 * math.pi * np.arange(n)[:, None] / n)
    zp = np.zeros((n, LANES), np.float32)
    zp[:, :HY_EMB] = np.concatenate([t, np.cos(phase), -np.sin(phase)], axis=-1)
    zp = np.concatenate([zp[0::2], zp[1::2]], axis=0)
    fw0p = jnp.zeros((LANES, HY_HID), F32).at[:HY_EMB].set(fw0)
    max_decay = math.log(HY_DECAY_TARGET) / HY_FAST_DECAY
    min_decay = math.log(HY_DECAY_TARGET) / HY_SLOW_DECAY
    deltas = np.abs(np.linspace(min_decay, max_decay, D))[None].astype(np.float32)
    tl = 256
    row = lambda i: (i, 0)
    return pl.pallas_call(
        _filter_kernel,
        grid=(n // tl,),
        in_specs=[pl.BlockSpec((tl, LANES), row), _resident((LANES, HY_HID)), _resident((1, HY_HID)),
                  _resident((HY_HID, HY_HID)), _resident((1, HY_HID)),
                  _resident((HY_HID, HY_HID)), _resident((1, HY_HID)),
                  _resident((3, HY_HID)), _resident((HY_HID, 2 * D)), _resident((1, D))],
        out_specs=[pl.BlockSpec((tl, D), row)] * 2,
        out_shape=[jax.ShapeDtypeStruct((n, D), F32)] * 2,
        compiler_params=_cparams(("arbitrary",)),
        name="hyena_filter",
    )(zp, fw0p, fb0[None], fw1, fb1[None], fw2, fb2[None], freq, fwout, deltas)


def _dft_tables(n):
    q = 1 << (int(math.log2(n)) // 2 + 1)
    m = np.arange(n, dtype=np.int64)[None, :]

    def thin(k):
        ang = ((k[:, None] * m) % (2 * n)) * (math.pi / n)
        return jnp.asarray(np.cos(ang), F32), jnp.asarray(np.sin(ang), F32)

    ca, sa = (v[:, None, :] for v in thin(q * np.arange(n // q, dtype=np.int64)))
    cb, sb = (v[None, :, :] for v in thin(np.arange(q, dtype=np.int64)))
    return (ca * cb - sa * sb).reshape(n, n), (sa * cb + ca * sb).reshape(n, n)


def _alt_sign(rows, cols):
    r = lax.broadcasted_iota(jnp.int32, (rows, cols), 0)
    return (1 - 2 * (r & 1)).astype(F32)


def _twiddles(h):
    ang = np.arange(h)[:, None] * (math.pi / (2 * h))
    return tuple(jnp.broadcast_to(jnp.asarray(f(ang), F32), (h, TC)) for f in (np.cos, np.sin))


def _spectrum_kernel(hf_ref, hb_ref, c_ref, s_ref, wc_ref, ws_ref,
                     kar_ref, kai_ref, kbr_ref, kbi_ref, kn_ref, *, h):
    hf = hf_ref[...]
    row = lax.broadcasted_iota(jnp.int32, hf.shape, 0)
    hb = jnp.where(row == 0, 0.0, hb_ref[...])
    cos_part = hf + hb
    sin_part = hb - hf
    ce, co = cos_part[0:h].astype(BF16), cos_part[h:2 * h].astype(BF16)
    se, so = sin_part[0:h].astype(BF16), sin_part[h:2 * h].astype(BF16)
    c, s = c_ref[...], s_ref[...]
    wc, ws = wc_ref[...], ws_ref[...]
    r_even = _dot(c, ce)
    r_odd = wc * _dot(c, co) - ws * _dot(s, co)
    i_even = _dot(s, se)
    i_odd = wc * _dot(s, so) + ws * _dot(c, so)
    n_fft = 4 * h
    k = lax.broadcasted_iota(jnp.int32, (h, hf.shape[1]), 0)
    scale = jnp.where(k == 0, 1.0 / n_fft, 2.0 / n_fft)
    kar_ref[...] = (r_even + r_odd) * scale
    kai_ref[...] = (i_even + i_odd) * scale
    kbr_ref[...] = (r_even - r_odd) * scale
    kbi_ref[...] = (i_even - i_odd) * scale
    alt = _alt_sign(h, hf.shape[1])
    mid_r = jnp.sum(cos_part[0:h] * alt, axis=0, keepdims=True) * (2.0 / n_fft)
    mid_i = jnp.sum(sin_part[h:2 * h] * alt, axis=0, keepdims=True) * (2.0 / n_fft)
    kn_ref[...] = jnp.concatenate([mid_r, mid_i, jnp.zeros((B - 2, hf.shape[1]), F32)], axis=0)


def _spectrum(hf, hb, ctab, stab, wc, ws):
    n = hf.shape[0]
    h = n // 2
    col = lambda j: (0, j)
    half = pl.BlockSpec((h, TC), col)
    return pl.pallas_call(
        functools.partial(_spectrum_kernel, h=h),
        grid=(D // TC,),
        in_specs=[pl.BlockSpec((n, TC), col)] * 2 + [_resident((h, h))] * 2
        + [_resident((h, TC))] * 2,
        out_specs=[half] * 4 + [pl.BlockSpec((B, TC), col)],
        out_shape=[jax.ShapeDtypeStruct((h, D), F32)] * 4 + [jax.ShapeDtypeStruct((B, D), F32)],
        compiler_params=_cparams(("arbitrary",)),
        name="hyena_spectrum",
    )(hf, hb, ctab, stab, wc, ws)


def _steps(refs, first, count):
    rows = pl.ds(first, count, stride=2)
    return jnp.concatenate([r[rows, :] for r in refs], axis=1)


def _long_conv_rows(t0, h, p_refs, x0_refs, spec, bias_ref, o_refs, scratch):
    kar_ref, kai_ref, kbr_ref, kbi_ref, kn_ref, c_ref, s_ref, wc_ref, ws_ref = spec
    peb_scr, pob_scr, sr_scr, ss_scr, dr_scr, ds_scr = scratch
    tk = min(h, 512)
    r0 = t0 // 2
    seq = slice(r0, r0 + h)
    chunks = [slice(k * tk, (k + 1) * tk) for k in range(h // tk)]
    p_even, p_odd = _steps(p_refs, t0, h), _steps(p_refs, t0 + 1, h)
    tc = p_even.shape[1]
    peb_scr[seq, :] = p_even.astype(BF16)
    pob_scr[seq, :] = p_odd.astype(BF16)
    for rows in chunks:
        dst = slice(r0 + rows.start, r0 + rows.stop)
        c, s = c_ref[rows, :], s_ref[rows, :]
        er, es = _dot(c, peb_scr[seq, :]), _dot(s, peb_scr[seq, :])
        odr, ods = _dot(c, pob_scr[seq, :]), _dot(s, pob_scr[seq, :])
        wc, ws = wc_ref[rows, :], ws_ref[rows, :]
        tr, ts = wc * odr - ws * ods, wc * ods + ws * odr
        ar, as_, br, bs = er + tr, es + ts, er - tr, es - ts
        kar, kai, kbr, kbi = kar_ref[rows, :], kai_ref[rows, :], kbr_ref[rows, :], kbi_ref[rows, :]
        yar, yas = ar * kar + as_ * kai, as_ * kar - ar * kai
        ybr, ybs = br * kbr + bs * kbi, bs * kbr - br * kbi
        sr_scr[dst, :] = (yar + ybr).astype(BF16)
        ss_scr[dst, :] = (yas + ybs).astype(BF16)
        dr, ds = yar - ybr, yas - ybs
        dr_scr[dst, :] = (dr * wc + ds * ws).astype(BF16)
        ds_scr[dst, :] = (ds * wc - dr * ws).astype(BF16)
    alt = _alt_sign(h, tc)
    xr = jnp.sum(p_even * alt, axis=0, keepdims=True)
    xs = jnp.sum(p_odd * alt, axis=0, keepdims=True)
    kr, ki = kn_ref[0:1, :], kn_ref[1:2, :]
    alt_chunk = _alt_sign(tk, tc)
    mids = (alt_chunk * (xr * kr + xs * ki), alt_chunk * (xs * kr - xr * ki))
    bias = bias_ref[...]
    for rows in chunks:
        c, s = c_ref[rows, :], s_ref[rows, :]
        ys = (_dot(c, sr_scr[seq, :]) + _dot(s, ss_scr[seq, :]),
              _dot(c, dr_scr[seq, :]) + _dot(s, ds_scr[seq, :]))
        for parity in range(2):
            first = t0 + 2 * rows.start + parity
            y = ys[parity] + mids[parity]
            out = _steps(x0_refs, first, tk) * (y + _steps(p_refs, first, tk) * bias)
            for q, o_ref in enumerate(o_refs):
                o_ref[pl.ds(first, tk, stride=2), :] = out[:, q * LANES:(q + 1) * LANES]


_N_SPEC = 9


def _long_conv_kernel(*refs):
    p_refs, x0_refs, bias_ref = refs[0:2], refs[2:4], refs[4]
    lat, ctx = refs[5:5 + _N_SPEC], refs[5 + _N_SPEC:5 + 2 * _N_SPEC]
    o_refs, scratch = refs[5 + 2 * _N_SPEC:7 + 2 * _N_SPEC], refs[7 + 2 * _N_SPEC:]
    _long_conv_rows(0, L // 2, p_refs, x0_refs, lat, bias_ref, o_refs, scratch)
    _long_conv_rows(L, CTX // 2, p_refs, x0_refs, ctx, bias_ref, o_refs, scratch)


def _long_conv(p2, x02, bias, spec_lat, spec_ctx):
    nc = D // TC
    halves = [pl.BlockSpec((ROWS // B, LANES),
                           functools.partial(lambda q, j: (0, 2 * ((j % B) * nc + j // B) + q), q))
              for q in range(TC // LANES)]
    ch = lambda j: (0, j // B)

    def spec_specs(h):
        return ([pl.BlockSpec((h, TC), ch)] * 4 + [pl.BlockSpec((B, TC), ch)]
                + [_resident((h, h))] * 2 + [_resident((h, TC))] * 2)

    half_out = pl.BlockSpec((ROWS // B, LANES), lambda j: (0, (j % B) * nc + j // B))
    return pl.pallas_call(
        _long_conv_kernel,
        grid=(B * D // TC,),
        scratch_shapes=[pltpu.VMEM((ROWS // (2 * B), TC), BF16)] * 6,
        in_specs=halves + halves + [pl.BlockSpec((1, TC), ch)] + spec_specs(L // 2)
        + spec_specs(CTX // 2),
        out_specs=[half_out] * (TC // LANES),
        out_shape=[jax.ShapeDtypeStruct((ROWS // B, B * D * LANES // TC), F32)] * (TC // LANES),
        compiler_params=_cparams(("arbitrary",)),
        name="hyena_long_conv",
    )(p2, p2, x02, x02, bias, *spec_lat, *spec_ctx)


def _gelu_tanh(x):
    return x * (0.5 * (1.0 + jnp.tanh(math.sqrt(2.0 / math.pi) * (x + 0.044715 * (x * x * x)))))


def _rg_coeffs(xc, hd, wai_ref, bai_ref, lam_ref, a_scr, b_scr):
    sl = slice(hd * RG_BLOCK, (hd + 1) * RG_BLOCK)
    lam = lam_ref[:, sl]
    softplus_neg = jnp.maximum(-lam, 0.0) + jnp.log1p(jnp.exp(-jnp.abs(lam)))
    pre = _dot(xc.astype(BF16), wai_ref[hd]) + bai_ref[:, 2 * hd * RG_BLOCK:2 * (hd + 1) * RG_BLOCK]
    gates = _sigmoid(pre)
    a = jnp.exp(-RG_C * gates[:, 0:RG_BLOCK] * softplus_neg)
    a_scr[:, sl] = a
    b_scr[:, sl] = jnp.sqrt((1.0 - a) * (1.0 + a)) * gates[:, RG_BLOCK:2 * RG_BLOCK] * xc


def _scan_tile(a_scr, b_scr, h_scr, emit, reverse):
    steps = a_scr.shape[0] // B

    def body(k, h):
        t = steps - 1 - k if reverse else k
        r0 = pl.multiple_of(t * B, B)
        h = a_scr[pl.ds(r0, B), :] * h + b_scr[pl.ds(r0, B), :]
        emit(r0, h)
        return h

    h_scr[...] = lax.fori_loop(0, steps, body, h_scr[...], unroll=8)


def _rg_fwd_order(s):
    return jnp.where(s < NI_CTX, NI_LAT + s, s - NI_CTX)


def _rg_in_kernel(xp_ref, x_ref, xn_ref, mod_ref, g_ref, w_ref, b_ref, cw_ref, cb_ref,
                  wai_ref, bai_ref, lam_ref, xc_ref, gate_ref, hs_ref, a_scr, b_scr, h_scr):
    s = pl.program_id(0)
    t = _rg_fwd_order(s)
    xa = jnp.concatenate([xp_ref[...], x_ref[...], xn_ref[...]], axis=0)
    h = _modulate(_rmsnorm(xa, g_ref[2:3]), _mod(mod_ref, 3), _mod(mod_ref, 4)).astype(BF16)
    keep_lo = jnp.where(_seq_first(t), 0.0, 1.0).astype(F32)
    keep_hi = jnp.where(_seq_last(t), 0.0, 1.0).astype(F32)
    for hd in range(RG_HEADS):
        sl = slice(hd * RG_BLOCK, (hd + 1) * RG_BLOCK)
        rec = slice(D + hd * RG_BLOCK, D + (hd + 1) * RG_BLOCK)
        gate_ref[:, sl] = (_dot(h[B:TI + B], w_ref[:, sl]) + b_ref[:, sl]).astype(BF16)
        u = _dot(h, w_ref[:, rec]) + b_ref[:, rec]
        taps = (jnp.concatenate([u[0:B] * keep_lo, u[B:TI]], axis=0),
                u[B:TI + B],
                jnp.concatenate([u[2 * B:TI + B], u[TI + B:TI + 2 * B] * keep_hi], axis=0),
                jnp.concatenate([u[3 * B:TI + B], u[TI + B:TI + 3 * B] * keep_hi], axis=0))
        xc = cb_ref[:, sl]
        for k, tap in enumerate(taps):
            xc = xc + cw_ref[k:k + 1, sl] * tap
        xc_ref[:, sl] = xc
        _rg_coeffs(xc, hd, wai_ref, bai_ref, lam_ref, a_scr, b_scr)

    @pl.when(s == 0)
    def _():
        h_scr[...] = jnp.zeros((B, D), F32)

    def emit(r0, hv):
        hs_ref[pl.ds(r0, B), :] = hv

    _scan_tile(a_scr, b_scr, h_scr, emit, reverse=False)


def _rg_gate_specs():
    return [_resident((RG_HEADS, RG_BLOCK, 2 * RG_BLOCK)), _resident((1, 2 * D)), _resident((1, D))]


def _rg_in(x, mods, g, w_in, b_in, conv_w, conv_b, wai, bai, lam):
    order = _rg_fwd_order
    mod_spec = pl.BlockSpec((None, B, N_MOD * D),
                            lambda s: (jnp.where(order(s) < NI_LAT, 0, 1), 0, 0))
    out_spec = pl.BlockSpec((TI, D), lambda s: (order(s), 0))
    return pl.pallas_call(
        _rg_in_kernel,
        grid=(NI,),
        in_specs=_halo_specs(order, B, 2 * B) + [
            mod_spec, _resident((6, D)), _resident((D, 2 * D)), _resident((1, 2 * D)),
            _resident((4, D)), _resident((1, D))] + _rg_gate_specs(),
        out_specs=[out_spec] * 3,
        out_shape=[jax.ShapeDtypeStruct((ROWS, D), F32), jax.ShapeDtypeStruct((ROWS, D), BF16),
                   jax.ShapeDtypeStruct((ROWS, D), F32)],
        scratch_shapes=[pltpu.VMEM((TI, D), F32), pltpu.VMEM((TI, D), F32),
                        pltpu.VMEM((B, D), F32)],
        compiler_params=_cparams(("arbitrary",)),
        name="rglru_in_fwd_scan",
    )(x, x, x, mods, g, w_in, b_in, conv_w, conv_b, wai, bai, lam)


def _rg_tile_coeffs(xc_ref, wai_ref, bai_ref, lam_ref, a_scr, b_scr):
    for hd in range(RG_HEADS):
        xc = xc_ref[:, hd * RG_BLOCK:(hd + 1) * RG_BLOCK]
        _rg_coeffs(xc, hd, wai_ref, bai_ref, lam_ref, a_scr, b_scr)


def _rg_ctx_bwd_kernel(xc_ref, wai_ref, bai_ref, lam_ref, h_ref, a_scr, b_scr, h_scr):
    _rg_tile_coeffs(xc_ref, wai_ref, bai_ref, lam_ref, a_scr, b_scr)

    @pl.when(pl.program_id(0) == 0)
    def _():
        h_scr[...] = jnp.zeros((B, D), F32)

    _scan_tile(a_scr, b_scr, h_scr, lambda r0, hv: None, reverse=True)
    h_ref[...] = h_scr[...]


def _rg_ctx_bwd(xc, wai, bai, lam):
    return pl.pallas_call(
        _rg_ctx_bwd_kernel,
        grid=(NT_CTX,),
        in_specs=[pl.BlockSpec((TM, D), lambda s: (NT - 1 - s, 0))] + _rg_gate_specs(),
        out_specs=pl.BlockSpec((B, D), lambda s: (0, 0)),
        out_shape=jax.ShapeDtypeStruct((B, D), F32),
        scratch_shapes=[pltpu.VMEM((TM, D), F32), pltpu.VMEM((TM, D), F32),
                        pltpu.VMEM((B, D), F32)],
        compiler_params=_cparams(("arbitrary",)),
        name="rglru_ctx_bwd_scan",
    )(xc, wai, bai, lam)


def _rg_out_ffn_kernel(xc_ref, gate_ref, hs_ref, x_ref, h0_ref, mod_ref, g_ref,
                       wai_ref, bai_ref, lam_ref, wo_ref, bo_ref, w1, w3, w2,
                       o_ref, a_scr, b_scr, hb_scr, h_scr, rows_scr):
    s = pl.program_id(0)

    def head_gates(hd):
        xc = xc_ref[:, hd * RG_BLOCK:(hd + 1) * RG_BLOCK]
        _rg_coeffs(xc, hd, wai_ref, bai_ref, lam_ref, a_scr, b_scr)

    @pl.when(s == 0)
    def _():
        h_scr[...] = h0_ref[...]
        for hd in range(RG_HEADS):
            head_gates(hd)

    @pl.when(s > 0)
    def _():
        y = (hs_ref[...] + hb_scr[...]) * _gelu_tanh(gate_ref[...].astype(F32))
        res = _mix_ffn_core(
            y, x_ref[...], mod_ref, g_ref, wo_ref, bo_ref, w1, w3, w2, chunks=F_QUARTERS,
            side_work=[functools.partial(head_gates, hd) for hd in range(RG_HEADS)])
        _put_rows(rows_scr, res)
        for b in range(B):
            o_ref[b] = _get_batch(rows_scr, b)

    @pl.when(s < NT_LAT)
    def _():
        def emit(r0, hv):
            hb_scr[pl.ds(r0, B), :] = hv

        _scan_tile(a_scr, b_scr, h_scr, emit, reverse=True)


def _rg_out_ffn(xc, gate, hs, x, h0, mods, g, wai, bai, lam, w_out, b_out, w1, w3, w2):
    scan_tile = lambda s: (jnp.maximum(NT_LAT - 1 - s, 0), 0)
    out_tile = lambda s: jnp.minimum(NT_LAT - s, NT_LAT - 1)
    prev = pl.BlockSpec((TM, D), lambda s: (out_tile(s), 0))
    return pl.pallas_call(
        _with_staged_weights(_rg_out_ffn_kernel, 1, 1, n_in=15),
        grid=(NT_LAT + 1,),
        in_specs=[pl.BlockSpec((TM, D), scan_tile), prev, prev, prev, _resident((B, D)),
                  pl.BlockSpec((None, B, N_MOD * D), lambda s: (0, 0, 0)), _resident((6, D))]
        + _rg_gate_specs() + [_resident((D, D)), _resident((1, D))] + _WEIGHT_SPECS,
        out_specs=pl.BlockSpec((B, TS, D), lambda s: (0, out_tile(s), 0)),
        out_shape=jax.ShapeDtypeStruct((B, L, D), F32),
        scratch_shapes=[pltpu.VMEM((TM, D), F32), pltpu.VMEM((TM, D), F32),
                        pltpu.VMEM((TM, D), F32), pltpu.VMEM((B, D), F32), _ROWS_SCRATCH]
        + _WEIGHT_SCRATCH,
        compiler_params=_cparams(("arbitrary",)),
        name="rglru_out_ffn",
    )(xc, gate, hs, x, h0, mods, g, wai, bai, lam, w_out, b_out, w1, w3, w2)


def _grid_pos():
    rows = L // GRID_W
    quarter = D // 4
    omega = POS_BASE ** (-np.arange(quarter) / quarter)

    def emb(q):
        ang = q[:, None] * omega[None]
        return jnp.asarray(np.concatenate([np.sin(ang), np.cos(ang)], axis=-1), F32)

    row_code = jnp.repeat(emb(np.arange(rows)), GRID_W, axis=0)
    col_code = jnp.tile(emb(np.arange(GRID_W)), (rows, 1))
    return jnp.concatenate([row_code, col_code], axis=-1)


def kernel(x, c, ctx, c_ctx, ada_w, ada_b, norm_g, ffn_w1, ffn_w3, ffn_w2, hy_w_in, hy_b_in, hy_conv_w, hy_conv_b, hy_fw0, hy_fb0, hy_fw1, hy_fb1, hy_fw2, hy_fb2, hy_freq, hy_fwout, hy_filt_bias, hy_w_out, hy_b_out, rg_w_in, rg_b_in, rg_conv_w, rg_conv_b, rg_wa, rg_ba, rg_wi, rg_bi, rg_lam, rg_w_out, rg_b_out):
    cc = jnp.concatenate([c, jnp.broadcast_to(c_ctx[None], (B, D))], axis=0)
    mods = _mods(cc, ada_w, ada_b).reshape(DEPTH, 2, B, N_MOD * D)
    w1, w3, w2 = ffn_w1, ffn_w3, ffn_w2

    g = norm_g[0]
    xs = _ffn_first(x, ctx, _grid_pos(), mods[0], g, w1, w3, w2)
    x0, p = _hy_in(xs, mods[0], g, hy_w_in[0].astype(BF16), hy_b_in[0][None],
                   hy_conv_w[0], hy_conv_b[0][None])
    fparams = (hy_fw0[0], hy_fb0[0], hy_fw1[0], hy_fb1[0], hy_fw2[0], hy_fb2[0],
               hy_freq[0], hy_fwout[0])
    specs = []
    for n in (L, CTX):
        hf, hb = _filters(n, *fparams)
        tables = [tab.astype(BF16) for tab in _dft_tables(n // 2)] + list(_twiddles(n // 2))
        specs.append(list(_spectrum(hf, hb, *tables)) + tables)
    y_lo, y_hi = _long_conv(p, x0, hy_filt_bias[0][None], *specs)
    xs = _hy_out_ffn(y_lo, y_hi, xs, mods[0], g, hy_w_out[0].astype(BF16), hy_b_out[0][None],
                     w1, w3, w2)

    g = norm_g[1]
    xs = _ffn_plain(xs, mods[1], g, w1, w3, w2, layer=1)
    wai = jnp.concatenate([rg_wa[0], rg_wi[0]], axis=-1).astype(BF16)
    per_head = lambda v: v.reshape(2, RG_HEADS, RG_BLOCK)
    bai = jnp.concatenate([per_head(rg_ba[0]), per_head(rg_bi[0])], axis=-1).reshape(2, 1, 2 * D)
    lam = rg_lam[0][:, None, :]
    xc, gate, hs = _rg_in(xs, mods[1], g, rg_w_in[0].astype(BF16), rg_b_in[0][None],
                          rg_conv_w[0], rg_conv_b[0][None], wai[0], bai[0], lam[0])
    h_ctx = _rg_ctx_bwd(xc, wai[1], bai[1], lam[1])
    return _rg_out_ffn(xc, gate, hs, xs, h_ctx, mods[1], g, wai[1], bai[1], lam[1],
                       rg_w_out[0].astype(BF16), rg_b_out[0][None], w1, w3, w2)
```

```python
import functools
import math

import jax
import jax.numpy as jnp
import numpy as np
from jax import lax
from jax.experimental import pallas as pl
from jax.experimental.pallas import tpu as pltpu

F32 = jnp.float32
BF16 = jnp.bfloat16

D = 1024
B = 8
LANES = 128
MXU_N = 256
L = 2048
CTX = 256
DEPTH = 2
GRID_W = 64
D_FF = 2816
N_MOD = 9
MACARON = 0.5
NORM_EPS = 1e-6
POS_BASE = 10000.0
HY_EMB = 33
HY_BANDS = 16
HY_HID = 64
HY_FAST_DECAY = 0.3
HY_SLOW_DECAY = 1.5
HY_DECAY_TARGET = 1e-2
RG_HEADS = 4
RG_BLOCK = D // RG_HEADS
RG_C = 8.0

ROWS_LAT = L * B
ROWS_CTX = CTX * B
ROWS = ROWS_LAT + ROWS_CTX
TM = 512
TS = TM // B
NT_LAT = ROWS_LAT // TM
NT_CTX = ROWS_CTX // TM
NT = NT_LAT + NT_CTX
TI = 1024
NI_LAT = ROWS_LAT // TI
NI_CTX = ROWS_CTX // TI
NI = NI_LAT + NI_CTX
F_CHUNKS = ((0, 1536), (1536, D_FF))
F_QUARTERS = ((0, 768), (768, 1536), (1536, 2304), (2304, D_FF))
TC = 256
VMEM_LIMIT = 58 * 1024 * 1024


def _cparams(sem):
    return pltpu.CompilerParams(dimension_semantics=sem, vmem_limit_bytes=VMEM_LIMIT)


def _resident(shape):
    nd = len(shape)
    return pl.BlockSpec(shape, lambda *_: (0,) * nd, pipeline_mode=pl.Buffered(1))


def _split(a):
    hi = a.astype(BF16)
    lo = (a - hi.astype(F32)).astype(BF16)
    return hi, lo


def _dot(a, b):
    return jnp.dot(a, b, preferred_element_type=F32)


def _dot3(a, b):
    ah, al = _split(a)
    bh, bl = _split(b)
    return _dot(ah, bh) + _dot(ah, bl) + _dot(al, bh)


def _rmsnorm(x, g):
    ms = jnp.mean(x * x, axis=-1, keepdims=True)
    return x * lax.rsqrt(ms + NORM_EPS) * g


def _mod(mod_ref, k):
    return mod_ref[:, k * D:(k + 1) * D]


def _per_batch(x, fn):
    rows = x.shape[0]
    return fn(x.reshape(rows // B, B, x.shape[1])).reshape(rows, x.shape[1])


def _modulate(xn, shift8, scale8):
    return _per_batch(xn, lambda v: v * (1.0 + scale8)[None] + shift8[None])


def _gated(z, gate8):
    return _per_batch(z, lambda v: v * gate8[None])


def _sigmoid(x):
    return jax.nn.sigmoid(x)


def _ffn(x, mod_ref, k0, g_pre, g_post, w1_ref, w3_ref, w2_ref, chunks=F_CHUNKS, side_work=()):
    h = _modulate(_rmsnorm(x, g_pre), _mod(mod_ref, k0), _mod(mod_ref, k0 + 1)).astype(BF16)
    y = None
    for idx, (c0, c1) in enumerate(chunks):
        a = _dot(h, w1_ref[:, c0:c1])
        b = _dot(h, w3_ref[:, c0:c1])
        if idx < len(side_work):
            side_work[idx]()
        act = (a * _sigmoid(a) * b).astype(BF16)
        part = _dot(act, w2_ref[c0:c1, :])
        y = part if y is None else y + part
    return x + MACARON * _gated(_rmsnorm(y, g_post), _mod(mod_ref, k0 + 2))


def _mods_kernel(cc_ref, w_ref, b_ref, o_ref):
    a = cc_ref[...]
    a = a * _sigmoid(a)
    o_ref[...] = _dot3(a, w_ref[...]) + b_ref[...]


def _mods(cc, ada_w, ada_b):
    tn = 1024
    n = N_MOD * D
    return pl.pallas_call(
        _mods_kernel,
        grid=(DEPTH, n // tn),
        in_specs=[
            pl.BlockSpec((2 * B, D), lambda i, j: (0, 0)),
            pl.BlockSpec((None, D, tn), lambda i, j: (i, 0, j)),
            pl.BlockSpec((None, 1, tn), lambda i, j: (i, 0, j)),
        ],
        out_specs=pl.BlockSpec((None, 2 * B, tn), lambda i, j: (i, 0, j)),
        out_shape=jax.ShapeDtypeStruct((DEPTH, 2 * B, n), F32),
        compiler_params=_cparams(("arbitrary", "arbitrary")),
        name="ada_mods",
    )(cc, ada_w, ada_b.reshape(DEPTH, 1, n))


def _mod_spec(nt_lat=NT_LAT):
    return pl.BlockSpec((None, B, N_MOD * D), lambda t: (jnp.where(t < nt_lat, 0, 1), 0, 0))


def _row_spec(width=D):
    return pl.BlockSpec((TM, width), lambda t: (t, 0))


def _wide_spec():
    return pl.BlockSpec((TS, B * D), lambda t: (t, 0))


W_CHUNKS = 8
W_SLOTS = 3
_WEIGHT_SPECS = [pl.BlockSpec(memory_space=pl.ANY)] * 3
_WEIGHT_SCRATCH = [pltpu.VMEM((D, D_FF), BF16), pltpu.VMEM((D, D_FF), BF16),
                   pltpu.VMEM((D_FF, D), BF16)]


def _stage_weights(srcs, dsts):
    chunks = [(src, dst, r0, dst.shape[0] // W_CHUNKS) for src, dst in zip(srcs, dsts)
              for r0 in range(0, dst.shape[0], dst.shape[0] // W_CHUNKS)]

    def run(stage_up, stage_down, sems):
        def copy(k):
            src, dst, r0, rows = chunks[k]
            stage = stage_up if dst.shape[1] == D_FF else stage_down
            return pltpu.make_async_copy(src.at[pl.ds(r0, rows), :], stage.at[k % W_SLOTS],
                                         sems.at[k % W_SLOTS])

        for k in range(W_SLOTS - 1):
            copy(k).start()
        for k, (_, dst, r0, rows) in enumerate(chunks):
            if k + W_SLOTS - 1 < len(chunks):
                copy(k + W_SLOTS - 1).start()
            copy(k).wait()
            stage = stage_up if dst.shape[1] == D_FF else stage_down
            dst[pl.ds(r0, rows), :] = stage[k % W_SLOTS].astype(BF16)

    pl.run_scoped(run, pltpu.VMEM((W_SLOTS, D // W_CHUNKS, D_FF), F32),
                  pltpu.VMEM((W_SLOTS, D_FF // W_CHUNKS, D), F32),
                  pltpu.SemaphoreType.DMA((W_SLOTS,)))


def _with_staged_weights(body, layer, half, n_in):
    def kernel(*refs):
        ins, rest = refs[:n_in], refs[n_in:]
        w1, w3, w2 = rest[-3:]

        @pl.when(pl.program_id(0) == 0)
        def _():
            _stage_weights([w.at[layer, half] for w in ins[-3:]], (w1, w3, w2))

        body(*ins[:-3], w1, w3, w2, *rest[:-3])

    return kernel


def _rows_of_batch(b):
    return pl.ds(b, TS, stride=B)


_ROWS_SCRATCH = pltpu.VMEM((D // LANES, TM, LANES), F32)


def _put_batch(rows_scr, b, val):
    for j in range(D // LANES):
        rows_scr[j, _rows_of_batch(b), :] = val[:, j * LANES:(j + 1) * LANES]


def _get_batch(rows_scr, b):
    return jnp.concatenate([rows_scr[j, _rows_of_batch(b), :] for j in range(D // LANES)], axis=1)


def _put_rows(rows_scr, val):
    for j in range(D // LANES):
        rows_scr[j] = val[:, j * LANES:(j + 1) * LANES]


def _get_rows(rows_scr):
    return jnp.concatenate([rows_scr[j] for j in range(D // LANES)], axis=1)


def _ffn_first_kernel(x_ref, ctx_ref, pos_ref, mod_ref, g_ref, w1, w3, w2, o_ref, rows_scr):
    is_latent = pl.program_id(0) < NT_LAT // 2
    for r in range(2):
        steps = slice(r * TS, (r + 1) * TS)
        for b in range(B):
            _put_batch(rows_scr.at[r], b,
                       jnp.where(is_latent, x_ref[b, steps] + pos_ref[steps], ctx_ref[b, steps]))
    for r in range(2):
        o_ref[r * TM:(r + 1) * TM, :] = _ffn(_get_rows(rows_scr.at[r]), mod_ref, 0,
                                             g_ref[0:1], g_ref[1:2], w1, w3, w2)


def _ffn_kernel(x_ref, mod_ref, g_ref, w1, w3, w2, o_ref):
    for r in range(x_ref.shape[0] // TM):
        rows = slice(r * TM, (r + 1) * TM)
        o_ref[rows, :] = _ffn(x_ref[rows, :], mod_ref, 0, g_ref[0:1], g_ref[1:2], w1, w3, w2)


def _ffn_first(x, ctx, pos, mods, g, w1, w3, w2):
    nt_lat = NT_LAT // 2
    lat = lambda t: jnp.minimum(t, nt_lat - 1)
    return pl.pallas_call(
        _with_staged_weights(_ffn_first_kernel, 0, 0, n_in=8),
        grid=(NT // 2,),
        in_specs=[pl.BlockSpec((B, 2 * TS, D), lambda t: (0, lat(t), 0)),
                  pl.BlockSpec((B, 2 * TS, D), lambda t: (0, jnp.maximum(t - nt_lat, 0), 0),
                               pipeline_mode=pl.Buffered(1)),
                  pl.BlockSpec((2 * TS, D), lambda t: (lat(t), 0)),
                  _mod_spec(nt_lat), _resident((6, D))] + _WEIGHT_SPECS,
        out_specs=pl.BlockSpec((2 * TM, D), lambda t: (t, 0)),
        out_shape=jax.ShapeDtypeStruct((ROWS, D), F32),
        scratch_shapes=[pltpu.VMEM((2, D // LANES, TM, LANES), F32)] + _WEIGHT_SCRATCH,
        compiler_params=_cparams(("arbitrary",)),
        name="ffn_first",
    )(x, ctx, pos, mods, g, w1, w3, w2)


def _ffn_plain(x, mods, g, w1, w3, w2, layer):
    return pl.pallas_call(
        _with_staged_weights(_ffn_kernel, layer, 0, n_in=6),
        grid=(NT // 2,),
        in_specs=[pl.BlockSpec((2 * TM, D), lambda t: (t, 0)),
                  pl.BlockSpec((None, B, N_MOD * D),
                               lambda t: (jnp.where(t < NT_LAT // 2, 0, 1), 0, 0)),
                  _resident((6, D))] + _WEIGHT_SPECS,
        out_specs=pl.BlockSpec((2 * TM, D), lambda t: (t, 0)),
        out_shape=jax.ShapeDtypeStruct((ROWS, D), F32),
        scratch_shapes=_WEIGHT_SCRATCH,
        compiler_params=_cparams(("arbitrary",)),
        name="ffn_pre",
    )(x, mods, g, w1, w3, w2)


def _mix_ffn_core(y, x, mod_ref, g_ref, wo_ref, bo_ref, w1, w3, w2, **ffn_kwargs):
    z = _dot(y.astype(BF16), wo_ref[...]) + bo_ref[...]
    x = x + _gated(_rmsnorm(z, g_ref[3:4]), _mod(mod_ref, 5))
    return _ffn(x, mod_ref, 6, g_ref[4:5], g_ref[5:6], w1, w3, w2, **ffn_kwargs)


def _hy_out_ffn_kernel(ylo_ref, yhi_ref, x_ref, mod_ref, g_ref, wo_ref, bo_ref, w1, w3, w2,
                       o_ref, rows_scr):
    halves = (ylo_ref, yhi_ref)
    for b in range(B):
        for j in range(D // LANES):
            col = (b * (D // TC) + j // 2) * LANES
            rows_scr[j, _rows_of_batch(b), :] = halves[j % 2][:, col:col + LANES]
    o_ref[...] = _mix_ffn_core(_get_rows(rows_scr), x_ref[...], mod_ref, g_ref, wo_ref, bo_ref,
                               w1, w3, w2)


def _hy_out_ffn(y_lo, y_hi, x, mods, g, w_out, b_out, w1, w3, w2):
    half = pl.BlockSpec((TS, B * D // 2), lambda t: (t, 0))
    return pl.pallas_call(
        _with_staged_weights(_hy_out_ffn_kernel, 0, 1, n_in=10),
        grid=(NT,),
        in_specs=[half, half, _row_spec(), _mod_spec(), _resident((6, D)), _resident((D, D)),
                  _resident((1, D))] + _WEIGHT_SPECS,
        out_specs=_row_spec(),
        out_shape=jax.ShapeDtypeStruct((ROWS, D), F32),
        scratch_shapes=[_ROWS_SCRATCH] + _WEIGHT_SCRATCH,
        compiler_params=_cparams(("arbitrary",)),
        name="hyena_out_ffn",
    )(y_lo, y_hi, x, mods, g, w_out, b_out, w1, w3, w2)


def _seq_first(t):
    return jnp.logical_or(t == 0, t == NI_LAT)


def _seq_last(t):
    return jnp.logical_or(t == NI_LAT - 1, t == NI - 1)


def _halo_specs(order, lo_rows, hi_rows):
    nlo = ROWS // lo_rows
    nhi = ROWS // hi_rows
    return [
        pl.BlockSpec((lo_rows, D), lambda s: (jnp.maximum(order(s) * (TI // lo_rows) - 1, 0), 0)),
        pl.BlockSpec((TI, D), lambda s: (order(s), 0)),
        pl.BlockSpec((hi_rows, D),
                     lambda s: (jnp.minimum((order(s) + 1) * (TI // hi_rows), nhi - 1), 0)),
    ]


def _hy_in_kernel(xp_ref, x_ref, xn_ref, mod_ref, g_ref, w_ref, b_ref, cw_ref, cb_ref,
                  x0_ref, p_ref, x0_scr, p_scr):
    t = pl.program_id(0)
    xa = jnp.concatenate([xp_ref[...], x_ref[...], xn_ref[...]], axis=0)
    h = _modulate(_rmsnorm(xa, g_ref[2:3]), _mod(mod_ref, 3), _mod(mod_ref, 4)).astype(BF16)
    keep_lo = jnp.where(_seq_first(t), 0.0, 1.0).astype(F32)
    keep_hi = jnp.where(_seq_last(t), 0.0, 1.0).astype(F32)

    def conv_cols(c0):
        cols = slice(c0, c0 + MXU_N)
        u = _dot(h, w_ref[:, cols]) + b_ref[:, cols]
        lo = jnp.concatenate([u[0:B] * keep_lo, u[B:TI]], axis=0)
        hi = jnp.concatenate([u[2 * B:TI + B], u[TI + B:TI + 2 * B] * keep_hi], axis=0)
        return (cb_ref[:, cols] + cw_ref[0:1, cols] * lo + cw_ref[1:2, cols] * u[B:TI + B]
                + cw_ref[2:3, cols] * hi)

    slabs = MXU_N // LANES
    for c in range(D // MXU_N):
        c0 = c * MXU_N
        vals = (conv_cols(c0), conv_cols(D + c0) * conv_cols(2 * D + c0))
        for out_ref, scr, val in zip((x0_ref, p_ref), (x0_scr, p_scr), vals):
            for j in range(slabs):
                scr[c * slabs + j] = val[:, j * LANES:(j + 1) * LANES]
            for b in range(B):
                out_ref[:, b * D + c0:b * D + c0 + MXU_N] = jnp.concatenate(
                    [scr[c * slabs + j, pl.ds(b, TI // B, stride=B), :] for j in range(slabs)], axis=1)


def _hy_in(x, mods, g, w_in, b_in, conv_w, conv_b):
    wide = pl.BlockSpec((TI // B, B * D), lambda t: (t, 0))
    rows_scratch = pltpu.VMEM((D // LANES, TI, LANES), F32)
    return pl.pallas_call(
        _hy_in_kernel,
        grid=(NI,),
        in_specs=_halo_specs(lambda s: s, B, B) + [
            _mod_spec(NI_LAT), _resident((6, D)), _resident((D, 3 * D)), _resident((1, 3 * D)),
            _resident((3, 3 * D)), _resident((1, 3 * D))],
        out_specs=[wide, wide],
        out_shape=[jax.ShapeDtypeStruct((ROWS // B, B * D), F32)] * 2,
        scratch_shapes=[rows_scratch, rows_scratch],
        compiler_params=_cparams(("arbitrary",)),
        name="hyena_in",
    )(x, x, x, mods, g, w_in, b_in, conv_w, conv_b)


def _filter_kernel(z_ref, fw0, fb0, fw1, fb1, fw2, fb2, freq, fwout, deltas, hf_ref, hb_ref):
    z = z_ref[...]
    h = jnp.sin(freq[0:1] * (_dot3(z, fw0[...]) + fb0[...]))
    h = jnp.sin(freq[1:2] * (_dot3(h, fw1[...]) + fb1[...]))
    h = jnp.sin(freq[2:3] * (_dot3(h, fw2[...]) + fb2[...]))
    filt = _dot3(h, fwout[...])
    decay = jnp.exp(-z[:, 0:1] * deltas[...])
    hf_ref[...] = filt[:, 0:D] * decay
    hb_ref[...] = filt[:, D:2 * D] * decay


def _filters(n, fw0, fb0, fw1, fb1, fw2, fb2, freq, fwout):
    t = np.linspace(0.0, 1.0, n)[:, None]
    bands = np.linspace(1e-4, HY_BANDS - 1, HY_BANDS)[None]
    phase = bands * (2.0 * math.pi * np.arange(n)[:, None] / n)
    zp = np.zeros((n, LANES), np.float32)
    zp[:, :HY_EMB] = np.concatenate([t, np.cos(phase), -np.sin(phase)], axis=-1)
    zp = np.concatenate([zp[0::2], zp[1::2]], axis=0)
    fw0p = jnp.zeros((LANES, HY_HID), F32).at[:HY_EMB].set(fw0)
    max_decay = math.log(HY_DECAY_TARGET) / HY_FAST_DECAY
    min_decay = math.log(HY_DECAY_TARGET) / HY_SLOW_DECAY
    deltas = np.abs(np.linspace(min_decay, max_decay, D))[None].astype(np.float32)
    tl = 256
    row = lambda i: (i, 0)
    return pl.pallas_call(
        _filter_kernel,
        grid=(n // tl,),
        in_specs=[pl.BlockSpec((tl, LANES), row), _resident((LANES, HY_HID)), _resident((1, HY_HID)),
                  _resident((HY_HID, HY_HID)), _resident((1, HY_HID)),
                  _resident((HY_HID, HY_HID)), _resident((1, HY_HID)),
                  _resident((3, HY_HID)), _resident((HY_HID, 2 * D)), _resident((1, D))],
        out_specs=[pl.BlockSpec((tl, D), row)] * 2,
        out_shape=[jax.ShapeDtypeStruct((n, D), F32)] * 2,
        compiler_params=_cparams(("arbitrary",)),
        name="hyena_filter",
    )(zp, fw0p, fb0[None], fw1, fb1[None], fw2, fb2[None], freq, fwout, deltas)


def _dft_tables(n):
    q = 1 << (int(math.log2(n)) // 2 + 1)
    m = np.arange(n, dtype=np.int64)[None, :]

    def thin(k):
        ang = ((k[:, None] * m) % (2 * n)) * (math.pi / n)
        return jnp.asarray(np.cos(ang), F32), jnp.asarray(np.sin(ang), F32)

    ca, sa = (v[:, None, :] for v in thin(q * np.arange(n // q, dtype=np.int64)))
    cb, sb = (v[None, :, :] for v in thin(np.arange(q, dtype=np.int64)))
    return (ca * cb - sa * sb).reshape(n, n), (sa * cb + ca * sb).reshape(n, n)


def _alt_sign(rows, cols):
    r = lax.broadcasted_iota(jnp.int32, (rows, cols), 0)
    return (1 - 2 * (r & 1)).astype(F32)


def _twiddles(h):
    ang = np.arange(h)[:, None] * (math.pi / (2 * h))
    return tuple(jnp.broadcast_to(jnp.asarray(f(ang), F32), (h, TC)) for f in (np.cos, np.sin))


def _spectrum_kernel(hf_ref, hb_ref, c_ref, s_ref, wc_ref, ws_ref,
                     kar_ref, kai_ref, kbr_ref, kbi_ref, kn_ref, *, h):
    hf = hf_ref[...]
    row = lax.broadcasted_iota(jnp.int32, hf.shape, 0)
    hb = jnp.where(row == 0, 0.0, hb_ref[...])
    cos_part = hf + hb
    sin_part = hb - hf
    ce, co = cos_part[0:h].astype(BF16), cos_part[h:2 * h].astype(BF16)
    se, so = sin_part[0:h].astype(BF16), sin_part[h:2 * h].astype(BF16)
    c, s = c_ref[...], s_ref[...]
    wc, ws = wc_ref[...], ws_ref[...]
    r_even = _dot(c, ce)
    r_odd = wc * _dot(c, co) - ws * _dot(s, co)
    i_even = _dot(s, se)
    i_odd = wc * _dot(s, so) + ws * _dot(c, so)
    n_fft = 4 * h
    k = lax.broadcasted_iota(jnp.int32, (h, hf.shape[1]), 0)
    scale = jnp.where(k == 0, 1.0 / n_fft, 2.0 / n_fft)
    kar_ref[...] = (r_even + r_odd) * scale
    kai_ref[...] = (i_even + i_odd) * scale
    kbr_ref[...] = (r_even - r_odd) * scale
    kbi_ref[...] = (i_even - i_odd) * scale
    alt = _alt_sign(h, hf.shape[1])
    mid_r = jnp.sum(cos_part[0:h] * alt, axis=0, keepdims=True) * (2.0 / n_fft)
    mid_i = jnp.sum(sin_part[h:2 * h] * alt, axis=0, keepdims=True) * (2.0 / n_fft)
    kn_ref[...] = jnp.concatenate([mid_r, mid_i, jnp.zeros((B - 2, hf.shape[1]), F32)], axis=0)


def _spectrum(hf, hb, ctab, stab, wc, ws):
    n = hf.shape[0]
    h = n // 2
    col = lambda j: (0, j)
    half = pl.BlockSpec((h, TC), col)
    return pl.pallas_call(
        functools.partial(_spectrum_kernel, h=h),
        grid=(D // TC,),
        in_specs=[pl.BlockSpec((n, TC), col)] * 2 + [_resident((h, h))] * 2
        + [_resident((h, TC))] * 2,
        out_specs=[half] * 4 + [pl.BlockSpec((B, TC), col)],
        out_shape=[jax.ShapeDtypeStruct((h, D), F32)] * 4 + [jax.ShapeDtypeStruct((B, D), F32)],
        compiler_params=_cparams(("arbitrary",)),
        name="hyena_spectrum",
    )(hf, hb, ctab, stab, wc, ws)


def _steps(refs, first, count):
    rows = pl.ds(first, count, stride=2)
    return jnp.concatenate([r[rows, :] for r in refs], axis=1)


def _long_conv_rows(t0, h, p_refs, x0_refs, spec, bias_ref, o_refs, scratch):
    kar_ref, kai_ref, kbr_ref, kbi_ref, kn_ref, c_ref, s_ref, wc_ref, ws_ref = spec
    peb_scr, pob_scr, sr_scr, ss_scr, dr_scr, ds_scr = scratch
    tk = min(h, 512)
    r0 = t0 // 2
    seq = slice(r0, r0 + h)
    chunks = [slice(k * tk, (k + 1) * tk) for k in range(h // tk)]
    p_even, p_odd = _steps(p_refs, t0, h), _steps(p_refs, t0 + 1, h)
    tc = p_even.shape[1]
    peb_scr[seq, :] = p_even.astype(BF16)
    pob_scr[seq, :] = p_odd.astype(BF16)
    for rows in chunks:
        dst = slice(r0 + rows.start, r0 + rows.stop)
        c, s = c_ref[rows, :], s_ref[rows, :]
        er, es = _dot(c, peb_scr[seq, :]), _dot(s, peb_scr[seq, :])
        odr, ods = _dot(c, pob_scr[seq, :]), _dot(s, pob_scr[seq, :])
        wc, ws = wc_ref[rows, :], ws_ref[rows, :]
        tr, ts = wc * odr - ws * ods, wc * ods + ws * odr
        ar, as_, br, bs = er + tr, es + ts, er - tr, es - ts
        kar, kai, kbr, kbi = kar_ref[rows, :], kai_ref[rows, :], kbr_ref[rows, :], kbi_ref[rows, :]
        yar, yas = ar * kar + as_ * kai, as_ * kar - ar * kai
        ybr, ybs = br * kbr + bs * kbi, bs * kbr - br * kbi
        sr_scr[dst, :] = (yar + ybr).astype(BF16)
        ss_scr[dst, :] = (yas + ybs).astype(BF16)
        dr, ds = yar - ybr, yas - ybs
        dr_scr[dst, :] = (dr * wc + ds * ws).astype(BF16)
        ds_scr[dst, :] = (ds * wc - dr * ws).astype(BF16)
    alt = _alt_sign(h, tc)
    xr = jnp.sum(p_even * alt, axis=0, keepdims=True)
    xs = jnp.sum(p_odd * alt, axis=0, keepdims=True)
    kr, ki = kn_ref[0:1, :], kn_ref[1:2, :]
    alt_chunk = _alt_sign(tk, tc)
    mids = (alt_chunk * (xr * kr + xs * ki), alt_chunk * (xs * kr - xr * ki))
    bias = bias_ref[...]
    for rows in chunks:
        c, s = c_ref[rows, :], s_ref[rows, :]
        ys = (_dot(c, sr_scr[seq, :]) + _dot(s, ss_scr[seq, :]),
              _dot(c, dr_scr[seq, :]) + _dot(s, ds_scr[seq, :]))
        for parity in range(2):
            first = t0 + 2 * rows.start + parity
            y = ys[parity] + mids[parity]
            out = _steps(x0_refs, first, tk) * (y + _steps(p_refs, first, tk) * bias)
            for q, o_ref in enumerate(o_refs):
                o_ref[pl.ds(first, tk, stride=2), :] = out[:, q * LANES:(q + 1) * LANES]


_N_SPEC = 9


def _long_conv_kernel(*refs):
    p_refs, x0_refs, bias_ref = refs[0:2], refs[2:4], refs[4]
    lat, ctx = refs[5:5 + _N_SPEC], refs[5 + _N_SPEC:5 + 2 * _N_SPEC]
    o_refs, scratch = refs[5 + 2 * _N_SPEC:7 + 2 * _N_SPEC], refs[7 + 2 * _N_SPEC:]
    _long_conv_rows(0, L // 2, p_refs, x0_refs, lat, bias_ref, o_refs, scratch)
    _long_conv_rows(L, CTX // 2, p_refs, x0_refs, ctx, bias_ref, o_refs, scratch)


def _long_conv(p2, x02, bias, spec_lat, spec_ctx):
    nc = D // TC
    halves = [pl.BlockSpec((ROWS // B, LANES),
                           functools.partial(lambda q, j: (0, 2 * ((j % B) * nc + j // B) + q), q))
              for q in range(TC // LANES)]
    ch = lambda j: (0, j // B)

    def spec_specs(h):
        return ([pl.BlockSpec((h, TC), ch)] * 4 + [pl.BlockSpec((B, TC), ch)]
                + [_resident((h, h))] * 2 + [_resident((h, TC))] * 2)

    half_out = pl.BlockSpec((ROWS // B, LANES), lambda j: (0, (j % B) * nc + j // B))
    return pl.pallas_call(
        _long_conv_kernel,
        grid=(B * D // TC,),
        scratch_shapes=[pltpu.VMEM((ROWS // (2 * B), TC), BF16)] * 6,
        in_specs=halves + halves + [pl.BlockSpec((1, TC), ch)] + spec_specs(L // 2)
        + spec_specs(CTX // 2),
        out_specs=[half_out] * (TC // LANES),
        out_shape=[jax.ShapeDtypeStruct((ROWS // B, B * D * LANES // TC), F32)] * (TC // LANES),
        compiler_params=_cparams(("arbitrary",)),
        name="hyena_long_conv",
    )(p2, p2, x02, x02, bias, *spec_lat, *spec_ctx)


def _gelu_tanh(x):
    return x * (0.5 * (1.0 + jnp.tanh(math.sqrt(2.0 / math.pi) * (x + 0.044715 * (x * x * x)))))


def _rg_coeffs(xc, hd, wai_ref, bai_ref, lam_ref, a_scr, b_scr):
    sl = slice(hd * RG_BLOCK, (hd + 1) * RG_BLOCK)
    lam = lam_ref[:, sl]
    softplus_neg = jnp.maximum(-lam, 0.0) + jnp.log1p(jnp.exp(-jnp.abs(lam)))
    rate = (-RG_C * math.log2(math.e)) * softplus_neg
    pre = _dot(xc.astype(BF16), wai_ref[hd]) + bai_ref[:, 2 * hd * RG_BLOCK:2 * (hd + 1) * RG_BLOCK]
    gates = _sigmoid(pre)
    a = jnp.exp2(gates[:, 0:RG_BLOCK] * rate)
    a_scr[:, sl] = a
    b_scr[:, sl] = jnp.sqrt((1.0 - a) * (1.0 + a)) * gates[:, RG_BLOCK:2 * RG_BLOCK] * xc


def _scan_tile(a_scr, b_scr, h_scr, emit, reverse):
    steps = a_scr.shape[0] // B

    def body(k, h):
        t = steps - 1 - k if reverse else k
        r0 = pl.multiple_of(t * B, B)
        h = a_scr[pl.ds(r0, B), :] * h + b_scr[pl.ds(r0, B), :]
        emit(r0, h)
        return h

    h_scr[...] = lax.fori_loop(0, steps, body, h_scr[...], unroll=8)


def _rg_fwd_order(s):
    return jnp.where(s < NI_CTX, NI_LAT + s, s - NI_CTX)


def _rg_in_kernel(xp_ref, x_ref, xn_ref, mod_ref, g_ref, w_ref, b_ref, cw_ref, cb_ref,
                  wai_ref, bai_ref, lam_ref, xc_ref, gate_ref, hs_ref, a_scr, b_scr, h_scr):
    s = pl.program_id(0)
    t = _rg_fwd_order(s)
    xa = jnp.concatenate([xp_ref[...], x_ref[...], xn_ref[...]], axis=0)
    h = _modulate(_rmsnorm(xa, g_ref[2:3]), _mod(mod_ref, 3), _mod(mod_ref, 4)).astype(BF16)
    keep_lo = jnp.where(_seq_first(t), 0.0, 1.0).astype(F32)
    keep_hi = jnp.where(_seq_last(t), 0.0, 1.0).astype(F32)
    for hd in range(RG_HEADS):
        sl = slice(hd * RG_BLOCK, (hd + 1) * RG_BLOCK)
        rec = slice(D + hd * RG_BLOCK, D + (hd + 1) * RG_BLOCK)
        gate_ref[:, sl] = (_dot(h[B:TI + B], w_ref[:, sl]) + b_ref[:, sl]).astype(BF16)
        u = _dot(h, w_ref[:, rec]) + b_ref[:, rec]
        taps = (jnp.concatenate([u[0:B] * keep_lo, u[B:TI]], axis=0),
                u[B:TI + B],
                jnp.concatenate([u[2 * B:TI + B], u[TI + B:TI + 2 * B] * keep_hi], axis=0),
                jnp.concatenate([u[3 * B:TI + B], u[TI + B:TI + 3 * B] * keep_hi], axis=0))
        xc = cb_ref[:, sl]
        for k, tap in enumerate(taps):
            xc = xc + cw_ref[k:k + 1, sl] * tap
        xc_ref[:, sl] = xc
        _rg_coeffs(xc, hd, wai_ref, bai_ref, lam_ref, a_scr, b_scr)

    @pl.when(s == 0)
    def _():
        h_scr[...] = jnp.zeros((B, D), F32)

    def emit(r0, hv):
        hs_ref[pl.ds(r0, B), :] = hv

    _scan_tile(a_scr, b_scr, h_scr, emit, reverse=False)


def _rg_gate_specs():
    return [_resident((RG_HEADS, RG_BLOCK, 2 * RG_BLOCK)), _resident((1, 2 * D)), _resident((1, D))]


def _rg_in(x, mods, g, w_in, b_in, conv_w, conv_b, wai, bai, lam):
    order = _rg_fwd_order
    mod_spec = pl.BlockSpec((None, B, N_MOD * D),
                            lambda s: (jnp.where(order(s) < NI_LAT, 0, 1), 0, 0))
    out_spec = pl.BlockSpec((TI, D), lambda s: (order(s), 0))
    return pl.pallas_call(
        _rg_in_kernel,
        grid=(NI,),
        in_specs=_halo_specs(order, B, 2 * B) + [
            mod_spec, _resident((6, D)), _resident((D, 2 * D)), _resident((1, 2 * D)),
            _resident((4, D)), _resident((1, D))] + _rg_gate_specs(),
        out_specs=[out_spec] * 3,
        out_shape=[jax.ShapeDtypeStruct((ROWS, D), F32), jax.ShapeDtypeStruct((ROWS, D), BF16),
                   jax.ShapeDtypeStruct((ROWS, D), F32)],
        scratch_shapes=[pltpu.VMEM((TI, D), F32), pltpu.VMEM((TI, D), F32),
                        pltpu.VMEM((B, D), F32)],
        compiler_params=_cparams(("arbitrary",)),
        name="rglru_in_fwd_scan",
    )(x, x, x, mods, g, w_in, b_in, conv_w, conv_b, wai, bai, lam)


def _rg_tile_coeffs(xc_ref, wai_ref, bai_ref, lam_ref, a_scr, b_scr):
    for hd in range(RG_HEADS):
        xc = xc_ref[:, hd * RG_BLOCK:(hd + 1) * RG_BLOCK]
        _rg_coeffs(xc, hd, wai_ref, bai_ref, lam_ref, a_scr, b_scr)


def _rg_ctx_bwd_kernel(xc_ref, wai_ref, bai_ref, lam_ref, h_ref, a_scr, b_scr, h_scr):
    _rg_tile_coeffs(xc_ref, wai_ref, bai_ref, lam_ref, a_scr, b_scr)

    @pl.when(pl.program_id(0) == 0)
    def _():
        h_scr[...] = jnp.zeros((B, D), F32)

    _scan_tile(a_scr, b_scr, h_scr, lambda r0, hv: None, reverse=True)
    h_ref[...] = h_scr[...]


def _rg_ctx_bwd(xc, wai, bai, lam):
    return pl.pallas_call(
        _rg_ctx_bwd_kernel,
        grid=(NT_CTX,),
        in_specs=[pl.BlockSpec((TM, D), lambda s: (NT - 1 - s, 0))] + _rg_gate_specs(),
        out_specs=pl.BlockSpec((B, D), lambda s: (0, 0)),
        out_shape=jax.ShapeDtypeStruct((B, D), F32),
        scratch_shapes=[pltpu.VMEM((TM, D), F32), pltpu.VMEM((TM, D), F32),
                        pltpu.VMEM((B, D), F32)],
        compiler_params=_cparams(("arbitrary",)),
        name="rglru_ctx_bwd_scan",
    )(xc, wai, bai, lam)


def _rg_out_ffn_kernel(xc_ref, gate_ref, hs_ref, x_ref, h0_ref, mod_ref, g_ref,
                       wai_ref, bai_ref, lam_ref, wo_ref, bo_ref, w1, w3, w2,
                       o_ref, a_scr, b_scr, hb_scr, h_scr, rows_scr):
    s = pl.program_id(0)

    def head_gates(hd):
        xc = xc_ref[:, hd * RG_BLOCK:(hd + 1) * RG_BLOCK]
        _rg_coeffs(xc, hd, wai_ref, bai_ref, lam_ref, a_scr, b_scr)

    @pl.when(s == 0)
    def _():
        h_scr[...] = h0_ref[...]
        for hd in range(RG_HEADS):
            head_gates(hd)

    @pl.when(s > 0)
    def _():
        y = (hs_ref[...] + hb_scr[...]) * _gelu_tanh(gate_ref[...].astype(F32))
        res = _mix_ffn_core(
            y, x_ref[...], mod_ref, g_ref, wo_ref, bo_ref, w1, w3, w2, chunks=F_QUARTERS,
            side_work=[functools.partial(head_gates, hd) for hd in range(RG_HEADS)])
        _put_rows(rows_scr, res)
        for b in range(B):
            o_ref[b] = _get_batch(rows_scr, b)

    @pl.when(s < NT_LAT)
    def _():
        def emit(r0, hv):
            hb_scr[pl.ds(r0, B), :] = hv

        _scan_tile(a_scr, b_scr, h_scr, emit, reverse=True)


def _rg_out_ffn(xc, gate, hs, x, h0, mods, g, wai, bai, lam, w_out, b_out, w1, w3, w2):
    scan_tile = lambda s: (jnp.maximum(NT_LAT - 1 - s, 0), 0)
    out_tile = lambda s: jnp.minimum(NT_LAT - s, NT_LAT - 1)
    prev = pl.BlockSpec((TM, D), lambda s: (out_tile(s), 0))
    return pl.pallas_call(
        _with_staged_weights(_rg_out_ffn_kernel, 1, 1, n_in=15),
        grid=(NT_LAT + 1,),
        in_specs=[pl.BlockSpec((TM, D), scan_tile), prev, prev, prev, _resident((B, D)),
                  pl.BlockSpec((None, B, N_MOD * D), lambda s: (0, 0, 0)), _resident((6, D))]
        + _rg_gate_specs() + [_resident((D, D)), _resident((1, D))] + _WEIGHT_SPECS,
        out_specs=pl.BlockSpec((B, TS, D), lambda s: (0, out_tile(s), 0)),
        out_shape=jax.ShapeDtypeStruct((B, L, D), F32),
        scratch_shapes=[pltpu.VMEM((TM, D), F32), pltpu.VMEM((TM, D), F32),
                        pltpu.VMEM((TM, D), F32), pltpu.VMEM((B, D), F32), _ROWS_SCRATCH]
        + _WEIGHT_SCRATCH,
        compiler_params=_cparams(("arbitrary",)),
        name="rglru_out_ffn",
    )(xc, gate, hs, x, h0, mods, g, wai, bai, lam, w_out, b_out, w1, w3, w2)


def _grid_pos():
    rows = L // GRID_W
    quarter = D // 4
    omega = POS_BASE ** (-np.arange(quarter) / quarter)

    def emb(q):
        ang = q[:, None] * omega[None]
        return jnp.asarray(np.concatenate([np.sin(ang), np.cos(ang)], axis=-1), F32)

    row_code = jnp.repeat(emb(np.arange(rows)), GRID_W, axis=0)
    col_code = jnp.tile(emb(np.arange(GRID_W)), (rows, 1))
    return jnp.concatenate([row_code, col_code], axis=-1)


def kernel(x, c, ctx, c_ctx, ada_w, ada_b, norm_g, ffn_w1, ffn_w3, ffn_w2, hy_w_in, hy_b_in, hy_conv_w, hy_conv_b, hy_fw0, hy_fb0, hy_fw1, hy_fb1, hy_fw2, hy_fb2, hy_freq, hy_fwout, hy_filt_bias, hy_w_out, hy_b_out, rg_w_in, rg_b_in, rg_conv_w, rg_conv_b, rg_wa, rg_ba, rg_wi, rg_bi, rg_lam, rg_w_out, rg_b_out):
    cc = jnp.concatenate([c, jnp.broadcast_to(c_ctx[None], (B, D))], axis=0)
    mods = _mods(cc, ada_w, ada_b).reshape(DEPTH, 2, B, N_MOD * D)
    w1, w3, w2 = ffn_w1, ffn_w3, ffn_w2

    g = norm_g[0]
    xs = _ffn_first(x, ctx, _grid_pos(), mods[0], g, w1, w3, w2)
    x0, p = _hy_in(xs, mods[0], g, hy_w_in[0].astype(BF16), hy_b_in[0][None],
                   hy_conv_w[0], hy_conv_b[0][None])
    fparams = (hy_fw0[0], hy_fb0[0], hy_fw1[0], hy_fb1[0], hy_fw2[0], hy_fb2[0],
               hy_freq[0], hy_fwout[0])
    specs = []
    for n in (L, CTX):
        hf, hb = _filters(n, *fparams)
        tables = [tab.astype(BF16) for tab in _dft_tables(n // 2)] + list(_twiddles(n // 2))
        specs.append(list(_spectrum(hf, hb, *tables)) + tables)
    y_lo, y_hi = _long_conv(p, x0, hy_filt_bias[0][None], *specs)
    xs = _hy_out_ffn(y_lo, y_hi, xs, mods[0], g, hy_w_out[0].astype(BF16), hy_b_out[0][None],
                     w1, w3, w2)

    g = norm_g[1]
    xs = _ffn_plain(xs, mods[1], g, w1, w3, w2, layer=1)
    wai = jnp.concatenate([rg_wa[0], rg_wi[0]], axis=-1).astype(BF16)
    per_head = lambda v: v.reshape(2, RG_HEADS, RG_BLOCK)
    bai = jnp.concatenate([per_head(rg_ba[0]), per_head(rg_bi[0])], axis=-1).reshape(2, 1, 2 * D)
    lam = rg_lam[0][:, None, :]
    xc, gate, hs = _rg_in(xs, mods[1], g, rg_w_in[0].astype(BF16), rg_b_in[0][None],
                          rg_conv_w[0], rg_conv_b[0][None], wai[0], bai[0], lam[0])
    h_ctx = _rg_ctx_bwd(xc, wai[1], bai[1], lam[1])
    return _rg_out_ffn(xc, gate, hs, xs, h_ctx, mods[1], g, wai[1], bai[1], lam[1],
                       rg_w_out[0].astype(BF16), rg_b_out[0][None], w1, w3, w2)
```

```python
import functools
import math

import jax
import jax.numpy as jnp
import numpy as np
from jax import lax
from jax.experimental import pallas as pl
from jax.experimental.pallas import tpu as pltpu

F32 = jnp.float32
BF16 = jnp.bfloat16

D = 1024
B = 8
LANES = 128
MXU_N = 256
L = 2048
CTX = 256
DEPTH = 2
GRID_W = 64
D_FF = 2816
N_MOD = 9
MACARON = 0.5
NORM_EPS = 1e-6
POS_BASE = 10000.0
HY_EMB = 33
HY_BANDS = 16
HY_HID = 64
HY_FAST_DECAY = 0.3
HY_SLOW_DECAY = 1.5
HY_DECAY_TARGET = 1e-2
RG_HEADS = 4
RG_BLOCK = D // RG_HEADS
RG_C = 8.0

ROWS_LAT = L * B
ROWS_CTX = CTX * B
ROWS = ROWS_LAT + ROWS_CTX
TM = 512
TS = TM // B
NT_LAT = ROWS_LAT // TM
NT_CTX = ROWS_CTX // TM
NT = NT_LAT + NT_CTX
TI = 1024
NI_LAT = ROWS_LAT // TI
NI_CTX = ROWS_CTX // TI
NI = NI_LAT + NI_CTX
F_CHUNKS = ((0, 1536), (1536, D_FF))
F_QUARTERS = ((0, 768), (768, 1536), (1536, 2304), (2304, D_FF))
TC = 256
VMEM_LIMIT = 58 * 1024 * 1024


def _cparams(sem):
    return pltpu.CompilerParams(dimension_semantics=sem, vmem_limit_bytes=VMEM_LIMIT)


def _resident(shape):
    nd = len(shape)
    return pl.BlockSpec(shape, lambda *_: (0,) * nd, pipeline_mode=pl.Buffered(1))


def _split(a):
    hi = a.astype(BF16)
    lo = (a - hi.astype(F32)).astype(BF16)
    return hi, lo


def _dot(a, b):
    return jnp.dot(a, b, preferred_element_type=F32)


def _dot3(a, b):
    ah, al = _split(a)
    bh, bl = _split(b)
    return _dot(ah, bh) + _dot(ah, bl) + _dot(al, bh)


def _rmsnorm(x, g):
    ms = jnp.mean(x * x, axis=-1, keepdims=True)
    return x * lax.rsqrt(ms + NORM_EPS) * g


def _mod(mod_ref, k):
    return mod_ref[:, k * D:(k + 1) * D]


def _per_batch(x, fn):
    rows = x.shape[0]
    return fn(x.reshape(rows // B, B, x.shape[1])).reshape(rows, x.shape[1])


def _modulate(xn, shift8, scale8):
    return _per_batch(xn, lambda v: v * (1.0 + scale8)[None] + shift8[None])


def _gated(z, gate8):
    return _per_batch(z, lambda v: v * gate8[None])


def _sigmoid(x):
    return jax.nn.sigmoid(x)


def _ffn(x, mod_ref, k0, g_pre, g_post, w1_ref, w3_ref, w2_ref, chunks=F_CHUNKS, side_work=()):
    h = _modulate(_rmsnorm(x, g_pre), _mod(mod_ref, k0), _mod(mod_ref, k0 + 1)).astype(BF16)
    y = None
    for idx, (c0, c1) in enumerate(chunks):
        a = _dot(h, w1_ref[:, c0:c1])
        b = _dot(h, w3_ref[:, c0:c1])
        if idx < len(side_work):
            side_work[idx]()
        act = (a * _sigmoid(a) * b).astype(BF16)
        part = _dot(act, w2_ref[c0:c1, :])
        y = part if y is None else y + part
    return x + MACARON * _gated(_rmsnorm(y, g_post), _mod(mod_ref, k0 + 2))


def _mods_kernel(cc_ref, w_ref, b_ref, o_ref):
    a = cc_ref[...]
    a = a * _sigmoid(a)
    o_ref[...] = _dot3(a, w_ref[...]) + b_ref[...]


def _mods(cc, ada_w, ada_b):
    tn = 1024
    n = N_MOD * D
    return pl.pallas_call(
        _mods_kernel,
        grid=(DEPTH, n // tn),
        in_specs=[
            pl.BlockSpec((2 * B, D), lambda i, j: (0, 0)),
            pl.BlockSpec((None, D, tn), lambda i, j: (i, 0, j)),
            pl.BlockSpec((None, 1, tn), lambda i, j: (i, 0, j)),
        ],
        out_specs=pl.BlockSpec((None, 2 * B, tn), lambda i, j: (i, 0, j)),
        out_shape=jax.ShapeDtypeStruct((DEPTH, 2 * B, n), F32),
        compiler_params=_cparams(("arbitrary", "arbitrary")),
        name="ada_mods",
    )(cc, ada_w, ada_b.reshape(DEPTH, 1, n))


def _mod_spec(nt_lat=NT_LAT):
    return pl.BlockSpec((None, B, N_MOD * D), lambda t: (jnp.where(t < nt_lat, 0, 1), 0, 0))


def _row_spec(width=D):
    return pl.BlockSpec((TM, width), lambda t: (t, 0))


def _wide_spec():
    return pl.BlockSpec((TS, B * D), lambda t: (t, 0))


W_CHUNKS = 8
W_SLOTS = 3
_WEIGHT_SPECS = [pl.BlockSpec(memory_space=pl.ANY)] * 3
_WEIGHT_SCRATCH = [pltpu.VMEM((D, D_FF), BF16), pltpu.VMEM((D, D_FF), BF16),
                   pltpu.VMEM((D_FF, D), BF16)]


def _stage_weights(srcs, dsts):
    chunks = [(src, dst, r0, dst.shape[0] // W_CHUNKS) for src, dst in zip(srcs, dsts)
              for r0 in range(0, dst.shape[0], dst.shape[0] // W_CHUNKS)]

    def run(stage_up, stage_down, sems):
        def copy(k):
            src, dst, r0, rows = chunks[k]
            stage = stage_up if dst.shape[1] == D_FF else stage_down
            return pltpu.make_async_copy(src.at[pl.ds(r0, rows), :], stage.at[k % W_SLOTS],
                                         sems.at[k % W_SLOTS])

        for k in range(W_SLOTS - 1):
            copy(k).start()
        for k, (_, dst, r0, rows) in enumerate(chunks):
            if k + W_SLOTS - 1 < len(chunks):
                copy(k + W_SLOTS - 1).start()
            copy(k).wait()
            stage = stage_up if dst.shape[1] == D_FF else stage_down
            dst[pl.ds(r0, rows), :] = stage[k % W_SLOTS].astype(BF16)

    pl.run_scoped(run, pltpu.VMEM((W_SLOTS, D // W_CHUNKS, D_FF), F32),
                  pltpu.VMEM((W_SLOTS, D_FF // W_CHUNKS, D), F32),
                  pltpu.SemaphoreType.DMA((W_SLOTS,)))


def _with_staged_weights(body, layer, half, n_in):
    def kernel(*refs):
        ins, rest = refs[:n_in], refs[n_in:]
        w1, w3, w2 = rest[-3:]

        @pl.when(pl.program_id(0) == 0)
        def _():
            _stage_weights([w.at[layer, half] for w in ins[-3:]], (w1, w3, w2))

        body(*ins[:-3], w1, w3, w2, *rest[:-3])

    return kernel


def _rows_of_batch(b):
    return pl.ds(b, TS, stride=B)


_ROWS_SCRATCH = pltpu.VMEM((D // LANES, TM, LANES), F32)


def _put_batch(rows_scr, b, val):
    for j in range(D // LANES):
        rows_scr[j, _rows_of_batch(b), :] = val[:, j * LANES:(j + 1) * LANES]


def _get_batch(rows_scr, b):
    return jnp.concatenate([rows_scr[j, _rows_of_batch(b), :] for j in range(D // LANES)], axis=1)


def _put_rows(rows_scr, val):
    for j in range(D // LANES):
        rows_scr[j] = val[:, j * LANES:(j + 1) * LANES]


def _get_rows(rows_scr):
    return jnp.concatenate([rows_scr[j] for j in range(D // LANES)], axis=1)


def _ffn_first_kernel(x_ref, ctx_ref, pos_ref, mod_ref, g_ref, w1, w3, w2, o_ref, rows_scr):
    is_latent = pl.program_id(0) < NT_LAT
    for b in range(B):
        _put_batch(rows_scr, b, jnp.where(is_latent, x_ref[b] + pos_ref[...], ctx_ref[b]))
    o_ref[...] = _ffn(_get_rows(rows_scr), mod_ref, 0, g_ref[0:1], g_ref[1:2], w1, w3, w2)


def _ffn_kernel(x_ref, mod_ref, g_ref, w1, w3, w2, o_ref):
    for r in range(x_ref.shape[0] // TM):
        rows = slice(r * TM, (r + 1) * TM)
        o_ref[rows, :] = _ffn(x_ref[rows, :], mod_ref, 0, g_ref[0:1], g_ref[1:2], w1, w3, w2)


def _ffn_first(x, ctx, pos, mods, g, w1, w3, w2):
    lat = lambda t: jnp.minimum(t, NT_LAT - 1)
    return pl.pallas_call(
        _with_staged_weights(_ffn_first_kernel, 0, 0, n_in=8),
        grid=(NT,),
        in_specs=[pl.BlockSpec((B, TS, D), lambda t: (0, lat(t), 0)),
                  pl.BlockSpec((B, TS, D), lambda t: (0, jnp.maximum(t - NT_LAT, 0), 0)),
                  pl.BlockSpec((TS, D), lambda t: (lat(t), 0)),
                  _mod_spec(), _resident((6, D))] + _WEIGHT_SPECS,
        out_specs=_row_spec(),
        out_shape=jax.ShapeDtypeStruct((ROWS, D), F32),
        scratch_shapes=[_ROWS_SCRATCH] + _WEIGHT_SCRATCH,
        compiler_params=_cparams(("arbitrary",)),
        name="ffn_first",
    )(x, ctx, pos, mods, g, w1, w3, w2)


def _ffn_plain(x, mods, g, w1, w3, w2, layer):
    return pl.pallas_call(
        _with_staged_weights(_ffn_kernel, layer, 0, n_in=6),
        grid=(NT // 2,),
        in_specs=[pl.BlockSpec((2 * TM, D), lambda t: (t, 0)),
                  pl.BlockSpec((None, B, N_MOD * D),
                               lambda t: (jnp.where(t < NT_LAT // 2, 0, 1), 0, 0)),
                  _resident((6, D))] + _WEIGHT_SPECS,
        out_specs=pl.BlockSpec((2 * TM, D), lambda t: (t, 0)),
        out_shape=jax.ShapeDtypeStruct((ROWS, D), F32),
        scratch_shapes=_WEIGHT_SCRATCH,
        compiler_params=_cparams(("arbitrary",)),
        name="ffn_pre",
    )(x, mods, g, w1, w3, w2)


def _mix_ffn_core(y, x, mod_ref, g_ref, wo_ref, bo_ref, w1, w3, w2, **ffn_kwargs):
    z = _dot(y.astype(BF16), wo_ref[...]) + bo_ref[...]
    x = x + _gated(_rmsnorm(z, g_ref[3:4]), _mod(mod_ref, 5))
    return _ffn(x, mod_ref, 6, g_ref[4:5], g_ref[5:6], w1, w3, w2, **ffn_kwargs)


def _hy_out_ffn_kernel(ylo_ref, yhi_ref, x_ref, mod_ref, g_ref, wo_ref, bo_ref, w1, w3, w2,
                       o_ref, rows_scr):
    halves = (ylo_ref, yhi_ref)
    for b in range(B):
        for j in range(D // LANES):
            col = (b * (D // TC) + j // 2) * LANES
            rows_scr[j, _rows_of_batch(b), :] = halves[j % 2][:, col:col + LANES]
    o_ref[...] = _mix_ffn_core(_get_rows(rows_scr), x_ref[...], mod_ref, g_ref, wo_ref, bo_ref,
                               w1, w3, w2)


def _hy_out_ffn(y_lo, y_hi, x, mods, g, w_out, b_out, w1, w3, w2):
    half = pl.BlockSpec((TS, B * D // 2), lambda t: (t, 0))
    return pl.pallas_call(
        _with_staged_weights(_hy_out_ffn_kernel, 0, 1, n_in=10),
        grid=(NT,),
        in_specs=[half, half, _row_spec(), _mod_spec(), _resident((6, D)), _resident((D, D)),
                  _resident((1, D))] + _WEIGHT_SPECS,
        out_specs=_row_spec(),
        out_shape=jax.ShapeDtypeStruct((ROWS, D), F32),
        scratch_shapes=[_ROWS_SCRATCH] + _WEIGHT_SCRATCH,
        compiler_params=_cparams(("arbitrary",)),
        name="hyena_out_ffn",
    )(y_lo, y_hi, x, mods, g, w_out, b_out, w1, w3, w2)


def _seq_first(t):
    return jnp.logical_or(t == 0, t == NI_LAT)


def _seq_last(t):
    return jnp.logical_or(t == NI_LAT - 1, t == NI - 1)


def _halo_specs(order, lo_rows, hi_rows):
    nlo = ROWS // lo_rows
    nhi = ROWS // hi_rows
    return [
        pl.BlockSpec((lo_rows, D), lambda s: (jnp.maximum(order(s) * (TI // lo_rows) - 1, 0), 0)),
        pl.BlockSpec((TI, D), lambda s: (order(s), 0)),
        pl.BlockSpec((hi_rows, D),
                     lambda s: (jnp.minimum((order(s) + 1) * (TI // hi_rows), nhi - 1), 0)),
    ]


def _hy_in_kernel(xp_ref, x_ref, xn_ref, mod_ref, g_ref, w_ref, b_ref, cw_ref, cb_ref,
                  x0_ref, p_ref, x0_scr, p_scr):
    t = pl.program_id(0)
    xa = jnp.concatenate([xp_ref[...], x_ref[...], xn_ref[...]], axis=0)
    h = _modulate(_rmsnorm(xa, g_ref[2:3]), _mod(mod_ref, 3), _mod(mod_ref, 4)).astype(BF16)
    keep_lo = jnp.where(_seq_first(t), 0.0, 1.0).astype(F32)
    keep_hi = jnp.where(_seq_last(t), 0.0, 1.0).astype(F32)

    def conv_cols(c0):
        cols = slice(c0, c0 + MXU_N)
        u = _dot(h, w_ref[:, cols]) + b_ref[:, cols]
        lo = jnp.concatenate([u[0:B] * keep_lo, u[B:TI]], axis=0)
        hi = jnp.concatenate([u[2 * B:TI + B], u[TI + B:TI + 2 * B] * keep_hi], axis=0)
        return (cb_ref[:, cols] + cw_ref[0:1, cols] * lo + cw_ref[1:2, cols] * u[B:TI + B]
                + cw_ref[2:3, cols] * hi)

    slabs = MXU_N // LANES
    for c in range(D // MXU_N):
        c0 = c * MXU_N
        vals = (conv_cols(c0), conv_cols(D + c0) * conv_cols(2 * D + c0))
        for out_ref, scr, val in zip((x0_ref, p_ref), (x0_scr, p_scr), vals):
            for j in range(slabs):
                scr[c * slabs + j] = val[:, j * LANES:(j + 1) * LANES]
            for b in range(B):
                out_ref[:, b * D + c0:b * D + c0 + MXU_N] = jnp.concatenate(
                    [scr[c * slabs + j, pl.ds(b, TI // B, stride=B), :] for j in range(slabs)], axis=1)


def _hy_in(x, mods, g, w_in, b_in, conv_w, conv_b):
    wide = pl.BlockSpec((TI // B, B * D), lambda t: (t, 0))
    rows_scratch = pltpu.VMEM((D // LANES, TI, LANES), F32)
    return pl.pallas_call(
        _hy_in_kernel,
        grid=(NI,),
        in_specs=_halo_specs(lambda s: s, B, B) + [
            _mod_spec(NI_LAT), _resident((6, D)), _resident((D, 3 * D)), _resident((1, 3 * D)),
            _resident((3, 3 * D)), _resident((1, 3 * D))],
        out_specs=[wide, wide],
        out_shape=[jax.ShapeDtypeStruct((ROWS // B, B * D), F32)] * 2,
        scratch_shapes=[rows_scratch, rows_scratch],
        compiler_params=_cparams(("arbitrary",)),
        name="hyena_in",
    )(x, x, x, mods, g, w_in, b_in, conv_w, conv_b)


def _filter_kernel(z_ref, fw0, fb0, fw1, fb1, fw2, fb2, freq, fwout, deltas, hf_ref, hb_ref):
    z = z_ref[...]
    h = jnp.sin(freq[0:1] * (_dot3(z, fw0[...]) + fb0[...]))
    h = jnp.sin(freq[1:2] * (_dot3(h, fw1[...]) + fb1[...]))
    h = jnp.sin(freq[2:3] * (_dot3(h, fw2[...]) + fb2[...]))
    filt = _dot3(h, fwout[...])
    decay = jnp.exp(-z[:, 0:1] * deltas[...])
    hf_ref[...] = filt[:, 0:D] * decay
    hb_ref[...] = filt[:, D:2 * D] * decay


def _filters(n, fw0, fb0, fw1, fb1, fw2, fb2, freq, fwout):
    t = np.linspace(0.0, 1.0, n)[:, None]
    bands = np.linspace(1e-4, HY_BANDS - 1, HY_BANDS)[None]
    phase = bands * (2.0 * math.pi * np.arange(n)[:, None] / n)
    zp = np.zeros((n, LANES), np.float32)
    zp[:, :HY_EMB] = np.concatenate([t, np.cos(phase), -np.sin(phase)], axis=-1)
    zp = np.concatenate([zp[r::4] for r in range(4)], axis=0)
    fw0p = jnp.zeros((LANES, HY_HID), F32).at[:HY_EMB].set(fw0)
    max_decay = math.log(HY_DECAY_TARGET) / HY_FAST_DECAY
    min_decay = math.log(HY_DECAY_TARGET) / HY_SLOW_DECAY
    deltas = np.abs(np.linspace(min_decay, max_decay, D))[None].astype(np.float32)
    tl = 256
    row = lambda i: (i, 0)
    return pl.pallas_call(
        _filter_kernel,
        grid=(n // tl,),
        in_specs=[pl.BlockSpec((tl, LANES), row), _resident((LANES, HY_HID)), _resident((1, HY_HID)),
                  _resident((HY_HID, HY_HID)), _resident((1, HY_HID)),
                  _resident((HY_HID, HY_HID)), _resident((1, HY_HID)),
                  _resident((3, HY_HID)), _resident((HY_HID, 2 * D)), _resident((1, D))],
        out_specs=[pl.BlockSpec((tl, D), row)] * 2,
        out_shape=[jax.ShapeDtypeStruct((n, D), F32)] * 2,
        compiler_params=_cparams(("arbitrary",)),
        name="hyena_filter",
    )(zp, fw0p, fb0[None], fw1, fb1[None], fw2, fb2[None], freq, fwout, deltas)


def _dft_tables(n):
    q = 1 << (int(math.log2(n)) // 2 + 1)
    m = np.arange(n, dtype=np.int64)[None, :]

    def thin(k):
        ang = ((k[:, None] * m) % (2 * n)) * (math.pi / n)
        return jnp.asarray(np.cos(ang), F32), jnp.asarray(np.sin(ang), F32)

    ca, sa = (v[:, None, :] for v in thin(q * np.arange(n // q, dtype=np.int64)))
    cb, sb = (v[None, :, :] for v in thin(np.arange(q, dtype=np.int64)))
    return (ca * cb - sa * sb).reshape(n, n), (sa * cb + ca * sb).reshape(n, n)


def _alt_sign(rows, cols):
    r = lax.broadcasted_iota(jnp.int32, (rows, cols), 0)
    return (1 - 2 * (r & 1)).astype(F32)


_N_GROUPS = 4
_ROOT_HALF = math.sqrt(0.5)


def _twiddles(g):
    k = np.arange(g)[:, None]
    parts = [f(r * k * (math.pi / (4 * g))) for r in (1, 2, 3) for f in (np.cos, np.sin)]
    return jnp.broadcast_to(jnp.asarray(np.concatenate(parts, axis=0), F32), (6 * g, TC))


def _cmul(ar, as_, br, bi):
    return ar * br + as_ * bi, as_ * br - ar * bi


def _fwd4(quarters, c, s, tw_ref, rows, g):
    ts = []
    for r, q in enumerate(quarters):
        qr, qs = _dot(c, q), _dot(s, q)
        if r:
            wc = tw_ref[pl.ds((2 * r - 2) * g + rows.start, rows.stop - rows.start), :]
            ws = tw_ref[pl.ds((2 * r - 1) * g + rows.start, rows.stop - rows.start), :]
            qr, qs = wc * qr - ws * qs, wc * qs + ws * qr
        ts.append((qr, qs))
    (t0r, t0s), (t1r, t1s), (t2r, t2s), (t3r, t3s) = ts
    er, es, fr, fs = t0r + t2r, t0s + t2s, t0r - t2r, t0s - t2s
    pr, ps, dr, ds = t1r + t3r, t1s + t3s, t1r - t3r, t1s - t3s
    return ((er + pr, es + ps), (er - pr, es - ps), (fr + ds, fs - dr), (fr - ds, fs + dr))


def _mid_freqs(sums):
    s0, s1, s2, s3 = sums
    a, b = _ROOT_HALF * (s1 - s3), _ROOT_HALF * (s1 + s3)
    return (s0 + a, s2 + b), (s0 - a, b - s2)


def _alt_sums(quarters):
    alt = _alt_sign(*quarters[0].shape)
    return [jnp.sum(q * alt, axis=0, keepdims=True) for q in quarters]


def _spectrum_kernel(hf_ref, hb_ref, c_ref, s_ref, tw_ref, kr_ref, ki_ref, kn_ref, *, g):
    hf = hf_ref[...]
    row = lax.broadcasted_iota(jnp.int32, hf.shape, 0)
    hb = jnp.where(row == 0, 0.0, hb_ref[...])
    cos_part = hf + hb
    sin_part = hb - hf
    cq = [cos_part[r * g:(r + 1) * g] for r in range(4)]
    sq = [sin_part[r * g:(r + 1) * g] for r in range(4)]
    rows = slice(0, g)
    c, s = c_ref[...], s_ref[...]
    groups_c = _fwd4([q.astype(BF16) for q in cq], c, s, tw_ref, rows, g)
    groups_s = _fwd4([q.astype(BF16) for q in sq], c, s, tw_ref, rows, g)
    n_fft = 8 * g
    k = lax.broadcasted_iota(jnp.int32, (g, hf.shape[1]), 0)
    scale = jnp.where(k == 0, 1.0 / n_fft, 2.0 / n_fft)
    for grp in range(_N_GROUPS):
        kr_ref[grp * g:(grp + 1) * g, :] = groups_c[grp][0] * scale
        ki_ref[grp * g:(grp + 1) * g, :] = groups_s[grp][1] * scale
    (cg, _), (c3g, _) = _mid_freqs(_alt_sums(cq))
    (_, sg), (_, s3g) = _mid_freqs(_alt_sums(sq))
    mids = [v * (2.0 / n_fft) for v in (cg, sg, c3g, s3g)]
    kn_ref[...] = jnp.concatenate(mids + [jnp.zeros((B - 4, hf.shape[1]), F32)], axis=0)


def _spectrum(hf, hb, ctab, stab, tw):
    n = hf.shape[0]
    g = n // 4
    col = lambda j: (0, j)
    return pl.pallas_call(
        functools.partial(_spectrum_kernel, g=g),
        grid=(D // TC,),
        in_specs=[pl.BlockSpec((n, TC), col)] * 2 + [_resident((g, g))] * 2
        + [_resident((6 * g, TC))],
        out_specs=[pl.BlockSpec((n, TC), col)] * 2 + [pl.BlockSpec((B, TC), col)],
        out_shape=[jax.ShapeDtypeStruct((n, D), F32)] * 2 + [jax.ShapeDtypeStruct((B, D), F32)],
        compiler_params=_cparams(("arbitrary",)),
        name="hyena_spectrum",
    )(hf, hb, ctab, stab, tw)


def _steps(refs, first, count):
    rows = pl.ds(first, count, stride=4)
    return jnp.concatenate([r[rows, :] for r in refs], axis=1)


def _long_conv_rows(t0, g, p_refs, x0_refs, spec, bias_ref, o_refs, scratch):
    kr_ref, ki_ref, kn_ref, c_ref, s_ref, tw_ref = spec
    q_scr, u_scr = scratch
    tk = min(g, 512)
    r0 = t0 // 4
    seq = slice(r0, r0 + g)
    chunks = [slice(k * tk, (k + 1) * tk) for k in range(g // tk)]
    quarters = [_steps(p_refs, t0 + r, g) for r in range(4)]
    tc = quarters[0].shape[1]
    for r in range(4):
        q_scr[r, seq, :] = quarters[r].astype(BF16)
    for rows in chunks:
        dst = slice(r0 + rows.start, r0 + rows.stop)
        c, s = c_ref[rows, :], s_ref[rows, :]
        groups = _fwd4([q_scr[r, seq, :] for r in range(4)], c, s, tw_ref, rows, g)
        ys = []
        for grp, (xr, xs) in enumerate(groups):
            k_rows = pl.ds(grp * g + rows.start, tk)
            ys.append(_cmul(xr, xs, kr_ref[k_rows, :], ki_ref[k_rows, :]))
        (y1r, y1s), (y2r, y2s), (y3r, y3s), (y4r, y4s) = ys
        pr, ps, mr, ms = y1r + y2r, y1s + y2s, y1r - y2r, y1s - y2s
        qr, qs, nr, ns = y3r + y4r, y3s + y4s, y3r - y4r, y3s - y4s
        us = [(pr + qr, ps + qs), (mr - ns, ms + nr), (pr - qr, ps - qs), (mr + ns, ms - nr)]
        for r, (ur, us_) in enumerate(us):
            if r:
                wc = tw_ref[pl.ds((2 * r - 2) * g + rows.start, tk), :]
                ws = tw_ref[pl.ds((2 * r - 1) * g + rows.start, tk), :]
                ur, us_ = ur * wc + us_ * ws, us_ * wc - ur * ws
            u_scr[2 * r, dst, :] = ur.astype(BF16)
            u_scr[2 * r + 1, dst, :] = us_.astype(BF16)
    (xgr, xgs), (x3r, x3s) = _mid_freqs(_alt_sums(quarters))
    ygr, ygs = _cmul(xgr, xgs, kn_ref[0:1, :], kn_ref[1:2, :])
    y3r_, y3s_ = _cmul(x3r, x3s, kn_ref[2:3, :], kn_ref[3:4, :])
    a = _ROOT_HALF
    mids = [ygr + y3r_, a * (ygr + ygs - y3r_ + y3s_), ygs - y3s_, a * (ygs - ygr + y3r_ + y3s_)]
    alt_chunk = _alt_sign(tk, tc)
    bias = bias_ref[...]
    for rows in chunks:
        c, s = c_ref[rows, :], s_ref[rows, :]
        for r in range(4):
            first = t0 + 4 * rows.start + r
            y = (_dot(c, u_scr[2 * r, seq, :]) + _dot(s, u_scr[2 * r + 1, seq, :])
                 + alt_chunk * mids[r])
            out = _steps(x0_refs, first, tk) * (y + _steps(p_refs, first, tk) * bias)
            for q, o_ref in enumerate(o_refs):
                o_ref[pl.ds(first, tk, stride=4), :] = out[:, q * LANES:(q + 1) * LANES]


_N_SPEC = 6


def _long_conv_kernel(*refs):
    p_refs, x0_refs, bias_ref = refs[0:2], refs[2:4], refs[4]
    lat, ctx = refs[5:5 + _N_SPEC], refs[5 + _N_SPEC:5 + 2 * _N_SPEC]
    o_refs, scratch = refs[5 + 2 * _N_SPEC:7 + 2 * _N_SPEC], refs[7 + 2 * _N_SPEC:]
    _long_conv_rows(0, L // 4, p_refs, x0_refs, lat, bias_ref, o_refs, scratch)
    _long_conv_rows(L, CTX // 4, p_refs, x0_refs, ctx, bias_ref, o_refs, scratch)


def _long_conv(p2, x02, bias, spec_lat, spec_ctx):
    nc = D // TC
    halves = [pl.BlockSpec((ROWS // B, LANES),
                           functools.partial(lambda q, j: (0, 2 * ((j % B) * nc + j // B) + q), q))
              for q in range(TC // LANES)]
    ch = lambda j: (0, j // B)

    def spec_specs(g):
        return ([pl.BlockSpec((4 * g, TC), ch)] * 2 + [pl.BlockSpec((B, TC), ch)]
                + [_resident((g, g))] * 2 + [_resident((6 * g, TC))])

    half_out = pl.BlockSpec((ROWS // B, LANES), lambda j: (0, (j % B) * nc + j // B))
    quarter_rows = ROWS // (4 * B)
    return pl.pallas_call(
        _long_conv_kernel,
        grid=(B * D // TC,),
        scratch_shapes=[pltpu.VMEM((4, quarter_rows, TC), BF16),
                        pltpu.VMEM((8, quarter_rows, TC), BF16)],
        in_specs=halves + halves + [pl.BlockSpec((1, TC), ch)] + spec_specs(L // 4)
        + spec_specs(CTX // 4),
        out_specs=[half_out] * (TC // LANES),
        out_shape=[jax.ShapeDtypeStruct((ROWS // B, B * D * LANES // TC), F32)] * (TC // LANES),
        compiler_params=_cparams(("arbitrary",)),
        name="hyena_long_conv",
    )(p2, p2, x02, x02, bias, *spec_lat, *spec_ctx)


def _gelu_tanh(x):
    return x * (0.5 * (1.0 + jnp.tanh(math.sqrt(2.0 / math.pi) * (x + 0.044715 * (x * x * x)))))


def _rg_coeffs(xc, hd, wai_ref, bai_ref, lam_ref, a_scr, b_scr):
    sl = slice(hd * RG_BLOCK, (hd + 1) * RG_BLOCK)
    lam = lam_ref[:, sl]
    softplus_neg = jnp.maximum(-lam, 0.0) + jnp.log1p(jnp.exp(-jnp.abs(lam)))
    rate = (-RG_C * math.log2(math.e)) * softplus_neg
    pre = _dot(xc.astype(BF16), wai_ref[hd]) + bai_ref[:, 2 * hd * RG_BLOCK:2 * (hd + 1) * RG_BLOCK]
    gates = _sigmoid(pre)
    a = jnp.exp2(gates[:, 0:RG_BLOCK] * rate)
    a_scr[:, sl] = a
    b_scr[:, sl] = jnp.sqrt((1.0 - a) * (1.0 + a)) * gates[:, RG_BLOCK:2 * RG_BLOCK] * xc


def _scan_tile(a_scr, b_scr, h_scr, emit, reverse):
    steps = a_scr.shape[0] // B

    def body(k, h):
        t = steps - 1 - k if reverse else k
        r0 = pl.multiple_of(t * B, B)
        h = a_scr[pl.ds(r0, B), :] * h + b_scr[pl.ds(r0, B), :]
        emit(r0, h)
        return h

    h_scr[...] = lax.fori_loop(0, steps, body, h_scr[...], unroll=8)


def _rg_fwd_order(s):
    return jnp.where(s < NI_CTX, NI_LAT + s, s - NI_CTX)


def _rg_in_kernel(xp_ref, x_ref, xn_ref, mod_ref, g_ref, w_ref, b_ref, cw_ref, cb_ref,
                  wai_ref, bai_ref, lam_ref, xc_ref, gate_ref, hs_ref, a_scr, b_scr, h_scr):
    s = pl.program_id(0)
    t = _rg_fwd_order(s)
    xa = jnp.concatenate([xp_ref[...], x_ref[...], xn_ref[...]], axis=0)
    h = _modulate(_rmsnorm(xa, g_ref[2:3]), _mod(mod_ref, 3), _mod(mod_ref, 4)).astype(BF16)
    keep_lo = jnp.where(_seq_first(t), 0.0, 1.0).astype(F32)
    keep_hi = jnp.where(_seq_last(t), 0.0, 1.0).astype(F32)
    for hd in range(RG_HEADS):
        sl = slice(hd * RG_BLOCK, (hd + 1) * RG_BLOCK)
        rec = slice(D + hd * RG_BLOCK, D + (hd + 1) * RG_BLOCK)
        gate_ref[:, sl] = (_dot(h[B:TI + B], w_ref[:, sl]) + b_ref[:, sl]).astype(BF16)
        u = _dot(h, w_ref[:, rec]) + b_ref[:, rec]
        taps = (jnp.concatenate([u[0:B] * keep_lo, u[B:TI]], axis=0),
                u[B:TI + B],
                jnp.concatenate([u[2 * B:TI + B], u[TI + B:TI + 2 * B] * keep_hi], axis=0),
                jnp.concatenate([u[3 * B:TI + B], u[TI + B:TI + 3 * B] * keep_hi], axis=0))
        xc = cb_ref[:, sl]
        for k, tap in enumerate(taps):
            xc = xc + cw_ref[k:k + 1, sl] * tap
        xc_ref[:, sl] = xc
        _rg_coeffs(xc, hd, wai_ref, bai_ref, lam_ref, a_scr, b_scr)

    @pl.when(s == 0)
    def _():
        h_scr[...] = jnp.zeros((B, D), F32)

    def emit(r0, hv):
        hs_ref[pl.ds(r0, B), :] = hv

    _scan_tile(a_scr, b_scr, h_scr, emit, reverse=False)


def _rg_gate_specs():
    return [_resident((RG_HEADS, RG_BLOCK, 2 * RG_BLOCK)), _resident((1, 2 * D)), _resident((1, D))]


def _rg_in(x, mods, g, w_in, b_in, conv_w, conv_b, wai, bai, lam):
    order = _rg_fwd_order
    mod_spec = pl.BlockSpec((None, B, N_MOD * D),
                            lambda s: (jnp.where(order(s) < NI_LAT, 0, 1), 0, 0))
    out_spec = pl.BlockSpec((TI, D), lambda s: (order(s), 0))
    return pl.pallas_call(
        _rg_in_kernel,
        grid=(NI,),
        in_specs=_halo_specs(order, B, 2 * B) + [
            mod_spec, _resident((6, D)), _resident((D, 2 * D)), _resident((1, 2 * D)),
            _resident((4, D)), _resident((1, D))] + _rg_gate_specs(),
        out_specs=[out_spec] * 3,
        out_shape=[jax.ShapeDtypeStruct((ROWS, D), F32), jax.ShapeDtypeStruct((ROWS, D), BF16),
                   jax.ShapeDtypeStruct((ROWS, D), F32)],
        scratch_shapes=[pltpu.VMEM((TI, D), F32), pltpu.VMEM((TI, D), F32),
                        pltpu.VMEM((B, D), F32)],
        compiler_params=_cparams(("arbitrary",)),
        name="rglru_in_fwd_scan",
    )(x, x, x, mods, g, w_in, b_in, conv_w, conv_b, wai, bai, lam)


def _rg_tile_coeffs(xc_ref, wai_ref, bai_ref, lam_ref, a_scr, b_scr):
    for hd in range(RG_HEADS):
        xc = xc_ref[:, hd * RG_BLOCK:(hd + 1) * RG_BLOCK]
        _rg_coeffs(xc, hd, wai_ref, bai_ref, lam_ref, a_scr, b_scr)


def _rg_ctx_bwd_kernel(xc_ref, wai_ref, bai_ref, lam_ref, h_ref, a_scr, b_scr, h_scr):
    _rg_tile_coeffs(xc_ref, wai_ref, bai_ref, lam_ref, a_scr, b_scr)

    @pl.when(pl.program_id(0) == 0)
    def _():
        h_scr[...] = jnp.zeros((B, D), F32)

    _scan_tile(a_scr, b_scr, h_scr, lambda r0, hv: None, reverse=True)
    h_ref[...] = h_scr[...]


def _rg_ctx_bwd(xc, wai, bai, lam):
    return pl.pallas_call(
        _rg_ctx_bwd_kernel,
        grid=(NT_CTX,),
        in_specs=[pl.BlockSpec((TM, D), lambda s: (NT - 1 - s, 0))] + _rg_gate_specs(),
        out_specs=pl.BlockSpec((B, D), lambda s: (0, 0)),
        out_shape=jax.ShapeDtypeStruct((B, D), F32),
        scratch_shapes=[pltpu.VMEM((TM, D), F32), pltpu.VMEM((TM, D), F32),
                        pltpu.VMEM((B, D), F32)],
        compiler_params=_cparams(("arbitrary",)),
        name="rglru_ctx_bwd_scan",
    )(xc, wai, bai, lam)


def _rg_out_ffn_kernel(xc_ref, gate_ref, hs_ref, x_ref, h0_ref, mod_ref, g_ref,
                       wai_ref, bai_ref, lam_ref, wo_ref, bo_ref, w1, w3, w2,
                       o_ref, a_scr, b_scr, hb_scr, h_scr, rows_scr):
    s = pl.program_id(0)

    def head_gates(hd):
        xc = xc_ref[:, hd * RG_BLOCK:(hd + 1) * RG_BLOCK]
        _rg_coeffs(xc, hd, wai_ref, bai_ref, lam_ref, a_scr, b_scr)

    @pl.when(s == 0)
    def _():
        h_scr[...] = h0_ref[...]
        for hd in range(RG_HEADS):
            head_gates(hd)

    @pl.when(s > 0)
    def _():
        y = (hs_ref[...] + hb_scr[...]) * _gelu_tanh(gate_ref[...].astype(F32))
        res = _mix_ffn_core(
            y, x_ref[...], mod_ref, g_ref, wo_ref, bo_ref, w1, w3, w2, chunks=F_QUARTERS,
            side_work=[functools.partial(head_gates, hd) for hd in range(RG_HEADS)])
        _put_rows(rows_scr, res)
        for b in range(B):
            o_ref[b] = _get_batch(rows_scr, b)

    @pl.when(s < NT_LAT)
    def _():
        def emit(r0, hv):
            hb_scr[pl.ds(r0, B), :] = hv

        _scan_tile(a_scr, b_scr, h_scr, emit, reverse=True)


def _rg_out_ffn(xc, gate, hs, x, h0, mods, g, wai, bai, lam, w_out, b_out, w1, w3, w2):
    scan_tile = lambda s: (jnp.maximum(NT_LAT - 1 - s, 0), 0)
    out_tile = lambda s: jnp.minimum(NT_LAT - s, NT_LAT - 1)
    prev = pl.BlockSpec((TM, D), lambda s: (out_tile(s), 0))
    return pl.pallas_call(
        _with_staged_weights(_rg_out_ffn_kernel, 1, 1, n_in=15),
        grid=(NT_LAT + 1,),
        in_specs=[pl.BlockSpec((TM, D), scan_tile), prev, prev, prev, _resident((B, D)),
                  pl.BlockSpec((None, B, N_MOD * D), lambda s: (0, 0, 0)), _resident((6, D))]
        + _rg_gate_specs() + [_resident((D, D)), _resident((1, D))] + _WEIGHT_SPECS,
        out_specs=pl.BlockSpec((B, TS, D), lambda s: (0, out_tile(s), 0)),
        out_shape=jax.ShapeDtypeStruct((B, L, D), F32),
        scratch_shapes=[pltpu.VMEM((TM, D), F32), pltpu.VMEM((TM, D), F32),
                        pltpu.VMEM((TM, D), F32), pltpu.VMEM((B, D), F32), _ROWS_SCRATCH]
        + _WEIGHT_SCRATCH,
        compiler_params=_cparams(("arbitrary",)),
        name="rglru_out_ffn",
    )(xc, gate, hs, x, h0, mods, g, wai, bai, lam, w_out, b_out, w1, w3, w2)


def _grid_pos():
    rows = L // GRID_W
    quarter = D // 4
    omega = POS_BASE ** (-np.arange(quarter) / quarter)

    def emb(q):
        ang = q[:, None] * omega[None]
        return jnp.asarray(np.concatenate([np.sin(ang), np.cos(ang)], axis=-1), F32)

    row_code = jnp.repeat(emb(np.arange(rows)), GRID_W, axis=0)
    col_code = jnp.tile(emb(np.arange(GRID_W)), (rows, 1))
    return jnp.concatenate([row_code, col_code], axis=-1)


def kernel(x, c, ctx, c_ctx, ada_w, ada_b, norm_g, ffn_w1, ffn_w3, ffn_w2, hy_w_in, hy_b_in, hy_conv_w, hy_conv_b, hy_fw0, hy_fb0, hy_fw1, hy_fb1, hy_fw2, hy_fb2, hy_freq, hy_fwout, hy_filt_bias, hy_w_out, hy_b_out, rg_w_in, rg_b_in, rg_conv_w, rg_conv_b, rg_wa, rg_ba, rg_wi, rg_bi, rg_lam, rg_w_out, rg_b_out):
    cc = jnp.concatenate([c, jnp.broadcast_to(c_ctx[None], (B, D))], axis=0)
    mods = _mods(cc, ada_w, ada_b).reshape(DEPTH, 2, B, N_MOD * D)
    w1, w3, w2 = ffn_w1, ffn_w3, ffn_w2

    g = norm_g[0]
    xs = _ffn_first(x, ctx, _grid_pos(), mods[0], g, w1, w3, w2)
    x0, p = _hy_in(xs, mods[0], g, hy_w_in[0].astype(BF16), hy_b_in[0][None],
                   hy_conv_w[0], hy_conv_b[0][None])
    fparams = (hy_fw0[0], hy_fb0[0], hy_fw1[0], hy_fb1[0], hy_fw2[0], hy_fb2[0],
               hy_freq[0], hy_fwout[0])
    specs = []
    for n in (L, CTX):
        hf, hb = _filters(n, *fparams)
        tables = [tab.astype(BF16) for tab in _dft_tables(n // 4)] + [_twiddles(n // 4)]
        specs.append(list(_spectrum(hf, hb, *tables)) + tables)
    y_lo, y_hi = _long_conv(p, x0, hy_filt_bias[0][None], *specs)
    xs = _hy_out_ffn(y_lo, y_hi, xs, mods[0], g, hy_w_out[0].astype(BF16), hy_b_out[0][None],
                     w1, w3, w2)

    g = norm_g[1]
    xs = _ffn_plain(xs, mods[1], g, w1, w3, w2, layer=1)
    wai = jnp.concatenate([rg_wa[0], rg_wi[0]], axis=-1).astype(BF16)
    per_head = lambda v: v.reshape(2, RG_HEADS, RG_BLOCK)
    bai = jnp.concatenate([per_head(rg_ba[0]), per_head(rg_bi[0])], axis=-1).reshape(2, 1, 2 * D)
    lam = rg_lam[0][:, None, :]
    xc, gate, hs = _rg_in(xs, mods[1], g, rg_w_in[0].astype(BF16), rg_b_in[0][None],
                          rg_conv_w[0], rg_conv_b[0][None], wai[0], bai[0], lam[0])
    h_ctx = _rg_ctx_bwd(xc, wai[1], bai[1], lam[1])
    return _rg_out_ffn(xc, gate, hs, xs, h_ctx, mods[1], g, wai[1], bai[1], lam[1],
                       rg_w_out[0].astype(BF16), rg_b_out[0][None], w1, w3, w2)
```

```python
import functools
import math

import jax
import jax.numpy as jnp
import numpy as np
from jax import lax
from jax.experimental import pallas as pl
from jax.experimental.pallas import tpu as pltpu

F32 = jnp.float32
BF16 = jnp.bfloat16

D = 1024
B = 8
LANES = 128
MXU_N = 256
L = 2048
CTX = 256
DEPTH = 2
GRID_W = 64
D_FF = 2816
N_MOD = 9
MACARON = 0.5
NORM_EPS = 1e-6
POS_BASE = 10000.0
HY_EMB = 33
HY_BANDS = 16
HY_HID = 64
HY_FAST_DECAY = 0.3
HY_SLOW_DECAY = 1.5
HY_DECAY_TARGET = 1e-2
RG_HEADS = 4
RG_BLOCK = D // RG_HEADS
RG_C = 8.0

ROWS_LAT = L * B
ROWS_CTX = CTX * B
ROWS = ROWS_LAT + ROWS_CTX
TM = 512
TS = TM // B
NT_LAT = ROWS_LAT // TM
NT_CTX = ROWS_CTX // TM
NT = NT_LAT + NT_CTX
TI = 1024
NI_LAT = ROWS_LAT // TI
NI_CTX = ROWS_CTX // TI
NI = NI_LAT + NI_CTX
F_CHUNKS = ((0, 1536), (1536, D_FF))
F_QUARTERS = ((0, 768), (768, 1536), (1536, 2304), (2304, D_FF))
TC = 256
VMEM_LIMIT = 58 * 1024 * 1024


def _cparams(sem):
    return pltpu.CompilerParams(dimension_semantics=sem, vmem_limit_bytes=VMEM_LIMIT)


def _resident(shape):
    nd = len(shape)
    return pl.BlockSpec(shape, lambda *_: (0,) * nd, pipeline_mode=pl.Buffered(1))


def _split(a):
    hi = a.astype(BF16)
    lo = (a - hi.astype(F32)).astype(BF16)
    return hi, lo


def _dot(a, b):
    return jnp.dot(a, b, preferred_element_type=F32)


def _dot3(a, b):
    ah, al = _split(a)
    bh, bl = _split(b)
    return _dot(ah, bh) + _dot(ah, bl) + _dot(al, bh)


def _rmsnorm(x, g):
    ms = jnp.mean(x * x, axis=-1, keepdims=True)
    return x * lax.rsqrt(ms + NORM_EPS) * g


def _mod(mod_ref, k):
    return mod_ref[:, k * D:(k + 1) * D]


def _per_batch(x, fn):
    rows = x.shape[0]
    return fn(x.reshape(rows // B, B, x.shape[1])).reshape(rows, x.shape[1])


def _modulate(xn, shift8, scale8):
    return _per_batch(xn, lambda v: v * (1.0 + scale8)[None] + shift8[None])


def _gated(z, gate8):
    return _per_batch(z, lambda v: v * gate8[None])


def _sigmoid(x):
    return jax.nn.sigmoid(x)


def _ffn(x, mod_ref, k0, g_pre, g_post, w1_ref, w3_ref, w2_ref, chunks=F_CHUNKS, side_work=()):
    h = _modulate(_rmsnorm(x, g_pre), _mod(mod_ref, k0), _mod(mod_ref, k0 + 1)).astype(BF16)
    y = None
    for idx, (c0, c1) in enumerate(chunks):
        a = _dot(h, w1_ref[:, c0:c1])
        b = _dot(h, w3_ref[:, c0:c1])
        if idx < len(side_work):
            side_work[idx]()
        act = (a * _sigmoid(a) * b).astype(BF16)
        part = _dot(act, w2_ref[c0:c1, :])
        y = part if y is None else y + part
    return x + MACARON * _gated(_rmsnorm(y, g_post), _mod(mod_ref, k0 + 2))


def _mods_kernel(cc_ref, w_ref, b_ref, o_ref):
    a = cc_ref[...]
    ah, al = _split(a * _sigmoid(a))
    wh, wl = _split(w_ref[...])
    r = _dot(jnp.concatenate([ah, al], axis=0), wh)
    o_ref[...] = r[0:2 * B] + r[2 * B:4 * B] + _dot(ah, wl) + b_ref[...]


def _mods(cc, ada_w, ada_b):
    tn = 1024
    n = N_MOD * D
    return pl.pallas_call(
        _mods_kernel,
        grid=(DEPTH, n // tn),
        in_specs=[
            pl.BlockSpec((2 * B, D), lambda i, j: (0, 0)),
            pl.BlockSpec((None, D, tn), lambda i, j: (i, 0, j)),
            pl.BlockSpec((None, 1, tn), lambda i, j: (i, 0, j)),
        ],
        out_specs=pl.BlockSpec((None, 2 * B, tn), lambda i, j: (i, 0, j)),
        out_shape=jax.ShapeDtypeStruct((DEPTH, 2 * B, n), F32),
        compiler_params=_cparams(("arbitrary", "arbitrary")),
        name="ada_mods",
    )(cc, ada_w, ada_b.reshape(DEPTH, 1, n))


def _mod_spec(nt_lat=NT_LAT):
    return pl.BlockSpec((None, B, N_MOD * D), lambda t: (jnp.where(t < nt_lat, 0, 1), 0, 0))


def _row_spec(width=D):
    return pl.BlockSpec((TM, width), lambda t: (t, 0))


def _wide_spec():
    return pl.BlockSpec((TS, B * D), lambda t: (t, 0))


W_CHUNKS = 8
W_SLOTS = 3
_WEIGHT_SPECS = [pl.BlockSpec(memory_space=pl.ANY)] * 3
_WEIGHT_SCRATCH = [pltpu.VMEM((D, D_FF), BF16), pltpu.VMEM((D, D_FF), BF16),
                   pltpu.VMEM((D_FF, D), BF16)]


def _stage_weights(srcs, dsts):
    chunks = [(src, dst, r0, dst.shape[0] // W_CHUNKS) for src, dst in zip(srcs, dsts)
              for r0 in range(0, dst.shape[0], dst.shape[0] // W_CHUNKS)]

    def run(stage_up, stage_down, sems):
        def copy(k):
            src, dst, r0, rows = chunks[k]
            stage = stage_up if dst.shape[1] == D_FF else stage_down
            return pltpu.make_async_copy(src.at[pl.ds(r0, rows), :], stage.at[k % W_SLOTS],
                                         sems.at[k % W_SLOTS])

        for k in range(W_SLOTS - 1):
            copy(k).start()
        for k, (_, dst, r0, rows) in enumerate(chunks):
            if k + W_SLOTS - 1 < len(chunks):
                copy(k + W_SLOTS - 1).start()
            copy(k).wait()
            stage = stage_up if dst.shape[1] == D_FF else stage_down
            dst[pl.ds(r0, rows), :] = stage[k % W_SLOTS].astype(BF16)

    pl.run_scoped(run, pltpu.VMEM((W_SLOTS, D // W_CHUNKS, D_FF), F32),
                  pltpu.VMEM((W_SLOTS, D_FF // W_CHUNKS, D), F32),
                  pltpu.SemaphoreType.DMA((W_SLOTS,)))


def _with_staged_weights(body, layer, half, n_in):
    def kernel(*refs):
        ins, rest = refs[:n_in], refs[n_in:]
        w1, w3, w2 = rest[-3:]

        @pl.when(pl.program_id(0) == 0)
        def _():
            _stage_weights([w.at[layer, half] for w in ins[-3:]], (w1, w3, w2))

        body(*ins[:-3], w1, w3, w2, *rest[:-3])

    return kernel


def _rows_of_batch(b):
    return pl.ds(b, TS, stride=B)


_ROWS_SCRATCH = pltpu.VMEM((D // LANES, TM, LANES), F32)


def _put_batch(rows_scr, b, val):
    for j in range(D // LANES):
        rows_scr[j, _rows_of_batch(b), :] = val[:, j * LANES:(j + 1) * LANES]


def _get_batch(rows_scr, b):
    return jnp.concatenate([rows_scr[j, _rows_of_batch(b), :] for j in range(D // LANES)], axis=1)


def _put_rows(rows_scr, val):
    for j in range(D // LANES):
        rows_scr[j] = val[:, j * LANES:(j + 1) * LANES]


def _get_rows(rows_scr):
    return jnp.concatenate([rows_scr[j] for j in range(D // LANES)], axis=1)


def _ffn_first_kernel(x_ref, ctx_ref, pos_ref, mod_ref, g_ref, w1, w3, w2, o_ref, rows_scr):
    is_latent = pl.program_id(0) < NT_LAT
    for b in range(B):
        _put_batch(rows_scr, b, jnp.where(is_latent, x_ref[b] + pos_ref[...], ctx_ref[b]))
    o_ref[...] = _ffn(_get_rows(rows_scr), mod_ref, 0, g_ref[0:1], g_ref[1:2], w1, w3, w2)


def _ffn_kernel(x_ref, mod_ref, g_ref, w1, w3, w2, o_ref):
    for r in range(x_ref.shape[0] // TM):
        rows = slice(r * TM, (r + 1) * TM)
        o_ref[rows, :] = _ffn(x_ref[rows, :], mod_ref, 0, g_ref[0:1], g_ref[1:2], w1, w3, w2)


def _ffn_first(x, ctx, pos, mods, g, w1, w3, w2):
    lat = lambda t: jnp.minimum(t, NT_LAT - 1)
    return pl.pallas_call(
        _with_staged_weights(_ffn_first_kernel, 0, 0, n_in=8),
        grid=(NT,),
        in_specs=[pl.BlockSpec((B, TS, D), lambda t: (0, lat(t), 0)),
                  pl.BlockSpec((B, TS, D), lambda t: (0, jnp.maximum(t - NT_LAT, 0), 0)),
                  pl.BlockSpec((TS, D), lambda t: (lat(t), 0)),
                  _mod_spec(), _resident((6, D))] + _WEIGHT_SPECS,
        out_specs=_row_spec(),
        out_shape=jax.ShapeDtypeStruct((ROWS, D), F32),
        scratch_shapes=[_ROWS_SCRATCH] + _WEIGHT_SCRATCH,
        compiler_params=_cparams(("arbitrary",)),
        name="ffn_first",
    )(x, ctx, pos, mods, g, w1, w3, w2)


def _ffn_plain(x, mods, g, w1, w3, w2, layer):
    return pl.pallas_call(
        _with_staged_weights(_ffn_kernel, layer, 0, n_in=6),
        grid=(NT // 2,),
        in_specs=[pl.BlockSpec((2 * TM, D), lambda t: (t, 0)),
                  pl.BlockSpec((None, B, N_MOD * D),
                               lambda t: (jnp.where(t < NT_LAT // 2, 0, 1), 0, 0)),
                  _resident((6, D))] + _WEIGHT_SPECS,
        out_specs=pl.BlockSpec((2 * TM, D), lambda t: (t, 0)),
        out_shape=jax.ShapeDtypeStruct((ROWS, D), F32),
        scratch_shapes=_WEIGHT_SCRATCH,
        compiler_params=_cparams(("arbitrary",)),
        name="ffn_pre",
    )(x, mods, g, w1, w3, w2)


def _mix_ffn_core(y, x, mod_ref, g_ref, wo_ref, bo_ref, w1, w3, w2, **ffn_kwargs):
    z = _dot(y.astype(BF16), wo_ref[...]) + bo_ref[...]
    x = x + _gated(_rmsnorm(z, g_ref[3:4]), _mod(mod_ref, 5))
    return _ffn(x, mod_ref, 6, g_ref[4:5], g_ref[5:6], w1, w3, w2, **ffn_kwargs)


def _hy_out_ffn_kernel(ylo_ref, yhi_ref, x_ref, mod_ref, g_ref, wo_ref, bo_ref, w1, w3, w2,
                       o_ref, rows_scr):
    halves = (ylo_ref, yhi_ref)
    for b in range(B):
        for j in range(D // LANES):
            col = (b * (D // TC) + j // 2) * LANES
            rows_scr[j, _rows_of_batch(b), :] = halves[j % 2][:, col:col + LANES]
    o_ref[...] = _mix_ffn_core(_get_rows(rows_scr), x_ref[...], mod_ref, g_ref, wo_ref, bo_ref,
                               w1, w3, w2)


def _hy_out_ffn(y_lo, y_hi, x, mods, g, w_out, b_out, w1, w3, w2):
    half = pl.BlockSpec((TS, B * D // 2), lambda t: (t, 0))
    return pl.pallas_call(
        _with_staged_weights(_hy_out_ffn_kernel, 0, 1, n_in=10),
        grid=(NT,),
        in_specs=[half, half, _row_spec(), _mod_spec(), _resident((6, D)), _resident((D, D)),
                  _resident((1, D))] + _WEIGHT_SPECS,
        out_specs=_row_spec(),
        out_shape=jax.ShapeDtypeStruct((ROWS, D), F32),
        scratch_shapes=[_ROWS_SCRATCH] + _WEIGHT_SCRATCH,
        compiler_params=_cparams(("arbitrary",)),
        name="hyena_out_ffn",
    )(y_lo, y_hi, x, mods, g, w_out, b_out, w1, w3, w2)


def _seq_first(t):
    return jnp.logical_or(t == 0, t == NI_LAT)


def _seq_last(t):
    return jnp.logical_or(t == NI_LAT - 1, t == NI - 1)


def _halo_specs(order, lo_rows, hi_rows):
    nlo = ROWS // lo_rows
    nhi = ROWS // hi_rows
    return [
        pl.BlockSpec((lo_rows, D), lambda s: (jnp.maximum(order(s) * (TI // lo_rows) - 1, 0), 0)),
        pl.BlockSpec((TI, D), lambda s: (order(s), 0)),
        pl.BlockSpec((hi_rows, D),
                     lambda s: (jnp.minimum((order(s) + 1) * (TI // hi_rows), nhi - 1), 0)),
    ]


def _hy_in_kernel(xp_ref, x_ref, xn_ref, mod_ref, g_ref, w_ref, b_ref, cw_ref, cb_ref,
                  x0_ref, p_ref, x0_scr, p_scr):
    t = pl.program_id(0)
    xa = jnp.concatenate([xp_ref[...], x_ref[...], xn_ref[...]], axis=0)
    h = _modulate(_rmsnorm(xa, g_ref[2:3]), _mod(mod_ref, 3), _mod(mod_ref, 4)).astype(BF16)
    keep_lo = jnp.where(_seq_first(t), 0.0, 1.0).astype(F32)
    keep_hi = jnp.where(_seq_last(t), 0.0, 1.0).astype(F32)

    def conv_cols(c0):
        cols = slice(c0, c0 + MXU_N)
        u = _dot(h, w_ref[:, cols]) + b_ref[:, cols]
        lo = jnp.concatenate([u[0:B] * keep_lo, u[B:TI]], axis=0)
        hi = jnp.concatenate([u[2 * B:TI + B], u[TI + B:TI + 2 * B] * keep_hi], axis=0)
        return (cb_ref[:, cols] + cw_ref[0:1, cols] * lo + cw_ref[1:2, cols] * u[B:TI + B]
                + cw_ref[2:3, cols] * hi)

    slabs = MXU_N // LANES
    for c in range(D // MXU_N):
        c0 = c * MXU_N
        vals = (conv_cols(c0), conv_cols(D + c0) * conv_cols(2 * D + c0))
        for out_ref, scr, val in zip((x0_ref, p_ref), (x0_scr, p_scr), vals):
            for j in range(slabs):
                scr[c * slabs + j] = val[:, j * LANES:(j + 1) * LANES]
            for b in range(B):
                out_ref[:, b * D + c0:b * D + c0 + MXU_N] = jnp.concatenate(
                    [scr[c * slabs + j, pl.ds(b, TI // B, stride=B), :] for j in range(slabs)], axis=1)


def _hy_in(x, mods, g, w_in, b_in, conv_w, conv_b):
    wide = pl.BlockSpec((TI // B, B * D), lambda t: (t, 0))
    rows_scratch = pltpu.VMEM((D // LANES, TI, LANES), F32)
    return pl.pallas_call(
        _hy_in_kernel,
        grid=(NI,),
        in_specs=_halo_specs(lambda s: s, B, B) + [
            _mod_spec(NI_LAT), _resident((6, D)), _resident((D, 3 * D)), _resident((1, 3 * D)),
            _resident((3, 3 * D)), _resident((1, 3 * D))],
        out_specs=[wide, wide],
        out_shape=[jax.ShapeDtypeStruct((ROWS // B, B * D), F32)] * 2,
        scratch_shapes=[rows_scratch, rows_scratch],
        compiler_params=_cparams(("arbitrary",)),
        name="hyena_in",
    )(x, x, x, mods, g, w_in, b_in, conv_w, conv_b)


def _filter_kernel(zt_ref, t_ref, fw0t, fb0, fw1t, fb1, fw2t, fb2, freq, fwout, deltas,
                   hf_ref, hb_ref):
    h = jnp.sin(freq[:, 0:1] * (_dot3(fw0t[...], zt_ref[...]) + fb0[...]))
    h = jnp.sin(freq[:, 1:2] * (_dot3(fw1t[...], h) + fb1[...]))
    h = jnp.sin(freq[:, 2:3] * (_dot3(fw2t[...], h) + fb2[...]))
    filt = _dot3(h.T, fwout[...])
    decay = jnp.exp(-t_ref[:, 0:1] * deltas[...])
    hf_ref[...] = filt[:, 0:D] * decay
    hb_ref[...] = filt[:, D:2 * D] * decay


def _filters(n, fw0, fb0, fw1, fb1, fw2, fb2, freq, fwout):
    t = np.linspace(0.0, 1.0, n)[:, None]
    bands = np.linspace(1e-4, HY_BANDS - 1, HY_BANDS)[None]
    phase = bands * (2.0 * math.pi * np.arange(n)[:, None] / n)
    zp = np.zeros((n, LANES), np.float32)
    zp[:, :HY_EMB] = np.concatenate([t, np.cos(phase), -np.sin(phase)], axis=-1)
    zp = np.concatenate([zp[r::4] for r in range(4)], axis=0)
    fw0t = jnp.zeros((HY_HID, LANES), F32).at[:, :HY_EMB].set(fw0.T)
    max_decay = math.log(HY_DECAY_TARGET) / HY_FAST_DECAY
    min_decay = math.log(HY_DECAY_TARGET) / HY_SLOW_DECAY
    deltas = np.abs(np.linspace(min_decay, max_decay, D))[None].astype(np.float32)
    tl = 256
    row = lambda i: (i, 0)
    unit = _resident((HY_HID, 1))
    return pl.pallas_call(
        _filter_kernel,
        grid=(n // tl,),
        in_specs=[pl.BlockSpec((LANES, tl), lambda i: (0, i)), pl.BlockSpec((tl, LANES), row),
                  _resident((HY_HID, LANES)), unit, _resident((HY_HID, HY_HID)), unit,
                  _resident((HY_HID, HY_HID)), unit, _resident((HY_HID, 3)),
                  _resident((HY_HID, 2 * D)), _resident((1, D))],
        out_specs=[pl.BlockSpec((tl, D), row)] * 2,
        out_shape=[jax.ShapeDtypeStruct((n, D), F32)] * 2,
        compiler_params=_cparams(("arbitrary",)),
        name="hyena_filter",
    )(np.ascontiguousarray(zp.T), zp, fw0t, fb0[:, None], fw1.T, fb1[:, None], fw2.T, fb2[:, None],
      freq.T, fwout, deltas)


def _dft_tables(n):
    q = 1 << (int(math.log2(n)) // 2 + 1)
    m = np.arange(n, dtype=np.int64)[None, :]

    def thin(k):
        ang = ((k[:, None] * m) % (2 * n)) * (math.pi / n)
        return jnp.asarray(np.cos(ang), F32), jnp.asarray(np.sin(ang), F32)

    ca, sa = (v[:, None, :] for v in thin(q * np.arange(n // q, dtype=np.int64)))
    cb, sb = (v[None, :, :] for v in thin(np.arange(q, dtype=np.int64)))
    return (ca * cb - sa * sb).reshape(n, n), (sa * cb + ca * sb).reshape(n, n)


def _alt_sign(rows, cols):
    r = lax.broadcasted_iota(jnp.int32, (rows, cols), 0)
    return (1 - 2 * (r & 1)).astype(F32)


_N_GROUPS = 4
_ROOT_HALF = math.sqrt(0.5)


def _twiddles(g):
    k = np.arange(g)[:, None]
    parts = [f(r * k * (math.pi / (4 * g))) for r in (1, 2, 3) for f in (np.cos, np.sin)]
    return jnp.broadcast_to(jnp.asarray(np.concatenate(parts, axis=0), F32), (6 * g, TC))


def _cmul(ar, as_, br, bi):
    return ar * br + as_ * bi, as_ * br - ar * bi


def _fwd4(quarters, c, s, tw_ref, rows, g):
    ts = []
    for r, q in enumerate(quarters):
        qr, qs = _dot(c, q), _dot(s, q)
        if r:
            wc = tw_ref[pl.ds((2 * r - 2) * g + rows.start, rows.stop - rows.start), :]
            ws = tw_ref[pl.ds((2 * r - 1) * g + rows.start, rows.stop - rows.start), :]
            qr, qs = wc * qr - ws * qs, wc * qs + ws * qr
        ts.append((qr, qs))
    (t0r, t0s), (t1r, t1s), (t2r, t2s), (t3r, t3s) = ts
    er, es, fr, fs = t0r + t2r, t0s + t2s, t0r - t2r, t0s - t2s
    pr, ps, dr, ds = t1r + t3r, t1s + t3s, t1r - t3r, t1s - t3s
    return ((er + pr, es + ps), (er - pr, es - ps), (fr + ds, fs - dr), (fr - ds, fs + dr))


def _mid_freqs(sums):
    s0, s1, s2, s3 = sums
    a, b = _ROOT_HALF * (s1 - s3), _ROOT_HALF * (s1 + s3)
    return (s0 + a, s2 + b), (s0 - a, b - s2)


def _alt_sums(quarters):
    alt = _alt_sign(*quarters[0].shape)
    return [jnp.sum(q * alt, axis=0, keepdims=True) for q in quarters]


def _spectrum_kernel(hf_ref, hb_ref, c_ref, s_ref, tw_ref, kr_ref, ki_ref, kn_ref, *, g):
    hf = hf_ref[...]
    row = lax.broadcasted_iota(jnp.int32, hf.shape, 0)
    hb = jnp.where(row == 0, 0.0, hb_ref[...])
    cos_part = hf + hb
    sin_part = hb - hf
    cq = [cos_part[r * g:(r + 1) * g] for r in range(4)]
    sq = [sin_part[r * g:(r + 1) * g] for r in range(4)]
    rows = slice(0, g)
    c, s = c_ref[...], s_ref[...]
    groups_c = _fwd4([q.astype(BF16) for q in cq], c, s, tw_ref, rows, g)
    groups_s = _fwd4([q.astype(BF16) for q in sq], c, s, tw_ref, rows, g)
    n_fft = 8 * g
    k = lax.broadcasted_iota(jnp.int32, (g, hf.shape[1]), 0)
    scale = jnp.where(k == 0, 1.0 / n_fft, 2.0 / n_fft)
    for grp in range(_N_GROUPS):
        kr_ref[grp * g:(grp + 1) * g, :] = groups_c[grp][0] * scale
        ki_ref[grp * g:(grp + 1) * g, :] = groups_s[grp][1] * scale
    (cg, _), (c3g, _) = _mid_freqs(_alt_sums(cq))
    (_, sg), (_, s3g) = _mid_freqs(_alt_sums(sq))
    mids = [v * (2.0 / n_fft) for v in (cg, sg, c3g, s3g)]
    kn_ref[...] = jnp.concatenate(mids + [jnp.zeros((B - 4, hf.shape[1]), F32)], axis=0)


def _spectrum(hf, hb, ctab, stab, tw):
    n = hf.shape[0]
    g = n // 4
    col = lambda j: (0, j)
    return pl.pallas_call(
        functools.partial(_spectrum_kernel, g=g),
        grid=(D // TC,),
        in_specs=[pl.BlockSpec((n, TC), col)] * 2 + [_resident((g, g))] * 2
        + [_resident((6 * g, TC))],
        out_specs=[pl.BlockSpec((n, TC), col)] * 2 + [pl.BlockSpec((B, TC), col)],
        out_shape=[jax.ShapeDtypeStruct((n, D), F32)] * 2 + [jax.ShapeDtypeStruct((B, D), F32)],
        compiler_params=_cparams(("arbitrary",)),
        name="hyena_spectrum",
    )(hf, hb, ctab, stab, tw)


def _steps(refs, first, count):
    rows = pl.ds(first, count, stride=4)
    return jnp.concatenate([r[rows, :] for r in refs], axis=1)


def _long_conv_rows(t0, g, p_refs, x0_refs, spec, bias_ref, o_refs, scratch):
    kr_ref, ki_ref, kn_ref, c_ref, s_ref, tw_ref = spec
    q_scr, u_scr = scratch
    tk = min(g, 512)
    r0 = t0 // 4
    seq = slice(r0, r0 + g)
    chunks = [slice(k * tk, (k + 1) * tk) for k in range(g // tk)]
    quarters = [_steps(p_refs, t0 + r, g) for r in range(4)]
    tc = quarters[0].shape[1]
    for r in range(4):
        q_scr[r, seq, :] = quarters[r].astype(BF16)
    for rows in chunks:
        dst = slice(r0 + rows.start, r0 + rows.stop)
        c, s = c_ref[rows, :], s_ref[rows, :]
        groups = _fwd4([q_scr[r, seq, :] for r in range(4)], c, s, tw_ref, rows, g)
        ys = []
        for grp, (xr, xs) in enumerate(groups):
            k_rows = pl.ds(grp * g + rows.start, tk)
            ys.append(_cmul(xr, xs, kr_ref[k_rows, :], ki_ref[k_rows, :]))
        (y1r, y1s), (y2r, y2s), (y3r, y3s), (y4r, y4s) = ys
        pr, ps, mr, ms = y1r + y2r, y1s + y2s, y1r - y2r, y1s - y2s
        qr, qs, nr, ns = y3r + y4r, y3s + y4s, y3r - y4r, y3s - y4s
        us = [(pr + qr, ps + qs), (mr - ns, ms + nr), (pr - qr, ps - qs), (mr + ns, ms - nr)]
        for r, (ur, us_) in enumerate(us):
            if r:
                wc = tw_ref[pl.ds((2 * r - 2) * g + rows.start, tk), :]
                ws = tw_ref[pl.ds((2 * r - 1) * g + rows.start, tk), :]
                ur, us_ = ur * wc + us_ * ws, us_ * wc - ur * ws
            u_scr[2 * r, dst, :] = ur.astype(BF16)
            u_scr[2 * r + 1, dst, :] = us_.astype(BF16)
    (xgr, xgs), (x3r, x3s) = _mid_freqs(_alt_sums(quarters))
    ygr, ygs = _cmul(xgr, xgs, kn_ref[0:1, :], kn_ref[1:2, :])
    y3r_, y3s_ = _cmul(x3r, x3s, kn_ref[2:3, :], kn_ref[3:4, :])
    a = _ROOT_HALF
    mids = [ygr + y3r_, a * (ygr + ygs - y3r_ + y3s_), ygs - y3s_, a * (ygs - ygr + y3r_ + y3s_)]
    alt_chunk = _alt_sign(tk, tc)
    bias = bias_ref[...]
    for rows in chunks:
        c, s = c_ref[rows, :], s_ref[rows, :]
        for r in range(4):
            first = t0 + 4 * rows.start + r
            y = (_dot(c, u_scr[2 * r, seq, :]) + _dot(s, u_scr[2 * r + 1, seq, :])
                 + alt_chunk * mids[r])
            out = _steps(x0_refs, first, tk) * (y + _steps(p_refs, first, tk) * bias)
            for q, o_ref in enumerate(o_refs):
                o_ref[pl.ds(first, tk, stride=4), :] = out[:, q * LANES:(q + 1) * LANES]


_N_SPEC = 6


def _long_conv_kernel(*refs):
    p_refs, x0_refs, bias_ref = refs[0:2], refs[2:4], refs[4]
    lat, ctx = refs[5:5 + _N_SPEC], refs[5 + _N_SPEC:5 + 2 * _N_SPEC]
    o_refs, scratch = refs[5 + 2 * _N_SPEC:7 + 2 * _N_SPEC], refs[7 + 2 * _N_SPEC:]
    _long_conv_rows(0, L // 4, p_refs, x0_refs, lat, bias_ref, o_refs, scratch)
    _long_conv_rows(L, CTX // 4, p_refs, x0_refs, ctx, bias_ref, o_refs, scratch)


def _long_conv(p2, x02, bias, spec_lat, spec_ctx):
    nc = D // TC
    halves = [pl.BlockSpec((ROWS // B, LANES),
                           functools.partial(lambda q, j: (0, 2 * ((j % B) * nc + j // B) + q), q))
              for q in range(TC // LANES)]
    ch = lambda j: (0, j // B)

    def spec_specs(g):
        return ([pl.BlockSpec((4 * g, TC), ch)] * 2 + [pl.BlockSpec((B, TC), ch)]
                + [_resident((g, g))] * 2 + [_resident((6 * g, TC))])

    half_out = pl.BlockSpec((ROWS // B, LANES), lambda j: (0, (j % B) * nc + j // B))
    quarter_rows = ROWS // (4 * B)
    return pl.pallas_call(
        _long_conv_kernel,
        grid=(B * D // TC,),
        scratch_shapes=[pltpu.VMEM((4, quarter_rows, TC), BF16),
                        pltpu.VMEM((8, quarter_rows, TC), BF16)],
        in_specs=halves + halves + [pl.BlockSpec((1, TC), ch)] + spec_specs(L // 4)
        + spec_specs(CTX // 4),
        out_specs=[half_out] * (TC // LANES),
        out_shape=[jax.ShapeDtypeStruct((ROWS // B, B * D * LANES // TC), F32)] * (TC // LANES),
        compiler_params=_cparams(("arbitrary",)),
        name="hyena_long_conv",
    )(p2, p2, x02, x02, bias, *spec_lat, *spec_ctx)


def _gelu_tanh(x):
    return x * (0.5 * (1.0 + jnp.tanh(math.sqrt(2.0 / math.pi) * (x + 0.044715 * (x * x * x)))))


def _rg_coeffs(xc, hd, wai_ref, bai_ref, lam_ref, a_scr, b_scr):
    sl = slice(hd * RG_BLOCK, (hd + 1) * RG_BLOCK)
    lam = lam_ref[:, sl]
    softplus_neg = jnp.maximum(-lam, 0.0) + jnp.log1p(jnp.exp(-jnp.abs(lam)))
    rate = (-RG_C * math.log2(math.e)) * softplus_neg
    pre = _dot(xc.astype(BF16), wai_ref[hd]) + bai_ref[:, 2 * hd * RG_BLOCK:2 * (hd + 1) * RG_BLOCK]
    gates = _sigmoid(pre)
    a = jnp.exp2(gates[:, 0:RG_BLOCK] * rate)
    a_scr[:, sl] = a
    b_scr[:, sl] = jnp.sqrt((1.0 - a) * (1.0 + a)) * gates[:, RG_BLOCK:2 * RG_BLOCK] * xc


def _scan_tile(a_scr, b_scr, h_scr, emit, reverse):
    steps = a_scr.shape[0] // B

    def body(k, h):
        t = steps - 1 - k if reverse else k
        r0 = pl.multiple_of(t * B, B)
        h = a_scr[pl.ds(r0, B), :] * h + b_scr[pl.ds(r0, B), :]
        emit(r0, h)
        return h

    h_scr[...] = lax.fori_loop(0, steps, body, h_scr[...], unroll=8)


def _rg_fwd_order(s):
    return jnp.where(s < NI_CTX, NI_LAT + s, s - NI_CTX)


def _rg_in_kernel(xp_ref, x_ref, xn_ref, mod_ref, g_ref, w_ref, b_ref, cw_ref, cb_ref,
                  wai_ref, bai_ref, lam_ref, xc_ref, gate_ref, hs_ref, a_scr, b_scr, h_scr):
    s = pl.program_id(0)
    t = _rg_fwd_order(s)
    xa = jnp.concatenate([xp_ref[...], x_ref[...], xn_ref[...]], axis=0)
    h = _modulate(_rmsnorm(xa, g_ref[2:3]), _mod(mod_ref, 3), _mod(mod_ref, 4)).astype(BF16)
    keep_lo = jnp.where(_seq_first(t), 0.0, 1.0).astype(F32)
    keep_hi = jnp.where(_seq_last(t), 0.0, 1.0).astype(F32)
    for hd in range(RG_HEADS):
        sl = slice(hd * RG_BLOCK, (hd + 1) * RG_BLOCK)
        rec = slice(D + hd * RG_BLOCK, D + (hd + 1) * RG_BLOCK)
        gate_ref[:, sl] = (_dot(h[B:TI + B], w_ref[:, sl]) + b_ref[:, sl]).astype(BF16)
        u = _dot(h, w_ref[:, rec]) + b_ref[:, rec]
        taps = (jnp.concatenate([u[0:B] * keep_lo, u[B:TI]], axis=0),
                u[B:TI + B],
                jnp.concatenate([u[2 * B:TI + B], u[TI + B:TI + 2 * B] * keep_hi], axis=0),
                jnp.concatenate([u[3 * B:TI + B], u[TI + B:TI + 3 * B] * keep_hi], axis=0))
        xc = cb_ref[:, sl]
        for k, tap in enumerate(taps):
            xc = xc + cw_ref[k:k + 1, sl] * tap
        xc_ref[:, sl] = xc
        _rg_coeffs(xc, hd, wai_ref, bai_ref, lam_ref, a_scr, b_scr)

    @pl.when(s == 0)
    def _():
        h_scr[...] = jnp.zeros((B, D), F32)

    def emit(r0, hv):
        hs_ref[pl.ds(r0, B), :] = hv

    _scan_tile(a_scr, b_scr, h_scr, emit, reverse=False)


def _rg_gate_specs():
    return [_resident((RG_HEADS, RG_BLOCK, 2 * RG_BLOCK)), _resident((1, 2 * D)), _resident((1, D))]


def _rg_in(x, mods, g, w_in, b_in, conv_w, conv_b, wai, bai, lam):
    order = _rg_fwd_order
    mod_spec = pl.BlockSpec((None, B, N_MOD * D),
                            lambda s: (jnp.where(order(s) < NI_LAT, 0, 1), 0, 0))
    out_spec = pl.BlockSpec((TI, D), lambda s: (order(s), 0))
    return pl.pallas_call(
        _rg_in_kernel,
        grid=(NI,),
        in_specs=_halo_specs(order, B, 2 * B) + [
            mod_spec, _resident((6, D)), _resident((D, 2 * D)), _resident((1, 2 * D)),
            _resident((4, D)), _resident((1, D))] + _rg_gate_specs(),
        out_specs=[out_spec] * 3,
        out_shape=[jax.ShapeDtypeStruct((ROWS, D), F32), jax.ShapeDtypeStruct((ROWS, D), BF16),
                   jax.ShapeDtypeStruct((ROWS, D), F32)],
        scratch_shapes=[pltpu.VMEM((TI, D), F32), pltpu.VMEM((TI, D), F32),
                        pltpu.VMEM((B, D), F32)],
        compiler_params=_cparams(("arbitrary",)),
        name="rglru_in_fwd_scan",
    )(x, x, x, mods, g, w_in, b_in, conv_w, conv_b, wai, bai, lam)


def _rg_tile_coeffs(xc_ref, wai_ref, bai_ref, lam_ref, a_scr, b_scr):
    for hd in range(RG_HEADS):
        xc = xc_ref[:, hd * RG_BLOCK:(hd + 1) * RG_BLOCK]
        _rg_coeffs(xc, hd, wai_ref, bai_ref, lam_ref, a_scr, b_scr)


def _rg_ctx_bwd_kernel(xc_ref, wai_ref, bai_ref, lam_ref, h_ref, a_scr, b_scr, h_scr):
    _rg_tile_coeffs(xc_ref, wai_ref, bai_ref, lam_ref, a_scr, b_scr)

    @pl.when(pl.program_id(0) == 0)
    def _():
        h_scr[...] = jnp.zeros((B, D), F32)

    _scan_tile(a_scr, b_scr, h_scr, lambda r0, hv: None, reverse=True)
    h_ref[...] = h_scr[...]


def _rg_ctx_bwd(xc, wai, bai, lam):
    return pl.pallas_call(
        _rg_ctx_bwd_kernel,
        grid=(NT_CTX,),
        in_specs=[pl.BlockSpec((TM, D), lambda s: (NT - 1 - s, 0))] + _rg_gate_specs(),
        out_specs=pl.BlockSpec((B, D), lambda s: (0, 0)),
        out_shape=jax.ShapeDtypeStruct((B, D), F32),
        scratch_shapes=[pltpu.VMEM((TM, D), F32), pltpu.VMEM((TM, D), F32),
                        pltpu.VMEM((B, D), F32)],
        compiler_params=_cparams(("arbitrary",)),
        name="rglru_ctx_bwd_scan",
    )(xc, wai, bai, lam)


def _rg_out_ffn_kernel(xc_ref, gate_ref, hs_ref, x_ref, h0_ref, mod_ref, g_ref,
                       wai_ref, bai_ref, lam_ref, wo_ref, bo_ref, w1, w3, w2,
                       o_ref, a_scr, b_scr, hb_scr, h_scr, rows_scr):
    s = pl.program_id(0)

    def head_gates(hd):
        xc = xc_ref[:, hd * RG_BLOCK:(hd + 1) * RG_BLOCK]
        _rg_coeffs(xc, hd, wai_ref, bai_ref, lam_ref, a_scr, b_scr)

    @pl.when(s == 0)
    def _():
        h_scr[...] = h0_ref[...]
        for hd in range(RG_HEADS):
            head_gates(hd)

    @pl.when(s > 0)
    def _():
        y = (hs_ref[...] + hb_scr[...]) * _gelu_tanh(gate_ref[...].astype(F32))
        res = _mix_ffn_core(
            y, x_ref[...], mod_ref, g_ref, wo_ref, bo_ref, w1, w3, w2, chunks=F_QUARTERS,
            side_work=[functools.partial(head_gates, hd) for hd in range(RG_HEADS)])
        _put_rows(rows_scr, res)
        for b in range(B):
            o_ref[b] = _get_batch(rows_scr, b)

    @pl.when(s < NT_LAT)
    def _():
        def emit(r0, hv):
            hb_scr[pl.ds(r0, B), :] = hv

        _scan_tile(a_scr, b_scr, h_scr, emit, reverse=True)


def _rg_out_ffn(xc, gate, hs, x, h0, mods, g, wai, bai, lam, w_out, b_out, w1, w3, w2):
    scan_tile = lambda s: (jnp.maximum(NT_LAT - 1 - s, 0), 0)
    out_tile = lambda s: jnp.minimum(NT_LAT - s, NT_LAT - 1)
    prev = pl.BlockSpec((TM, D), lambda s: (out_tile(s), 0))
    return pl.pallas_call(
        _with_staged_weights(_rg_out_ffn_kernel, 1, 1, n_in=15),
        grid=(NT_LAT + 1,),
        in_specs=[pl.BlockSpec((TM, D), scan_tile), prev, prev, prev, _resident((B, D)),
                  pl.BlockSpec((None, B, N_MOD * D), lambda s: (0, 0, 0)), _resident((6, D))]
        + _rg_gate_specs() + [_resident((D, D)), _resident((1, D))] + _WEIGHT_SPECS,
        out_specs=pl.BlockSpec((B, TS, D), lambda s: (0, out_tile(s), 0)),
        out_shape=jax.ShapeDtypeStruct((B, L, D), F32),
        scratch_shapes=[pltpu.VMEM((TM, D), F32), pltpu.VMEM((TM, D), F32),
                        pltpu.VMEM((TM, D), F32), pltpu.VMEM((B, D), F32), _ROWS_SCRATCH]
        + _WEIGHT_SCRATCH,
        compiler_params=_cparams(("arbitrary",)),
        name="rglru_out_ffn",
    )(xc, gate, hs, x, h0, mods, g, wai, bai, lam, w_out, b_out, w1, w3, w2)


def _grid_pos():
    rows = L // GRID_W
    quarter = D // 4
    omega = POS_BASE ** (-np.arange(quarter) / quarter)

    def emb(q):
        ang = q[:, None] * omega[None]
        return jnp.asarray(np.concatenate([np.sin(ang), np.cos(ang)], axis=-1), F32)

    row_code = jnp.repeat(emb(np.arange(rows)), GRID_W, axis=0)
    col_code = jnp.tile(emb(np.arange(GRID_W)), (rows, 1))
    return jnp.concatenate([row_code, col_code], axis=-1)


def kernel(x, c, ctx, c_ctx, ada_w, ada_b, norm_g, ffn_w1, ffn_w3, ffn_w2, hy_w_in, hy_b_in, hy_conv_w, hy_conv_b, hy_fw0, hy_fb0, hy_fw1, hy_fb1, hy_fw2, hy_fb2, hy_freq, hy_fwout, hy_filt_bias, hy_w_out, hy_b_out, rg_w_in, rg_b_in, rg_conv_w, rg_conv_b, rg_wa, rg_ba, rg_wi, rg_bi, rg_lam, rg_w_out, rg_b_out):
    cc = jnp.concatenate([c, jnp.broadcast_to(c_ctx[None], (B, D))], axis=0)
    mods = _mods(cc, ada_w, ada_b).reshape(DEPTH, 2, B, N_MOD * D)
    w1, w3, w2 = ffn_w1, ffn_w3, ffn_w2

    g = norm_g[0]
    xs = _ffn_first(x, ctx, _grid_pos(), mods[0], g, w1, w3, w2)
    x0, p = _hy_in(xs, mods[0], g, hy_w_in[0].astype(BF16), hy_b_in[0][None],
                   hy_conv_w[0], hy_conv_b[0][None])
    fparams = (hy_fw0[0], hy_fb0[0], hy_fw1[0], hy_fb1[0], hy_fw2[0], hy_fb2[0],
               hy_freq[0], hy_fwout[0])
    specs = []
    for n in (L, CTX):
        hf, hb = _filters(n, *fparams)
        tables = [tab.astype(BF16) for tab in _dft_tables(n // 4)] + [_twiddles(n // 4)]
        specs.append(list(_spectrum(hf, hb, *tables)) + tables)
    y_lo, y_hi = _long_conv(p, x0, hy_filt_bias[0][None], *specs)
    xs = _hy_out_ffn(y_lo, y_hi, xs, mods[0], g, hy_w_out[0].astype(BF16), hy_b_out[0][None],
                     w1, w3, w2)

    g = norm_g[1]
    xs = _ffn_plain(xs, mods[1], g, w1, w3, w2, layer=1)
    wai = jnp.concatenate([rg_wa[0], rg_wi[0]], axis=-1).astype(BF16)
    per_head = lambda v: v.reshape(2, RG_HEADS, RG_BLOCK)
    bai = jnp.concatenate([per_head(rg_ba[0]), per_head(rg_bi[0])], axis=-1).reshape(2, 1, 2 * D)
    lam = rg_lam[0][:, None, :]
    xc, gate, hs = _rg_in(xs, mods[1], g, rg_w_in[0].astype(BF16), rg_b_in[0][None],
                          rg_conv_w[0], rg_conv_b[0][None], wai[0], bai[0], lam[0])
    h_ctx = _rg_ctx_bwd(xc, wai[1], bai[1], lam[1])
    return _rg_out_ffn(xc, gate, hs, xs, h_ctx, mods[1], g, wai[1], bai[1], lam[1],
                       rg_w_out[0].astype(BF16), rg_b_out[0][None], w1, w3, w2)
```

```python
import functools
import math

import jax
import jax.numpy as jnp
import numpy as np
from jax import lax
from jax.experimental import pallas as pl
from jax.experimental.pallas import tpu as pltpu

F32 = jnp.float32
BF16 = jnp.bfloat16

D = 1024
B = 8
LANES = 128
MXU_N = 256
L = 2048
CTX = 256
DEPTH = 2
GRID_W = 64
D_FF = 2816
N_MOD = 9
MACARON = 0.5
NORM_EPS = 1e-6
POS_BASE = 10000.0
HY_EMB = 33
HY_BANDS = 16
HY_HID = 64
HY_FAST_DECAY = 0.3
HY_SLOW_DECAY = 1.5
HY_DECAY_TARGET = 1e-2
RG_HEADS = 4
RG_BLOCK = D // RG_HEADS
RG_C = 8.0

ROWS_LAT = L * B
ROWS_CTX = CTX * B
ROWS = ROWS_LAT + ROWS_CTX
TM = 512
TS = TM // B
NT_LAT = ROWS_LAT // TM
NT_CTX = ROWS_CTX // TM
NT = NT_LAT + NT_CTX
TI = 1024
NI_LAT = ROWS_LAT // TI
NI_CTX = ROWS_CTX // TI
NI = NI_LAT + NI_CTX
F_CHUNKS = ((0, 1536), (1536, D_FF))
F_QUARTERS = ((0, 768), (768, 1536), (1536, 2304), (2304, D_FF))
TC = 256
VMEM_LIMIT = 58 * 1024 * 1024


def _cparams(sem):
    return pltpu.CompilerParams(dimension_semantics=sem, vmem_limit_bytes=VMEM_LIMIT)


def _resident(shape):
    nd = len(shape)
    return pl.BlockSpec(shape, lambda *_: (0,) * nd, pipeline_mode=pl.Buffered(1))


def _split(a):
    hi = a.astype(BF16)
    lo = (a - hi.astype(F32)).astype(BF16)
    return hi, lo


def _dot(a, b):
    return jnp.dot(a, b, preferred_element_type=F32)


def _dot3(a, b):
    ah, al = _split(a)
    bh, bl = _split(b)
    return _dot(ah, bh) + _dot(ah, bl) + _dot(al, bh)


def _rmsnorm(x, g):
    ms = jnp.mean(x * x, axis=-1, keepdims=True)
    return x * lax.rsqrt(ms + NORM_EPS) * g


def _mod(mod_ref, k):
    return mod_ref[:, k * D:(k + 1) * D]


def _per_batch(x, fn):
    rows = x.shape[0]
    return fn(x.reshape(rows // B, B, x.shape[1])).reshape(rows, x.shape[1])


def _modulate(xn, shift8, scale8):
    return _per_batch(xn, lambda v: v * (1.0 + scale8)[None] + shift8[None])


def _gated(z, gate8):
    return _per_batch(z, lambda v: v * gate8[None])


def _sigmoid(x):
    return jax.nn.sigmoid(x)


def _ffn(x, mod_ref, k0, g_pre, g_post, w1_ref, w3_ref, w2_ref, chunks=F_CHUNKS, side_work=()):
    h = _modulate(_rmsnorm(x, g_pre), _mod(mod_ref, k0), _mod(mod_ref, k0 + 1)).astype(BF16)
    y = None
    for idx, (c0, c1) in enumerate(chunks):
        a = _dot(h, w1_ref[:, c0:c1])
        b = _dot(h, w3_ref[:, c0:c1])
        if idx < len(side_work):
            side_work[idx]()
        act = (a * _sigmoid(a) * b).astype(BF16)
        part = _dot(act, w2_ref[c0:c1, :])
        y = part if y is None else y + part
    return x + MACARON * _gated(_rmsnorm(y, g_post), _mod(mod_ref, k0 + 2))


def _mods_kernel(cc_ref, w_ref, b_ref, o_ref):
    @pl.when(pl.program_id(1) == 0)
    def _():
        o_ref[...] = jnp.broadcast_to(b_ref[...], o_ref.shape)

    a = cc_ref[...]
    ah, al = _split(a * _sigmoid(a))
    wh, wl = _split(w_ref[...])
    r = _dot(jnp.concatenate([ah, al], axis=0), wh)
    o_ref[...] += r[0:2 * B] + r[2 * B:4 * B] + _dot(ah, wl)


def _mods(cc, ada_w, ada_b):
    tk = 128
    n = N_MOD * D
    return pl.pallas_call(
        _mods_kernel,
        grid=(DEPTH, D // tk),
        in_specs=[
            pl.BlockSpec((2 * B, tk), lambda i, k: (0, k)),
            pl.BlockSpec((None, tk, n), lambda i, k: (i, k, 0)),
            pl.BlockSpec((None, 1, n), lambda i, k: (i, 0, 0)),
        ],
        out_specs=pl.BlockSpec((None, 2 * B, n), lambda i, k: (i, 0, 0)),
        out_shape=jax.ShapeDtypeStruct((DEPTH, 2 * B, n), F32),
        compiler_params=_cparams(("arbitrary", "arbitrary")),
        name="ada_mods",
    )(cc, ada_w, ada_b.reshape(DEPTH, 1, n))


def _mod_spec(nt_lat=NT_LAT):
    return pl.BlockSpec((None, B, N_MOD * D), lambda t: (jnp.where(t < nt_lat, 0, 1), 0, 0))


def _row_spec(width=D):
    return pl.BlockSpec((TM, width), lambda t: (t, 0))


def _wide_spec():
    return pl.BlockSpec((TS, B * D), lambda t: (t, 0))


W_CHUNKS = 8
W_SLOTS = 3
_HBM = pl.BlockSpec(memory_space=pl.ANY)
_WEIGHT_SPECS = [_HBM] * 3
_WEIGHT_SCRATCH = [pltpu.VMEM((D, D_FF), BF16), pltpu.VMEM((D, D_FF), BF16),
                   pltpu.VMEM((D_FF, D), BF16)]


def _stage_weights(srcs, dsts):
    chunks = [(src, dst, r0, dst.shape[0] // W_CHUNKS) for src, dst in zip(srcs, dsts)
              for r0 in range(0, dst.shape[0], dst.shape[0] // W_CHUNKS)]
    shapes = sorted({(rows, dst.shape[1]) for _, dst, _, rows in chunks})

    def run(*scoped):
        stages, sems = dict(zip(shapes, scoped[:-1])), scoped[-1]

        def slot(k):
            _, dst, _, rows = chunks[k]
            return stages[(rows, dst.shape[1])].at[k % W_SLOTS]

        def copy(k):
            src, _, r0, rows = chunks[k]
            return pltpu.make_async_copy(src.at[pl.ds(r0, rows), :], slot(k), sems.at[k % W_SLOTS])

        for k in range(W_SLOTS - 1):
            copy(k).start()
        for k, (_, dst, r0, rows) in enumerate(chunks):
            if k + W_SLOTS - 1 < len(chunks):
                copy(k + W_SLOTS - 1).start()
            copy(k).wait()
            dst[pl.ds(r0, rows), :] = slot(k)[...].astype(BF16)

    pl.run_scoped(run, *[pltpu.VMEM((W_SLOTS,) + shape, F32) for shape in shapes],
                  pltpu.SemaphoreType.DMA((W_SLOTS,)))


def _with_staged_weights(body, n_in, views):
    n_w = len(views)

    def kernel(*refs):
        ins, rest = refs[:n_in], refs[n_in:]
        copies = rest[-n_w:]

        @pl.when(pl.program_id(0) == 0)
        def _():
            _stage_weights([view(w) for view, w in zip(views, ins[-n_w:])], copies)

        body(*ins[:-n_w], *copies, *rest[:-n_w])

    return kernel


def _ffn_views(layer, half):
    return [lambda w: w.at[layer, half]] * 3


def _first(w):
    return w.at[0]


def _rows_of_batch(b):
    return pl.ds(b, TS, stride=B)


_ROWS_SCRATCH = pltpu.VMEM((D // LANES, TM, LANES), F32)


def _put_batch(rows_scr, b, val):
    for j in range(D // LANES):
        rows_scr[j, _rows_of_batch(b), :] = val[:, j * LANES:(j + 1) * LANES]


def _get_batch(rows_scr, b):
    return jnp.concatenate([rows_scr[j, _rows_of_batch(b), :] for j in range(D // LANES)], axis=1)


def _put_rows(rows_scr, val):
    for j in range(D // LANES):
        rows_scr[j] = val[:, j * LANES:(j + 1) * LANES]


def _get_rows(rows_scr):
    return jnp.concatenate([rows_scr[j] for j in range(D // LANES)], axis=1)


def _ffn_first_kernel(x_ref, ctx_ref, pos_ref, mod_ref, g_ref, w1, w3, w2, o_ref, rows_scr):
    is_latent = pl.program_id(0) < NT_LAT
    for b in range(B):
        _put_batch(rows_scr, b, jnp.where(is_latent, x_ref[b] + pos_ref[...], ctx_ref[b]))
    o_ref[...] = _ffn(_get_rows(rows_scr), mod_ref, 0, g_ref[0:1], g_ref[1:2], w1, w3, w2)


def _ffn_kernel(x_ref, mod_ref, g_ref, w1, w3, w2, o_ref):
    for r in range(x_ref.shape[0] // TM):
        rows = slice(r * TM, (r + 1) * TM)
        o_ref[rows, :] = _ffn(x_ref[rows, :], mod_ref, 0, g_ref[0:1], g_ref[1:2], w1, w3, w2)


def _ffn_first(x, ctx, pos, mods, g, w1, w3, w2):
    lat = lambda t: jnp.minimum(t, NT_LAT - 1)
    return pl.pallas_call(
        _with_staged_weights(_ffn_first_kernel, 8, _ffn_views(0, 0)),
        grid=(NT,),
        in_specs=[pl.BlockSpec((B, TS, D), lambda t: (0, lat(t), 0)),
                  pl.BlockSpec((B, TS, D), lambda t: (0, jnp.maximum(t - NT_LAT, 0), 0)),
                  pl.BlockSpec((TS, D), lambda t: (lat(t), 0)),
                  _mod_spec(), _resident((6, D))] + _WEIGHT_SPECS,
        out_specs=_row_spec(),
        out_shape=jax.ShapeDtypeStruct((ROWS, D), F32),
        scratch_shapes=[_ROWS_SCRATCH] + _WEIGHT_SCRATCH,
        compiler_params=_cparams(("arbitrary",)),
        name="ffn_first",
    )(x, ctx, pos, mods, g, w1, w3, w2)


def _ffn_plain(x, mods, g, w1, w3, w2, layer):
    return pl.pallas_call(
        _with_staged_weights(_ffn_kernel, 6, _ffn_views(layer, 0)),
        grid=(NT // 2,),
        in_specs=[pl.BlockSpec((2 * TM, D), lambda t: (t, 0)),
                  pl.BlockSpec((None, B, N_MOD * D),
                               lambda t: (jnp.where(t < NT_LAT // 2, 0, 1), 0, 0)),
                  _resident((6, D))] + _WEIGHT_SPECS,
        out_specs=pl.BlockSpec((2 * TM, D), lambda t: (t, 0)),
        out_shape=jax.ShapeDtypeStruct((ROWS, D), F32),
        scratch_shapes=_WEIGHT_SCRATCH,
        compiler_params=_cparams(("arbitrary",)),
        name="ffn_pre",
    )(x, mods, g, w1, w3, w2)


def _mix_ffn_core(y, x, mod_ref, g_ref, wo_ref, bo_ref, w1, w3, w2, **ffn_kwargs):
    z = _dot(y.astype(BF16), wo_ref[...]) + bo_ref[...]
    x = x + _gated(_rmsnorm(z, g_ref[3:4]), _mod(mod_ref, 5))
    return _ffn(x, mod_ref, 6, g_ref[4:5], g_ref[5:6], w1, w3, w2, **ffn_kwargs)


def _hy_out_ffn_kernel(ylo_ref, yhi_ref, x_ref, mod_ref, g_ref, bo_ref, wo_ref, w1, w3, w2,
                       o_ref, rows_scr):
    halves = (ylo_ref, yhi_ref)
    for b in range(B):
        for j in range(D // LANES):
            col = (b * (D // TC) + j // 2) * LANES
            rows_scr[j, _rows_of_batch(b), :] = halves[j % 2][:, col:col + LANES]
    o_ref[...] = _mix_ffn_core(_get_rows(rows_scr), x_ref[...], mod_ref, g_ref, wo_ref, bo_ref,
                               w1, w3, w2)


def _hy_out_ffn(y_lo, y_hi, x, mods, g, w_out, b_out, w1, w3, w2):
    half = pl.BlockSpec((TS, B * D // 2), lambda t: (t, 0))
    return pl.pallas_call(
        _with_staged_weights(_hy_out_ffn_kernel, 10, [_first] + _ffn_views(0, 1)),
        grid=(NT,),
        in_specs=[half, half, _row_spec(), _mod_spec(), _resident((6, D)), _resident((1, D)),
                  _HBM] + _WEIGHT_SPECS,
        out_specs=_row_spec(),
        out_shape=jax.ShapeDtypeStruct((ROWS, D), F32),
        scratch_shapes=[_ROWS_SCRATCH, pltpu.VMEM((D, D), BF16)] + _WEIGHT_SCRATCH,
        compiler_params=_cparams(("arbitrary",)),
        name="hyena_out_ffn",
    )(y_lo, y_hi, x, mods, g, b_out, w_out, w1, w3, w2)


def _seq_first(t):
    return jnp.logical_or(t == 0, t == NI_LAT)


def _seq_last(t):
    return jnp.logical_or(t == NI_LAT - 1, t == NI - 1)


def _halo_specs(order, lo_rows, hi_rows):
    nlo = ROWS // lo_rows
    nhi = ROWS // hi_rows
    return [
        pl.BlockSpec((lo_rows, D), lambda s: (jnp.maximum(order(s) * (TI // lo_rows) - 1, 0), 0)),
        pl.BlockSpec((TI, D), lambda s: (order(s), 0)),
        pl.BlockSpec((hi_rows, D),
                     lambda s: (jnp.minimum((order(s) + 1) * (TI // hi_rows), nhi - 1), 0)),
    ]


def _hy_in_kernel(xp_ref, x_ref, xn_ref, mod_ref, g_ref, b_ref, cw_ref, cb_ref, w_ref,
                  x0_ref, p_ref, x0_scr, p_scr):
    t = pl.program_id(0)
    xa = jnp.concatenate([xp_ref[...], x_ref[...], xn_ref[...]], axis=0)
    h = _modulate(_rmsnorm(xa, g_ref[2:3]), _mod(mod_ref, 3), _mod(mod_ref, 4)).astype(BF16)
    keep_lo = jnp.where(_seq_first(t), 0.0, 1.0).astype(F32)
    keep_hi = jnp.where(_seq_last(t), 0.0, 1.0).astype(F32)

    def conv_cols(c0):
        cols = slice(c0, c0 + MXU_N)
        u = _dot(h, w_ref[:, cols]) + b_ref[:, cols]
        lo = jnp.concatenate([u[0:B] * keep_lo, u[B:TI]], axis=0)
        hi = jnp.concatenate([u[2 * B:TI + B], u[TI + B:TI + 2 * B] * keep_hi], axis=0)
        return (cb_ref[:, cols] + cw_ref[0:1, cols] * lo + cw_ref[1:2, cols] * u[B:TI + B]
                + cw_ref[2:3, cols] * hi)

    slabs = MXU_N // LANES
    for c in range(D // MXU_N):
        c0 = c * MXU_N
        vals = (conv_cols(c0), conv_cols(D + c0) * conv_cols(2 * D + c0))
        for out_ref, scr, val in zip((x0_ref, p_ref), (x0_scr, p_scr), vals):
            for j in range(slabs):
                scr[c * slabs + j] = val[:, j * LANES:(j + 1) * LANES]
            for b in range(B):
                out_ref[:, b * D + c0:b * D + c0 + MXU_N] = jnp.concatenate(
                    [scr[c * slabs + j, pl.ds(b, TI // B, stride=B), :] for j in range(slabs)], axis=1)


def _hy_in(x, mods, g, w_in, b_in, conv_w, conv_b):
    wide = pl.BlockSpec((TI // B, B * D), lambda t: (t, 0))
    rows_scratch = pltpu.VMEM((D // LANES, TI, LANES), F32)
    return pl.pallas_call(
        _with_staged_weights(_hy_in_kernel, 9, [_first]),
        grid=(NI,),
        in_specs=_halo_specs(lambda s: s, B, B) + [
            _mod_spec(NI_LAT), _resident((6, D)), _resident((1, 3 * D)),
            _resident((3, 3 * D)), _resident((1, 3 * D)), _HBM],
        out_specs=[wide, wide],
        out_shape=[jax.ShapeDtypeStruct((ROWS // B, B * D), F32)] * 2,
        scratch_shapes=[rows_scratch, rows_scratch, pltpu.VMEM((D, 3 * D), BF16)],
        compiler_params=_cparams(("arbitrary",)),
        name="hyena_in",
    )(x, x, x, mods, g, b_in, conv_w, conv_b, w_in)


def _filter_kernel(zt_ref, t_ref, fw0t, fb0, fw1t, fb1, fw2t, fb2, freq, fwout, deltas,
                   hf_ref, hb_ref):
    h = jnp.sin(freq[:, 0:1] * (_dot3(fw0t[...], zt_ref[...]) + fb0[...]))
    h = jnp.sin(freq[:, 1:2] * (_dot3(fw1t[...], h) + fb1[...]))
    h = jnp.sin(freq[:, 2:3] * (_dot3(fw2t[...], h) + fb2[...]))
    filt = _dot3(h.T, fwout[...])
    decay = jnp.exp(-t_ref[:, 0:1] * deltas[...])
    hf_ref[...] = filt[:, 0:D] * decay
    hb_ref[...] = filt[:, D:2 * D] * decay


def _filters(n, fw0, fb0, fw1, fb1, fw2, fb2, freq, fwout):
    t = np.linspace(0.0, 1.0, n)[:, None]
    bands = np.linspace(1e-4, HY_BANDS - 1, HY_BANDS)[None]
    phase = bands * (2.0 * math.pi * np.arange(n)[:, None] / n)
    zp = np.zeros((n, LANES), np.float32)
    zp[:, :HY_EMB] = np.concatenate([t, np.cos(phase), -np.sin(phase)], axis=-1)
    zp = np.concatenate([zp[r::4] for r in range(4)], axis=0)
    fw0t = jnp.zeros((HY_HID, LANES), F32).at[:, :HY_EMB].set(fw0.T)
    max_decay = math.log(HY_DECAY_TARGET) / HY_FAST_DECAY
    min_decay = math.log(HY_DECAY_TARGET) / HY_SLOW_DECAY
    deltas = np.abs(np.linspace(min_decay, max_decay, D))[None].astype(np.float32)
    tl = 256
    row = lambda i: (i, 0)
    unit = _resident((HY_HID, 1))
    return pl.pallas_call(
        _filter_kernel,
        grid=(n // tl,),
        in_specs=[pl.BlockSpec((LANES, tl), lambda i: (0, i)), pl.BlockSpec((tl, LANES), row),
                  _resident((HY_HID, LANES)), unit, _resident((HY_HID, HY_HID)), unit,
                  _resident((HY_HID, HY_HID)), unit, _resident((HY_HID, 3)),
                  _resident((HY_HID, 2 * D)), _resident((1, D))],
        out_specs=[pl.BlockSpec((tl, D), row)] * 2,
        out_shape=[jax.ShapeDtypeStruct((n, D), F32)] * 2,
        compiler_params=_cparams(("arbitrary",)),
        name="hyena_filter",
    )(np.ascontiguousarray(zp.T), zp, fw0t, fb0[:, None], fw1.T, fb1[:, None], fw2.T, fb2[:, None],
      freq.T, fwout, deltas)


def _dft_tables(n):
    q = 1 << (int(math.log2(n)) // 2 + 1)
    m = np.arange(n, dtype=np.int64)[None, :]

    def thin(k):
        ang = ((k[:, None] * m) % (2 * n)) * (math.pi / n)
        return jnp.asarray(np.cos(ang), F32), jnp.asarray(np.sin(ang), F32)

    ca, sa = (v[:, None, :] for v in thin(q * np.arange(n // q, dtype=np.int64)))
    cb, sb = (v[None, :, :] for v in thin(np.arange(q, dtype=np.int64)))
    return (ca * cb - sa * sb).reshape(n, n), (sa * cb + ca * sb).reshape(n, n)


def _alt_sign(rows, cols):
    r = lax.broadcasted_iota(jnp.int32, (rows, cols), 0)
    return (1 - 2 * (r & 1)).astype(F32)


_N_GROUPS = 4
_ROOT_HALF = math.sqrt(0.5)


def _twiddles(g):
    k = np.arange(g)[:, None]
    parts = [f(r * k * (math.pi / (4 * g))) for r in (1, 2, 3) for f in (np.cos, np.sin)]
    return jnp.broadcast_to(jnp.asarray(np.concatenate(parts, axis=0), F32), (6 * g, TC))


def _cmul(ar, as_, br, bi):
    return ar * br + as_ * bi, as_ * br - ar * bi


def _fwd4(quarters, c, s, tw_ref, rows, g):
    ts = []
    for r, q in enumerate(quarters):
        qr, qs = _dot(c, q), _dot(s, q)
        if r:
            wc = tw_ref[pl.ds((2 * r - 2) * g + rows.start, rows.stop - rows.start), :]
            ws = tw_ref[pl.ds((2 * r - 1) * g + rows.start, rows.stop - rows.start), :]
            qr, qs = wc * qr - ws * qs, wc * qs + ws * qr
        ts.append((qr, qs))
    (t0r, t0s), (t1r, t1s), (t2r, t2s), (t3r, t3s) = ts
    er, es, fr, fs = t0r + t2r, t0s + t2s, t0r - t2r, t0s - t2s
    pr, ps, dr, ds = t1r + t3r, t1s + t3s, t1r - t3r, t1s - t3s
    return ((er + pr, es + ps), (er - pr, es - ps), (fr + ds, fs - dr), (fr - ds, fs + dr))


def _mid_freqs(sums):
    s0, s1, s2, s3 = sums
    a, b = _ROOT_HALF * (s1 - s3), _ROOT_HALF * (s1 + s3)
    return (s0 + a, s2 + b), (s0 - a, b - s2)


def _alt_sums(quarters):
    alt = _alt_sign(*quarters[0].shape)
    return [jnp.sum(q * alt, axis=0, keepdims=True) for q in quarters]


def _spectrum_kernel(hf_ref, hb_ref, c_ref, s_ref, tw_ref, kr_ref, ki_ref, kn_ref, *, g):
    hf = hf_ref[...]
    row = lax.broadcasted_iota(jnp.int32, hf.shape, 0)
    hb = jnp.where(row == 0, 0.0, hb_ref[...])
    cos_part = hf + hb
    sin_part = hb - hf
    cq = [cos_part[r * g:(r + 1) * g] for r in range(4)]
    sq = [sin_part[r * g:(r + 1) * g] for r in range(4)]
    rows = slice(0, g)
    c, s = c_ref[...], s_ref[...]
    groups_c = _fwd4([q.astype(BF16) for q in cq], c, s, tw_ref, rows, g)
    groups_s = _fwd4([q.astype(BF16) for q in sq], c, s, tw_ref, rows, g)
    n_fft = 8 * g
    k = lax.broadcasted_iota(jnp.int32, (g, hf.shape[1]), 0)
    scale = jnp.where(k == 0, 1.0 / n_fft, 2.0 / n_fft)
    for grp in range(_N_GROUPS):
        kr_ref[grp * g:(grp + 1) * g, :] = groups_c[grp][0] * scale
        ki_ref[grp * g:(grp + 1) * g, :] = groups_s[grp][1] * scale
    (cg, _), (c3g, _) = _mid_freqs(_alt_sums(cq))
    (_, sg), (_, s3g) = _mid_freqs(_alt_sums(sq))
    mids = [v * (2.0 / n_fft) for v in (cg, sg, c3g, s3g)]
    kn_ref[...] = jnp.concatenate(mids + [jnp.zeros((B - 4, hf.shape[1]), F32)], axis=0)


def _spectrum(hf, hb, ctab, stab, tw):
    n = hf.shape[0]
    g = n // 4
    col = lambda j: (0, j)
    return pl.pallas_call(
        functools.partial(_spectrum_kernel, g=g),
        grid=(D // TC,),
        in_specs=[pl.BlockSpec((n, TC), col)] * 2 + [_resident((g, g))] * 2
        + [_resident((6 * g, TC))],
        out_specs=[pl.BlockSpec((n, TC), col)] * 2 + [pl.BlockSpec((B, TC), col)],
        out_shape=[jax.ShapeDtypeStruct((n, D), F32)] * 2 + [jax.ShapeDtypeStruct((B, D), F32)],
        compiler_params=_cparams(("arbitrary",)),
        name="hyena_spectrum",
    )(hf, hb, ctab, stab, tw)


def _steps(refs, first, count):
    rows = pl.ds(first, count, stride=4)
    return jnp.concatenate([r[rows, :] for r in refs], axis=1)


def _long_conv_rows(t0, g, p_refs, x0_refs, spec, bias_ref, o_refs, scratch):
    kr_ref, ki_ref, kn_ref, c_ref, s_ref, tw_ref = spec
    q_scr, u_scr = scratch
    tk = min(g, 512)
    r0 = t0 // 4
    seq = slice(r0, r0 + g)
    chunks = [slice(k * tk, (k + 1) * tk) for k in range(g // tk)]
    quarters = [_steps(p_refs, t0 + r, g) for r in range(4)]
    tc = quarters[0].shape[1]
    for r in range(4):
        q_scr[r, seq, :] = quarters[r].astype(BF16)
    for rows in chunks:
        dst = slice(r0 + rows.start, r0 + rows.stop)
        c, s = c_ref[rows, :], s_ref[rows, :]
        groups = _fwd4([q_scr[r, seq, :] for r in range(4)], c, s, tw_ref, rows, g)
        ys = []
        for grp, (xr, xs) in enumerate(groups):
            k_rows = pl.ds(grp * g + rows.start, tk)
            ys.append(_cmul(xr, xs, kr_ref[k_rows, :], ki_ref[k_rows, :]))
        (y1r, y1s), (y2r, y2s), (y3r, y3s), (y4r, y4s) = ys
        pr, ps, mr, ms = y1r + y2r, y1s + y2s, y1r - y2r, y1s - y2s
        qr, qs, nr, ns = y3r + y4r, y3s + y4s, y3r - y4r, y3s - y4s
        us = [(pr + qr, ps + qs), (mr - ns, ms + nr), (pr - qr, ps - qs), (mr + ns, ms - nr)]
        for r, (ur, us_) in enumerate(us):
            if r:
                wc = tw_ref[pl.ds((2 * r - 2) * g + rows.start, tk), :]
                ws = tw_ref[pl.ds((2 * r - 1) * g + rows.start, tk), :]
                ur, us_ = ur * wc + us_ * ws, us_ * wc - ur * ws
            u_scr[2 * r, dst, :] = ur.astype(BF16)
            u_scr[2 * r + 1, dst, :] = us_.astype(BF16)
    (xgr, xgs), (x3r, x3s) = _mid_freqs(_alt_sums(quarters))
    ygr, ygs = _cmul(xgr, xgs, kn_ref[0:1, :], kn_ref[1:2, :])
    y3r_, y3s_ = _cmul(x3r, x3s, kn_ref[2:3, :], kn_ref[3:4, :])
    a = _ROOT_HALF
    mids = [ygr + y3r_, a * (ygr + ygs - y3r_ + y3s_), ygs - y3s_, a * (ygs - ygr + y3r_ + y3s_)]
    alt_chunk = _alt_sign(tk, tc)
    bias = bias_ref[...]
    for rows in chunks:
        c, s = c_ref[rows, :], s_ref[rows, :]
        for r in range(4):
            first = t0 + 4 * rows.start + r
            y = (_dot(c, u_scr[2 * r, seq, :]) + _dot(s, u_scr[2 * r + 1, seq, :])
                 + alt_chunk * mids[r])
            out = _steps(x0_refs, first, tk) * (y + _steps(p_refs, first, tk) * bias)
            for q, o_ref in enumerate(o_refs):
                o_ref[pl.ds(first, tk, stride=4), :] = out[:, q * LANES:(q + 1) * LANES]


_N_SPEC = 6


def _long_conv_kernel(*refs):
    p_refs, x0_refs, bias_ref = refs[0:2], refs[2:4], refs[4]
    lat, ctx = refs[5:5 + _N_SPEC], refs[5 + _N_SPEC:5 + 2 * _N_SPEC]
    o_refs, scratch = refs[5 + 2 * _N_SPEC:7 + 2 * _N_SPEC], refs[7 + 2 * _N_SPEC:]
    _long_conv_rows(0, L // 4, p_refs, x0_refs, lat, bias_ref, o_refs, scratch)
    _long_conv_rows(L, CTX // 4, p_refs, x0_refs, ctx, bias_ref, o_refs, scratch)


def _long_conv(p2, x02, bias, spec_lat, spec_ctx):
    nc = D // TC
    halves = [pl.BlockSpec((ROWS // B, LANES),
                           functools.partial(lambda q, j: (0, 2 * ((j % B) * nc + j // B) + q), q))
              for q in range(TC // LANES)]
    ch = lambda j: (0, j // B)

    def spec_specs(g):
        return ([pl.BlockSpec((4 * g, TC), ch)] * 2 + [pl.BlockSpec((B, TC), ch)]
                + [_resident((g, g))] * 2 + [_resident((6 * g, TC))])

    half_out = pl.BlockSpec((ROWS // B, LANES), lambda j: (0, (j % B) * nc + j // B))
    quarter_rows = ROWS // (4 * B)
    return pl.pallas_call(
        _long_conv_kernel,
        grid=(B * D // TC,),
        scratch_shapes=[pltpu.VMEM((4, quarter_rows, TC), BF16),
                        pltpu.VMEM((8, quarter_rows, TC), BF16)],
        in_specs=halves + halves + [pl.BlockSpec((1, TC), ch)] + spec_specs(L // 4)
        + spec_specs(CTX // 4),
        out_specs=[half_out] * (TC // LANES),
        out_shape=[jax.ShapeDtypeStruct((ROWS // B, B * D * LANES // TC), F32)] * (TC // LANES),
        compiler_params=_cparams(("arbitrary",)),
        name="hyena_long_conv",
    )(p2, p2, x02, x02, bias, *spec_lat, *spec_ctx)


def _gelu_tanh(x):
    return x * (0.5 * (1.0 + jnp.tanh(math.sqrt(2.0 / math.pi) * (x + 0.044715 * (x * x * x)))))


def _rg_coeffs(xc, hd, wai_ref, bai_ref, lam_ref, a_scr, b_scr):
    sl = slice(hd * RG_BLOCK, (hd + 1) * RG_BLOCK)
    lam = lam_ref[:, sl]
    softplus_neg = jnp.maximum(-lam, 0.0) + jnp.log1p(jnp.exp(-jnp.abs(lam)))
    rate = (-RG_C * math.log2(math.e)) * softplus_neg
    pre = _dot(xc.astype(BF16), wai_ref[hd]) + bai_ref[:, 2 * hd * RG_BLOCK:2 * (hd + 1) * RG_BLOCK]
    gates = _sigmoid(pre)
    a = jnp.exp2(gates[:, 0:RG_BLOCK] * rate)
    a_scr[:, sl] = a
    b_scr[:, sl] = jnp.sqrt((1.0 - a) * (1.0 + a)) * gates[:, RG_BLOCK:2 * RG_BLOCK] * xc


def _scan_tile(a_scr, b_scr, h_scr, emit, reverse):
    steps = a_scr.shape[0] // B

    def body(k, h):
        t = steps - 1 - k if reverse else k
        r0 = pl.multiple_of(t * B, B)
        h = a_scr[pl.ds(r0, B), :] * h + b_scr[pl.ds(r0, B), :]
        emit(r0, h)
        return h

    h_scr[...] = lax.fori_loop(0, steps, body, h_scr[...], unroll=8)


def _rg_fwd_order(s):
    return jnp.where(s < NI_CTX, NI_LAT + s, s - NI_CTX)


def _rg_in_kernel(xp_ref, x_ref, xn_ref, mod_ref, g_ref, b_ref, cw_ref, cb_ref,
                  wai_ref, bai_ref, lam_ref, w_ref, xc_ref, gate_ref, hs_ref, a_scr, b_scr, h_scr):
    s = pl.program_id(0)
    t = _rg_fwd_order(s)
    xa = jnp.concatenate([xp_ref[...], x_ref[...], xn_ref[...]], axis=0)
    h = _modulate(_rmsnorm(xa, g_ref[2:3]), _mod(mod_ref, 3), _mod(mod_ref, 4)).astype(BF16)
    keep_lo = jnp.where(_seq_first(t), 0.0, 1.0).astype(F32)
    keep_hi = jnp.where(_seq_last(t), 0.0, 1.0).astype(F32)
    for hd in range(RG_HEADS):
        sl = slice(hd * RG_BLOCK, (hd + 1) * RG_BLOCK)
        rec = slice(D + hd * RG_BLOCK, D + (hd + 1) * RG_BLOCK)
        gate_ref[:, sl] = (_dot(h[B:TI + B], w_ref[:, sl]) + b_ref[:, sl]).astype(BF16)
        u = _dot(h, w_ref[:, rec]) + b_ref[:, rec]
        taps = (jnp.concatenate([u[0:B] * keep_lo, u[B:TI]], axis=0),
                u[B:TI + B],
                jnp.concatenate([u[2 * B:TI + B], u[TI + B:TI + 2 * B] * keep_hi], axis=0),
                jnp.concatenate([u[3 * B:TI + B], u[TI + B:TI + 3 * B] * keep_hi], axis=0))
        xc = cb_ref[:, sl]
        for k, tap in enumerate(taps):
            xc = xc + cw_ref[k:k + 1, sl] * tap
        xc_ref[:, sl] = xc
        _rg_coeffs(xc, hd, wai_ref, bai_ref, lam_ref, a_scr, b_scr)

    @pl.when(s == 0)
    def _():
        h_scr[...] = jnp.zeros((B, D), F32)

    def emit(r0, hv):
        hs_ref[pl.ds(r0, B), :] = hv

    _scan_tile(a_scr, b_scr, h_scr, emit, reverse=False)


def _rg_gate_specs():
    return [_resident((RG_HEADS, RG_BLOCK, 2 * RG_BLOCK)), _resident((1, 2 * D)), _resident((1, D))]


def _rg_in(x, mods, g, w_in, b_in, conv_w, conv_b, wai, bai, lam):
    order = _rg_fwd_order
    mod_spec = pl.BlockSpec((None, B, N_MOD * D),
                            lambda s: (jnp.where(order(s) < NI_LAT, 0, 1), 0, 0))
    out_spec = pl.BlockSpec((TI, D), lambda s: (order(s), 0))
    return pl.pallas_call(
        _with_staged_weights(_rg_in_kernel, 12, [_first]),
        grid=(NI,),
        in_specs=_halo_specs(order, B, 2 * B) + [
            mod_spec, _resident((6, D)), _resident((1, 2 * D)),
            _resident((4, D)), _resident((1, D))] + _rg_gate_specs() + [_HBM],
        out_specs=[out_spec] * 3,
        out_shape=[jax.ShapeDtypeStruct((ROWS, D), F32), jax.ShapeDtypeStruct((ROWS, D), BF16),
                   jax.ShapeDtypeStruct((ROWS, D), F32)],
        scratch_shapes=[pltpu.VMEM((TI, D), F32), pltpu.VMEM((TI, D), F32),
                        pltpu.VMEM((B, D), F32), pltpu.VMEM((D, 2 * D), BF16)],
        compiler_params=_cparams(("arbitrary",)),
        name="rglru_in_fwd_scan",
    )(x, x, x, mods, g, b_in, conv_w, conv_b, wai, bai, lam, w_in)


def _rg_tile_coeffs(xc_ref, wai_ref, bai_ref, lam_ref, a_scr, b_scr):
    for hd in range(RG_HEADS):
        xc = xc_ref[:, hd * RG_BLOCK:(hd + 1) * RG_BLOCK]
        _rg_coeffs(xc, hd, wai_ref, bai_ref, lam_ref, a_scr, b_scr)


def _rg_ctx_bwd_kernel(xc_ref, wai_ref, bai_ref, lam_ref, h_ref, a_scr, b_scr, h_scr):
    _rg_tile_coeffs(xc_ref, wai_ref, bai_ref, lam_ref, a_scr, b_scr)

    @pl.when(pl.program_id(0) == 0)
    def _():
        h_scr[...] = jnp.zeros((B, D), F32)

    _scan_tile(a_scr, b_scr, h_scr, lambda r0, hv: None, reverse=True)
    h_ref[...] = h_scr[...]


def _rg_ctx_bwd(xc, wai, bai, lam):
    return pl.pallas_call(
        _rg_ctx_bwd_kernel,
        grid=(NT_CTX,),
        in_specs=[pl.BlockSpec((TM, D), lambda s: (NT - 1 - s, 0))] + _rg_gate_specs(),
        out_specs=pl.BlockSpec((B, D), lambda s: (0, 0)),
        out_shape=jax.ShapeDtypeStruct((B, D), F32),
        scratch_shapes=[pltpu.VMEM((TM, D), F32), pltpu.VMEM((TM, D), F32),
                        pltpu.VMEM((B, D), F32)],
        compiler_params=_cparams(("arbitrary",)),
        name="rglru_ctx_bwd_scan",
    )(xc, wai, bai, lam)


def _rg_out_ffn_kernel(xc_ref, gate_ref, hs_ref, x_ref, h0_ref, mod_ref, g_ref,
                       wai_ref, bai_ref, lam_ref, bo_ref, wo_ref, w1, w3, w2,
                       o_ref, a_scr, b_scr, hb_scr, h_scr, rows_scr):
    s = pl.program_id(0)

    def head_gates(hd):
        xc = xc_ref[:, hd * RG_BLOCK:(hd + 1) * RG_BLOCK]
        _rg_coeffs(xc, hd, wai_ref, bai_ref, lam_ref, a_scr, b_scr)

    @pl.when(s == 0)
    def _():
        h_scr[...] = h0_ref[...]
        for hd in range(RG_HEADS):
            head_gates(hd)

    @pl.when(s > 0)
    def _():
        y = (hs_ref[...] + hb_scr[...]) * _gelu_tanh(gate_ref[...].astype(F32))
        res = _mix_ffn_core(
            y, x_ref[...], mod_ref, g_ref, wo_ref, bo_ref, w1, w3, w2, chunks=F_QUARTERS,
            side_work=[functools.partial(head_gates, hd) for hd in range(RG_HEADS)])
        _put_rows(rows_scr, res)
        for b in range(B):
            o_ref[b] = _get_batch(rows_scr, b)

    @pl.when(s < NT_LAT)
    def _():
        def emit(r0, hv):
            hb_scr[pl.ds(r0, B), :] = hv

        _scan_tile(a_scr, b_scr, h_scr, emit, reverse=True)


def _rg_out_ffn(xc, gate, hs, x, h0, mods, g, wai, bai, lam, w_out, b_out, w1, w3, w2):
    scan_tile = lambda s: (jnp.maximum(NT_LAT - 1 - s, 0), 0)
    out_tile = lambda s: jnp.minimum(NT_LAT - s, NT_LAT - 1)
    prev = pl.BlockSpec((TM, D), lambda s: (out_tile(s), 0))
    return pl.pallas_call(
        _with_staged_weights(_rg_out_ffn_kernel, 15, [_first] + _ffn_views(1, 1)),
        grid=(NT_LAT + 1,),
        in_specs=[pl.BlockSpec((TM, D), scan_tile), prev, prev, prev, _resident((B, D)),
                  pl.BlockSpec((None, B, N_MOD * D), lambda s: (0, 0, 0)), _resident((6, D))]
        + _rg_gate_specs() + [_resident((1, D)), _HBM] + _WEIGHT_SPECS,
        out_specs=pl.BlockSpec((B, TS, D), lambda s: (0, out_tile(s), 0)),
        out_shape=jax.ShapeDtypeStruct((B, L, D), F32),
        scratch_shapes=[pltpu.VMEM((TM, D), F32), pltpu.VMEM((TM, D), F32),
                        pltpu.VMEM((TM, D), F32), pltpu.VMEM((B, D), F32), _ROWS_SCRATCH,
                        pltpu.VMEM((D, D), BF16)] + _WEIGHT_SCRATCH,
        compiler_params=_cparams(("arbitrary",)),
        name="rglru_out_ffn",
    )(xc, gate, hs, x, h0, mods, g, wai, bai, lam, b_out, w_out, w1, w3, w2)


def _grid_pos():
    rows = L // GRID_W
    quarter = D // 4
    omega = POS_BASE ** (-np.arange(quarter) / quarter)

    def emb(q):
        ang = q[:, None] * omega[None]
        return jnp.asarray(np.concatenate([np.sin(ang), np.cos(ang)], axis=-1), F32)

    row_code = jnp.repeat(emb(np.arange(rows)), GRID_W, axis=0)
    col_code = jnp.tile(emb(np.arange(GRID_W)), (rows, 1))
    return jnp.concatenate([row_code, col_code], axis=-1)


def kernel(x, c, ctx, c_ctx, ada_w, ada_b, norm_g, ffn_w1, ffn_w3, ffn_w2, hy_w_in, hy_b_in, hy_conv_w, hy_conv_b, hy_fw0, hy_fb0, hy_fw1, hy_fb1, hy_fw2, hy_fb2, hy_freq, hy_fwout, hy_filt_bias, hy_w_out, hy_b_out, rg_w_in, rg_b_in, rg_conv_w, rg_conv_b, rg_wa, rg_ba, rg_wi, rg_bi, rg_lam, rg_w_out, rg_b_out):
    cc = jnp.concatenate([c, jnp.broadcast_to(c_ctx[None], (B, D))], axis=0)
    mods = _mods(cc, ada_w, ada_b).reshape(DEPTH, 2, B, N_MOD * D)
    w1, w3, w2 = ffn_w1, ffn_w3, ffn_w2

    g = norm_g[0]
    xs = _ffn_first(x, ctx, _grid_pos(), mods[0], g, w1, w3, w2)
    x0, p = _hy_in(xs, mods[0], g, hy_w_in, hy_b_in[0][None],
                   hy_conv_w[0], hy_conv_b[0][None])
    fparams = (hy_fw0[0], hy_fb0[0], hy_fw1[0], hy_fb1[0], hy_fw2[0], hy_fb2[0],
               hy_freq[0], hy_fwout[0])
    specs = []
    for n in (L, CTX):
        hf, hb = _filters(n, *fparams)
        tables = [tab.astype(BF16) for tab in _dft_tables(n // 4)] + [_twiddles(n // 4)]
        specs.append(list(_spectrum(hf, hb, *tables)) + tables)
    y_lo, y_hi = _long_conv(p, x0, hy_filt_bias[0][None], *specs)
    xs = _hy_out_ffn(y_lo, y_hi, xs, mods[0], g, hy_w_out, hy_b_out[0][None],
                     w1, w3, w2)

    g = norm_g[1]
    xs = _ffn_plain(xs, mods[1], g, w1, w3, w2, layer=1)
    wai = jnp.concatenate([rg_wa[0], rg_wi[0]], axis=-1).astype(BF16)
    per_head = lambda v: v.reshape(2, RG_HEADS, RG_BLOCK)
    bai = jnp.concatenate([per_head(rg_ba[0]), per_head(rg_bi[0])], axis=-1).reshape(2, 1, 2 * D)
    lam = rg_lam[0][:, None, :]
    xc, gate, hs = _rg_in(xs, mods[1], g, rg_w_in, rg_b_in[0][None],
                          rg_conv_w[0], rg_conv_b[0][None], wai[0], bai[0], lam[0])
    h_ctx = _rg_ctx_bwd(xc, wai[1], bai[1], lam[1])
    return _rg_out_ffn(xc, gate, hs, xs, h_ctx, mods[1], g, wai[1], bai[1], lam[1],
                       rg_w_out, rg_b_out[0][None], w1, w3, w2)
```

```python
import functools
import math

import jax
import jax.numpy as jnp
import numpy as np
from jax import lax
from jax.experimental import pallas as pl
from jax.experimental.pallas import tpu as pltpu

F32 = jnp.float32
BF16 = jnp.bfloat16

D = 1024
B = 8
LANES = 128
MXU_N = 256
L = 2048
CTX = 256
DEPTH = 2
GRID_W = 64
D_FF = 2816
N_MOD = 9
MACARON = 0.5
NORM_EPS = 1e-6
POS_BASE = 10000.0
HY_EMB = 33
HY_BANDS = 16
HY_HID = 64
HY_FAST_DECAY = 0.3
HY_SLOW_DECAY = 1.5
HY_DECAY_TARGET = 1e-2
RG_HEADS = 4
RG_BLOCK = D // RG_HEADS
RG_C = 8.0

ROWS_LAT = L * B
ROWS_CTX = CTX * B
ROWS = ROWS_LAT + ROWS_CTX
TM = 512
TS = TM // B
NT_LAT = ROWS_LAT // TM
NT_CTX = ROWS_CTX // TM
NT = NT_LAT + NT_CTX
TI = 1024
NI_LAT = ROWS_LAT // TI
NI_CTX = ROWS_CTX // TI
NI = NI_LAT + NI_CTX
F_CHUNKS = ((0, 1536), (1536, D_FF))
F_QUARTERS = ((0, 768), (768, 1536), (1536, 2304), (2304, D_FF))
TC = 256
VMEM_LIMIT = 58 * 1024 * 1024


def _cparams(sem):
    return pltpu.CompilerParams(dimension_semantics=sem, vmem_limit_bytes=VMEM_LIMIT)


def _resident(shape):
    nd = len(shape)
    return pl.BlockSpec(shape, lambda *_: (0,) * nd, pipeline_mode=pl.Buffered(1))


def _split(a):
    hi = a.astype(BF16)
    lo = (a - hi.astype(F32)).astype(BF16)
    return hi, lo


def _dot(a, b):
    return jnp.dot(a, b, preferred_element_type=F32)


def _dot3(a, b):
    ah, al = _split(a)
    bh, bl = _split(b)
    return _dot(ah, bh) + _dot(ah, bl) + _dot(al, bh)


def _rmsnorm(x, g):
    ms = jnp.mean(x * x, axis=-1, keepdims=True)
    return x * lax.rsqrt(ms + NORM_EPS) * g


def _mod(mod_ref, k):
    return mod_ref[:, k * D:(k + 1) * D]


def _per_batch(x, fn):
    rows = x.shape[0]
    return fn(x.reshape(rows // B, B, x.shape[1])).reshape(rows, x.shape[1])


def _modulate(xn, shift8, scale8):
    return _per_batch(xn, lambda v: v * (1.0 + scale8)[None] + shift8[None])


def _gated(z, gate8):
    return _per_batch(z, lambda v: v * gate8[None])


def _sigmoid(x):
    return jax.nn.sigmoid(x)


def _ffn(x, mod_ref, k0, g_pre, g_post, w1_ref, w3_ref, w2_ref, chunks=F_CHUNKS, side_work=()):
    h = _modulate(_rmsnorm(x, g_pre), _mod(mod_ref, k0), _mod(mod_ref, k0 + 1)).astype(BF16)
    y = None
    for idx, (c0, c1) in enumerate(chunks):
        a = _dot(h, w1_ref[:, c0:c1])
        b = _dot(h, w3_ref[:, c0:c1])
        if idx < len(side_work):
            side_work[idx]()
        act = (a * _sigmoid(a) * b).astype(BF16)
        part = _dot(act, w2_ref[c0:c1, :])
        y = part if y is None else y + part
    return x + MACARON * _gated(_rmsnorm(y, g_post), _mod(mod_ref, k0 + 2))


def _mods_kernel(cc_ref, w_ref, b_ref, o_ref):
    @pl.when(pl.program_id(1) == 0)
    def _():
        o_ref[...] = jnp.broadcast_to(b_ref[...], o_ref.shape)

    a = cc_ref[...]
    ah, al = _split(a * _sigmoid(a))
    wh, wl = _split(w_ref[...])
    r = _dot(jnp.concatenate([ah, al], axis=0), wh)
    o_ref[...] += r[0:2 * B] + r[2 * B:4 * B] + _dot(ah, wl)


def _mods(cc, ada_w, ada_b):
    tk = 128
    n = N_MOD * D
    return pl.pallas_call(
        _mods_kernel,
        grid=(DEPTH, D // tk),
        in_specs=[
            pl.BlockSpec((2 * B, tk), lambda i, k: (0, k)),
            pl.BlockSpec((None, tk, n), lambda i, k: (i, k, 0)),
            pl.BlockSpec((None, 1, n), lambda i, k: (i, 0, 0)),
        ],
        out_specs=pl.BlockSpec((None, 2 * B, n), lambda i, k: (i, 0, 0)),
        out_shape=jax.ShapeDtypeStruct((DEPTH, 2 * B, n), F32),
        compiler_params=_cparams(("arbitrary", "arbitrary")),
        name="ada_mods",
    )(cc, ada_w, ada_b.reshape(DEPTH, 1, n))


def _mod_spec(nt_lat=NT_LAT):
    return pl.BlockSpec((None, B, N_MOD * D), lambda t: (jnp.where(t < nt_lat, 0, 1), 0, 0))


def _row_spec(width=D):
    return pl.BlockSpec((TM, width), lambda t: (t, 0))


def _wide_spec():
    return pl.BlockSpec((TS, B * D), lambda t: (t, 0))


W_CHUNKS = 8
W_SLOTS = 3
_HBM = pl.BlockSpec(memory_space=pl.ANY)
_WEIGHT_SPECS = [_HBM] * 3
_WEIGHT_SCRATCH = [pltpu.VMEM((D, D_FF), BF16), pltpu.VMEM((D, D_FF), BF16),
                   pltpu.VMEM((D_FF, D), BF16)]


def _stage_weights(srcs, dsts):
    chunks = [(src, dst, r0, dst.shape[0] // W_CHUNKS) for src, dst in zip(srcs, dsts)
              for r0 in range(0, dst.shape[0], dst.shape[0] // W_CHUNKS)]
    shapes = sorted({(rows, dst.shape[1]) for _, dst, _, rows in chunks})

    def run(*scoped):
        stages, sems = dict(zip(shapes, scoped[:-1])), scoped[-1]

        def slot(k):
            _, dst, _, rows = chunks[k]
            return stages[(rows, dst.shape[1])].at[k % W_SLOTS]

        def copy(k):
            src, _, r0, rows = chunks[k]
            return pltpu.make_async_copy(src.at[pl.ds(r0, rows), :], slot(k), sems.at[k % W_SLOTS])

        for k in range(W_SLOTS - 1):
            copy(k).start()
        for k, (_, dst, r0, rows) in enumerate(chunks):
            if k + W_SLOTS - 1 < len(chunks):
                copy(k + W_SLOTS - 1).start()
            copy(k).wait()
            dst[pl.ds(r0, rows), :] = slot(k)[...].astype(BF16)

    pl.run_scoped(run, *[pltpu.VMEM((W_SLOTS,) + shape, F32) for shape in shapes],
                  pltpu.SemaphoreType.DMA((W_SLOTS,)))


def _with_staged_weights(body, n_in, views):
    n_w = len(views)

    def kernel(*refs):
        ins, rest = refs[:n_in], refs[n_in:]
        copies = rest[-n_w:]

        @pl.when(pl.program_id(0) == 0)
        def _():
            _stage_weights([view(w) for view, w in zip(views, ins[-n_w:])], copies)

        body(*ins[:-n_w], *copies, *rest[:-n_w])

    return kernel


def _ffn_views(layer, half):
    return [lambda w: w.at[layer, half]] * 3


def _first(w):
    return w.at[0]


def _rows_of_batch(b):
    return pl.ds(b, TS, stride=B)


_ROWS_SCRATCH = pltpu.VMEM((D // LANES, TM, LANES), F32)


def _put_batch(rows_scr, b, val):
    for j in range(D // LANES):
        rows_scr[j, _rows_of_batch(b), :] = val[:, j * LANES:(j + 1) * LANES]


def _get_batch(rows_scr, b):
    return jnp.concatenate([rows_scr[j, _rows_of_batch(b), :] for j in range(D // LANES)], axis=1)


def _put_rows(rows_scr, val):
    for j in range(D // LANES):
        rows_scr[j] = val[:, j * LANES:(j + 1) * LANES]


def _get_rows(rows_scr):
    return jnp.concatenate([rows_scr[j] for j in range(D // LANES)], axis=1)


def _ffn_first_kernel(x_ref, ctx_ref, pos_ref, mod_ref, g_ref, w1, w3, w2, o_ref, rows_scr):
    is_latent = pl.program_id(0) < NT_LAT
    for b in range(B):
        _put_batch(rows_scr, b, jnp.where(is_latent, x_ref[b] + pos_ref[...], ctx_ref[b]))
    o_ref[...] = _ffn(_get_rows(rows_scr), mod_ref, 0, g_ref[0:1], g_ref[1:2], w1, w3, w2)


def _ffn_kernel(x_ref, mod_ref, g_ref, w1, w3, w2, o_ref):
    for r in range(x_ref.shape[0] // TM):
        rows = slice(r * TM, (r + 1) * TM)
        o_ref[rows, :] = _ffn(x_ref[rows, :], mod_ref, 0, g_ref[0:1], g_ref[1:2], w1, w3, w2)


def _ffn_first(x, ctx, pos, mods, g, w1, w3, w2):
    lat = lambda t: jnp.minimum(t, NT_LAT - 1)
    return pl.pallas_call(
        _with_staged_weights(_ffn_first_kernel, 8, _ffn_views(0, 0)),
        grid=(NT,),
        in_specs=[pl.BlockSpec((B, TS, D), lambda t: (0, lat(t), 0)),
                  pl.BlockSpec((B, TS, D), lambda t: (0, jnp.maximum(t - NT_LAT, 0), 0)),
                  pl.BlockSpec((TS, D), lambda t: (lat(t), 0)),
                  _mod_spec(), _resident((6, D))] + _WEIGHT_SPECS,
        out_specs=_row_spec(),
        out_shape=jax.ShapeDtypeStruct((ROWS, D), F32),
        scratch_shapes=[_ROWS_SCRATCH] + _WEIGHT_SCRATCH,
        compiler_params=_cparams(("arbitrary",)),
        name="ffn_first",
    )(x, ctx, pos, mods, g, w1, w3, w2)


def _ffn_plain(x, mods, g, w1, w3, w2, layer):
    return pl.pallas_call(
        _with_staged_weights(_ffn_kernel, 6, _ffn_views(layer, 0)),
        grid=(NT // 2,),
        in_specs=[pl.BlockSpec((2 * TM, D), lambda t: (t, 0)),
                  pl.BlockSpec((None, B, N_MOD * D),
                               lambda t: (jnp.where(t < NT_LAT // 2, 0, 1), 0, 0)),
                  _resident((6, D))] + _WEIGHT_SPECS,
        out_specs=pl.BlockSpec((2 * TM, D), lambda t: (t, 0)),
        out_shape=jax.ShapeDtypeStruct((ROWS, D), F32),
        scratch_shapes=_WEIGHT_SCRATCH,
        compiler_params=_cparams(("arbitrary",)),
        name="ffn_pre",
    )(x, mods, g, w1, w3, w2)


def _mix_ffn_core(y, x, mod_ref, g_ref, wo_ref, bo_ref, w1, w3, w2, **ffn_kwargs):
    z = _dot(y.astype(BF16), wo_ref[...]) + bo_ref[...]
    x = x + _gated(_rmsnorm(z, g_ref[3:4]), _mod(mod_ref, 5))
    return _ffn(x, mod_ref, 6, g_ref[4:5], g_ref[5:6], w1, w3, w2, **ffn_kwargs)


def _hy_out_ffn_kernel(ylo_ref, yhi_ref, x_ref, mod_ref, g_ref, bo_ref, wo_ref, w1, w3, w2,
                       o_ref, rows_scr):
    halves = (ylo_ref, yhi_ref)
    for b in range(B):
        for j in range(D // LANES):
            col = (b * (D // TC) + j // 2) * LANES
            rows_scr[j, _rows_of_batch(b), :] = halves[j % 2][:, col:col + LANES]
    o_ref[...] = _mix_ffn_core(_get_rows(rows_scr), x_ref[...], mod_ref, g_ref, wo_ref, bo_ref,
                               w1, w3, w2)


def _hy_out_ffn(y_lo, y_hi, x, mods, g, w_out, b_out, w1, w3, w2):
    half = pl.BlockSpec((TS, B * D // 2), lambda t: (t, 0))
    return pl.pallas_call(
        _with_staged_weights(_hy_out_ffn_kernel, 10, [_first] + _ffn_views(0, 1)),
        grid=(NT,),
        in_specs=[half, half, _row_spec(), _mod_spec(), _resident((6, D)), _resident((1, D)),
                  _HBM] + _WEIGHT_SPECS,
        out_specs=_row_spec(),
        out_shape=jax.ShapeDtypeStruct((ROWS, D), F32),
        scratch_shapes=[_ROWS_SCRATCH, pltpu.VMEM((D, D), BF16)] + _WEIGHT_SCRATCH,
        compiler_params=_cparams(("arbitrary",)),
        name="hyena_out_ffn",
    )(y_lo, y_hi, x, mods, g, b_out, w_out, w1, w3, w2)


def _seq_first(t):
    return jnp.logical_or(t == 0, t == NI_LAT)


def _seq_last(t):
    return jnp.logical_or(t == NI_LAT - 1, t == NI - 1)


def _halo_specs(order, lo_rows, hi_rows):
    nlo = ROWS // lo_rows
    nhi = ROWS // hi_rows
    return [
        pl.BlockSpec((lo_rows, D), lambda s: (jnp.maximum(order(s) * (TI // lo_rows) - 1, 0), 0)),
        pl.BlockSpec((TI, D), lambda s: (order(s), 0)),
        pl.BlockSpec((hi_rows, D),
                     lambda s: (jnp.minimum((order(s) + 1) * (TI // hi_rows), nhi - 1), 0)),
    ]


def _hy_in_kernel(xp_ref, x_ref, xn_ref, mod_ref, g_ref, b_ref, cw_ref, cb_ref, w_ref,
                  x0_ref, p_ref, x0_scr, p_scr):
    t = pl.program_id(0)
    xa = jnp.concatenate([xp_ref[...], x_ref[...], xn_ref[...]], axis=0)
    h = _modulate(_rmsnorm(xa, g_ref[2:3]), _mod(mod_ref, 3), _mod(mod_ref, 4)).astype(BF16)
    keep_lo = jnp.where(_seq_first(t), 0.0, 1.0).astype(F32)
    keep_hi = jnp.where(_seq_last(t), 0.0, 1.0).astype(F32)

    def conv_cols(c0):
        cols = slice(c0, c0 + MXU_N)
        u = _dot(h, w_ref[:, cols]) + b_ref[:, cols]
        lo = jnp.concatenate([u[0:B] * keep_lo, u[B:TI]], axis=0)
        hi = jnp.concatenate([u[2 * B:TI + B], u[TI + B:TI + 2 * B] * keep_hi], axis=0)
        return (cb_ref[:, cols] + cw_ref[0:1, cols] * lo + cw_ref[1:2, cols] * u[B:TI + B]
                + cw_ref[2:3, cols] * hi)

    slabs = MXU_N // LANES
    for c in range(D // MXU_N):
        c0 = c * MXU_N
        vals = (conv_cols(c0), conv_cols(D + c0) * conv_cols(2 * D + c0))
        for out_ref, scr, val in zip((x0_ref, p_ref), (x0_scr, p_scr), vals):
            for j in range(slabs):
                scr[c * slabs + j] = val[:, j * LANES:(j + 1) * LANES]
            for b in range(B):
                out_ref[:, b * D + c0:b * D + c0 + MXU_N] = jnp.concatenate(
                    [scr[c * slabs + j, pl.ds(b, TI // B, stride=B), :] for j in range(slabs)], axis=1)


def _hy_in(x, mods, g, w_in, b_in, conv_w, conv_b):
    wide = pl.BlockSpec((TI // B, B * D), lambda t: (t, 0))
    rows_scratch = pltpu.VMEM((D // LANES, TI, LANES), F32)
    return pl.pallas_call(
        _with_staged_weights(_hy_in_kernel, 9, [_first]),
        grid=(NI,),
        in_specs=_halo_specs(lambda s: s, B, B) + [
            _mod_spec(NI_LAT), _resident((6, D)), _resident((1, 3 * D)),
            _resident((3, 3 * D)), _resident((1, 3 * D)), _HBM],
        out_specs=[wide, wide],
        out_shape=[jax.ShapeDtypeStruct((ROWS // B, B * D), F32)] * 2,
        scratch_shapes=[rows_scratch, rows_scratch, pltpu.VMEM((D, 3 * D), BF16)],
        compiler_params=_cparams(("arbitrary",)),
        name="hyena_in",
    )(x, x, x, mods, g, b_in, conv_w, conv_b, w_in)


def _filter_kernel(zt_ref, t_ref, fw0t, fb0, fw1t, fb1, fw2t, fb2, freq, fwout, deltas,
                   hf_ref, hb_ref):
    h = jnp.sin(freq[:, 0:1] * (_dot3(fw0t[...], zt_ref[...]) + fb0[...]))
    h = jnp.sin(freq[:, 1:2] * (_dot3(fw1t[...], h) + fb1[...]))
    h = jnp.sin(freq[:, 2:3] * (_dot3(fw2t[...], h) + fb2[...]))
    filt = _dot3(h.T, fwout[...])
    decay = jnp.exp(-t_ref[:, 0:1] * deltas[...])
    hf_ref[...] = filt[:, 0:D] * decay
    hb_ref[...] = filt[:, D:2 * D] * decay


def _filters(n, fw0, fb0, fw1, fb1, fw2, fb2, freq, fwout):
    t = np.linspace(0.0, 1.0, n)[:, None]
    bands = np.linspace(1e-4, HY_BANDS - 1, HY_BANDS)[None]
    phase = bands * (2.0 * math.pi * np.arange(n)[:, None] / n)
    zp = np.zeros((n, LANES), np.float32)
    zp[:, :HY_EMB] = np.concatenate([t, np.cos(phase), -np.sin(phase)], axis=-1)
    zp = np.concatenate([zp[r::4] for r in range(4)], axis=0)
    fw0t = jnp.zeros((HY_HID, LANES), F32).at[:, :HY_EMB].set(fw0.T)
    max_decay = math.log(HY_DECAY_TARGET) / HY_FAST_DECAY
    min_decay = math.log(HY_DECAY_TARGET) / HY_SLOW_DECAY
    deltas = np.abs(np.linspace(min_decay, max_decay, D))[None].astype(np.float32)
    tl = 256
    row = lambda i: (i, 0)
    unit = _resident((HY_HID, 1))
    return pl.pallas_call(
        _filter_kernel,
        grid=(n // tl,),
        in_specs=[pl.BlockSpec((LANES, tl), lambda i: (0, i)), pl.BlockSpec((tl, LANES), row),
                  _resident((HY_HID, LANES)), unit, _resident((HY_HID, HY_HID)), unit,
                  _resident((HY_HID, HY_HID)), unit, _resident((HY_HID, 3)),
                  _resident((HY_HID, 2 * D)), _resident((1, D))],
        out_specs=[pl.BlockSpec((tl, D), row)] * 2,
        out_shape=[jax.ShapeDtypeStruct((n, D), F32)] * 2,
        compiler_params=_cparams(("arbitrary",)),
        name="hyena_filter",
    )(np.ascontiguousarray(zp.T), zp, fw0t, fb0[:, None], fw1.T, fb1[:, None], fw2.T, fb2[:, None],
      freq.T, fwout, deltas)


def _dft_tables(n):
    q = 1 << (int(math.log2(n)) // 2 + 1)
    m = np.arange(n, dtype=np.int64)[None, :]

    def thin(k):
        ang = ((k[:, None] * m) % (2 * n)) * (math.pi / n)
        return jnp.asarray(np.cos(ang), F32), jnp.asarray(np.sin(ang), F32)

    ca, sa = (v[:, None, :] for v in thin(q * np.arange(n // q, dtype=np.int64)))
    cb, sb = (v[None, :, :] for v in thin(np.arange(q, dtype=np.int64)))
    return (ca * cb - sa * sb).reshape(n, n), (sa * cb + ca * sb).reshape(n, n)


def _alt_sign(rows, cols):
    r = lax.broadcasted_iota(jnp.int32, (rows, cols), 0)
    return (1 - 2 * (r & 1)).astype(F32)


_N_GROUPS = 4
_ROOT_HALF = math.sqrt(0.5)


def _twiddles(g):
    k = np.arange(g)[:, None]
    parts = [f(r * k * (math.pi / (4 * g))) for r in (1, 2, 3) for f in (np.cos, np.sin)]
    return jnp.broadcast_to(jnp.asarray(np.concatenate(parts, axis=0), F32), (6 * g, TC))


def _cmul(ar, as_, br, bi):
    return ar * br + as_ * bi, as_ * br - ar * bi


def _fwd4(quarters, c, s, tw_ref, rows, g):
    ts = []
    for r, q in enumerate(quarters):
        qr, qs = _dot(c, q), _dot(s, q)
        if r:
            wc = tw_ref[pl.ds((2 * r - 2) * g + rows.start, rows.stop - rows.start), :]
            ws = tw_ref[pl.ds((2 * r - 1) * g + rows.start, rows.stop - rows.start), :]
            qr, qs = wc * qr - ws * qs, wc * qs + ws * qr
        ts.append((qr, qs))
    (t0r, t0s), (t1r, t1s), (t2r, t2s), (t3r, t3s) = ts
    er, es, fr, fs = t0r + t2r, t0s + t2s, t0r - t2r, t0s - t2s
    pr, ps, dr, ds = t1r + t3r, t1s + t3s, t1r - t3r, t1s - t3s
    return ((er + pr, es + ps), (er - pr, es - ps), (fr + ds, fs - dr), (fr - ds, fs + dr))


def _mid_freqs(sums):
    s0, s1, s2, s3 = sums
    a, b = _ROOT_HALF * (s1 - s3), _ROOT_HALF * (s1 + s3)
    return (s0 + a, s2 + b), (s0 - a, b - s2)


def _alt_sums(quarters):
    alt = _alt_sign(*quarters[0].shape)
    return [jnp.sum(q * alt, axis=0, keepdims=True) for q in quarters]


def _spectrum_kernel(hf_ref, hb_ref, c_ref, s_ref, tw_ref, kr_ref, ki_ref, kn_ref, *, g):
    hf = hf_ref[...]
    row = lax.broadcasted_iota(jnp.int32, hf.shape, 0)
    hb = jnp.where(row == 0, 0.0, hb_ref[...])
    cos_part = hf + hb
    sin_part = hb - hf
    cq = [cos_part[r * g:(r + 1) * g] for r in range(4)]
    sq = [sin_part[r * g:(r + 1) * g] for r in range(4)]
    rows = slice(0, g)
    c, s = c_ref[...], s_ref[...]
    groups_c = _fwd4([q.astype(BF16) for q in cq], c, s, tw_ref, rows, g)
    groups_s = _fwd4([q.astype(BF16) for q in sq], c, s, tw_ref, rows, g)
    n_fft = 8 * g
    k = lax.broadcasted_iota(jnp.int32, (g, hf.shape[1]), 0)
    scale = jnp.where(k == 0, 1.0 / n_fft, 2.0 / n_fft)
    for grp in range(_N_GROUPS):
        kr_ref[grp * g:(grp + 1) * g, :] = groups_c[grp][0] * scale
        ki_ref[grp * g:(grp + 1) * g, :] = groups_s[grp][1] * scale
    (cg, _), (c3g, _) = _mid_freqs(_alt_sums(cq))
    (_, sg), (_, s3g) = _mid_freqs(_alt_sums(sq))
    mids = [v * (2.0 / n_fft) for v in (cg, sg, c3g, s3g)]
    kn_ref[...] = jnp.concatenate(mids + [jnp.zeros((B - 4, hf.shape[1]), F32)], axis=0)


def _spectrum(hf, hb, ctab, stab, tw):
    n = hf.shape[0]
    g = n // 4
    col = lambda j: (0, j)
    return pl.pallas_call(
        functools.partial(_spectrum_kernel, g=g),
        grid=(D // TC,),
        in_specs=[pl.BlockSpec((n, TC), col)] * 2 + [_resident((g, g))] * 2
        + [_resident((6 * g, TC))],
        out_specs=[pl.BlockSpec((n, TC), col)] * 2 + [pl.BlockSpec((B, TC), col)],
        out_shape=[jax.ShapeDtypeStruct((n, D), F32)] * 2 + [jax.ShapeDtypeStruct((B, D), F32)],
        compiler_params=_cparams(("arbitrary",)),
        name="hyena_spectrum",
    )(hf, hb, ctab, stab, tw)


def _steps(refs, first, count):
    rows = pl.ds(first, count, stride=4)
    return jnp.concatenate([r[rows, :] for r in refs], axis=1)


def _long_conv_rows(t0, g, p_refs, x0_refs, spec, bias_ref, o_refs, scratch):
    kr_ref, ki_ref, kn_ref, c_ref, s_ref, tw_ref = spec
    q_scr, u_scr = scratch
    tk = min(g, 512)
    r0 = t0 // 4
    seq = slice(r0, r0 + g)
    chunks = [slice(k * tk, (k + 1) * tk) for k in range(g // tk)]
    quarters = [_steps(p_refs, t0 + r, g) for r in range(4)]
    tc = quarters[0].shape[1]
    for r in range(4):
        q_scr[r, seq, :] = quarters[r].astype(BF16)
    for rows in chunks:
        dst = slice(r0 + rows.start, r0 + rows.stop)
        c, s = c_ref[rows, :], s_ref[rows, :]
        groups = _fwd4([q_scr[r, seq, :] for r in range(4)], c, s, tw_ref, rows, g)
        ys = []
        for grp, (xr, xs) in enumerate(groups):
            k_rows = pl.ds(grp * g + rows.start, tk)
            ys.append(_cmul(xr, xs, kr_ref[k_rows, :], ki_ref[k_rows, :]))
        (y1r, y1s), (y2r, y2s), (y3r, y3s), (y4r, y4s) = ys
        pr, ps, mr, ms = y1r + y2r, y1s + y2s, y1r - y2r, y1s - y2s
        qr, qs, nr, ns = y3r + y4r, y3s + y4s, y3r - y4r, y3s - y4s
        us = [(pr + qr, ps + qs), (mr - ns, ms + nr), (pr - qr, ps - qs), (mr + ns, ms - nr)]
        for r, (ur, us_) in enumerate(us):
            if r:
                wc = tw_ref[pl.ds((2 * r - 2) * g + rows.start, tk), :]
                ws = tw_ref[pl.ds((2 * r - 1) * g + rows.start, tk), :]
                ur, us_ = ur * wc + us_ * ws, us_ * wc - ur * ws
            u_scr[2 * r, dst, :] = ur.astype(BF16)
            u_scr[2 * r + 1, dst, :] = us_.astype(BF16)
    (xgr, xgs), (x3r, x3s) = _mid_freqs(_alt_sums(quarters))
    ygr, ygs = _cmul(xgr, xgs, kn_ref[0:1, :], kn_ref[1:2, :])
    y3r_, y3s_ = _cmul(x3r, x3s, kn_ref[2:3, :], kn_ref[3:4, :])
    a = _ROOT_HALF
    mids = [ygr + y3r_, a * (ygr + ygs - y3r_ + y3s_), ygs - y3s_, a * (ygs - ygr + y3r_ + y3s_)]
    alt_chunk = _alt_sign(tk, tc)
    bias = bias_ref[...]
    for rows in chunks:
        c, s = c_ref[rows, :], s_ref[rows, :]
        for r in range(4):
            first = t0 + 4 * rows.start + r
            y = (_dot(c, u_scr[2 * r, seq, :]) + _dot(s, u_scr[2 * r + 1, seq, :])
                 + alt_chunk * mids[r])
            out = _steps(x0_refs, first, tk) * (y + _steps(p_refs, first, tk) * bias)
            for q, o_ref in enumerate(o_refs):
                o_ref[pl.ds(first, tk, stride=4), :] = out[:, q * LANES:(q + 1) * LANES]


_N_SPEC = 6


def _long_conv_kernel(*refs):
    p_refs, x0_refs, bias_ref = refs[0:2], refs[2:4], refs[4]
    lat, ctx = refs[5:5 + _N_SPEC], refs[5 + _N_SPEC:5 + 2 * _N_SPEC]
    o_refs, scratch = refs[5 + 2 * _N_SPEC:7 + 2 * _N_SPEC], refs[7 + 2 * _N_SPEC:]
    _long_conv_rows(0, L // 4, p_refs, x0_refs, lat, bias_ref, o_refs, scratch)
    _long_conv_rows(L, CTX // 4, p_refs, x0_refs, ctx, bias_ref, o_refs, scratch)


def _long_conv(p2, x02, bias, spec_lat, spec_ctx):
    nc = D // TC
    halves = [pl.BlockSpec((ROWS // B, LANES),
                           functools.partial(lambda q, j: (0, 2 * ((j % B) * nc + j // B) + q), q))
              for q in range(TC // LANES)]
    ch = lambda j: (0, j // B)

    def spec_specs(g):
        return ([pl.BlockSpec((4 * g, TC), ch)] * 2 + [pl.BlockSpec((B, TC), ch)]
                + [_resident((g, g))] * 2 + [_resident((6 * g, TC))])

    half_out = pl.BlockSpec((ROWS // B, LANES), lambda j: (0, (j % B) * nc + j // B))
    quarter_rows = ROWS // (4 * B)
    return pl.pallas_call(
        _long_conv_kernel,
        grid=(B * D // TC,),
        scratch_shapes=[pltpu.VMEM((4, quarter_rows, TC), BF16),
                        pltpu.VMEM((8, quarter_rows, TC), BF16)],
        in_specs=halves + halves + [pl.BlockSpec((1, TC), ch)] + spec_specs(L // 4)
        + spec_specs(CTX // 4),
        out_specs=[half_out] * (TC // LANES),
        out_shape=[jax.ShapeDtypeStruct((ROWS // B, B * D * LANES // TC), F32)] * (TC // LANES),
        compiler_params=_cparams(("arbitrary",)),
        name="hyena_long_conv",
    )(p2, p2, x02, x02, bias, *spec_lat, *spec_ctx)


def _gelu_tanh(x):
    return x * (0.5 * (1.0 + jnp.tanh(math.sqrt(2.0 / math.pi) * (x + 0.044715 * (x * x * x)))))


_TINY = 1e-30


def _rg_coeffs(xc, hd, wai_ref, bai_ref, lam_ref, a_scr, b_scr):
    sl = slice(hd * RG_BLOCK, (hd + 1) * RG_BLOCK)
    lam = lam_ref[:, sl]
    softplus_neg = jnp.maximum(-lam, 0.0) + jnp.log1p(jnp.exp(-jnp.abs(lam)))
    rate = (-RG_C * math.log2(math.e)) * softplus_neg
    pre = _dot(xc.astype(BF16), wai_ref[hd]) + bai_ref[:, 2 * hd * RG_BLOCK:2 * (hd + 1) * RG_BLOCK]
    gates = _sigmoid(pre)
    a = jnp.exp2(gates[:, 0:RG_BLOCK] * rate)
    a_scr[:, sl] = a
    v = (1.0 - a) * (1.0 + a)
    root = v * lax.rsqrt(jnp.maximum(v, _TINY))
    b_scr[:, sl] = root * gates[:, RG_BLOCK:2 * RG_BLOCK] * xc


def _scan_tile(a_scr, b_scr, h_scr, emit, reverse):
    steps = a_scr.shape[0] // B

    def body(k, h):
        t = steps - 1 - k if reverse else k
        r0 = pl.multiple_of(t * B, B)
        h = a_scr[pl.ds(r0, B), :] * h + b_scr[pl.ds(r0, B), :]
        emit(r0, h)
        return h

    h_scr[...] = lax.fori_loop(0, steps, body, h_scr[...], unroll=8)


def _rg_fwd_order(s):
    return jnp.where(s < NI_CTX, NI_LAT + s, s - NI_CTX)


def _rg_in_kernel(xp_ref, x_ref, xn_ref, mod_ref, g_ref, b_ref, cw_ref, cb_ref,
                  wai_ref, bai_ref, lam_ref, w_ref, xc_ref, gate_ref, hs_ref, a_scr, b_scr, h_scr):
    s = pl.program_id(0)
    t = _rg_fwd_order(s)
    xa = jnp.concatenate([xp_ref[...], x_ref[...], xn_ref[...]], axis=0)
    h = _modulate(_rmsnorm(xa, g_ref[2:3]), _mod(mod_ref, 3), _mod(mod_ref, 4)).astype(BF16)
    keep_lo = jnp.where(_seq_first(t), 0.0, 1.0).astype(F32)
    keep_hi = jnp.where(_seq_last(t), 0.0, 1.0).astype(F32)
    for hd in range(RG_HEADS):
        sl = slice(hd * RG_BLOCK, (hd + 1) * RG_BLOCK)
        rec = slice(D + hd * RG_BLOCK, D + (hd + 1) * RG_BLOCK)
        gate_ref[:, sl] = (_dot(h[B:TI + B], w_ref[:, sl]) + b_ref[:, sl]).astype(BF16)
        u = _dot(h, w_ref[:, rec]) + b_ref[:, rec]
        taps = (jnp.concatenate([u[0:B] * keep_lo, u[B:TI]], axis=0),
                u[B:TI + B],
                jnp.concatenate([u[2 * B:TI + B], u[TI + B:TI + 2 * B] * keep_hi], axis=0),
                jnp.concatenate([u[3 * B:TI + B], u[TI + B:TI + 3 * B] * keep_hi], axis=0))
        xc = cb_ref[:, sl]
        for k, tap in enumerate(taps):
            xc = xc + cw_ref[k:k + 1, sl] * tap
        xc_ref[:, sl] = xc
        _rg_coeffs(xc, hd, wai_ref, bai_ref, lam_ref, a_scr, b_scr)

    @pl.when(s == 0)
    def _():
        h_scr[...] = jnp.zeros((B, D), F32)

    def emit(r0, hv):
        hs_ref[pl.ds(r0, B), :] = hv

    _scan_tile(a_scr, b_scr, h_scr, emit, reverse=False)


def _rg_gate_specs():
    return [_resident((RG_HEADS, RG_BLOCK, 2 * RG_BLOCK)), _resident((1, 2 * D)), _resident((1, D))]


def _rg_in(x, mods, g, w_in, b_in, conv_w, conv_b, wai, bai, lam):
    order = _rg_fwd_order
    mod_spec = pl.BlockSpec((None, B, N_MOD * D),
                            lambda s: (jnp.where(order(s) < NI_LAT, 0, 1), 0, 0))
    out_spec = pl.BlockSpec((TI, D), lambda s: (order(s), 0))
    return pl.pallas_call(
        _with_staged_weights(_rg_in_kernel, 12, [_first]),
        grid=(NI,),
        in_specs=_halo_specs(order, B, 2 * B) + [
            mod_spec, _resident((6, D)), _resident((1, 2 * D)),
            _resident((4, D)), _resident((1, D))] + _rg_gate_specs() + [_HBM],
        out_specs=[out_spec] * 3,
        out_shape=[jax.ShapeDtypeStruct((ROWS, D), F32), jax.ShapeDtypeStruct((ROWS, D), BF16),
                   jax.ShapeDtypeStruct((ROWS, D), F32)],
        scratch_shapes=[pltpu.VMEM((TI, D), F32), pltpu.VMEM((TI, D), F32),
                        pltpu.VMEM((B, D), F32), pltpu.VMEM((D, 2 * D), BF16)],
        compiler_params=_cparams(("arbitrary",)),
        name="rglru_in_fwd_scan",
    )(x, x, x, mods, g, b_in, conv_w, conv_b, wai, bai, lam, w_in)


def _rg_tile_coeffs(xc_ref, wai_ref, bai_ref, lam_ref, a_scr, b_scr):
    for hd in range(RG_HEADS):
        xc = xc_ref[:, hd * RG_BLOCK:(hd + 1) * RG_BLOCK]
        _rg_coeffs(xc, hd, wai_ref, bai_ref, lam_ref, a_scr, b_scr)


def _rg_ctx_bwd_kernel(xc_ref, wai_ref, bai_ref, lam_ref, h_ref, a_scr, b_scr, h_scr):
    _rg_tile_coeffs(xc_ref, wai_ref, bai_ref, lam_ref, a_scr, b_scr)

    @pl.when(pl.program_id(0) == 0)
    def _():
        h_scr[...] = jnp.zeros((B, D), F32)

    _scan_tile(a_scr, b_scr, h_scr, lambda r0, hv: None, reverse=True)
    h_ref[...] = h_scr[...]


def _rg_ctx_bwd(xc, wai, bai, lam):
    return pl.pallas_call(
        _rg_ctx_bwd_kernel,
        grid=(NT_CTX,),
        in_specs=[pl.BlockSpec((TM, D), lambda s: (NT - 1 - s, 0))] + _rg_gate_specs(),
        out_specs=pl.BlockSpec((B, D), lambda s: (0, 0)),
        out_shape=jax.ShapeDtypeStruct((B, D), F32),
        scratch_shapes=[pltpu.VMEM((TM, D), F32), pltpu.VMEM((TM, D), F32),
                        pltpu.VMEM((B, D), F32)],
        compiler_params=_cparams(("arbitrary",)),
        name="rglru_ctx_bwd_scan",
    )(xc, wai, bai, lam)


def _rg_out_ffn_kernel(xc_ref, gate_ref, hs_ref, x_ref, h0_ref, mod_ref, g_ref,
                       wai_ref, bai_ref, lam_ref, bo_ref, wo_ref, w1, w3, w2,
                       o_ref, a_scr, b_scr, hb_scr, h_scr, rows_scr):
    s = pl.program_id(0)

    def head_gates(hd):
        xc = xc_ref[:, hd * RG_BLOCK:(hd + 1) * RG_BLOCK]
        _rg_coeffs(xc, hd, wai_ref, bai_ref, lam_ref, a_scr, b_scr)

    @pl.when(s == 0)
    def _():
        h_scr[...] = h0_ref[...]
        for hd in range(RG_HEADS):
            head_gates(hd)

    @pl.when(s > 0)
    def _():
        y = (hs_ref[...] + hb_scr[...]) * _gelu_tanh(gate_ref[...].astype(F32))
        res = _mix_ffn_core(
            y, x_ref[...], mod_ref, g_ref, wo_ref, bo_ref, w1, w3, w2, chunks=F_QUARTERS,
            side_work=[functools.partial(head_gates, hd) for hd in range(RG_HEADS)])
        _put_rows(rows_scr, res)
        for b in range(B):
            o_ref[b] = _get_batch(rows_scr, b)

    @pl.when(s < NT_LAT)
    def _():
        def emit(r0, hv):
            hb_scr[pl.ds(r0, B), :] = hv

        _scan_tile(a_scr, b_scr, h_scr, emit, reverse=True)


def _rg_out_ffn(xc, gate, hs, x, h0, mods, g, wai, bai, lam, w_out, b_out, w1, w3, w2):
    scan_tile = lambda s: (jnp.maximum(NT_LAT - 1 - s, 0), 0)
    out_tile = lambda s: jnp.minimum(NT_LAT - s, NT_LAT - 1)
    prev = pl.BlockSpec((TM, D), lambda s: (out_tile(s), 0))
    return pl.pallas_call(
        _with_staged_weights(_rg_out_ffn_kernel, 15, [_first] + _ffn_views(1, 1)),
        grid=(NT_LAT + 1,),
        in_specs=[pl.BlockSpec((TM, D), scan_tile), prev, prev, prev, _resident((B, D)),
                  pl.BlockSpec((None, B, N_MOD * D), lambda s: (0, 0, 0)), _resident((6, D))]
        + _rg_gate_specs() + [_resident((1, D)), _HBM] + _WEIGHT_SPECS,
        out_specs=pl.BlockSpec((B, TS, D), lambda s: (0, out_tile(s), 0)),
        out_shape=jax.ShapeDtypeStruct((B, L, D), F32),
        scratch_shapes=[pltpu.VMEM((TM, D), F32), pltpu.VMEM((TM, D), F32),
                        pltpu.VMEM((TM, D), F32), pltpu.VMEM((B, D), F32), _ROWS_SCRATCH,
                        pltpu.VMEM((D, D), BF16)] + _WEIGHT_SCRATCH,
        compiler_params=_cparams(("arbitrary",)),
        name="rglru_out_ffn",
    )(xc, gate, hs, x, h0, mods, g, wai, bai, lam, b_out, w_out, w1, w3, w2)


def _grid_pos():
    rows = L // GRID_W
    quarter = D // 4
    omega = POS_BASE ** (-np.arange(quarter) / quarter)

    def emb(q):
        ang = q[:, None] * omega[None]
        return jnp.asarray(np.concatenate([np.sin(ang), np.cos(ang)], axis=-1), F32)

    row_code = jnp.repeat(emb(np.arange(rows)), GRID_W, axis=0)
    col_code = jnp.tile(emb(np.arange(GRID_W)), (rows, 1))
    return jnp.concatenate([row_code, col_code], axis=-1)


def kernel(x, c, ctx, c_ctx, ada_w, ada_b, norm_g, ffn_w1, ffn_w3, ffn_w2, hy_w_in, hy_b_in, hy_conv_w, hy_conv_b, hy_fw0, hy_fb0, hy_fw1, hy_fb1, hy_fw2, hy_fb2, hy_freq, hy_fwout, hy_filt_bias, hy_w_out, hy_b_out, rg_w_in, rg_b_in, rg_conv_w, rg_conv_b, rg_wa, rg_ba, rg_wi, rg_bi, rg_lam, rg_w_out, rg_b_out):
    cc = jnp.concatenate([c, jnp.broadcast_to(c_ctx[None], (B, D))], axis=0)
    mods = _mods(cc, ada_w, ada_b).reshape(DEPTH, 2, B, N_MOD * D)
    w1, w3, w2 = ffn_w1, ffn_w3, ffn_w2

    g = norm_g[0]
    xs = _ffn_first(x, ctx, _grid_pos(), mods[0], g, w1, w3, w2)
    x0, p = _hy_in(xs, mods[0], g, hy_w_in, hy_b_in[0][None],
                   hy_conv_w[0], hy_conv_b[0][None])
    fparams = (hy_fw0[0], hy_fb0[0], hy_fw1[0], hy_fb1[0], hy_fw2[0], hy_fb2[0],
               hy_freq[0], hy_fwout[0])
    specs = []
    for n in (L, CTX):
        hf, hb = _filters(n, *fparams)
        tables = [tab.astype(BF16) for tab in _dft_tables(n // 4)] + [_twiddles(n // 4)]
        specs.append(list(_spectrum(hf, hb, *tables)) + tables)
    y_lo, y_hi = _long_conv(p, x0, hy_filt_bias[0][None], *specs)
    xs = _hy_out_ffn(y_lo, y_hi, xs, mods[0], g, hy_w_out, hy_b_out[0][None],
                     w1, w3, w2)

    g = norm_g[1]
    xs = _ffn_plain(xs, mods[1], g, w1, w3, w2, layer=1)
    wai = jnp.concatenate([rg_wa[0], rg_wi[0]], axis=-1).astype(BF16)
    per_head = lambda v: v.reshape(2, RG_HEADS, RG_BLOCK)
    bai = jnp.concatenate([per_head(rg_ba[0]), per_head(rg_bi[0])], axis=-1).reshape(2, 1, 2 * D)
    lam = rg_lam[0][:, None, :]
    xc, gate, hs = _rg_in(xs, mods[1], g, rg_w_in, rg_b_in[0][None],
                          rg_conv_w[0], rg_conv_b[0][None], wai[0], bai[0], lam[0])
    h_ctx = _rg_ctx_bwd(xc, wai[1], bai[1], lam[1])
    return _rg_out_ffn(xc, gate, hs, xs, h_ctx, mods[1], g, wai[1], bai[1], lam[1],
                       rg_w_out, rg_b_out[0][None], w1, w3, w2)
```

```python
import functools
import math

import jax
import jax.numpy as jnp
import numpy as np
from jax import lax
from jax.experimental import pallas as pl
from jax.experimental.pallas import tpu as pltpu

F32 = jnp.float32
BF16 = jnp.bfloat16

D = 1024
B = 8
LANES = 128
MXU_N = 256
L = 2048
CTX = 256
DEPTH = 2
GRID_W = 64
D_FF = 2816
N_MOD = 9
MACARON = 0.5
NORM_EPS = 1e-6
POS_BASE = 10000.0
HY_EMB = 33
HY_BANDS = 16
HY_HID = 64
HY_FAST_DECAY = 0.3
HY_SLOW_DECAY = 1.5
HY_DECAY_TARGET = 1e-2
RG_HEADS = 4
RG_BLOCK = D // RG_HEADS
RG_C = 8.0

ROWS_LAT = L * B
ROWS_CTX = CTX * B
ROWS = ROWS_LAT + ROWS_CTX
TM = 512
TS = TM // B
NT_LAT = ROWS_LAT // TM
NT_CTX = ROWS_CTX // TM
NT = NT_LAT + NT_CTX
TI = 1024
NI_LAT = ROWS_LAT // TI
NI_CTX = ROWS_CTX // TI
NI = NI_LAT + NI_CTX
F_CHUNKS = ((0, 1536), (1536, D_FF))
F_QUARTERS = ((0, 768), (768, 1536), (1536, 2304), (2304, D_FF))
TC = 256
VMEM_LIMIT = 58 * 1024 * 1024


def _cparams(sem):
    return pltpu.CompilerParams(dimension_semantics=sem, vmem_limit_bytes=VMEM_LIMIT)


def _resident(shape):
    nd = len(shape)
    return pl.BlockSpec(shape, lambda *_: (0,) * nd, pipeline_mode=pl.Buffered(1))


def _split(a):
    hi = a.astype(BF16)
    lo = (a - hi.astype(F32)).astype(BF16)
    return hi, lo


def _dot(a, b):
    return jnp.dot(a, b, preferred_element_type=F32)


def _dot3(a, b):
    ah, al = _split(a)
    bh, bl = _split(b)
    return _dot(ah, bh) + _dot(ah, bl) + _dot(al, bh)


def _rmsnorm(x, g):
    ms = jnp.mean(x * x, axis=-1, keepdims=True)
    return x * lax.rsqrt(ms + NORM_EPS) * g


def _mod(mod_ref, k):
    return mod_ref[:, k * D:(k + 1) * D]


def _per_batch(x, fn):
    rows = x.shape[0]
    return fn(x.reshape(rows // B, B, x.shape[1])).reshape(rows, x.shape[1])


def _modulate(xn, shift8, scale8):
    return _per_batch(xn, lambda v: v * (1.0 + scale8)[None] + shift8[None])


def _gated(z, gate8):
    return _per_batch(z, lambda v: v * gate8[None])


def _sigmoid(x):
    return jax.nn.sigmoid(x)


def _ffn(x, mod_ref, k0, g_pre, g_post, w1_ref, w3_ref, w2_ref, chunks=F_CHUNKS, side_work=()):
    h = _modulate(_rmsnorm(x, g_pre), _mod(mod_ref, k0), _mod(mod_ref, k0 + 1)).astype(BF16)
    y = None
    for idx, (c0, c1) in enumerate(chunks):
        a = _dot(h, w1_ref[:, c0:c1])
        b = _dot(h, w3_ref[:, c0:c1])
        if idx < len(side_work):
            side_work[idx]()
        act = (a * _sigmoid(a) * b).astype(BF16)
        part = _dot(act, w2_ref[c0:c1, :])
        y = part if y is None else y + part
    return x + MACARON * _gated(_rmsnorm(y, g_post), _mod(mod_ref, k0 + 2))


def _mods_kernel(cc_ref, w_ref, b_ref, o_ref):
    @pl.when(pl.program_id(1) == 0)
    def _():
        o_ref[...] = jnp.broadcast_to(b_ref[...], o_ref.shape)

    a = cc_ref[...]
    ah, al = _split(a * _sigmoid(a))
    wh, wl = _split(w_ref[...])
    r = _dot(jnp.concatenate([ah, al], axis=0), wh)
    o_ref[...] += r[0:2 * B] + r[2 * B:4 * B] + _dot(ah, wl)


def _mods(cc, ada_w, ada_b):
    tk = 128
    n = N_MOD * D
    return pl.pallas_call(
        _mods_kernel,
        grid=(DEPTH, D // tk),
        in_specs=[
            pl.BlockSpec((2 * B, tk), lambda i, k: (0, k)),
            pl.BlockSpec((None, tk, n), lambda i, k: (i, k, 0)),
            pl.BlockSpec((None, 1, n), lambda i, k: (i, 0, 0)),
        ],
        out_specs=pl.BlockSpec((None, 2 * B, n), lambda i, k: (i, 0, 0)),
        out_shape=jax.ShapeDtypeStruct((DEPTH, 2 * B, n), F32),
        compiler_params=_cparams(("arbitrary", "arbitrary")),
        name="ada_mods",
    )(cc, ada_w, ada_b.reshape(DEPTH, 1, n))


def _mod_spec(nt_lat=NT_LAT):
    return pl.BlockSpec((None, B, N_MOD * D), lambda t: (jnp.where(t < nt_lat, 0, 1), 0, 0))


def _row_spec(width=D):
    return pl.BlockSpec((TM, width), lambda t: (t, 0))


def _wide_spec():
    return pl.BlockSpec((TS, B * D), lambda t: (t, 0))


W_CHUNKS = 8
W_SLOTS = 3
_HBM = pl.BlockSpec(memory_space=pl.ANY)
_WEIGHT_SPECS = [_HBM] * 3
_WEIGHT_SCRATCH = [pltpu.VMEM((D, D_FF), BF16), pltpu.VMEM((D, D_FF), BF16),
                   pltpu.VMEM((D_FF, D), BF16)]


def _stage_weights(srcs, dsts):
    chunks = [(src, dst, r0, dst.shape[0] // W_CHUNKS) for src, dst in zip(srcs, dsts)
              for r0 in range(0, dst.shape[0], dst.shape[0] // W_CHUNKS)]
    shapes = sorted({(rows, dst.shape[1]) for _, dst, _, rows in chunks})

    def run(*scoped):
        stages, sems = dict(zip(shapes, scoped[:-1])), scoped[-1]

        def slot(k):
            _, dst, _, rows = chunks[k]
            return stages[(rows, dst.shape[1])].at[k % W_SLOTS]

        def copy(k):
            src, _, r0, rows = chunks[k]
            return pltpu.make_async_copy(src.at[pl.ds(r0, rows), :], slot(k), sems.at[k % W_SLOTS])

        for k in range(W_SLOTS - 1):
            copy(k).start()
        for k, (_, dst, r0, rows) in enumerate(chunks):
            if k + W_SLOTS - 1 < len(chunks):
                copy(k + W_SLOTS - 1).start()
            copy(k).wait()
            dst[pl.ds(r0, rows), :] = slot(k)[...].astype(BF16)

    pl.run_scoped(run, *[pltpu.VMEM((W_SLOTS,) + shape, F32) for shape in shapes],
                  pltpu.SemaphoreType.DMA((W_SLOTS,)))


def _with_staged_weights(body, n_in, views):
    n_w = len(views)

    def kernel(*refs):
        ins, rest = refs[:n_in], refs[n_in:]
        copies = rest[-n_w:]

        @pl.when(pl.program_id(0) == 0)
        def _():
            _stage_weights([view(w) for view, w in zip(views, ins[-n_w:])], copies)

        body(*ins[:-n_w], *copies, *rest[:-n_w])

    return kernel


def _ffn_views(layer, half):
    return [lambda w: w.at[layer, half]] * 3


def _first(w):
    return w.at[0]


def _rows_of_batch(b):
    return pl.ds(b, TS, stride=B)


_ROWS_SCRATCH = pltpu.VMEM((D // LANES, TM, LANES), F32)


def _put_batch(rows_scr, b, val):
    for j in range(D // LANES):
        rows_scr[j, _rows_of_batch(b), :] = val[:, j * LANES:(j + 1) * LANES]


def _get_batch(rows_scr, b):
    return jnp.concatenate([rows_scr[j, _rows_of_batch(b), :] for j in range(D // LANES)], axis=1)


def _put_rows(rows_scr, val):
    for j in range(D // LANES):
        rows_scr[j] = val[:, j * LANES:(j + 1) * LANES]


def _get_rows(rows_scr):
    return jnp.concatenate([rows_scr[j] for j in range(D // LANES)], axis=1)


def _ffn_first_kernel(x_ref, ctx_ref, pos_ref, mod_ref, g_ref, w1, w3, w2, o_ref, rows_scr):
    is_latent = pl.program_id(0) < NT_LAT
    for b in range(B):
        _put_batch(rows_scr, b, jnp.where(is_latent, x_ref[b] + pos_ref[...], ctx_ref[b]))
    o_ref[...] = _ffn(_get_rows(rows_scr), mod_ref, 0, g_ref[0:1], g_ref[1:2], w1, w3, w2)


def _ffn_kernel(x_ref, mod_ref, g_ref, w1, w3, w2, o_ref):
    for r in range(x_ref.shape[0] // TM):
        rows = slice(r * TM, (r + 1) * TM)
        o_ref[rows, :] = _ffn(x_ref[rows, :], mod_ref, 0, g_ref[0:1], g_ref[1:2], w1, w3, w2)


def _ffn_first(x, ctx, pos, mods, g, w1, w3, w2):
    lat = lambda t: jnp.minimum(t, NT_LAT - 1)
    return pl.pallas_call(
        _with_staged_weights(_ffn_first_kernel, 8, _ffn_views(0, 0)),
        grid=(NT,),
        in_specs=[pl.BlockSpec((B, TS, D), lambda t: (0, lat(t), 0)),
                  pl.BlockSpec((B, TS, D), lambda t: (0, jnp.maximum(t - NT_LAT, 0), 0)),
                  pl.BlockSpec((TS, D), lambda t: (lat(t), 0)),
                  _mod_spec(), _resident((6, D))] + _WEIGHT_SPECS,
        out_specs=_row_spec(),
        out_shape=jax.ShapeDtypeStruct((ROWS, D), F32),
        scratch_shapes=[_ROWS_SCRATCH] + _WEIGHT_SCRATCH,
        compiler_params=_cparams(("arbitrary",)),
        name="ffn_first",
    )(x, ctx, pos, mods, g, w1, w3, w2)


def _ffn_plain(x, mods, g, w1, w3, w2, layer):
    return pl.pallas_call(
        _with_staged_weights(_ffn_kernel, 6, _ffn_views(layer, 0)),
        grid=(NT // 2,),
        in_specs=[pl.BlockSpec((2 * TM, D), lambda t: (t, 0)),
                  pl.BlockSpec((None, B, N_MOD * D),
                               lambda t: (jnp.where(t < NT_LAT // 2, 0, 1), 0, 0)),
                  _resident((6, D))] + _WEIGHT_SPECS,
        out_specs=pl.BlockSpec((2 * TM, D), lambda t: (t, 0)),
        out_shape=jax.ShapeDtypeStruct((ROWS, D), F32),
        scratch_shapes=_WEIGHT_SCRATCH,
        compiler_params=_cparams(("arbitrary",)),
        name="ffn_pre",
    )(x, mods, g, w1, w3, w2)


def _mix_ffn_core(y, x, mod_ref, g_ref, wo_ref, bo_ref, w1, w3, w2, **ffn_kwargs):
    z = _dot(y.astype(BF16), wo_ref[...]) + bo_ref[...]
    x = x + _gated(_rmsnorm(z, g_ref[3:4]), _mod(mod_ref, 5))
    return _ffn(x, mod_ref, 6, g_ref[4:5], g_ref[5:6], w1, w3, w2, **ffn_kwargs)


def _hy_out_ffn_kernel(ylo_ref, yhi_ref, x_ref, mod_ref, g_ref, bo_ref, wo_ref, w1, w3, w2,
                       o_ref, rows_scr):
    halves = (ylo_ref, yhi_ref)
    for b in range(B):
        for j in range(D // LANES):
            col = (b * (D // TC) + j // 2) * LANES
            rows_scr[j, _rows_of_batch(b), :] = halves[j % 2][:, col:col + LANES]
    o_ref[...] = _mix_ffn_core(_get_rows(rows_scr), x_ref[...], mod_ref, g_ref, wo_ref, bo_ref,
                               w1, w3, w2)


def _hy_out_ffn(y_lo, y_hi, x, mods, g, w_out, b_out, w1, w3, w2):
    half = pl.BlockSpec((TS, B * D // 2), lambda t: (t, 0))
    return pl.pallas_call(
        _with_staged_weights(_hy_out_ffn_kernel, 10, [_first] + _ffn_views(0, 1)),
        grid=(NT,),
        in_specs=[half, half, _row_spec(), _mod_spec(), _resident((6, D)), _resident((1, D)),
                  _HBM] + _WEIGHT_SPECS,
        out_specs=_row_spec(),
        out_shape=jax.ShapeDtypeStruct((ROWS, D), F32),
        scratch_shapes=[_ROWS_SCRATCH, pltpu.VMEM((D, D), BF16)] + _WEIGHT_SCRATCH,
        compiler_params=_cparams(("arbitrary",)),
        name="hyena_out_ffn",
    )(y_lo, y_hi, x, mods, g, b_out, w_out, w1, w3, w2)


def _seq_first(t):
    return jnp.logical_or(t == 0, t == NI_LAT)


def _seq_last(t):
    return jnp.logical_or(t == NI_LAT - 1, t == NI - 1)


def _halo_specs(order, lo_rows, hi_rows):
    nlo = ROWS // lo_rows
    nhi = ROWS // hi_rows
    return [
        pl.BlockSpec((lo_rows, D), lambda s: (jnp.maximum(order(s) * (TI // lo_rows) - 1, 0), 0)),
        pl.BlockSpec((TI, D), lambda s: (order(s), 0)),
        pl.BlockSpec((hi_rows, D),
                     lambda s: (jnp.minimum((order(s) + 1) * (TI // hi_rows), nhi - 1), 0)),
    ]


def _hy_in_kernel(xp_ref, x_ref, xn_ref, mod_ref, g_ref, b_ref, cw_ref, cb_ref, w_ref,
                  x0_ref, p_ref, x0_scr, p_scr):
    t = pl.program_id(0)
    xa = jnp.concatenate([xp_ref[...], x_ref[...], xn_ref[...]], axis=0)
    h = _modulate(_rmsnorm(xa, g_ref[2:3]), _mod(mod_ref, 3), _mod(mod_ref, 4)).astype(BF16)
    keep_lo = jnp.where(_seq_first(t), 0.0, 1.0).astype(F32)
    keep_hi = jnp.where(_seq_last(t), 0.0, 1.0).astype(F32)

    def conv_cols(c0):
        cols = slice(c0, c0 + MXU_N)
        u = _dot(h, w_ref[:, cols]) + b_ref[:, cols]
        lo = jnp.concatenate([u[0:B] * keep_lo, u[B:TI]], axis=0)
        hi = jnp.concatenate([u[2 * B:TI + B], u[TI + B:TI + 2 * B] * keep_hi], axis=0)
        return (cb_ref[:, cols] + cw_ref[0:1, cols] * lo + cw_ref[1:2, cols] * u[B:TI + B]
                + cw_ref[2:3, cols] * hi)

    slabs = MXU_N // LANES
    for c in range(D // MXU_N):
        c0 = c * MXU_N
        vals = (conv_cols(c0), conv_cols(D + c0) * conv_cols(2 * D + c0))
        for out_ref, scr, val in zip((x0_ref, p_ref), (x0_scr, p_scr), vals):
            for j in range(slabs):
                scr[c * slabs + j] = val[:, j * LANES:(j + 1) * LANES]
            for b in range(B):
                out_ref[:, b * D + c0:b * D + c0 + MXU_N] = jnp.concatenate(
                    [scr[c * slabs + j, pl.ds(b, TI // B, stride=B), :] for j in range(slabs)], axis=1)


def _hy_in(x, mods, g, w_in, b_in, conv_w, conv_b):
    wide = pl.BlockSpec((TI // B, B * D), lambda t: (t, 0))
    rows_scratch = pltpu.VMEM((D // LANES, TI, LANES), F32)
    return pl.pallas_call(
        _with_staged_weights(_hy_in_kernel, 9, [_first]),
        grid=(NI,),
        in_specs=_halo_specs(lambda s: s, B, B) + [
            _mod_spec(NI_LAT), _resident((6, D)), _resident((1, 3 * D)),
            _resident((3, 3 * D)), _resident((1, 3 * D)), _HBM],
        out_specs=[wide, wide],
        out_shape=[jax.ShapeDtypeStruct((ROWS // B, B * D), F32)] * 2,
        scratch_shapes=[rows_scratch, rows_scratch, pltpu.VMEM((D, 3 * D), BF16)],
        compiler_params=_cparams(("arbitrary",)),
        name="hyena_in",
    )(x, x, x, mods, g, b_in, conv_w, conv_b, w_in)


def _filter_kernel(zt_ref, t_ref, fw0t, fb0, fw1t, fb1, fw2t, fb2, freq, fwout, deltas,
                   hf_ref, hb_ref):
    h = jnp.sin(freq[:, 0:1] * (_dot3(fw0t[...], zt_ref[...]) + fb0[...]))
    h = jnp.sin(freq[:, 1:2] * (_dot3(fw1t[...], h) + fb1[...]))
    h = jnp.sin(freq[:, 2:3] * (_dot3(fw2t[...], h) + fb2[...]))
    filt = _dot3(h.T, fwout[...])
    decay = jnp.exp(-t_ref[:, 0:1] * deltas[...])
    hf_ref[...] = filt[:, 0:D] * decay
    hb_ref[...] = filt[:, D:2 * D] * decay


def _filters(n, fw0, fb0, fw1, fb1, fw2, fb2, freq, fwout):
    t = np.linspace(0.0, 1.0, n)[:, None]
    bands = np.linspace(1e-4, HY_BANDS - 1, HY_BANDS)[None]
    phase = bands * (2.0 * math.pi * np.arange(n)[:, None] / n)
    zp = np.zeros((n, LANES), np.float32)
    zp[:, :HY_EMB] = np.concatenate([t, np.cos(phase), -np.sin(phase)], axis=-1)
    zp = np.concatenate([zp[r::4] for r in range(4)], axis=0)
    fw0t = jnp.zeros((HY_HID, LANES), F32).at[:, :HY_EMB].set(fw0.T)
    max_decay = math.log(HY_DECAY_TARGET) / HY_FAST_DECAY
    min_decay = math.log(HY_DECAY_TARGET) / HY_SLOW_DECAY
    deltas = np.abs(np.linspace(min_decay, max_decay, D))[None].astype(np.float32)
    tl = 256
    row = lambda i: (i, 0)
    unit = _resident((HY_HID, 1))
    return pl.pallas_call(
        _filter_kernel,
        grid=(n // tl,),
        in_specs=[pl.BlockSpec((LANES, tl), lambda i: (0, i)), pl.BlockSpec((tl, LANES), row),
                  _resident((HY_HID, LANES)), unit, _resident((HY_HID, HY_HID)), unit,
                  _resident((HY_HID, HY_HID)), unit, _resident((HY_HID, 3)),
                  _resident((HY_HID, 2 * D)), _resident((1, D))],
        out_specs=[pl.BlockSpec((tl, D), row)] * 2,
        out_shape=[jax.ShapeDtypeStruct((n, D), F32)] * 2,
        compiler_params=_cparams(("arbitrary",)),
        name="hyena_filter",
    )(np.ascontiguousarray(zp.T), zp, fw0t, fb0[:, None], fw1.T, fb1[:, None], fw2.T, fb2[:, None],
      freq.T, fwout, deltas)


def _dft_tables(n):
    q = 1 << (int(math.log2(n)) // 2 + 1)
    m = np.arange(n, dtype=np.int64)[None, :]

    def thin(k):
        ang = ((k[:, None] * m) % (2 * n)) * (math.pi / n)
        return jnp.asarray(np.cos(ang), F32), jnp.asarray(np.sin(ang), F32)

    ca, sa = (v[:, None, :] for v in thin(q * np.arange(n // q, dtype=np.int64)))
    cb, sb = (v[None, :, :] for v in thin(np.arange(q, dtype=np.int64)))
    return (ca * cb - sa * sb).reshape(n, n), (sa * cb + ca * sb).reshape(n, n)


def _alt_sign(rows, cols):
    r = lax.broadcasted_iota(jnp.int32, (rows, cols), 0)
    return (1 - 2 * (r & 1)).astype(F32)


_N_GROUPS = 4
_ROOT_HALF = math.sqrt(0.5)


def _radix4_tables(g):
    c, s = _dft_tables(g)
    k = np.arange(g)[:, None] * (math.pi / (4 * g))
    fwd, inv = [], []
    for r in range(4):
        wc, ws = jnp.asarray(np.cos(r * k), F32), jnp.asarray(np.sin(r * k), F32)
        blocks = (c * wc - s * ws, s * wc + c * ws)
        fwd += blocks
        inv += [b.T for b in blocks]
    return (jnp.concatenate(fwd, axis=0).astype(BF16), jnp.concatenate(inv, axis=0).astype(BF16))


def _cmul(ar, as_, br, bi):
    return ar * br + as_ * bi, as_ * br - ar * bi


def _table_rows(tab_ref, r, rows, g):
    count = rows.stop - rows.start
    return (tab_ref[pl.ds(2 * r * g + rows.start, count), :],
            tab_ref[pl.ds((2 * r + 1) * g + rows.start, count), :])


def _fwd4(quarters, tab_ref, rows, g):
    ts = []
    for r, q in enumerate(quarters):
        c, s = _table_rows(tab_ref, r, rows, g)
        ts.append((_dot(c, q), _dot(s, q)))
    (t0r, t0s), (t1r, t1s), (t2r, t2s), (t3r, t3s) = ts
    er, es, fr, fs = t0r + t2r, t0s + t2s, t0r - t2r, t0s - t2s
    pr, ps, dr, ds = t1r + t3r, t1s + t3s, t1r - t3r, t1s - t3s
    return ((er + pr, es + ps), (er - pr, es - ps), (fr + ds, fs - dr), (fr - ds, fs + dr))


def _mid_freqs(sums):
    s0, s1, s2, s3 = sums
    a, b = _ROOT_HALF * (s1 - s3), _ROOT_HALF * (s1 + s3)
    return (s0 + a, s2 + b), (s0 - a, b - s2)


def _alt_sums(quarters):
    alt = _alt_sign(*quarters[0].shape)
    return [jnp.sum(q * alt, axis=0, keepdims=True) for q in quarters]


def _spectrum_kernel(hf_ref, hb_ref, tab_ref, kr_ref, ki_ref, kn_ref, *, g):
    hf = hf_ref[...]
    row = lax.broadcasted_iota(jnp.int32, hf.shape, 0)
    hb = jnp.where(row == 0, 0.0, hb_ref[...])
    cos_part = hf + hb
    sin_part = hb - hf
    cq = [cos_part[r * g:(r + 1) * g] for r in range(4)]
    sq = [sin_part[r * g:(r + 1) * g] for r in range(4)]
    rows = slice(0, g)
    groups_c = _fwd4([q.astype(BF16) for q in cq], tab_ref, rows, g)
    groups_s = _fwd4([q.astype(BF16) for q in sq], tab_ref, rows, g)
    n_fft = 8 * g
    k = lax.broadcasted_iota(jnp.int32, (g, hf.shape[1]), 0)
    scale = jnp.where(k == 0, 1.0 / n_fft, 2.0 / n_fft)
    for grp in range(_N_GROUPS):
        kr_ref[grp * g:(grp + 1) * g, :] = groups_c[grp][0] * scale
        ki_ref[grp * g:(grp + 1) * g, :] = groups_s[grp][1] * scale
    (cg, _), (c3g, _) = _mid_freqs(_alt_sums(cq))
    (_, sg), (_, s3g) = _mid_freqs(_alt_sums(sq))
    mids = [v * (2.0 / n_fft) for v in (cg, sg, c3g, s3g)]
    kn_ref[...] = jnp.concatenate(mids + [jnp.zeros((B - 4, hf.shape[1]), F32)], axis=0)


def _spectrum(hf, hb, fwd_table):
    n = hf.shape[0]
    g = n // 4
    col = lambda j: (0, j)
    return pl.pallas_call(
        functools.partial(_spectrum_kernel, g=g),
        grid=(D // TC,),
        in_specs=[pl.BlockSpec((n, TC), col)] * 2 + [_resident((8 * g, g))],
        out_specs=[pl.BlockSpec((n, TC), col)] * 2 + [pl.BlockSpec((B, TC), col)],
        out_shape=[jax.ShapeDtypeStruct((n, D), F32)] * 2 + [jax.ShapeDtypeStruct((B, D), F32)],
        compiler_params=_cparams(("arbitrary",)),
        name="hyena_spectrum",
    )(hf, hb, fwd_table)


def _steps(refs, first, count):
    rows = pl.ds(first, count, stride=4)
    return jnp.concatenate([r[rows, :] for r in refs], axis=1)


def _long_conv_rows(t0, g, p_refs, x0_refs, spec, bias_ref, o_refs, scratch):
    kr_ref, ki_ref, kn_ref, fwd_ref, inv_ref = spec
    q_scr, u_scr = scratch
    tk = min(g, 512)
    r0 = t0 // 4
    seq = slice(r0, r0 + g)
    chunks = [slice(k * tk, (k + 1) * tk) for k in range(g // tk)]
    quarters = [_steps(p_refs, t0 + r, g) for r in range(4)]
    tc = quarters[0].shape[1]
    for r in range(4):
        q_scr[r, seq, :] = quarters[r].astype(BF16)
    for rows in chunks:
        dst = slice(r0 + rows.start, r0 + rows.stop)
        groups = _fwd4([q_scr[r, seq, :] for r in range(4)], fwd_ref, rows, g)
        ys = []
        for grp, (xr, xs) in enumerate(groups):
            k_rows = pl.ds(grp * g + rows.start, tk)
            ys.append(_cmul(xr, xs, kr_ref[k_rows, :], ki_ref[k_rows, :]))
        (y1r, y1s), (y2r, y2s), (y3r, y3s), (y4r, y4s) = ys
        pr, ps, mr, ms = y1r + y2r, y1s + y2s, y1r - y2r, y1s - y2s
        qr, qs, nr, ns = y3r + y4r, y3s + y4s, y3r - y4r, y3s - y4s
        us = [(pr + qr, ps + qs), (mr - ns, ms + nr), (pr - qr, ps - qs), (mr + ns, ms - nr)]
        for r, (ur, us_) in enumerate(us):
            u_scr[2 * r, dst, :] = ur.astype(BF16)
            u_scr[2 * r + 1, dst, :] = us_.astype(BF16)
    (xgr, xgs), (x3r, x3s) = _mid_freqs(_alt_sums(quarters))
    ygr, ygs = _cmul(xgr, xgs, kn_ref[0:1, :], kn_ref[1:2, :])
    y3r_, y3s_ = _cmul(x3r, x3s, kn_ref[2:3, :], kn_ref[3:4, :])
    a = _ROOT_HALF
    mids = [ygr + y3r_, a * (ygr + ygs - y3r_ + y3s_), ygs - y3s_, a * (ygs - ygr + y3r_ + y3s_)]
    alt_chunk = _alt_sign(tk, tc)
    bias = bias_ref[...]
    for rows in chunks:
        for r in range(4):
            first = t0 + 4 * rows.start + r
            c, s = _table_rows(inv_ref, r, rows, g)
            y = (_dot(c, u_scr[2 * r, seq, :]) + _dot(s, u_scr[2 * r + 1, seq, :])
                 + alt_chunk * mids[r])
            out = _steps(x0_refs, first, tk) * (y + _steps(p_refs, first, tk) * bias)
            for q, o_ref in enumerate(o_refs):
                o_ref[pl.ds(first, tk, stride=4), :] = out[:, q * LANES:(q + 1) * LANES]


_N_SPEC = 5


def _long_conv_kernel(*refs):
    p_refs, x0_refs, bias_ref = refs[0:2], refs[2:4], refs[4]
    lat, ctx = refs[5:5 + _N_SPEC], refs[5 + _N_SPEC:5 + 2 * _N_SPEC]
    o_refs, scratch = refs[5 + 2 * _N_SPEC:7 + 2 * _N_SPEC], refs[7 + 2 * _N_SPEC:]
    _long_conv_rows(0, L // 4, p_refs, x0_refs, lat, bias_ref, o_refs, scratch)
    _long_conv_rows(L, CTX // 4, p_refs, x0_refs, ctx, bias_ref, o_refs, scratch)


def _long_conv(p2, x02, bias, spec_lat, spec_ctx):
    nc = D // TC
    halves = [pl.BlockSpec((ROWS // B, LANES),
                           functools.partial(lambda q, j: (0, 2 * ((j % B) * nc + j // B) + q), q))
              for q in range(TC // LANES)]
    ch = lambda j: (0, j // B)

    def spec_specs(g):
        return ([pl.BlockSpec((4 * g, TC), ch)] * 2 + [pl.BlockSpec((B, TC), ch)]
                + [_resident((8 * g, g))] * 2)

    half_out = pl.BlockSpec((ROWS // B, LANES), lambda j: (0, (j % B) * nc + j // B))
    quarter_rows = ROWS // (4 * B)
    return pl.pallas_call(
        _long_conv_kernel,
        grid=(B * D // TC,),
        scratch_shapes=[pltpu.VMEM((4, quarter_rows, TC), BF16),
                        pltpu.VMEM((8, quarter_rows, TC), BF16)],
        in_specs=halves + halves + [pl.BlockSpec((1, TC), ch)] + spec_specs(L // 4)
        + spec_specs(CTX // 4),
        out_specs=[half_out] * (TC // LANES),
        out_shape=[jax.ShapeDtypeStruct((ROWS // B, B * D * LANES // TC), F32)] * (TC // LANES),
        compiler_params=_cparams(("arbitrary",)),
        name="hyena_long_conv",
    )(p2, p2, x02, x02, bias, *spec_lat, *spec_ctx)


def _gelu_tanh(x):
    return x * (0.5 * (1.0 + jnp.tanh(math.sqrt(2.0 / math.pi) * (x + 0.044715 * (x * x * x)))))


_TINY = 1e-30


def _rg_coeffs(xc, hd, wai_ref, bai_ref, lam_ref, a_scr, b_scr):
    sl = slice(hd * RG_BLOCK, (hd + 1) * RG_BLOCK)
    lam = lam_ref[:, sl]
    softplus_neg = jnp.maximum(-lam, 0.0) + jnp.log1p(jnp.exp(-jnp.abs(lam)))
    rate = (-RG_C * math.log2(math.e)) * softplus_neg
    pre = _dot(xc.astype(BF16), wai_ref[hd]) + bai_ref[:, 2 * hd * RG_BLOCK:2 * (hd + 1) * RG_BLOCK]
    gates = _sigmoid(pre)
    a = jnp.exp2(gates[:, 0:RG_BLOCK] * rate)
    a_scr[:, sl] = a
    v = (1.0 - a) * (1.0 + a)
    root = v * lax.rsqrt(jnp.maximum(v, _TINY))
    b_scr[:, sl] = root * gates[:, RG_BLOCK:2 * RG_BLOCK] * xc


def _scan_tile(a_scr, b_scr, h_scr, emit, reverse):
    steps = a_scr.shape[0] // B

    def body(k, h):
        t = steps - 1 - k if reverse else k
        r0 = pl.multiple_of(t * B, B)
        h = a_scr[pl.ds(r0, B), :] * h + b_scr[pl.ds(r0, B), :]
        emit(r0, h)
        return h

    h_scr[...] = lax.fori_loop(0, steps, body, h_scr[...], unroll=8)


def _rg_fwd_order(s):
    return jnp.where(s < NI_CTX, NI_LAT + s, s - NI_CTX)


def _rg_in_kernel(xp_ref, x_ref, xn_ref, mod_ref, g_ref, b_ref, cw_ref, cb_ref,
                  wai_ref, bai_ref, lam_ref, w_ref, xc_ref, gate_ref, hs_ref, a_scr, b_scr, h_scr):
    s = pl.program_id(0)
    t = _rg_fwd_order(s)
    xa = jnp.concatenate([xp_ref[...], x_ref[...], xn_ref[...]], axis=0)
    h = _modulate(_rmsnorm(xa, g_ref[2:3]), _mod(mod_ref, 3), _mod(mod_ref, 4)).astype(BF16)
    keep_lo = jnp.where(_seq_first(t), 0.0, 1.0).astype(F32)
    keep_hi = jnp.where(_seq_last(t), 0.0, 1.0).astype(F32)
    for hd in range(RG_HEADS):
        sl = slice(hd * RG_BLOCK, (hd + 1) * RG_BLOCK)
        rec = slice(D + hd * RG_BLOCK, D + (hd + 1) * RG_BLOCK)
        gate_ref[:, sl] = (_dot(h[B:TI + B], w_ref[:, sl]) + b_ref[:, sl]).astype(BF16)
        u = _dot(h, w_ref[:, rec]) + b_ref[:, rec]
        taps = (jnp.concatenate([u[0:B] * keep_lo, u[B:TI]], axis=0),
                u[B:TI + B],
                jnp.concatenate([u[2 * B:TI + B], u[TI + B:TI + 2 * B] * keep_hi], axis=0),
                jnp.concatenate([u[3 * B:TI + B], u[TI + B:TI + 3 * B] * keep_hi], axis=0))
        xc = cb_ref[:, sl]
        for k, tap in enumerate(taps):
            xc = xc + cw_ref[k:k + 1, sl] * tap
        xc_ref[:, sl] = xc
        _rg_coeffs(xc, hd, wai_ref, bai_ref, lam_ref, a_scr, b_scr)

    @pl.when(s == 0)
    def _():
        h_scr[...] = jnp.zeros((B, D), F32)

    def emit(r0, hv):
        hs_ref[pl.ds(r0, B), :] = hv

    _scan_tile(a_scr, b_scr, h_scr, emit, reverse=False)


def _rg_gate_specs():
    return [_resident((RG_HEADS, RG_BLOCK, 2 * RG_BLOCK)), _resident((1, 2 * D)), _resident((1, D))]


def _rg_in(x, mods, g, w_in, b_in, conv_w, conv_b, wai, bai, lam):
    order = _rg_fwd_order
    mod_spec = pl.BlockSpec((None, B, N_MOD * D),
                            lambda s: (jnp.where(order(s) < NI_LAT, 0, 1), 0, 0))
    out_spec = pl.BlockSpec((TI, D), lambda s: (order(s), 0))
    return pl.pallas_call(
        _with_staged_weights(_rg_in_kernel, 12, [_first]),
        grid=(NI,),
        in_specs=_halo_specs(order, B, 2 * B) + [
            mod_spec, _resident((6, D)), _resident((1, 2 * D)),
            _resident((4, D)), _resident((1, D))] + _rg_gate_specs() + [_HBM],
        out_specs=[out_spec] * 3,
        out_shape=[jax.ShapeDtypeStruct((ROWS, D), F32), jax.ShapeDtypeStruct((ROWS, D), BF16),
                   jax.ShapeDtypeStruct((ROWS, D), F32)],
        scratch_shapes=[pltpu.VMEM((TI, D), F32), pltpu.VMEM((TI, D), F32),
                        pltpu.VMEM((B, D), F32), pltpu.VMEM((D, 2 * D), BF16)],
        compiler_params=_cparams(("arbitrary",)),
        name="rglru_in_fwd_scan",
    )(x, x, x, mods, g, b_in, conv_w, conv_b, wai, bai, lam, w_in)


def _rg_tile_coeffs(xc_ref, wai_ref, bai_ref, lam_ref, a_scr, b_scr):
    for hd in range(RG_HEADS):
        xc = xc_ref[:, hd * RG_BLOCK:(hd + 1) * RG_BLOCK]
        _rg_coeffs(xc, hd, wai_ref, bai_ref, lam_ref, a_scr, b_scr)


def _rg_ctx_bwd_kernel(xc_ref, wai_ref, bai_ref, lam_ref, h_ref, a_scr, b_scr, h_scr):
    _rg_tile_coeffs(xc_ref, wai_ref, bai_ref, lam_ref, a_scr, b_scr)

    @pl.when(pl.program_id(0) == 0)
    def _():
        h_scr[...] = jnp.zeros((B, D), F32)

    _scan_tile(a_scr, b_scr, h_scr, lambda r0, hv: None, reverse=True)
    h_ref[...] = h_scr[...]


def _rg_ctx_bwd(xc, wai, bai, lam):
    return pl.pallas_call(
        _rg_ctx_bwd_kernel,
        grid=(NT_CTX,),
        in_specs=[pl.BlockSpec((TM, D), lambda s: (NT - 1 - s, 0))] + _rg_gate_specs(),
        out_specs=pl.BlockSpec((B, D), lambda s: (0, 0)),
        out_shape=jax.ShapeDtypeStruct((B, D), F32),
        scratch_shapes=[pltpu.VMEM((TM, D), F32), pltpu.VMEM((TM, D), F32),
                        pltpu.VMEM((B, D), F32)],
        compiler_params=_cparams(("arbitrary",)),
        name="rglru_ctx_bwd_scan",
    )(xc, wai, bai, lam)


def _rg_out_ffn_kernel(xc_ref, gate_ref, hs_ref, x_ref, h0_ref, mod_ref, g_ref,
                       wai_ref, bai_ref, lam_ref, bo_ref, wo_ref, w1, w3, w2,
                       o_ref, a_scr, b_scr, hb_scr, h_scr, rows_scr):
    s = pl.program_id(0)

    def head_gates(hd):
        xc = xc_ref[:, hd * RG_BLOCK:(hd + 1) * RG_BLOCK]
        _rg_coeffs(xc, hd, wai_ref, bai_ref, lam_ref, a_scr, b_scr)

    @pl.when(s == 0)
    def _():
        h_scr[...] = h0_ref[...]
        for hd in range(RG_HEADS):
            head_gates(hd)

    @pl.when(s > 0)
    def _():
        y = (hs_ref[...] + hb_scr[...]) * _gelu_tanh(gate_ref[...].astype(F32))
        res = _mix_ffn_core(
            y, x_ref[...], mod_ref, g_ref, wo_ref, bo_ref, w1, w3, w2, chunks=F_QUARTERS,
            side_work=[functools.partial(head_gates, hd) for hd in range(RG_HEADS)])
        _put_rows(rows_scr, res)
        for b in range(B):
            o_ref[b] = _get_batch(rows_scr, b)

    @pl.when(s < NT_LAT)
    def _():
        def emit(r0, hv):
            hb_scr[pl.ds(r0, B), :] = hv

        _scan_tile(a_scr, b_scr, h_scr, emit, reverse=True)


def _rg_out_ffn(xc, gate, hs, x, h0, mods, g, wai, bai, lam, w_out, b_out, w1, w3, w2):
    scan_tile = lambda s: (jnp.maximum(NT_LAT - 1 - s, 0), 0)
    out_tile = lambda s: jnp.minimum(NT_LAT - s, NT_LAT - 1)
    prev = pl.BlockSpec((TM, D), lambda s: (out_tile(s), 0))
    return pl.pallas_call(
        _with_staged_weights(_rg_out_ffn_kernel, 15, [_first] + _ffn_views(1, 1)),
        grid=(NT_LAT + 1,),
        in_specs=[pl.BlockSpec((TM, D), scan_tile), prev, prev, prev, _resident((B, D)),
                  pl.BlockSpec((None, B, N_MOD * D), lambda s: (0, 0, 0)), _resident((6, D))]
        + _rg_gate_specs() + [_resident((1, D)), _HBM] + _WEIGHT_SPECS,
        out_specs=pl.BlockSpec((B, TS, D), lambda s: (0, out_tile(s), 0)),
        out_shape=jax.ShapeDtypeStruct((B, L, D), F32),
        scratch_shapes=[pltpu.VMEM((TM, D), F32), pltpu.VMEM((TM, D), F32),
                        pltpu.VMEM((TM, D), F32), pltpu.VMEM((B, D), F32), _ROWS_SCRATCH,
                        pltpu.VMEM((D, D), BF16)] + _WEIGHT_SCRATCH,
        compiler_params=_cparams(("arbitrary",)),
        name="rglru_out_ffn",
    )(xc, gate, hs, x, h0, mods, g, wai, bai, lam, b_out, w_out, w1, w3, w2)


def _grid_pos():
    rows = L // GRID_W
    quarter = D // 4
    omega = POS_BASE ** (-np.arange(quarter) / quarter)

    def emb(q):
        ang = q[:, None] * omega[None]
        return jnp.asarray(np.concatenate([np.sin(ang), np.cos(ang)], axis=-1), F32)

    row_code = jnp.repeat(emb(np.arange(rows)), GRID_W, axis=0)
    col_code = jnp.tile(emb(np.arange(GRID_W)), (rows, 1))
    return jnp.concatenate([row_code, col_code], axis=-1)


def kernel(x, c, ctx, c_ctx, ada_w, ada_b, norm_g, ffn_w1, ffn_w3, ffn_w2, hy_w_in, hy_b_in, hy_conv_w, hy_conv_b, hy_fw0, hy_fb0, hy_fw1, hy_fb1, hy_fw2, hy_fb2, hy_freq, hy_fwout, hy_filt_bias, hy_w_out, hy_b_out, rg_w_in, rg_b_in, rg_conv_w, rg_conv_b, rg_wa, rg_ba, rg_wi, rg_bi, rg_lam, rg_w_out, rg_b_out):
    cc = jnp.concatenate([c, jnp.broadcast_to(c_ctx[None], (B, D))], axis=0)
    mods = _mods(cc, ada_w, ada_b).reshape(DEPTH, 2, B, N_MOD * D)
    w1, w3, w2 = ffn_w1, ffn_w3, ffn_w2

    g = norm_g[0]
    xs = _ffn_first(x, ctx, _grid_pos(), mods[0], g, w1, w3, w2)
    x0, p = _hy_in(xs, mods[0], g, hy_w_in, hy_b_in[0][None],
                   hy_conv_w[0], hy_conv_b[0][None])
    fparams = (hy_fw0[0], hy_fb0[0], hy_fw1[0], hy_fb1[0], hy_fw2[0], hy_fb2[0],
               hy_freq[0], hy_fwout[0])
    specs = []
    for n in (L, CTX):
        hf, hb = _filters(n, *fparams)
        fwd_table, inv_table = _radix4_tables(n // 4)
        specs.append(list(_spectrum(hf, hb, fwd_table)) + [fwd_table, inv_table])
    y_lo, y_hi = _long_conv(p, x0, hy_filt_bias[0][None], *specs)
    xs = _hy_out_ffn(y_lo, y_hi, xs, mods[0], g, hy_w_out, hy_b_out[0][None],
                     w1, w3, w2)

    g = norm_g[1]
    xs = _ffn_plain(xs, mods[1], g, w1, w3, w2, layer=1)
    wai = jnp.concatenate([rg_wa[0], rg_wi[0]], axis=-1).astype(BF16)
    per_head = lambda v: v.reshape(2, RG_HEADS, RG_BLOCK)
    bai = jnp.concatenate([per_head(rg_ba[0]), per_head(rg_bi[0])], axis=-1).reshape(2, 1, 2 * D)
    lam = rg_lam[0][:, None, :]
    xc, gate, hs = _rg_in(xs, mods[1], g, rg_w_in, rg_b_in[0][None],
                          rg_conv_w[0], rg_conv_b[0][None], wai[0], bai[0], lam[0])
    h_ctx = _rg_ctx_bwd(xc, wai[1], bai[1], lam[1])
    return _rg_out_ffn(xc, gate, hs, xs, h_ctx, mods[1], g, wai[1], bai[1], lam[1],
                       rg_w_out, rg_b_out[0][None], w1, w3, w2)
```

```python
import functools
import math

import jax
import jax.numpy as jnp
import numpy as np
from jax import lax
from jax.experimental import pallas as pl
from jax.experimental.pallas import tpu as pltpu

F32 = jnp.float32
BF16 = jnp.bfloat16

D = 1024
B = 8
LANES = 128
MXU_N = 256
L = 2048
CTX = 256
DEPTH = 2
GRID_W = 64
D_FF = 2816
N_MOD = 9
MACARON = 0.5
NORM_EPS = 1e-6
POS_BASE = 10000.0
HY_EMB = 33
HY_BANDS = 16
HY_HID = 64
HY_FAST_DECAY = 0.3
HY_SLOW_DECAY = 1.5
HY_DECAY_TARGET = 1e-2
RG_HEADS = 4
RG_BLOCK = D // RG_HEADS
RG_C = 8.0

ROWS_LAT = L * B
ROWS_CTX = CTX * B
ROWS = ROWS_LAT + ROWS_CTX
TM = 512
TS = TM // B
NT_LAT = ROWS_LAT // TM
NT_CTX = ROWS_CTX // TM
NT = NT_LAT + NT_CTX
TI = 1024
NI_LAT = ROWS_LAT // TI
NI_CTX = ROWS_CTX // TI
NI = NI_LAT + NI_CTX
F_CHUNKS = ((0, 1536), (1536, D_FF))
F_QUARTERS = ((0, 768), (768, 1536), (1536, 2304), (2304, D_FF))
TC = 256
VMEM_LIMIT = 58 * 1024 * 1024


def _cparams(sem):
    return pltpu.CompilerParams(dimension_semantics=sem, vmem_limit_bytes=VMEM_LIMIT)


def _resident(shape):
    nd = len(shape)
    return pl.BlockSpec(shape, lambda *_: (0,) * nd, pipeline_mode=pl.Buffered(1))


def _split(a):
    hi = a.astype(BF16)
    lo = (a - hi.astype(F32)).astype(BF16)
    return hi, lo


def _dot(a, b):
    return jnp.dot(a, b, preferred_element_type=F32)


def _dot3(a, b):
    ah, al = _split(a)
    bh, bl = _split(b)
    return _dot(ah, bh) + _dot(ah, bl) + _dot(al, bh)


def _rmsnorm(x, g):
    ms = jnp.mean(x * x, axis=-1, keepdims=True)
    return x * lax.rsqrt(ms + NORM_EPS) * g


def _mod(mod_ref, k):
    return mod_ref[:, k * D:(k + 1) * D]


def _per_batch(x, fn):
    rows = x.shape[0]
    return fn(x.reshape(rows // B, B, x.shape[1])).reshape(rows, x.shape[1])


def _modulate(xn, shift8, scale8):
    return _per_batch(xn, lambda v: v * (1.0 + scale8)[None] + shift8[None])


def _gated(z, gate8):
    return _per_batch(z, lambda v: v * gate8[None])


def _sigmoid(x):
    return jax.nn.sigmoid(x)


def _ffn(x, mod_ref, k0, g_pre, g_post, w1_ref, w3_ref, w2_ref, chunks=F_CHUNKS, side_work=()):
    h = _modulate(_rmsnorm(x, g_pre), _mod(mod_ref, k0), _mod(mod_ref, k0 + 1)).astype(BF16)
    y = None
    for idx, (c0, c1) in enumerate(chunks):
        a = _dot(h, w1_ref[:, c0:c1])
        b = _dot(h, w3_ref[:, c0:c1])
        if idx < len(side_work):
            side_work[idx]()
        act = (a * _sigmoid(a) * b).astype(BF16)
        part = _dot(act, w2_ref[c0:c1, :])
        y = part if y is None else y + part
    return x + MACARON * _gated(_rmsnorm(y, g_post), _mod(mod_ref, k0 + 2))


def _mods_kernel(cc_ref, w_ref, b_ref, o_ref):
    @pl.when(pl.program_id(1) == 0)
    def _():
        o_ref[...] = jnp.broadcast_to(b_ref[...], o_ref.shape)

    a = cc_ref[...]
    ah, al = _split(a * _sigmoid(a))
    wh, wl = _split(w_ref[...])
    r = _dot(jnp.concatenate([ah, al], axis=0), wh)
    o_ref[...] += r[0:2 * B] + r[2 * B:4 * B] + _dot(ah, wl)


def _mods(cc, ada_w, ada_b):
    tk = 128
    n = N_MOD * D
    return pl.pallas_call(
        _mods_kernel,
        grid=(DEPTH, D // tk),
        in_specs=[
            pl.BlockSpec((2 * B, tk), lambda i, k: (0, k)),
            pl.BlockSpec((None, tk, n), lambda i, k: (i, k, 0)),
            pl.BlockSpec((None, 1, n), lambda i, k: (i, 0, 0)),
        ],
        out_specs=pl.BlockSpec((None, 2 * B, n), lambda i, k: (i, 0, 0)),
        out_shape=jax.ShapeDtypeStruct((DEPTH, 2 * B, n), F32),
        compiler_params=_cparams(("arbitrary", "arbitrary")),
        name="ada_mods",
    )(cc, ada_w, ada_b.reshape(DEPTH, 1, n))


def _mod_spec(nt_lat=NT_LAT):
    return pl.BlockSpec((None, B, N_MOD * D), lambda t: (jnp.where(t < nt_lat, 0, 1), 0, 0))


def _row_spec(width=D):
    return pl.BlockSpec((TM, width), lambda t: (t, 0))


def _wide_spec():
    return pl.BlockSpec((TS, B * D), lambda t: (t, 0))


W_CHUNKS = 8
W_SLOTS = 3
_HBM = pl.BlockSpec(memory_space=pl.ANY)
_WEIGHT_SPECS = [_HBM] * 3
_WEIGHT_SCRATCH = [pltpu.VMEM((D, D_FF), BF16), pltpu.VMEM((D, D_FF), BF16),
                   pltpu.VMEM((D_FF, D), BF16)]


def _stage_weights(srcs, dsts):
    chunks = [(src, dst, r0, dst.shape[0] // W_CHUNKS) for src, dst in zip(srcs, dsts)
              for r0 in range(0, dst.shape[0], dst.shape[0] // W_CHUNKS)]
    shapes = sorted({(rows, dst.shape[1]) for _, dst, _, rows in chunks})

    def run(*scoped):
        stages, sems = dict(zip(shapes, scoped[:-1])), scoped[-1]

        def slot(k):
            _, dst, _, rows = chunks[k]
            return stages[(rows, dst.shape[1])].at[k % W_SLOTS]

        def copy(k):
            src, _, r0, rows = chunks[k]
            return pltpu.make_async_copy(src.at[pl.ds(r0, rows), :], slot(k), sems.at[k % W_SLOTS])

        for k in range(W_SLOTS - 1):
            copy(k).start()
        for k, (_, dst, r0, rows) in enumerate(chunks):
            if k + W_SLOTS - 1 < len(chunks):
                copy(k + W_SLOTS - 1).start()
            copy(k).wait()
            dst[pl.ds(r0, rows), :] = slot(k)[...].astype(BF16)

    pl.run_scoped(run, *[pltpu.VMEM((W_SLOTS,) + shape, F32) for shape in shapes],
                  pltpu.SemaphoreType.DMA((W_SLOTS,)))


def _with_staged_weights(body, n_in, views):
    n_w = len(views)

    def kernel(*refs):
        ins, rest = refs[:n_in], refs[n_in:]
        copies = rest[-n_w:]

        @pl.when(pl.program_id(0) == 0)
        def _():
            _stage_weights([view(w) for view, w in zip(views, ins[-n_w:])], copies)

        body(*ins[:-n_w], *copies, *rest[:-n_w])

    return kernel


def _ffn_views(layer, half):
    return [lambda w: w.at[layer, half]] * 3


def _first(w):
    return w.at[0]


def _rows_of_batch(b):
    return pl.ds(b, TS, stride=B)


_ROWS_SCRATCH = pltpu.VMEM((D // LANES, TM, LANES), F32)


def _put_batch(rows_scr, b, val):
    for j in range(D // LANES):
        rows_scr[j, _rows_of_batch(b), :] = val[:, j * LANES:(j + 1) * LANES]


def _get_batch(rows_scr, b):
    return jnp.concatenate([rows_scr[j, _rows_of_batch(b), :] for j in range(D // LANES)], axis=1)


def _put_rows(rows_scr, val):
    for j in range(D // LANES):
        rows_scr[j] = val[:, j * LANES:(j + 1) * LANES]


def _get_rows(rows_scr):
    return jnp.concatenate([rows_scr[j] for j in range(D // LANES)], axis=1)


def _ffn_first_kernel(x_ref, ctx_ref, pos_ref, mod_ref, g_ref, w1, w3, w2, o_ref, rows_scr):
    is_latent = pl.program_id(0) < NT_LAT
    for b in range(B):
        _put_batch(rows_scr, b, jnp.where(is_latent, x_ref[b] + pos_ref[...], ctx_ref[b]))
    o_ref[...] = _ffn(_get_rows(rows_scr), mod_ref, 0, g_ref[0:1], g_ref[1:2], w1, w3, w2)


def _ffn_kernel(x_ref, mod_ref, g_ref, w1, w3, w2, o_ref):
    for r in range(x_ref.shape[0] // TM):
        rows = slice(r * TM, (r + 1) * TM)
        o_ref[rows, :] = _ffn(x_ref[rows, :], mod_ref, 0, g_ref[0:1], g_ref[1:2], w1, w3, w2)


def _ffn_first(x, ctx, pos, mods, g, w1, w3, w2):
    lat = lambda t: jnp.minimum(t, NT_LAT - 1)
    return pl.pallas_call(
        _with_staged_weights(_ffn_first_kernel, 8, _ffn_views(0, 0)),
        grid=(NT,),
        in_specs=[pl.BlockSpec((B, TS, D), lambda t: (0, lat(t), 0)),
                  pl.BlockSpec((B, TS, D), lambda t: (0, jnp.maximum(t - NT_LAT, 0), 0)),
                  pl.BlockSpec((TS, D), lambda t: (lat(t), 0)),
                  _mod_spec(), _resident((6, D))] + _WEIGHT_SPECS,
        out_specs=_row_spec(),
        out_shape=jax.ShapeDtypeStruct((ROWS, D), F32),
        scratch_shapes=[_ROWS_SCRATCH] + _WEIGHT_SCRATCH,
        compiler_params=_cparams(("arbitrary",)),
        name="ffn_first",
    )(x, ctx, pos, mods, g, w1, w3, w2)


def _ffn_plain(x, mods, g, w1, w3, w2, layer):
    return pl.pallas_call(
        _with_staged_weights(_ffn_kernel, 6, _ffn_views(layer, 0)),
        grid=(NT // 2,),
        in_specs=[pl.BlockSpec((2 * TM, D), lambda t: (t, 0)),
                  pl.BlockSpec((None, B, N_MOD * D),
                               lambda t: (jnp.where(t < NT_LAT // 2, 0, 1), 0, 0)),
                  _resident((6, D))] + _WEIGHT_SPECS,
        out_specs=pl.BlockSpec((2 * TM, D), lambda t: (t, 0)),
        out_shape=jax.ShapeDtypeStruct((ROWS, D), F32),
        scratch_shapes=_WEIGHT_SCRATCH,
        compiler_params=_cparams(("arbitrary",)),
        name="ffn_pre",
    )(x, mods, g, w1, w3, w2)


def _mix_ffn_core(y, x, mod_ref, g_ref, wo_ref, bo_ref, w1, w3, w2, **ffn_kwargs):
    z = _dot(y.astype(BF16), wo_ref[...]) + bo_ref[...]
    x = x + _gated(_rmsnorm(z, g_ref[3:4]), _mod(mod_ref, 5))
    return _ffn(x, mod_ref, 6, g_ref[4:5], g_ref[5:6], w1, w3, w2, **ffn_kwargs)


def _hy_out_ffn_kernel(ylo_ref, yhi_ref, x_ref, mod_ref, g_ref, bo_ref, wo_ref, w1, w3, w2,
                       o_ref, rows_scr):
    halves = (ylo_ref, yhi_ref)
    for b in range(B):
        for j in range(D // LANES):
            col = (b * (D // TC) + j // 2) * LANES
            rows_scr[j, _rows_of_batch(b), :] = halves[j % 2][:, col:col + LANES]
    o_ref[...] = _mix_ffn_core(_get_rows(rows_scr), x_ref[...], mod_ref, g_ref, wo_ref, bo_ref,
                               w1, w3, w2)


def _hy_out_ffn(y_lo, y_hi, x, mods, g, w_out, b_out, w1, w3, w2):
    half = pl.BlockSpec((TS, B * D // 2), lambda t: (t, 0))
    return pl.pallas_call(
        _with_staged_weights(_hy_out_ffn_kernel, 10, [_first] + _ffn_views(0, 1)),
        grid=(NT,),
        in_specs=[half, half, _row_spec(), _mod_spec(), _resident((6, D)), _resident((1, D)),
                  _HBM] + _WEIGHT_SPECS,
        out_specs=_row_spec(),
        out_shape=jax.ShapeDtypeStruct((ROWS, D), F32),
        scratch_shapes=[_ROWS_SCRATCH, pltpu.VMEM((D, D), BF16)] + _WEIGHT_SCRATCH,
        compiler_params=_cparams(("arbitrary",)),
        name="hyena_out_ffn",
    )(y_lo, y_hi, x, mods, g, b_out, w_out, w1, w3, w2)


def _seq_first(t):
    return jnp.logical_or(t == 0, t == NI_LAT)


def _seq_last(t):
    return jnp.logical_or(t == NI_LAT - 1, t == NI - 1)


def _halo_specs(order, lo_rows, hi_rows):
    nlo = ROWS // lo_rows
    nhi = ROWS // hi_rows
    return [
        pl.BlockSpec((lo_rows, D), lambda s: (jnp.maximum(order(s) * (TI // lo_rows) - 1, 0), 0)),
        pl.BlockSpec((TI, D), lambda s: (order(s), 0)),
        pl.BlockSpec((hi_rows, D),
                     lambda s: (jnp.minimum((order(s) + 1) * (TI // hi_rows), nhi - 1), 0)),
    ]


def _hy_in_kernel(xp_ref, x_ref, xn_ref, mod_ref, g_ref, b_ref, cw_ref, cb_ref, w_ref,
                  x0_ref, p_ref, x0_scr, p_scr):
    t = pl.program_id(0)
    xa = jnp.concatenate([xp_ref[...], x_ref[...], xn_ref[...]], axis=0)
    h = _modulate(_rmsnorm(xa, g_ref[2:3]), _mod(mod_ref, 3), _mod(mod_ref, 4)).astype(BF16)
    keep_lo = jnp.where(_seq_first(t), 0.0, 1.0).astype(F32)
    keep_hi = jnp.where(_seq_last(t), 0.0, 1.0).astype(F32)

    def conv_cols(c0):
        cols = slice(c0, c0 + MXU_N)
        u = _dot(h, w_ref[:, cols]) + b_ref[:, cols]
        lo = jnp.concatenate([u[0:B] * keep_lo, u[B:TI]], axis=0)
        hi = jnp.concatenate([u[2 * B:TI + B], u[TI + B:TI + 2 * B] * keep_hi], axis=0)
        return (cb_ref[:, cols] + cw_ref[0:1, cols] * lo + cw_ref[1:2, cols] * u[B:TI + B]
                + cw_ref[2:3, cols] * hi)

    slabs = MXU_N // LANES
    for c in range(D // MXU_N):
        c0 = c * MXU_N
        vals = (conv_cols(c0), conv_cols(D + c0) * conv_cols(2 * D + c0))
        for out_ref, scr, val in zip((x0_ref, p_ref), (x0_scr, p_scr), vals):
            for j in range(slabs):
                scr[c * slabs + j] = val[:, j * LANES:(j + 1) * LANES]
            for b in range(B):
                out_ref[:, b * D + c0:b * D + c0 + MXU_N] = jnp.concatenate(
                    [scr[c * slabs + j, pl.ds(b, TI // B, stride=B), :] for j in range(slabs)], axis=1)


def _hy_in(x, mods, g, w_in, b_in, conv_w, conv_b):
    wide = pl.BlockSpec((TI // B, B * D), lambda t: (t, 0))
    rows_scratch = pltpu.VMEM((D // LANES, TI, LANES), F32)
    return pl.pallas_call(
        _with_staged_weights(_hy_in_kernel, 9, [_first]),
        grid=(NI,),
        in_specs=_halo_specs(lambda s: s, B, B) + [
            _mod_spec(NI_LAT), _resident((6, D)), _resident((1, 3 * D)),
            _resident((3, 3 * D)), _resident((1, 3 * D)), _HBM],
        out_specs=[wide, wide],
        out_shape=[jax.ShapeDtypeStruct((ROWS // B, B * D), F32)] * 2,
        scratch_shapes=[rows_scratch, rows_scratch, pltpu.VMEM((D, 3 * D), BF16)],
        compiler_params=_cparams(("arbitrary",)),
        name="hyena_in",
    )(x, x, x, mods, g, b_in, conv_w, conv_b, w_in)


def _filter_kernel(zt_ref, t_ref, fw0t, fb0, fw1t, fb1, fw2t, fb2, freq, fwout, deltas,
                   hf_ref, hb_ref):
    h = jnp.sin(freq[:, 0:1] * (_dot3(fw0t[...], zt_ref[...]) + fb0[...]))
    h = jnp.sin(freq[:, 1:2] * (_dot3(fw1t[...], h) + fb1[...]))
    h = jnp.sin(freq[:, 2:3] * (_dot3(fw2t[...], h) + fb2[...]))
    filt = _dot3(h.T, fwout[...])
    decay = jnp.exp(-t_ref[:, 0:1] * deltas[...])
    hf_ref[...] = filt[:, 0:D] * decay
    hb_ref[...] = filt[:, D:2 * D] * decay


def _filters(n, fw0, fb0, fw1, fb1, fw2, fb2, freq, fwout):
    t = np.linspace(0.0, 1.0, n)[:, None]
    bands = np.linspace(1e-4, HY_BANDS - 1, HY_BANDS)[None]
    phase = bands * (2.0 * math.pi * np.arange(n)[:, None] / n)
    zp = np.zeros((n, LANES), np.float32)
    zp[:, :HY_EMB] = np.concatenate([t, np.cos(phase), -np.sin(phase)], axis=-1)
    zp = np.concatenate([zp[r::4] for r in range(4)], axis=0)
    fw0t = jnp.zeros((HY_HID, LANES), F32).at[:, :HY_EMB].set(fw0.T)
    max_decay = math.log(HY_DECAY_TARGET) / HY_FAST_DECAY
    min_decay = math.log(HY_DECAY_TARGET) / HY_SLOW_DECAY
    deltas = np.abs(np.linspace(min_decay, max_decay, D))[None].astype(np.float32)
    tl = 256
    row = lambda i: (i, 0)
    unit = _resident((HY_HID, 1))
    return pl.pallas_call(
        _filter_kernel,
        grid=(n // tl,),
        in_specs=[pl.BlockSpec((LANES, tl), lambda i: (0, i)), pl.BlockSpec((tl, LANES), row),
                  _resident((HY_HID, LANES)), unit, _resident((HY_HID, HY_HID)), unit,
                  _resident((HY_HID, HY_HID)), unit, _resident((HY_HID, 3)),
                  _resident((HY_HID, 2 * D)), _resident((1, D))],
        out_specs=[pl.BlockSpec((tl, D), row)] * 2,
        out_shape=[jax.ShapeDtypeStruct((n, D), F32)] * 2,
        compiler_params=_cparams(("arbitrary",)),
        name="hyena_filter",
    )(np.ascontiguousarray(zp.T), zp, fw0t, fb0[:, None], fw1.T, fb1[:, None], fw2.T, fb2[:, None],
      freq.T, fwout, deltas)


def _alt_sign(rows, cols):
    r = lax.broadcasted_iota(jnp.int32, (rows, cols), 0)
    return (1 - 2 * (r & 1)).astype(F32)


_N_GROUPS = 4
_ROOT_HALF = math.sqrt(0.5)


def _radix4_tables(g):
    k = np.arange(g, dtype=np.int64)
    theta = ((k[:, None] * k[None, :]) % (2 * g)) * (math.pi / g)
    phase = k[:, None] * (math.pi / (4 * g))
    fwd = [f(theta + r * phase) for r in range(4) for f in (np.cos, np.sin)]
    as_bf16 = lambda blocks: jnp.asarray(np.concatenate(blocks, axis=0), F32).astype(BF16)
    return as_bf16(fwd), as_bf16([b.T for b in fwd])


def _cmul(ar, as_, br, bi):
    return ar * br + as_ * bi, as_ * br - ar * bi


def _table_rows(tab_ref, r, rows, g):
    count = rows.stop - rows.start
    return (tab_ref[pl.ds(2 * r * g + rows.start, count), :],
            tab_ref[pl.ds((2 * r + 1) * g + rows.start, count), :])


def _fwd4(quarters, tab_ref, rows, g):
    ts = []
    for r, q in enumerate(quarters):
        c, s = _table_rows(tab_ref, r, rows, g)
        ts.append((_dot(c, q), _dot(s, q)))
    (t0r, t0s), (t1r, t1s), (t2r, t2s), (t3r, t3s) = ts
    er, es, fr, fs = t0r + t2r, t0s + t2s, t0r - t2r, t0s - t2s
    pr, ps, dr, ds = t1r + t3r, t1s + t3s, t1r - t3r, t1s - t3s
    return ((er + pr, es + ps), (er - pr, es - ps), (fr + ds, fs - dr), (fr - ds, fs + dr))


def _mid_freqs(sums):
    s0, s1, s2, s3 = sums
    a, b = _ROOT_HALF * (s1 - s3), _ROOT_HALF * (s1 + s3)
    return (s0 + a, s2 + b), (s0 - a, b - s2)


def _alt_sums(quarters):
    alt = _alt_sign(*quarters[0].shape)
    return [jnp.sum(q * alt, axis=0, keepdims=True) for q in quarters]


def _spectrum_kernel(hf_ref, hb_ref, tab_ref, kr_ref, ki_ref, kn_ref, *, g):
    hf = hf_ref[...]
    row = lax.broadcasted_iota(jnp.int32, hf.shape, 0)
    hb = jnp.where(row == 0, 0.0, hb_ref[...])
    cos_part = hf + hb
    sin_part = hb - hf
    cq = [cos_part[r * g:(r + 1) * g] for r in range(4)]
    sq = [sin_part[r * g:(r + 1) * g] for r in range(4)]
    rows = slice(0, g)
    groups_c = _fwd4([q.astype(BF16) for q in cq], tab_ref, rows, g)
    groups_s = _fwd4([q.astype(BF16) for q in sq], tab_ref, rows, g)
    n_fft = 8 * g
    k = lax.broadcasted_iota(jnp.int32, (g, hf.shape[1]), 0)
    scale = jnp.where(k == 0, 1.0 / n_fft, 2.0 / n_fft)
    for grp in range(_N_GROUPS):
        kr_ref[grp * g:(grp + 1) * g, :] = groups_c[grp][0] * scale
        ki_ref[grp * g:(grp + 1) * g, :] = groups_s[grp][1] * scale
    (cg, _), (c3g, _) = _mid_freqs(_alt_sums(cq))
    (_, sg), (_, s3g) = _mid_freqs(_alt_sums(sq))
    mids = [v * (2.0 / n_fft) for v in (cg, sg, c3g, s3g)]
    kn_ref[...] = jnp.concatenate(mids + [jnp.zeros((B - 4, hf.shape[1]), F32)], axis=0)


def _spectrum(hf, hb, fwd_table):
    n = hf.shape[0]
    g = n // 4
    col = lambda j: (0, j)
    return pl.pallas_call(
        functools.partial(_spectrum_kernel, g=g),
        grid=(D // TC,),
        in_specs=[pl.BlockSpec((n, TC), col)] * 2 + [_resident((8 * g, g))],
        out_specs=[pl.BlockSpec((n, TC), col)] * 2 + [pl.BlockSpec((B, TC), col)],
        out_shape=[jax.ShapeDtypeStruct((n, D), F32)] * 2 + [jax.ShapeDtypeStruct((B, D), F32)],
        compiler_params=_cparams(("arbitrary",)),
        name="hyena_spectrum",
    )(hf, hb, fwd_table)


def _steps(refs, first, count):
    rows = pl.ds(first, count, stride=4)
    return jnp.concatenate([r[rows, :] for r in refs], axis=1)


def _long_conv_rows(t0, g, p_refs, x0_refs, spec, bias_ref, o_refs, scratch):
    kr_ref, ki_ref, kn_ref, fwd_ref, inv_ref = spec
    q_scr, u_scr = scratch
    tk = min(g, 512)
    r0 = t0 // 4
    seq = slice(r0, r0 + g)
    chunks = [slice(k * tk, (k + 1) * tk) for k in range(g // tk)]
    quarters = [_steps(p_refs, t0 + r, g) for r in range(4)]
    tc = quarters[0].shape[1]
    for r in range(4):
        q_scr[r, seq, :] = quarters[r].astype(BF16)
    for rows in chunks:
        dst = slice(r0 + rows.start, r0 + rows.stop)
        groups = _fwd4([q_scr[r, seq, :] for r in range(4)], fwd_ref, rows, g)
        ys = []
        for grp, (xr, xs) in enumerate(groups):
            k_rows = pl.ds(grp * g + rows.start, tk)
            ys.append(_cmul(xr, xs, kr_ref[k_rows, :], ki_ref[k_rows, :]))
        (y1r, y1s), (y2r, y2s), (y3r, y3s), (y4r, y4s) = ys
        pr, ps, mr, ms = y1r + y2r, y1s + y2s, y1r - y2r, y1s - y2s
        qr, qs, nr, ns = y3r + y4r, y3s + y4s, y3r - y4r, y3s - y4s
        us = [(pr + qr, ps + qs), (mr - ns, ms + nr), (pr - qr, ps - qs), (mr + ns, ms - nr)]
        for r, (ur, us_) in enumerate(us):
            u_scr[2 * r, dst, :] = ur.astype(BF16)
            u_scr[2 * r + 1, dst, :] = us_.astype(BF16)
    (xgr, xgs), (x3r, x3s) = _mid_freqs(_alt_sums(quarters))
    ygr, ygs = _cmul(xgr, xgs, kn_ref[0:1, :], kn_ref[1:2, :])
    y3r_, y3s_ = _cmul(x3r, x3s, kn_ref[2:3, :], kn_ref[3:4, :])
    a = _ROOT_HALF
    mids = [ygr + y3r_, a * (ygr + ygs - y3r_ + y3s_), ygs - y3s_, a * (ygs - ygr + y3r_ + y3s_)]
    alt_chunk = _alt_sign(tk, tc)
    bias = bias_ref[...]
    for rows in chunks:
        for r in range(4):
            first = t0 + 4 * rows.start + r
            c, s = _table_rows(inv_ref, r, rows, g)
            y = (_dot(c, u_scr[2 * r, seq, :]) + _dot(s, u_scr[2 * r + 1, seq, :])
                 + alt_chunk * mids[r])
            out = _steps(x0_refs, first, tk) * (y + _steps(p_refs, first, tk) * bias)
            for q, o_ref in enumerate(o_refs):
                o_ref[pl.ds(first, tk, stride=4), :] = out[:, q * LANES:(q + 1) * LANES]


_N_SPEC = 5


def _long_conv_kernel(*refs):
    p_refs, x0_refs, bias_ref = refs[0:2], refs[2:4], refs[4]
    lat, ctx = refs[5:5 + _N_SPEC], refs[5 + _N_SPEC:5 + 2 * _N_SPEC]
    o_refs, scratch = refs[5 + 2 * _N_SPEC:7 + 2 * _N_SPEC], refs[7 + 2 * _N_SPEC:]
    _long_conv_rows(0, L // 4, p_refs, x0_refs, lat, bias_ref, o_refs, scratch)
    _long_conv_rows(L, CTX // 4, p_refs, x0_refs, ctx, bias_ref, o_refs, scratch)


def _long_conv(p2, x02, bias, spec_lat, spec_ctx):
    nc = D // TC
    halves = [pl.BlockSpec((ROWS // B, LANES),
                           functools.partial(lambda q, j: (0, 2 * ((j % B) * nc + j // B) + q), q))
              for q in range(TC // LANES)]
    ch = lambda j: (0, j // B)

    def spec_specs(g):
        return ([pl.BlockSpec((4 * g, TC), ch)] * 2 + [pl.BlockSpec((B, TC), ch)]
                + [_resident((8 * g, g))] * 2)

    half_out = pl.BlockSpec((ROWS // B, LANES), lambda j: (0, (j % B) * nc + j // B))
    quarter_rows = ROWS // (4 * B)
    return pl.pallas_call(
        _long_conv_kernel,
        grid=(B * D // TC,),
        scratch_shapes=[pltpu.VMEM((4, quarter_rows, TC), BF16),
                        pltpu.VMEM((8, quarter_rows, TC), BF16)],
        in_specs=halves + halves + [pl.BlockSpec((1, TC), ch)] + spec_specs(L // 4)
        + spec_specs(CTX // 4),
        out_specs=[half_out] * (TC // LANES),
        out_shape=[jax.ShapeDtypeStruct((ROWS // B, B * D * LANES // TC), F32)] * (TC // LANES),
        compiler_params=_cparams(("arbitrary",)),
        name="hyena_long_conv",
    )(p2, p2, x02, x02, bias, *spec_lat, *spec_ctx)


def _gelu_tanh(x):
    return x * (0.5 * (1.0 + jnp.tanh(math.sqrt(2.0 / math.pi) * (x + 0.044715 * (x * x * x)))))


_TINY = 1e-30


def _rg_coeffs(xc, hd, wai_ref, bai_ref, lam_ref, a_scr, b_scr):
    sl = slice(hd * RG_BLOCK, (hd + 1) * RG_BLOCK)
    lam = lam_ref[:, sl]
    softplus_neg = jnp.maximum(-lam, 0.0) + jnp.log1p(jnp.exp(-jnp.abs(lam)))
    rate = (-RG_C * math.log2(math.e)) * softplus_neg
    pre = _dot(xc.astype(BF16), wai_ref[hd]) + bai_ref[:, 2 * hd * RG_BLOCK:2 * (hd + 1) * RG_BLOCK]
    gates = _sigmoid(pre)
    a = jnp.exp2(gates[:, 0:RG_BLOCK] * rate)
    a_scr[:, sl] = a
    v = (1.0 - a) * (1.0 + a)
    root = v * lax.rsqrt(jnp.maximum(v, _TINY))
    b_scr[:, sl] = root * gates[:, RG_BLOCK:2 * RG_BLOCK] * xc


def _scan_tile(a_scr, b_scr, h_scr, emit, reverse):
    steps = a_scr.shape[0] // B

    def body(k, h):
        t = steps - 1 - k if reverse else k
        r0 = pl.multiple_of(t * B, B)
        h = a_scr[pl.ds(r0, B), :] * h + b_scr[pl.ds(r0, B), :]
        emit(r0, h)
        return h

    h_scr[...] = lax.fori_loop(0, steps, body, h_scr[...], unroll=8)


def _rg_fwd_order(s):
    return jnp.where(s < NI_CTX, NI_LAT + s, s - NI_CTX)


def _rg_in_kernel(xp_ref, x_ref, xn_ref, mod_ref, g_ref, b_ref, cw_ref, cb_ref,
                  wai_ref, bai_ref, lam_ref, w_ref, xc_ref, gate_ref, hs_ref, a_scr, b_scr, h_scr):
    s = pl.program_id(0)
    t = _rg_fwd_order(s)
    xa = jnp.concatenate([xp_ref[...], x_ref[...], xn_ref[...]], axis=0)
    h = _modulate(_rmsnorm(xa, g_ref[2:3]), _mod(mod_ref, 3), _mod(mod_ref, 4)).astype(BF16)
    keep_lo = jnp.where(_seq_first(t), 0.0, 1.0).astype(F32)
    keep_hi = jnp.where(_seq_last(t), 0.0, 1.0).astype(F32)
    for hd in range(RG_HEADS):
        sl = slice(hd * RG_BLOCK, (hd + 1) * RG_BLOCK)
        rec = slice(D + hd * RG_BLOCK, D + (hd + 1) * RG_BLOCK)
        gate_ref[:, sl] = (_dot(h[B:TI + B], w_ref[:, sl]) + b_ref[:, sl]).astype(BF16)
        u = _dot(h, w_ref[:, rec]) + b_ref[:, rec]
        taps = (jnp.concatenate([u[0:B] * keep_lo, u[B:TI]], axis=0),
                u[B:TI + B],
                jnp.concatenate([u[2 * B:TI + B], u[TI + B:TI + 2 * B] * keep_hi], axis=0),
                jnp.concatenate([u[3 * B:TI + B], u[TI + B:TI + 3 * B] * keep_hi], axis=0))
        xc = cb_ref[:, sl]
        for k, tap in enumerate(taps):
            xc = xc + cw_ref[k:k + 1, sl] * tap
        xc_ref[:, sl] = xc
        _rg_coeffs(xc, hd, wai_ref, bai_ref, lam_ref, a_scr, b_scr)

    @pl.when(s == 0)
    def _():
        h_scr[...] = jnp.zeros((B, D), F32)

    def emit(r0, hv):
        hs_ref[pl.ds(r0, B), :] = hv

    _scan_tile(a_scr, b_scr, h_scr, emit, reverse=False)


def _rg_gate_specs():
    return [_resident((RG_HEADS, RG_BLOCK, 2 * RG_BLOCK)), _resident((1, 2 * D)), _resident((1, D))]


def _rg_in(x, mods, g, w_in, b_in, conv_w, conv_b, wai, bai, lam):
    order = _rg_fwd_order
    mod_spec = pl.BlockSpec((None, B, N_MOD * D),
                            lambda s: (jnp.where(order(s) < NI_LAT, 0, 1), 0, 0))
    out_spec = pl.BlockSpec((TI, D), lambda s: (order(s), 0))
    return pl.pallas_call(
        _with_staged_weights(_rg_in_kernel, 12, [_first]),
        grid=(NI,),
        in_specs=_halo_specs(order, B, 2 * B) + [
            mod_spec, _resident((6, D)), _resident((1, 2 * D)),
            _resident((4, D)), _resident((1, D))] + _rg_gate_specs() + [_HBM],
        out_specs=[out_spec] * 3,
        out_shape=[jax.ShapeDtypeStruct((ROWS, D), F32), jax.ShapeDtypeStruct((ROWS, D), BF16),
                   jax.ShapeDtypeStruct((ROWS, D), F32)],
        scratch_shapes=[pltpu.VMEM((TI, D), F32), pltpu.VMEM((TI, D), F32),
                        pltpu.VMEM((B, D), F32), pltpu.VMEM((D, 2 * D), BF16)],
        compiler_params=_cparams(("arbitrary",)),
        name="rglru_in_fwd_scan",
    )(x, x, x, mods, g, b_in, conv_w, conv_b, wai, bai, lam, w_in)


def _rg_tile_coeffs(xc_ref, wai_ref, bai_ref, lam_ref, a_scr, b_scr):
    for hd in range(RG_HEADS):
        xc = xc_ref[:, hd * RG_BLOCK:(hd + 1) * RG_BLOCK]
        _rg_coeffs(xc, hd, wai_ref, bai_ref, lam_ref, a_scr, b_scr)


def _rg_ctx_bwd_kernel(xc_ref, wai_ref, bai_ref, lam_ref, h_ref, a_scr, b_scr, h_scr):
    _rg_tile_coeffs(xc_ref, wai_ref, bai_ref, lam_ref, a_scr, b_scr)

    @pl.when(pl.program_id(0) == 0)
    def _():
        h_scr[...] = jnp.zeros((B, D), F32)

    _scan_tile(a_scr, b_scr, h_scr, lambda r0, hv: None, reverse=True)
    h_ref[...] = h_scr[...]


def _rg_ctx_bwd(xc, wai, bai, lam):
    return pl.pallas_call(
        _rg_ctx_bwd_kernel,
        grid=(NT_CTX,),
        in_specs=[pl.BlockSpec((TM, D), lambda s: (NT - 1 - s, 0))] + _rg_gate_specs(),
        out_specs=pl.BlockSpec((B, D), lambda s: (0, 0)),
        out_shape=jax.ShapeDtypeStruct((B, D), F32),
        scratch_shapes=[pltpu.VMEM((TM, D), F32), pltpu.VMEM((TM, D), F32),
                        pltpu.VMEM((B, D), F32)],
        compiler_params=_cparams(("arbitrary",)),
        name="rglru_ctx_bwd_scan",
    )(xc, wai, bai, lam)


def _rg_out_ffn_kernel(xc_ref, gate_ref, hs_ref, x_ref, h0_ref, mod_ref, g_ref,
                       wai_ref, bai_ref, lam_ref, bo_ref, wo_ref, w1, w3, w2,
                       o_ref, a_scr, b_scr, hb_scr, h_scr, rows_scr):
    s = pl.program_id(0)

    def head_gates(hd):
        xc = xc_ref[:, hd * RG_BLOCK:(hd + 1) * RG_BLOCK]
        _rg_coeffs(xc, hd, wai_ref, bai_ref, lam_ref, a_scr, b_scr)

    @pl.when(s == 0)
    def _():
        h_scr[...] = h0_ref[...]
        for hd in range(RG_HEADS):
            head_gates(hd)

    @pl.when(s > 0)
    def _():
        y = (hs_ref[...] + hb_scr[...]) * _gelu_tanh(gate_ref[...].astype(F32))
        res = _mix_ffn_core(
            y, x_ref[...], mod_ref, g_ref, wo_ref, bo_ref, w1, w3, w2, chunks=F_QUARTERS,
            side_work=[functools.partial(head_gates, hd) for hd in range(RG_HEADS)])
        _put_rows(rows_scr, res)
        for b in range(B):
            o_ref[b] = _get_batch(rows_scr, b)

    @pl.when(s < NT_LAT)
    def _():
        def emit(r0, hv):
            hb_scr[pl.ds(r0, B), :] = hv

        _scan_tile(a_scr, b_scr, h_scr, emit, reverse=True)


def _rg_out_ffn(xc, gate, hs, x, h0, mods, g, wai, bai, lam, w_out, b_out, w1, w3, w2):
    scan_tile = lambda s: (jnp.maximum(NT_LAT - 1 - s, 0), 0)
    out_tile = lambda s: jnp.minimum(NT_LAT - s, NT_LAT - 1)
    prev = pl.BlockSpec((TM, D), lambda s: (out_tile(s), 0))
    return pl.pallas_call(
        _with_staged_weights(_rg_out_ffn_kernel, 15, [_first] + _ffn_views(1, 1)),
        grid=(NT_LAT + 1,),
        in_specs=[pl.BlockSpec((TM, D), scan_tile), prev, prev, prev, _resident((B, D)),
                  pl.BlockSpec((None, B, N_MOD * D), lambda s: (0, 0, 0)), _resident((6, D))]
        + _rg_gate_specs() + [_resident((1, D)), _HBM] + _WEIGHT_SPECS,
        out_specs=pl.BlockSpec((B, TS, D), lambda s: (0, out_tile(s), 0)),
        out_shape=jax.ShapeDtypeStruct((B, L, D), F32),
        scratch_shapes=[pltpu.VMEM((TM, D), F32), pltpu.VMEM((TM, D), F32),
                        pltpu.VMEM((TM, D), F32), pltpu.VMEM((B, D), F32), _ROWS_SCRATCH,
                        pltpu.VMEM((D, D), BF16)] + _WEIGHT_SCRATCH,
        compiler_params=_cparams(("arbitrary",)),
        name="rglru_out_ffn",
    )(xc, gate, hs, x, h0, mods, g, wai, bai, lam, b_out, w_out, w1, w3, w2)


def _grid_pos():
    rows = L // GRID_W
    quarter = D // 4
    omega = POS_BASE ** (-np.arange(quarter) / quarter)

    def emb(q):
        ang = q[:, None] * omega[None]
        return jnp.asarray(np.concatenate([np.sin(ang), np.cos(ang)], axis=-1), F32)

    row_code = jnp.repeat(emb(np.arange(rows)), GRID_W, axis=0)
    col_code = jnp.tile(emb(np.arange(GRID_W)), (rows, 1))
    return jnp.concatenate([row_code, col_code], axis=-1)


def kernel(x, c, ctx, c_ctx, ada_w, ada_b, norm_g, ffn_w1, ffn_w3, ffn_w2, hy_w_in, hy_b_in, hy_conv_w, hy_conv_b, hy_fw0, hy_fb0, hy_fw1, hy_fb1, hy_fw2, hy_fb2, hy_freq, hy_fwout, hy_filt_bias, hy_w_out, hy_b_out, rg_w_in, rg_b_in, rg_conv_w, rg_conv_b, rg_wa, rg_ba, rg_wi, rg_bi, rg_lam, rg_w_out, rg_b_out):
    cc = jnp.concatenate([c, jnp.broadcast_to(c_ctx[None], (B, D))], axis=0)
    mods = _mods(cc, ada_w, ada_b).reshape(DEPTH, 2, B, N_MOD * D)
    w1, w3, w2 = ffn_w1, ffn_w3, ffn_w2

    g = norm_g[0]
    xs = _ffn_first(x, ctx, _grid_pos(), mods[0], g, w1, w3, w2)
    x0, p = _hy_in(xs, mods[0], g, hy_w_in, hy_b_in[0][None],
                   hy_conv_w[0], hy_conv_b[0][None])
    fparams = (hy_fw0[0], hy_fb0[0], hy_fw1[0], hy_fb1[0], hy_fw2[0], hy_fb2[0],
               hy_freq[0], hy_fwout[0])
    specs = []
    for n in (L, CTX):
        hf, hb = _filters(n, *fparams)
        fwd_table, inv_table = _radix4_tables(n // 4)
        specs.append(list(_spectrum(hf, hb, fwd_table)) + [fwd_table, inv_table])
    y_lo, y_hi = _long_conv(p, x0, hy_filt_bias[0][None], *specs)
    xs = _hy_out_ffn(y_lo, y_hi, xs, mods[0], g, hy_w_out, hy_b_out[0][None],
                     w1, w3, w2)

    g = norm_g[1]
    xs = _ffn_plain(xs, mods[1], g, w1, w3, w2, layer=1)
    wai = jnp.concatenate([rg_wa[0], rg_wi[0]], axis=-1).astype(BF16)
    per_head = lambda v: v.reshape(2, RG_HEADS, RG_BLOCK)
    bai = jnp.concatenate([per_head(rg_ba[0]), per_head(rg_bi[0])], axis=-1).reshape(2, 1, 2 * D)
    lam = rg_lam[0][:, None, :]
    xc, gate, hs = _rg_in(xs, mods[1], g, rg_w_in, rg_b_in[0][None],
                          rg_conv_w[0], rg_conv_b[0][None], wai[0], bai[0], lam[0])
    h_ctx = _rg_ctx_bwd(xc, wai[1], bai[1], lam[1])
    return _rg_out_ffn(xc, gate, hs, xs, h_ctx, mods[1], g, wai[1], bai[1], lam[1],
                       rg_w_out, rg_b_out[0][None], w1, w3, w2)
```

```python
import functools
import math

import jax
import jax.numpy as jnp
import numpy as np
from jax import lax
from jax.experimental import pallas as pl
from jax.experimental.pallas import tpu as pltpu

F32 = jnp.float32
BF16 = jnp.bfloat16

D = 1024
B = 8
LANES = 128
MXU_N = 256
L = 2048
CTX = 256
DEPTH = 2
GRID_W = 64
D_FF = 2816
N_MOD = 9
MACARON = 0.5
NORM_EPS = 1e-6
POS_BASE = 10000.0
HY_EMB = 33
HY_BANDS = 16
HY_HID = 64
HY_FAST_DECAY = 0.3
HY_SLOW_DECAY = 1.5
HY_DECAY_TARGET = 1e-2
RG_HEADS = 4
RG_BLOCK = D // RG_HEADS
RG_C = 8.0

ROWS_LAT = L * B
ROWS_CTX = CTX * B
ROWS = ROWS_LAT + ROWS_CTX
TM = 512
TS = TM // B
NT_LAT = ROWS_LAT // TM
NT_CTX = ROWS_CTX // TM
NT = NT_LAT + NT_CTX
TI = 1024
NI_LAT = ROWS_LAT // TI
NI_CTX = ROWS_CTX // TI
NI = NI_LAT + NI_CTX
F_CHUNKS = ((0, 1536), (1536, D_FF))
F_QUARTERS = ((0, 768), (768, 1536), (1536, 2304), (2304, D_FF))
TC = 256
VMEM_LIMIT = 58 * 1024 * 1024


def _cparams(sem):
    return pltpu.CompilerParams(dimension_semantics=sem, vmem_limit_bytes=VMEM_LIMIT)


def _resident(shape):
    nd = len(shape)
    return pl.BlockSpec(shape, lambda *_: (0,) * nd, pipeline_mode=pl.Buffered(1))


def _split(a):
    hi = a.astype(BF16)
    lo = (a - hi.astype(F32)).astype(BF16)
    return hi, lo


def _dot(a, b):
    return jnp.dot(a, b, preferred_element_type=F32)


def _dot3(a, b):
    ah, al = _split(a)
    bh, bl = _split(b)
    return _dot(ah, bh) + _dot(ah, bl) + _dot(al, bh)


def _rmsnorm(x, g):
    ms = jnp.mean(x * x, axis=-1, keepdims=True)
    return x * lax.rsqrt(ms + NORM_EPS) * g


def _mod(mod_ref, k):
    return mod_ref[:, k * D:(k + 1) * D]


def _per_batch(x, fn):
    rows = x.shape[0]
    return fn(x.reshape(rows // B, B, x.shape[1])).reshape(rows, x.shape[1])


def _modulate(xn, shift8, scale8):
    return _per_batch(xn, lambda v: v * (1.0 + scale8)[None] + shift8[None])


def _gated(z, gate8):
    return _per_batch(z, lambda v: v * gate8[None])


def _sigmoid(x):
    return jax.nn.sigmoid(x)


def _ffn(x, mod_ref, k0, g_pre, g_post, w1_ref, w3_ref, w2_ref, chunks=F_CHUNKS, side_work=()):
    h = _modulate(_rmsnorm(x, g_pre), _mod(mod_ref, k0), _mod(mod_ref, k0 + 1)).astype(BF16)
    y = None
    for idx, (c0, c1) in enumerate(chunks):
        a = _dot(h, w1_ref[:, c0:c1])
        b = _dot(h, w3_ref[:, c0:c1])
        if idx < len(side_work):
            side_work[idx]()
        act = (a * _sigmoid(a) * b).astype(BF16)
        part = _dot(act, w2_ref[c0:c1, :])
        y = part if y is None else y + part
    return x + MACARON * _gated(_rmsnorm(y, g_post), _mod(mod_ref, k0 + 2))


def _mods_kernel(c_ref, cctx_ref, w_ref, b_ref, o_ref):
    @pl.when(pl.program_id(1) == 0)
    def _():
        o_ref[...] = jnp.broadcast_to(b_ref[...], o_ref.shape)

    a = jnp.concatenate([c_ref[...], jnp.broadcast_to(cctx_ref[...], c_ref.shape)], axis=0)
    ah, al = _split(a * _sigmoid(a))
    wh, wl = _split(w_ref[...])
    r = _dot(jnp.concatenate([ah, al], axis=0), wh)
    o_ref[...] += r[0:2 * B] + r[2 * B:4 * B] + _dot(ah, wl)


def _mods(c, c_ctx, ada_w, ada_b):
    tk = 128
    n = N_MOD * D
    return pl.pallas_call(
        _mods_kernel,
        grid=(DEPTH, D // tk),
        in_specs=[
            pl.BlockSpec((B, tk), lambda i, k: (0, k)),
            pl.BlockSpec((1, tk), lambda i, k: (0, k)),
            pl.BlockSpec((None, tk, n), lambda i, k: (i, k, 0)),
            pl.BlockSpec((None, 1, n), lambda i, k: (i, 0, 0)),
        ],
        out_specs=pl.BlockSpec((None, 2 * B, n), lambda i, k: (i, 0, 0)),
        out_shape=jax.ShapeDtypeStruct((DEPTH, 2 * B, n), F32),
        compiler_params=_cparams(("arbitrary", "arbitrary")),
        name="ada_mods",
    )(c, c_ctx[None], ada_w, ada_b.reshape(DEPTH, 1, n))


def _mod_spec(layer, nt_lat=NT_LAT, order=lambda t: t):
    return pl.BlockSpec((None, None, B, N_MOD * D),
                        lambda t: (layer, jnp.where(order(t) < nt_lat, 0, 1), 0, 0))


def _row_spec(width=D):
    return pl.BlockSpec((TM, width), lambda t: (t, 0))


def _wide_spec():
    return pl.BlockSpec((TS, B * D), lambda t: (t, 0))


W_CHUNKS = 8
W_SLOTS = 3
_HBM = pl.BlockSpec(memory_space=pl.ANY)
_WEIGHT_SPECS = [_HBM] * 3
_WEIGHT_SCRATCH = [pltpu.VMEM((D, D_FF), BF16), pltpu.VMEM((D, D_FF), BF16),
                   pltpu.VMEM((D_FF, D), BF16)]


def _stage_weights(srcs, dsts):
    chunks = [(src, dst, r0, dst.shape[0] // W_CHUNKS) for src, dst in zip(srcs, dsts)
              for r0 in range(0, dst.shape[0], dst.shape[0] // W_CHUNKS)]
    shapes = sorted({(rows, dst.shape[1]) for _, dst, _, rows in chunks})

    def run(*scoped):
        stages, sems = dict(zip(shapes, scoped[:-1])), scoped[-1]

        def slot(k):
            _, dst, _, rows = chunks[k]
            return stages[(rows, dst.shape[1])].at[k % W_SLOTS]

        def copy(k):
            src, _, r0, rows = chunks[k]
            return pltpu.make_async_copy(src.at[pl.ds(r0, rows), :], slot(k), sems.at[k % W_SLOTS])

        for k in range(W_SLOTS - 1):
            copy(k).start()
        for k, (_, dst, r0, rows) in enumerate(chunks):
            if k + W_SLOTS - 1 < len(chunks):
                copy(k + W_SLOTS - 1).start()
            copy(k).wait()
            dst[pl.ds(r0, rows), :] = slot(k)[...].astype(BF16)

    pl.run_scoped(run, *[pltpu.VMEM((W_SLOTS,) + shape, F32) for shape in shapes],
                  pltpu.SemaphoreType.DMA((W_SLOTS,)))


def _with_staged_weights(body, n_in, views):
    n_w = len(views)

    def kernel(*refs):
        ins, rest = refs[:n_in], refs[n_in:]
        copies = rest[-n_w:]

        @pl.when(pl.program_id(0) == 0)
        def _():
            _stage_weights([view(w) for view, w in zip(views, ins[-n_w:])], copies)

        body(*ins[:-n_w], *copies, *rest[:-n_w])

    return kernel


def _ffn_views(layer, half):
    return [lambda w: w.at[layer, half]] * 3


def _first(w):
    return w.at[0]


def _rows_of_batch(b):
    return pl.ds(b, TS, stride=B)


_ROWS_SCRATCH = pltpu.VMEM((D // LANES, TM, LANES), F32)


def _put_batch(rows_scr, b, val):
    for j in range(D // LANES):
        rows_scr[j, _rows_of_batch(b), :] = val[:, j * LANES:(j + 1) * LANES]


def _get_batch(rows_scr, b):
    return jnp.concatenate([rows_scr[j, _rows_of_batch(b), :] for j in range(D // LANES)], axis=1)


def _put_rows(rows_scr, val):
    for j in range(D // LANES):
        rows_scr[j] = val[:, j * LANES:(j + 1) * LANES]


def _get_rows(rows_scr):
    return jnp.concatenate([rows_scr[j] for j in range(D // LANES)], axis=1)


def _ffn_first_kernel(x_ref, ctx_ref, pos_ref, mod_ref, g_ref, w1, w3, w2, o_ref, rows_scr):
    is_latent = pl.program_id(0) < NT_LAT
    for b in range(B):
        _put_batch(rows_scr, b, jnp.where(is_latent, x_ref[b] + pos_ref[...], ctx_ref[b]))
    o_ref[...] = _ffn(_get_rows(rows_scr), mod_ref, 0, g_ref[0:1], g_ref[1:2], w1, w3, w2)


def _ffn_kernel(x_ref, mod_ref, g_ref, w1, w3, w2, o_ref):
    for r in range(x_ref.shape[0] // TM):
        rows = slice(r * TM, (r + 1) * TM)
        o_ref[rows, :] = _ffn(x_ref[rows, :], mod_ref, 0, g_ref[0:1], g_ref[1:2], w1, w3, w2)


def _ffn_first(x, ctx, pos, mods, g, w1, w3, w2):
    lat = lambda t: jnp.minimum(t, NT_LAT - 1)
    return pl.pallas_call(
        _with_staged_weights(_ffn_first_kernel, 8, _ffn_views(0, 0)),
        grid=(NT,),
        in_specs=[pl.BlockSpec((B, TS, D), lambda t: (0, lat(t), 0)),
                  pl.BlockSpec((B, TS, D), lambda t: (0, jnp.maximum(t - NT_LAT, 0), 0)),
                  pl.BlockSpec((TS, D), lambda t: (lat(t), 0)),
                  _mod_spec(0), _resident((6, D))] + _WEIGHT_SPECS,
        out_specs=_row_spec(),
        out_shape=jax.ShapeDtypeStruct((ROWS, D), F32),
        scratch_shapes=[_ROWS_SCRATCH] + _WEIGHT_SCRATCH,
        compiler_params=_cparams(("arbitrary",)),
        name="ffn_first",
    )(x, ctx, pos, mods, g, w1, w3, w2)


def _ffn_plain(x, mods, g, w1, w3, w2, layer):
    return pl.pallas_call(
        _with_staged_weights(_ffn_kernel, 6, _ffn_views(layer, 0)),
        grid=(NT // 2,),
        in_specs=[pl.BlockSpec((2 * TM, D), lambda t: (t, 0)),
                  _mod_spec(layer, NT_LAT // 2), _resident((6, D))] + _WEIGHT_SPECS,
        out_specs=pl.BlockSpec((2 * TM, D), lambda t: (t, 0)),
        out_shape=jax.ShapeDtypeStruct((ROWS, D), F32),
        scratch_shapes=_WEIGHT_SCRATCH,
        compiler_params=_cparams(("arbitrary",)),
        name="ffn_pre",
    )(x, mods, g, w1, w3, w2)


def _mix_ffn_core(y, x, mod_ref, g_ref, wo_ref, bo_ref, w1, w3, w2, **ffn_kwargs):
    z = _dot(y.astype(BF16), wo_ref[...]) + bo_ref[...]
    x = x + _gated(_rmsnorm(z, g_ref[3:4]), _mod(mod_ref, 5))
    return _ffn(x, mod_ref, 6, g_ref[4:5], g_ref[5:6], w1, w3, w2, **ffn_kwargs)


def _hy_out_ffn_kernel(ylo_ref, yhi_ref, x_ref, mod_ref, g_ref, bo_ref, wo_ref, w1, w3, w2,
                       o_ref, rows_scr):
    halves = (ylo_ref, yhi_ref)
    for b in range(B):
        for j in range(D // LANES):
            col = (b * (D // TC) + j // 2) * LANES
            rows_scr[j, _rows_of_batch(b), :] = halves[j % 2][:, col:col + LANES]
    o_ref[...] = _mix_ffn_core(_get_rows(rows_scr), x_ref[...], mod_ref, g_ref, wo_ref, bo_ref,
                               w1, w3, w2)


def _hy_out_ffn(y_lo, y_hi, x, mods, g, w_out, b_out, w1, w3, w2):
    half = pl.BlockSpec((TS, B * D // 2), lambda t: (t, 0))
    return pl.pallas_call(
        _with_staged_weights(_hy_out_ffn_kernel, 10, [_first] + _ffn_views(0, 1)),
        grid=(NT,),
        in_specs=[half, half, _row_spec(), _mod_spec(0), _resident((6, D)), _resident((1, D)),
                  _HBM] + _WEIGHT_SPECS,
        out_specs=_row_spec(),
        out_shape=jax.ShapeDtypeStruct((ROWS, D), F32),
        scratch_shapes=[_ROWS_SCRATCH, pltpu.VMEM((D, D), BF16)] + _WEIGHT_SCRATCH,
        compiler_params=_cparams(("arbitrary",)),
        name="hyena_out_ffn",
    )(y_lo, y_hi, x, mods, g, b_out, w_out, w1, w3, w2)


def _seq_first(t):
    return jnp.logical_or(t == 0, t == NI_LAT)


def _seq_last(t):
    return jnp.logical_or(t == NI_LAT - 1, t == NI - 1)


def _halo_specs(order, lo_rows, hi_rows):
    nlo = ROWS // lo_rows
    nhi = ROWS // hi_rows
    return [
        pl.BlockSpec((lo_rows, D), lambda s: (jnp.maximum(order(s) * (TI // lo_rows) - 1, 0), 0)),
        pl.BlockSpec((TI, D), lambda s: (order(s), 0)),
        pl.BlockSpec((hi_rows, D),
                     lambda s: (jnp.minimum((order(s) + 1) * (TI // hi_rows), nhi - 1), 0)),
    ]


def _hy_in_kernel(xp_ref, x_ref, xn_ref, mod_ref, g_ref, b_ref, cw_ref, cb_ref, w_ref,
                  x0_ref, p_ref, x0_scr, p_scr):
    t = pl.program_id(0)
    xa = jnp.concatenate([xp_ref[...], x_ref[...], xn_ref[...]], axis=0)
    h = _modulate(_rmsnorm(xa, g_ref[2:3]), _mod(mod_ref, 3), _mod(mod_ref, 4)).astype(BF16)
    keep_lo = jnp.where(_seq_first(t), 0.0, 1.0).astype(F32)
    keep_hi = jnp.where(_seq_last(t), 0.0, 1.0).astype(F32)

    def conv_cols(c0):
        cols = slice(c0, c0 + MXU_N)
        u = _dot(h, w_ref[:, cols]) + b_ref[:, cols]
        lo = jnp.concatenate([u[0:B] * keep_lo, u[B:TI]], axis=0)
        hi = jnp.concatenate([u[2 * B:TI + B], u[TI + B:TI + 2 * B] * keep_hi], axis=0)
        return (cb_ref[:, cols] + cw_ref[0:1, cols] * lo + cw_ref[1:2, cols] * u[B:TI + B]
                + cw_ref[2:3, cols] * hi)

    slabs = MXU_N // LANES
    for c in range(D // MXU_N):
        c0 = c * MXU_N
        vals = (conv_cols(c0), conv_cols(D + c0) * conv_cols(2 * D + c0))
        for out_ref, scr, val in zip((x0_ref, p_ref), (x0_scr, p_scr), vals):
            for j in range(slabs):
                scr[c * slabs + j] = val[:, j * LANES:(j + 1) * LANES]
            for b in range(B):
                out_ref[:, b * D + c0:b * D + c0 + MXU_N] = jnp.concatenate(
                    [scr[c * slabs + j, pl.ds(b, TI // B, stride=B), :] for j in range(slabs)], axis=1)


def _hy_in(x, mods, g, w_in, b_in, conv_w, conv_b):
    wide = pl.BlockSpec((TI // B, B * D), lambda t: (t, 0))
    rows_scratch = pltpu.VMEM((D // LANES, TI, LANES), F32)
    return pl.pallas_call(
        _with_staged_weights(_hy_in_kernel, 9, [_first]),
        grid=(NI,),
        in_specs=_halo_specs(lambda s: s, B, B) + [
            _mod_spec(0, NI_LAT), _resident((6, D)), _resident((1, 3 * D)),
            _resident((3, 3 * D)), _resident((1, 3 * D)), _HBM],
        out_specs=[wide, wide],
        out_shape=[jax.ShapeDtypeStruct((ROWS // B, B * D), F32)] * 2,
        scratch_shapes=[rows_scratch, rows_scratch, pltpu.VMEM((D, 3 * D), BF16)],
        compiler_params=_cparams(("arbitrary",)),
        name="hyena_in",
    )(x, x, x, mods, g, b_in, conv_w, conv_b, w_in)


def _filter_kernel(zt_ref, t_ref, fw0t, fb0, fw1t, fb1, fw2t, fb2, freq, fwout, deltas,
                   hf_ref, hb_ref):
    h = jnp.sin(freq[:, 0:1] * (_dot3(fw0t[...], zt_ref[...]) + fb0[...]))
    h = jnp.sin(freq[:, 1:2] * (_dot3(fw1t[...], h) + fb1[...]))
    h = jnp.sin(freq[:, 2:3] * (_dot3(fw2t[...], h) + fb2[...]))
    filt = _dot3(h.T, fwout[...])
    decay = jnp.exp(-t_ref[:, 0:1] * deltas[...])
    hf_ref[...] = filt[:, 0:D] * decay
    hb_ref[...] = filt[:, D:2 * D] * decay


def _filters(n, fw0, fb0, fw1, fb1, fw2, fb2, freq, fwout):
    t = np.linspace(0.0, 1.0, n)[:, None]
    bands = np.linspace(1e-4, HY_BANDS - 1, HY_BANDS)[None]
    phase = bands * (2.0 * math.pi * np.arange(n)[:, None] / n)
    zp = np.zeros((n, LANES), np.float32)
    zp[:, :HY_EMB] = np.concatenate([t, np.cos(phase), -np.sin(phase)], axis=-1)
    zp = np.concatenate([zp[r::4] for r in range(4)], axis=0)
    fw0t = jnp.zeros((HY_HID, LANES), F32).at[:, :HY_EMB].set(fw0.T)
    max_decay = math.log(HY_DECAY_TARGET) / HY_FAST_DECAY
    min_decay = math.log(HY_DECAY_TARGET) / HY_SLOW_DECAY
    deltas = np.abs(np.linspace(min_decay, max_decay, D))[None].astype(np.float32)
    tl = 256
    row = lambda i: (i, 0)
    unit = _resident((HY_HID, 1))
    return pl.pallas_call(
        _filter_kernel,
        grid=(n // tl,),
        in_specs=[pl.BlockSpec((LANES, tl), lambda i: (0, i)), pl.BlockSpec((tl, LANES), row),
                  _resident((HY_HID, LANES)), unit, _resident((HY_HID, HY_HID)), unit,
                  _resident((HY_HID, HY_HID)), unit, _resident((HY_HID, 3)),
                  _resident((HY_HID, 2 * D)), _resident((1, D))],
        out_specs=[pl.BlockSpec((tl, D), row)] * 2,
        out_shape=[jax.ShapeDtypeStruct((n, D), F32)] * 2,
        compiler_params=_cparams(("arbitrary",)),
        name="hyena_filter",
    )(np.ascontiguousarray(zp.T), zp, fw0t, fb0[:, None], fw1.T, fb1[:, None], fw2.T, fb2[:, None],
      freq.T, fwout, deltas)


def _alt_sign(rows, cols):
    r = lax.broadcasted_iota(jnp.int32, (rows, cols), 0)
    return (1 - 2 * (r & 1)).astype(F32)


_N_GROUPS = 4
_ROOT_HALF = math.sqrt(0.5)


def _radix4_tables(g):
    k = np.arange(g, dtype=np.int64)
    theta = ((k[:, None] * k[None, :]) % (2 * g)) * (math.pi / g)
    phase = k[:, None] * (math.pi / (4 * g))
    fwd = [f(theta + r * phase) for r in range(4) for f in (np.cos, np.sin)]
    as_bf16 = lambda blocks: jnp.asarray(np.concatenate(blocks, axis=0), F32).astype(BF16)
    return as_bf16(fwd), as_bf16([b.T for b in fwd])


def _cmul(ar, as_, br, bi):
    return ar * br + as_ * bi, as_ * br - ar * bi


def _table_rows(tab_ref, r, rows, g):
    count = rows.stop - rows.start
    return (tab_ref[pl.ds(2 * r * g + rows.start, count), :],
            tab_ref[pl.ds((2 * r + 1) * g + rows.start, count), :])


def _fwd4(quarters, tab_ref, rows, g):
    ts = []
    for r, q in enumerate(quarters):
        c, s = _table_rows(tab_ref, r, rows, g)
        ts.append((_dot(c, q), _dot(s, q)))
    (t0r, t0s), (t1r, t1s), (t2r, t2s), (t3r, t3s) = ts
    er, es, fr, fs = t0r + t2r, t0s + t2s, t0r - t2r, t0s - t2s
    pr, ps, dr, ds = t1r + t3r, t1s + t3s, t1r - t3r, t1s - t3s
    return ((er + pr, es + ps), (er - pr, es - ps), (fr + ds, fs - dr), (fr - ds, fs + dr))


def _mid_freqs(sums):
    s0, s1, s2, s3 = sums
    a, b = _ROOT_HALF * (s1 - s3), _ROOT_HALF * (s1 + s3)
    return (s0 + a, s2 + b), (s0 - a, b - s2)


def _alt_sums(quarters):
    alt = _alt_sign(*quarters[0].shape)
    return [jnp.sum(q * alt, axis=0, keepdims=True) for q in quarters]


def _spectrum_kernel(hf_ref, hb_ref, tab_ref, kr_ref, ki_ref, kn_ref, *, g):
    hf = hf_ref[...]
    row = lax.broadcasted_iota(jnp.int32, hf.shape, 0)
    hb = jnp.where(row == 0, 0.0, hb_ref[...])
    cos_part = hf + hb
    sin_part = hb - hf
    cq = [cos_part[r * g:(r + 1) * g] for r in range(4)]
    sq = [sin_part[r * g:(r + 1) * g] for r in range(4)]
    rows = slice(0, g)
    groups_c = _fwd4([q.astype(BF16) for q in cq], tab_ref, rows, g)
    groups_s = _fwd4([q.astype(BF16) for q in sq], tab_ref, rows, g)
    n_fft = 8 * g
    k = lax.broadcasted_iota(jnp.int32, (g, hf.shape[1]), 0)
    scale = jnp.where(k == 0, 1.0 / n_fft, 2.0 / n_fft)
    for grp in range(_N_GROUPS):
        kr_ref[grp * g:(grp + 1) * g, :] = groups_c[grp][0] * scale
        ki_ref[grp * g:(grp + 1) * g, :] = groups_s[grp][1] * scale
    (cg, _), (c3g, _) = _mid_freqs(_alt_sums(cq))
    (_, sg), (_, s3g) = _mid_freqs(_alt_sums(sq))
    mids = [v * (2.0 / n_fft) for v in (cg, sg, c3g, s3g)]
    kn_ref[...] = jnp.concatenate(mids + [jnp.zeros((B - 4, hf.shape[1]), F32)], axis=0)


def _spectrum(hf, hb, fwd_table):
    n = hf.shape[0]
    g = n // 4
    col = lambda j: (0, j)
    return pl.pallas_call(
        functools.partial(_spectrum_kernel, g=g),
        grid=(D // TC,),
        in_specs=[pl.BlockSpec((n, TC), col)] * 2 + [_resident((8 * g, g))],
        out_specs=[pl.BlockSpec((n, TC), col)] * 2 + [pl.BlockSpec((B, TC), col)],
        out_shape=[jax.ShapeDtypeStruct((n, D), F32)] * 2 + [jax.ShapeDtypeStruct((B, D), F32)],
        compiler_params=_cparams(("arbitrary",)),
        name="hyena_spectrum",
    )(hf, hb, fwd_table)


def _steps(refs, first, count):
    rows = pl.ds(first, count, stride=4)
    return jnp.concatenate([r[rows, :] for r in refs], axis=1)


def _long_conv_rows(t0, g, p_refs, x0_refs, spec, bias_ref, o_refs, scratch):
    kr_ref, ki_ref, kn_ref, fwd_ref, inv_ref = spec
    q_scr, u_scr = scratch
    tk = min(g, 512)
    r0 = t0 // 4
    seq = slice(r0, r0 + g)
    chunks = [slice(k * tk, (k + 1) * tk) for k in range(g // tk)]
    quarters = [_steps(p_refs, t0 + r, g) for r in range(4)]
    tc = quarters[0].shape[1]
    for r in range(4):
        q_scr[r, seq, :] = quarters[r].astype(BF16)
    for rows in chunks:
        dst = slice(r0 + rows.start, r0 + rows.stop)
        groups = _fwd4([q_scr[r, seq, :] for r in range(4)], fwd_ref, rows, g)
        ys = []
        for grp, (xr, xs) in enumerate(groups):
            k_rows = pl.ds(grp * g + rows.start, tk)
            ys.append(_cmul(xr, xs, kr_ref[k_rows, :], ki_ref[k_rows, :]))
        (y1r, y1s), (y2r, y2s), (y3r, y3s), (y4r, y4s) = ys
        pr, ps, mr, ms = y1r + y2r, y1s + y2s, y1r - y2r, y1s - y2s
        qr, qs, nr, ns = y3r + y4r, y3s + y4s, y3r - y4r, y3s - y4s
        us = [(pr + qr, ps + qs), (mr - ns, ms + nr), (pr - qr, ps - qs), (mr + ns, ms - nr)]
        for r, (ur, us_) in enumerate(us):
            u_scr[2 * r, dst, :] = ur.astype(BF16)
            u_scr[2 * r + 1, dst, :] = us_.astype(BF16)
    (xgr, xgs), (x3r, x3s) = _mid_freqs(_alt_sums(quarters))
    ygr, ygs = _cmul(xgr, xgs, kn_ref[0:1, :], kn_ref[1:2, :])
    y3r_, y3s_ = _cmul(x3r, x3s, kn_ref[2:3, :], kn_ref[3:4, :])
    a = _ROOT_HALF
    mids = [ygr + y3r_, a * (ygr + ygs - y3r_ + y3s_), ygs - y3s_, a * (ygs - ygr + y3r_ + y3s_)]
    alt_chunk = _alt_sign(tk, tc)
    bias = bias_ref[...]
    for rows in chunks:
        for r in range(4):
            first = t0 + 4 * rows.start + r
            c, s = _table_rows(inv_ref, r, rows, g)
            y = (_dot(c, u_scr[2 * r, seq, :]) + _dot(s, u_scr[2 * r + 1, seq, :])
                 + alt_chunk * mids[r])
            out = _steps(x0_refs, first, tk) * (y + _steps(p_refs, first, tk) * bias)
            for q, o_ref in enumerate(o_refs):
                o_ref[pl.ds(first, tk, stride=4), :] = out[:, q * LANES:(q + 1) * LANES]


_N_SPEC = 5


def _long_conv_kernel(*refs):
    p_refs, x0_refs, bias_ref = refs[0:2], refs[2:4], refs[4]
    lat, ctx = refs[5:5 + _N_SPEC], refs[5 + _N_SPEC:5 + 2 * _N_SPEC]
    o_refs, scratch = refs[5 + 2 * _N_SPEC:7 + 2 * _N_SPEC], refs[7 + 2 * _N_SPEC:]
    _long_conv_rows(0, L // 4, p_refs, x0_refs, lat, bias_ref, o_refs, scratch)
    _long_conv_rows(L, CTX // 4, p_refs, x0_refs, ctx, bias_ref, o_refs, scratch)


def _long_conv(p2, x02, bias, spec_lat, spec_ctx):
    nc = D // TC
    halves = [pl.BlockSpec((ROWS // B, LANES),
                           functools.partial(lambda q, j: (0, 2 * ((j % B) * nc + j // B) + q), q))
              for q in range(TC // LANES)]
    ch = lambda j: (0, j // B)

    def spec_specs(g):
        return ([pl.BlockSpec((4 * g, TC), ch)] * 2 + [pl.BlockSpec((B, TC), ch)]
                + [_resident((8 * g, g))] * 2)

    half_out = pl.BlockSpec((ROWS // B, LANES), lambda j: (0, (j % B) * nc + j // B))
    quarter_rows = ROWS // (4 * B)
    return pl.pallas_call(
        _long_conv_kernel,
        grid=(B * D // TC,),
        scratch_shapes=[pltpu.VMEM((4, quarter_rows, TC), BF16),
                        pltpu.VMEM((8, quarter_rows, TC), BF16)],
        in_specs=halves + halves + [pl.BlockSpec((1, TC), ch)] + spec_specs(L // 4)
        + spec_specs(CTX // 4),
        out_specs=[half_out] * (TC // LANES),
        out_shape=[jax.ShapeDtypeStruct((ROWS // B, B * D * LANES // TC), F32)] * (TC // LANES),
        compiler_params=_cparams(("arbitrary",)),
        name="hyena_long_conv",
    )(p2, p2, x02, x02, bias, *spec_lat, *spec_ctx)


def _gelu_tanh(x):
    return x * (0.5 * (1.0 + jnp.tanh(math.sqrt(2.0 / math.pi) * (x + 0.044715 * (x * x * x)))))


_TINY = 1e-30


def _rg_coeffs(xc, hd, wai_ref, bai_ref, lam_ref, a_scr, b_scr):
    sl = slice(hd * RG_BLOCK, (hd + 1) * RG_BLOCK)
    lam = lam_ref[:, sl]
    softplus_neg = jnp.maximum(-lam, 0.0) + jnp.log1p(jnp.exp(-jnp.abs(lam)))
    rate = (-RG_C * math.log2(math.e)) * softplus_neg
    pre = _dot(xc.astype(BF16), wai_ref[hd]) + bai_ref[:, 2 * hd * RG_BLOCK:2 * (hd + 1) * RG_BLOCK]
    gates = _sigmoid(pre)
    a = jnp.exp2(gates[:, 0:RG_BLOCK] * rate)
    a_scr[:, sl] = a
    v = (1.0 - a) * (1.0 + a)
    root = v * lax.rsqrt(jnp.maximum(v, _TINY))
    b_scr[:, sl] = root * gates[:, RG_BLOCK:2 * RG_BLOCK] * xc


def _scan_tile(a_scr, b_scr, h_scr, emit, reverse):
    steps = a_scr.shape[0] // B

    def body(k, h):
        t = steps - 1 - k if reverse else k
        r0 = pl.multiple_of(t * B, B)
        h = a_scr[pl.ds(r0, B), :] * h + b_scr[pl.ds(r0, B), :]
        emit(r0, h)
        return h

    h_scr[...] = lax.fori_loop(0, steps, body, h_scr[...], unroll=8)


def _rg_fwd_order(s):
    return jnp.where(s < NI_CTX, NI_LAT + s, s - NI_CTX)


def _rg_in_kernel(xp_ref, x_ref, xn_ref, mod_ref, g_ref, b_ref, cw_ref, cb_ref,
                  wai_ref, bai_ref, lam_ref, w_ref, xc_ref, gate_ref, hs_ref, a_scr, b_scr, h_scr):
    s = pl.program_id(0)
    t = _rg_fwd_order(s)
    xa = jnp.concatenate([xp_ref[...], x_ref[...], xn_ref[...]], axis=0)
    h = _modulate(_rmsnorm(xa, g_ref[2:3]), _mod(mod_ref, 3), _mod(mod_ref, 4)).astype(BF16)
    keep_lo = jnp.where(_seq_first(t), 0.0, 1.0).astype(F32)
    keep_hi = jnp.where(_seq_last(t), 0.0, 1.0).astype(F32)
    for hd in range(RG_HEADS):
        sl = slice(hd * RG_BLOCK, (hd + 1) * RG_BLOCK)
        rec = slice(D + hd * RG_BLOCK, D + (hd + 1) * RG_BLOCK)
        gate_ref[:, sl] = (_dot(h[B:TI + B], w_ref[:, sl]) + b_ref[:, sl]).astype(BF16)
        u = _dot(h, w_ref[:, rec]) + b_ref[:, rec]
        taps = (jnp.concatenate([u[0:B] * keep_lo, u[B:TI]], axis=0),
                u[B:TI + B],
                jnp.concatenate([u[2 * B:TI + B], u[TI + B:TI + 2 * B] * keep_hi], axis=0),
                jnp.concatenate([u[3 * B:TI + B], u[TI + B:TI + 3 * B] * keep_hi], axis=0))
        xc = cb_ref[:, sl]
        for k, tap in enumerate(taps):
            xc = xc + cw_ref[k:k + 1, sl] * tap
        xc_ref[:, sl] = xc
        _rg_coeffs(xc, hd, wai_ref, bai_ref, lam_ref, a_scr, b_scr)

    @pl.when(s == 0)
    def _():
        h_scr[...] = jnp.zeros((B, D), F32)

    def emit(r0, hv):
        hs_ref[pl.ds(r0, B), :] = hv

    _scan_tile(a_scr, b_scr, h_scr, emit, reverse=False)


def _rg_gate_specs():
    return [_resident((RG_HEADS, RG_BLOCK, 2 * RG_BLOCK)), _resident((1, 2 * D)), _resident((1, D))]


def _rg_in(x, mods, g, w_in, b_in, conv_w, conv_b, wai, bai, lam):
    order = _rg_fwd_order
    mod_spec = _mod_spec(1, NI_LAT, order)
    out_spec = pl.BlockSpec((TI, D), lambda s: (order(s), 0))
    return pl.pallas_call(
        _with_staged_weights(_rg_in_kernel, 12, [_first]),
        grid=(NI,),
        in_specs=_halo_specs(order, B, 2 * B) + [
            mod_spec, _resident((6, D)), _resident((1, 2 * D)),
            _resident((4, D)), _resident((1, D))] + _rg_gate_specs() + [_HBM],
        out_specs=[out_spec] * 3,
        out_shape=[jax.ShapeDtypeStruct((ROWS, D), F32), jax.ShapeDtypeStruct((ROWS, D), BF16),
                   jax.ShapeDtypeStruct((ROWS, D), F32)],
        scratch_shapes=[pltpu.VMEM((TI, D), F32), pltpu.VMEM((TI, D), F32),
                        pltpu.VMEM((B, D), F32), pltpu.VMEM((D, 2 * D), BF16)],
        compiler_params=_cparams(("arbitrary",)),
        name="rglru_in_fwd_scan",
    )(x, x, x, mods, g, b_in, conv_w, conv_b, wai, bai, lam, w_in)


def _rg_tile_coeffs(xc_ref, wai_ref, bai_ref, lam_ref, a_scr, b_scr):
    for hd in range(RG_HEADS):
        xc = xc_ref[:, hd * RG_BLOCK:(hd + 1) * RG_BLOCK]
        _rg_coeffs(xc, hd, wai_ref, bai_ref, lam_ref, a_scr, b_scr)


def _rg_ctx_bwd_kernel(xc_ref, wai_ref, bai_ref, lam_ref, h_ref, a_scr, b_scr, h_scr):
    _rg_tile_coeffs(xc_ref, wai_ref, bai_ref, lam_ref, a_scr, b_scr)

    @pl.when(pl.program_id(0) == 0)
    def _():
        h_scr[...] = jnp.zeros((B, D), F32)

    _scan_tile(a_scr, b_scr, h_scr, lambda r0, hv: None, reverse=True)
    h_ref[...] = h_scr[...]


def _rg_ctx_bwd(xc, wai, bai, lam):
    return pl.pallas_call(
        _rg_ctx_bwd_kernel,
        grid=(NT_CTX,),
        in_specs=[pl.BlockSpec((TM, D), lambda s: (NT - 1 - s, 0))] + _rg_gate_specs(),
        out_specs=pl.BlockSpec((B, D), lambda s: (0, 0)),
        out_shape=jax.ShapeDtypeStruct((B, D), F32),
        scratch_shapes=[pltpu.VMEM((TM, D), F32), pltpu.VMEM((TM, D), F32),
                        pltpu.VMEM((B, D), F32)],
        compiler_params=_cparams(("arbitrary",)),
        name="rglru_ctx_bwd_scan",
    )(xc, wai, bai, lam)


def _rg_out_ffn_kernel(xc_ref, gate_ref, hs_ref, x_ref, h0_ref, mod_ref, g_ref,
                       wai_ref, bai_ref, lam_ref, bo_ref, wo_ref, w1, w3, w2,
                       o_ref, a_scr, b_scr, hb_scr, h_scr, rows_scr):
    s = pl.program_id(0)

    def head_gates(hd):
        xc = xc_ref[:, hd * RG_BLOCK:(hd + 1) * RG_BLOCK]
        _rg_coeffs(xc, hd, wai_ref, bai_ref, lam_ref, a_scr, b_scr)

    @pl.when(s == 0)
    def _():
        h_scr[...] = h0_ref[...]
        for hd in range(RG_HEADS):
            head_gates(hd)

    @pl.when(s > 0)
    def _():
        y = (hs_ref[...] + hb_scr[...]) * _gelu_tanh(gate_ref[...].astype(F32))
        res = _mix_ffn_core(
            y, x_ref[...], mod_ref, g_ref, wo_ref, bo_ref, w1, w3, w2, chunks=F_QUARTERS,
            side_work=[functools.partial(head_gates, hd) for hd in range(RG_HEADS)])
        _put_rows(rows_scr, res)
        for b in range(B):
            o_ref[b] = _get_batch(rows_scr, b)

    @pl.when(s < NT_LAT)
    def _():
        def emit(r0, hv):
            hb_scr[pl.ds(r0, B), :] = hv

        _scan_tile(a_scr, b_scr, h_scr, emit, reverse=True)


def _rg_out_ffn(xc, gate, hs, x, h0, mods, g, wai, bai, lam, w_out, b_out, w1, w3, w2):
    scan_tile = lambda s: (jnp.maximum(NT_LAT - 1 - s, 0), 0)
    out_tile = lambda s: jnp.minimum(NT_LAT - s, NT_LAT - 1)
    prev = pl.BlockSpec((TM, D), lambda s: (out_tile(s), 0))
    return pl.pallas_call(
        _with_staged_weights(_rg_out_ffn_kernel, 15, [_first] + _ffn_views(1, 1)),
        grid=(NT_LAT + 1,),
        in_specs=[pl.BlockSpec((TM, D), scan_tile), prev, prev, prev, _resident((B, D)),
                  _mod_spec(1, NT_LAT + 1), _resident((6, D))]
        + _rg_gate_specs() + [_resident((1, D)), _HBM] + _WEIGHT_SPECS,
        out_specs=pl.BlockSpec((B, TS, D), lambda s: (0, out_tile(s), 0)),
        out_shape=jax.ShapeDtypeStruct((B, L, D), F32),
        scratch_shapes=[pltpu.VMEM((TM, D), F32), pltpu.VMEM((TM, D), F32),
                        pltpu.VMEM((TM, D), F32), pltpu.VMEM((B, D), F32), _ROWS_SCRATCH,
                        pltpu.VMEM((D, D), BF16)] + _WEIGHT_SCRATCH,
        compiler_params=_cparams(("arbitrary",)),
        name="rglru_out_ffn",
    )(xc, gate, hs, x, h0, mods, g, wai, bai, lam, b_out, w_out, w1, w3, w2)


def _grid_pos():
    rows = L // GRID_W
    quarter = D // 4
    omega = POS_BASE ** (-np.arange(quarter) / quarter)

    def emb(q):
        ang = q[:, None] * omega[None]
        return jnp.asarray(np.concatenate([np.sin(ang), np.cos(ang)], axis=-1), F32)

    row_code = jnp.repeat(emb(np.arange(rows)), GRID_W, axis=0)
    col_code = jnp.tile(emb(np.arange(GRID_W)), (rows, 1))
    return jnp.concatenate([row_code, col_code], axis=-1)


def kernel(x, c, ctx, c_ctx, ada_w, ada_b, norm_g, ffn_w1, ffn_w3, ffn_w2, hy_w_in, hy_b_in, hy_conv_w, hy_conv_b, hy_fw0, hy_fb0, hy_fw1, hy_fb1, hy_fw2, hy_fb2, hy_freq, hy_fwout, hy_filt_bias, hy_w_out, hy_b_out, rg_w_in, rg_b_in, rg_conv_w, rg_conv_b, rg_wa, rg_ba, rg_wi, rg_bi, rg_lam, rg_w_out, rg_b_out):
    mods = _mods(c, c_ctx, ada_w, ada_b).reshape(DEPTH, 2, B, N_MOD * D)
    w1, w3, w2 = ffn_w1, ffn_w3, ffn_w2

    g = norm_g[0]
    xs = _ffn_first(x, ctx, _grid_pos(), mods, g, w1, w3, w2)
    x0, p = _hy_in(xs, mods, g, hy_w_in, hy_b_in[0][None],
                   hy_conv_w[0], hy_conv_b[0][None])
    fparams = (hy_fw0[0], hy_fb0[0], hy_fw1[0], hy_fb1[0], hy_fw2[0], hy_fb2[0],
               hy_freq[0], hy_fwout[0])
    specs = []
    for n in (L, CTX):
        hf, hb = _filters(n, *fparams)
        fwd_table, inv_table = _radix4_tables(n // 4)
        specs.append(list(_spectrum(hf, hb, fwd_table)) + [fwd_table, inv_table])
    y_lo, y_hi = _long_conv(p, x0, hy_filt_bias[0][None], *specs)
    xs = _hy_out_ffn(y_lo, y_hi, xs, mods, g, hy_w_out, hy_b_out[0][None],
                     w1, w3, w2)

    g = norm_g[1]
    xs = _ffn_plain(xs, mods, g, w1, w3, w2, layer=1)
    wai = jnp.concatenate([rg_wa[0], rg_wi[0]], axis=-1).astype(BF16)
    per_head = lambda v: v.reshape(2, RG_HEADS, RG_BLOCK)
    bai = jnp.concatenate([per_head(rg_ba[0]), per_head(rg_bi[0])], axis=-1).reshape(2, 1, 2 * D)
    lam = rg_lam[0][:, None, :]
    xc, gate, hs = _rg_in(xs, mods, g, rg_w_in, rg_b_in[0][None],
                          rg_conv_w[0], rg_conv_b[0][None], wai[0], bai[0], lam[0])
    h_ctx = _rg_ctx_bwd(xc, wai[1], bai[1], lam[1])
    return _rg_out_ffn(xc, gate, hs, xs, h_ctx, mods, g, wai[1], bai[1], lam[1],
                       rg_w_out, rg_b_out[0][None], w1, w3, w2)
```

```python
import functools
import math

import jax
import jax.numpy as jnp
import numpy as np
from jax import lax
from jax.experimental import pallas as pl
from jax.experimental.pallas import tpu as pltpu

F32 = jnp.float32
BF16 = jnp.bfloat16

D = 1024
B = 8
LANES = 128
MXU_N = 256
L = 2048
CTX = 256
DEPTH = 2
GRID_W = 64
D_FF = 2816
N_MOD = 9
MACARON = 0.5
NORM_EPS = 1e-6
POS_BASE = 10000.0
HY_EMB = 33
HY_BANDS = 16
HY_HID = 64
HY_FAST_DECAY = 0.3
HY_SLOW_DECAY = 1.5
HY_DECAY_TARGET = 1e-2
RG_HEADS = 4
RG_BLOCK = D // RG_HEADS
RG_C = 8.0

ROWS_LAT = L * B
ROWS_CTX = CTX * B
ROWS = ROWS_LAT + ROWS_CTX
TM = 512
TS = TM // B
NT_LAT = ROWS_LAT // TM
NT_CTX = ROWS_CTX // TM
NT = NT_LAT + NT_CTX
TI = 1024
NI_LAT = ROWS_LAT // TI
NI_CTX = ROWS_CTX // TI
NI = NI_LAT + NI_CTX
F_CHUNKS = ((0, 1536), (1536, D_FF))
F_QUARTERS = ((0, 768), (768, 1536), (1536, 2304), (2304, D_FF))
TC = 256
VMEM_LIMIT = 58 * 1024 * 1024


def _cparams(sem):
    return pltpu.CompilerParams(dimension_semantics=sem, vmem_limit_bytes=VMEM_LIMIT)


def _resident(shape):
    nd = len(shape)
    return pl.BlockSpec(shape, lambda *_: (0,) * nd, pipeline_mode=pl.Buffered(1))


def _split(a):
    hi = a.astype(BF16)
    lo = (a - hi.astype(F32)).astype(BF16)
    return hi, lo


def _dot(a, b):
    return jnp.dot(a, b, preferred_element_type=F32)


def _dot3(a, b):
    ah, al = _split(a)
    bh, bl = _split(b)
    return _dot(ah, bh) + _dot(ah, bl) + _dot(al, bh)


def _rmsnorm(x, g):
    ms = jnp.mean(x * x, axis=-1, keepdims=True)
    return x * lax.rsqrt(ms + NORM_EPS) * g


def _mod(mod_ref, k):
    return mod_ref[:, k * D:(k + 1) * D]


def _per_batch(x, fn):
    rows = x.shape[0]
    return fn(x.reshape(rows // B, B, x.shape[1])).reshape(rows, x.shape[1])


def _modulate(xn, shift8, scale8):
    return _per_batch(xn, lambda v: v * (1.0 + scale8)[None] + shift8[None])


def _gated(z, gate8):
    return _per_batch(z, lambda v: v * gate8[None])


def _sigmoid(x):
    return jax.nn.sigmoid(x)


def _ffn(x, mod_ref, k0, g_pre, g_post, w1_ref, w3_ref, w2_ref, chunks=F_CHUNKS, side_work=()):
    h = _modulate(_rmsnorm(x, g_pre), _mod(mod_ref, k0), _mod(mod_ref, k0 + 1)).astype(BF16)
    y = None
    for idx, (c0, c1) in enumerate(chunks):
        a = _dot(h, w1_ref[:, c0:c1])
        b = _dot(h, w3_ref[:, c0:c1])
        if idx < len(side_work):
            side_work[idx]()
        act = (a * _sigmoid(a) * b).astype(BF16)
        part = _dot(act, w2_ref[c0:c1, :])
        y = part if y is None else y + part
    return x + MACARON * _gated(_rmsnorm(y, g_post), _mod(mod_ref, k0 + 2))


def _mods_kernel(c_ref, cctx_ref, w_ref, b_ref, o_ref):
    @pl.when(pl.program_id(1) == 0)
    def _():
        o_ref[...] = jnp.broadcast_to(b_ref[...], o_ref.shape)

    a = jnp.concatenate([c_ref[...], jnp.broadcast_to(cctx_ref[...], c_ref.shape)], axis=0)
    ah, al = _split(a * _sigmoid(a))
    wh, wl = _split(w_ref[...])
    r = _dot(jnp.concatenate([ah, al], axis=0), wh)
    o_ref[...] += r[0:2 * B] + r[2 * B:4 * B] + _dot(ah, wl)


def _mods(c, c_ctx, ada_w, ada_b):
    tk = 128
    n = N_MOD * D
    return pl.pallas_call(
        _mods_kernel,
        grid=(DEPTH, D // tk),
        in_specs=[
            pl.BlockSpec((B, tk), lambda i, k: (0, k)),
            pl.BlockSpec((1, tk), lambda i, k: (0, k)),
            pl.BlockSpec((None, tk, n), lambda i, k: (i, k, 0)),
            pl.BlockSpec((None, 1, n), lambda i, k: (i, 0, 0)),
        ],
        out_specs=pl.BlockSpec((None, 2 * B, n), lambda i, k: (i, 0, 0)),
        out_shape=jax.ShapeDtypeStruct((DEPTH, 2 * B, n), F32),
        compiler_params=_cparams(("arbitrary", "arbitrary")),
        name="ada_mods",
    )(c, c_ctx[None], ada_w, ada_b.reshape(DEPTH, 1, n))


def _mod_spec(layer, nt_lat=NT_LAT, order=lambda t: t):
    return pl.BlockSpec((None, None, B, N_MOD * D),
                        lambda t: (layer, jnp.where(order(t) < nt_lat, 0, 1), 0, 0))


def _row_spec(width=D):
    return pl.BlockSpec((TM, width), lambda t: (t, 0))


W_CHUNKS = 8
W_SLOTS = 3
_HBM = pl.BlockSpec(memory_space=pl.ANY)
_WEIGHT_SPECS = [_HBM] * 3
_WEIGHT_SCRATCH = [pltpu.VMEM((D, D_FF), BF16), pltpu.VMEM((D, D_FF), BF16),
                   pltpu.VMEM((D_FF, D), BF16)]


def _stage_weights(srcs, dsts):
    chunks = [(src, dst, r0, dst.shape[0] // W_CHUNKS) for src, dst in zip(srcs, dsts)
              for r0 in range(0, dst.shape[0], dst.shape[0] // W_CHUNKS)]
    shapes = sorted({(rows, dst.shape[1]) for _, dst, _, rows in chunks})

    def run(*scoped):
        stages, sems = dict(zip(shapes, scoped[:-1])), scoped[-1]

        def slot(k):
            _, dst, _, rows = chunks[k]
            return stages[(rows, dst.shape[1])].at[k % W_SLOTS]

        def copy(k):
            src, _, r0, rows = chunks[k]
            return pltpu.make_async_copy(src.at[pl.ds(r0, rows), :], slot(k), sems.at[k % W_SLOTS])

        for k in range(W_SLOTS - 1):
            copy(k).start()
        for k, (_, dst, r0, rows) in enumerate(chunks):
            if k + W_SLOTS - 1 < len(chunks):
                copy(k + W_SLOTS - 1).start()
            copy(k).wait()
            dst[pl.ds(r0, rows), :] = slot(k)[...].astype(BF16)

    pl.run_scoped(run, *[pltpu.VMEM((W_SLOTS,) + shape, F32) for shape in shapes],
                  pltpu.SemaphoreType.DMA((W_SLOTS,)))


def _with_staged_weights(body, n_in, views):
    n_w = len(views)

    def kernel(*refs):
        ins, rest = refs[:n_in], refs[n_in:]
        copies = rest[-n_w:]

        @pl.when(pl.program_id(0) == 0)
        def _():
            _stage_weights([view(w) for view, w in zip(views, ins[-n_w:])], copies)

        body(*ins[:-n_w], *copies, *rest[:-n_w])

    return kernel


def _ffn_views(layer, half):
    return [lambda w: w.at[layer, half]] * 3


def _first(w):
    return w.at[0]


def _rows_of_batch(b):
    return pl.ds(b, TS, stride=B)


_ROWS_SCRATCH = pltpu.VMEM((D // LANES, TM, LANES), F32)


def _put_batch(rows_scr, b, val):
    for j in range(D // LANES):
        rows_scr[j, _rows_of_batch(b), :] = val[:, j * LANES:(j + 1) * LANES]


def _get_batch(rows_scr, b):
    return jnp.concatenate([rows_scr[j, _rows_of_batch(b), :] for j in range(D // LANES)], axis=1)


def _put_rows(rows_scr, val):
    for j in range(D // LANES):
        rows_scr[j] = val[:, j * LANES:(j + 1) * LANES]


def _get_rows(rows_scr):
    return jnp.concatenate([rows_scr[j] for j in range(D // LANES)], axis=1)


def _ffn_first_kernel(x_ref, ctx_ref, pos_ref, mod_ref, g_ref, w1, w3, w2, o_ref, rows_scr):
    is_latent = pl.program_id(0) < NT_LAT
    for b in range(B):
        _put_batch(rows_scr, b, jnp.where(is_latent, x_ref[b] + pos_ref[...], ctx_ref[b]))
    o_ref[...] = _ffn(_get_rows(rows_scr), mod_ref, 0, g_ref[0:1], g_ref[1:2], w1, w3, w2)


def _ffn_kernel(x_ref, mod_ref, g_ref, w1, w3, w2, o_ref):
    for r in range(x_ref.shape[0] // TM):
        rows = slice(r * TM, (r + 1) * TM)
        o_ref[rows, :] = _ffn(x_ref[rows, :], mod_ref, 0, g_ref[0:1], g_ref[1:2], w1, w3, w2)


def _ffn_first(x, ctx, pos, mods, g, w1, w3, w2):
    lat = lambda t: jnp.minimum(t, NT_LAT - 1)
    return pl.pallas_call(
        _with_staged_weights(_ffn_first_kernel, 8, _ffn_views(0, 0)),
        grid=(NT,),
        in_specs=[pl.BlockSpec((B, TS, D), lambda t: (0, lat(t), 0)),
                  pl.BlockSpec((B, TS, D), lambda t: (0, jnp.maximum(t - NT_LAT, 0), 0)),
                  pl.BlockSpec((TS, D), lambda t: (lat(t), 0)),
                  _mod_spec(0), _resident((6, D))] + _WEIGHT_SPECS,
        out_specs=_row_spec(),
        out_shape=jax.ShapeDtypeStruct((ROWS, D), F32),
        scratch_shapes=[_ROWS_SCRATCH] + _WEIGHT_SCRATCH,
        compiler_params=_cparams(("arbitrary",)),
        name="ffn_first",
    )(x, ctx, pos, mods, g, w1, w3, w2)


def _ffn_plain(x, mods, g, w1, w3, w2, layer):
    return pl.pallas_call(
        _with_staged_weights(_ffn_kernel, 6, _ffn_views(layer, 0)),
        grid=(NT // 2,),
        in_specs=[pl.BlockSpec((2 * TM, D), lambda t: (t, 0)),
                  _mod_spec(layer, NT_LAT // 2), _resident((6, D))] + _WEIGHT_SPECS,
        out_specs=pl.BlockSpec((2 * TM, D), lambda t: (t, 0)),
        out_shape=jax.ShapeDtypeStruct((ROWS, D), F32),
        scratch_shapes=_WEIGHT_SCRATCH,
        compiler_params=_cparams(("arbitrary",)),
        name="ffn_pre",
    )(x, mods, g, w1, w3, w2)


def _mix_ffn_core(y, x, mod_ref, g_ref, wo_ref, bo_ref, w1, w3, w2, **ffn_kwargs):
    z = _dot(y.astype(BF16), wo_ref[...]) + bo_ref[...]
    x = x + _gated(_rmsnorm(z, g_ref[3:4]), _mod(mod_ref, 5))
    return _ffn(x, mod_ref, 6, g_ref[4:5], g_ref[5:6], w1, w3, w2, **ffn_kwargs)


def _hy_out_ffn_kernel(ylo_ref, yhi_ref, x_ref, mod_ref, g_ref, bo_ref, wo_ref, w1, w3, w2,
                       o_ref, rows_scr):
    halves = (ylo_ref, yhi_ref)
    for b in range(B):
        for j in range(D // LANES):
            col = (b * (D // TC) + j // 2) * LANES
            rows_scr[j, _rows_of_batch(b), :] = halves[j % 2][:, col:col + LANES]
    o_ref[...] = _mix_ffn_core(_get_rows(rows_scr), x_ref[...], mod_ref, g_ref, wo_ref, bo_ref,
                               w1, w3, w2)


def _hy_out_ffn(y_lo, y_hi, x, mods, g, w_out, b_out, w1, w3, w2):
    half = pl.BlockSpec((TS, B * D // 2), lambda t: (t, 0))
    return pl.pallas_call(
        _with_staged_weights(_hy_out_ffn_kernel, 10, [_first] + _ffn_views(0, 1)),
        grid=(NT,),
        in_specs=[half, half, _row_spec(), _mod_spec(0), _resident((6, D)), _resident((1, D)),
                  _HBM] + _WEIGHT_SPECS,
        out_specs=_row_spec(),
        out_shape=jax.ShapeDtypeStruct((ROWS, D), F32),
        scratch_shapes=[_ROWS_SCRATCH, pltpu.VMEM((D, D), BF16)] + _WEIGHT_SCRATCH,
        compiler_params=_cparams(("arbitrary",)),
        name="hyena_out_ffn",
    )(y_lo, y_hi, x, mods, g, b_out, w_out, w1, w3, w2)


def _seq_first(t):
    return jnp.logical_or(t == 0, t == NI_LAT)


def _seq_last(t):
    return jnp.logical_or(t == NI_LAT - 1, t == NI - 1)


def _halo_specs(order, lo_rows, hi_rows):
    nhi = ROWS // hi_rows
    return [
        pl.BlockSpec((lo_rows, D), lambda s: (jnp.maximum(order(s) * (TI // lo_rows) - 1, 0), 0)),
        pl.BlockSpec((TI, D), lambda s: (order(s), 0)),
        pl.BlockSpec((hi_rows, D),
                     lambda s: (jnp.minimum((order(s) + 1) * (TI // hi_rows), nhi - 1), 0)),
    ]


def _hy_in_kernel(xp_ref, x_ref, xn_ref, mod_ref, g_ref, b_ref, cw_ref, cb_ref, w_ref,
                  x0_ref, p_ref, x0_scr, p_scr):
    t = pl.program_id(0)
    xa = jnp.concatenate([xp_ref[...], x_ref[...], xn_ref[...]], axis=0)
    h = _modulate(_rmsnorm(xa, g_ref[2:3]), _mod(mod_ref, 3), _mod(mod_ref, 4)).astype(BF16)
    keep_lo = jnp.where(_seq_first(t), 0.0, 1.0).astype(F32)
    keep_hi = jnp.where(_seq_last(t), 0.0, 1.0).astype(F32)

    def conv_cols(c0):
        cols = slice(c0, c0 + MXU_N)
        u = _dot(h, w_ref[:, cols]) + b_ref[:, cols]
        lo = jnp.concatenate([u[0:B] * keep_lo, u[B:TI]], axis=0)
        hi = jnp.concatenate([u[2 * B:TI + B], u[TI + B:TI + 2 * B] * keep_hi], axis=0)
        return (cb_ref[:, cols] + cw_ref[0:1, cols] * lo + cw_ref[1:2, cols] * u[B:TI + B]
                + cw_ref[2:3, cols] * hi)

    slabs = MXU_N // LANES
    for c in range(D // MXU_N):
        c0 = c * MXU_N
        vals = (conv_cols(c0), conv_cols(D + c0) * conv_cols(2 * D + c0))
        for out_ref, scr, val in zip((x0_ref, p_ref), (x0_scr, p_scr), vals):
            for j in range(slabs):
                scr[c * slabs + j] = val[:, j * LANES:(j + 1) * LANES]
            for b in range(B):
                out_ref[:, b * D + c0:b * D + c0 + MXU_N] = jnp.concatenate(
                    [scr[c * slabs + j, pl.ds(b, TI // B, stride=B), :] for j in range(slabs)], axis=1)


def _hy_in(x, mods, g, w_in, b_in, conv_w, conv_b):
    wide = pl.BlockSpec((TI // B, B * D), lambda t: (t, 0))
    rows_scratch = pltpu.VMEM((D // LANES, TI, LANES), F32)
    return pl.pallas_call(
        _with_staged_weights(_hy_in_kernel, 9, [_first]),
        grid=(NI,),
        in_specs=_halo_specs(lambda s: s, B, B) + [
            _mod_spec(0, NI_LAT), _resident((6, D)), _resident((1, 3 * D)),
            _resident((3, 3 * D)), _resident((1, 3 * D)), _HBM],
        out_specs=[wide, wide],
        out_shape=[jax.ShapeDtypeStruct((ROWS // B, B * D), F32)] * 2,
        scratch_shapes=[rows_scratch, rows_scratch, pltpu.VMEM((D, 3 * D), BF16)],
        compiler_params=_cparams(("arbitrary",)),
        name="hyena_in",
    )(x, x, x, mods, g, b_in, conv_w, conv_b, w_in)


def _filter_kernel(zt_ref, t_ref, fw0t, fb0, fw1t, fb1, fw2t, fb2, freq, fwout, deltas,
                   hf_ref, hb_ref):
    h = jnp.sin(freq[:, 0:1] * (_dot3(fw0t[...], zt_ref[...]) + fb0[...]))
    h = jnp.sin(freq[:, 1:2] * (_dot3(fw1t[...], h) + fb1[...]))
    h = jnp.sin(freq[:, 2:3] * (_dot3(fw2t[...], h) + fb2[...]))
    filt = _dot3(h.T, fwout[...])
    decay = jnp.exp(-t_ref[:, 0:1] * deltas[...])
    hf_ref[...] = filt[:, 0:D] * decay
    hb_ref[...] = filt[:, D:2 * D] * decay


def _filters(n, fw0, fb0, fw1, fb1, fw2, fb2, freq, fwout):
    t = np.linspace(0.0, 1.0, n)[:, None]
    bands = np.linspace(1e-4, HY_BANDS - 1, HY_BANDS)[None]
    phase = bands * (2.0 * math.pi * np.arange(n)[:, None] / n)
    zp = np.zeros((n, LANES), np.float32)
    zp[:, :HY_EMB] = np.concatenate([t, np.cos(phase), -np.sin(phase)], axis=-1)
    zp = np.concatenate([zp[r::4] for r in range(4)], axis=0)
    fw0t = jnp.zeros((HY_HID, LANES), F32).at[:, :HY_EMB].set(fw0.T)
    max_decay = math.log(HY_DECAY_TARGET) / HY_FAST_DECAY
    min_decay = math.log(HY_DECAY_TARGET) / HY_SLOW_DECAY
    deltas = np.abs(np.linspace(min_decay, max_decay, D))[None].astype(np.float32)
    tl = 256
    row = lambda i: (i, 0)
    unit = _resident((HY_HID, 1))
    return pl.pallas_call(
        _filter_kernel,
        grid=(n // tl,),
        in_specs=[pl.BlockSpec((LANES, tl), lambda i: (0, i)), pl.BlockSpec((tl, LANES), row),
                  _resident((HY_HID, LANES)), unit, _resident((HY_HID, HY_HID)), unit,
                  _resident((HY_HID, HY_HID)), unit, _resident((HY_HID, 3)),
                  _resident((HY_HID, 2 * D)), _resident((1, D))],
        out_specs=[pl.BlockSpec((tl, D), row)] * 2,
        out_shape=[jax.ShapeDtypeStruct((n, D), F32)] * 2,
        compiler_params=_cparams(("arbitrary",)),
        name="hyena_filter",
    )(np.ascontiguousarray(zp.T), zp, fw0t, fb0[:, None], fw1.T, fb1[:, None], fw2.T, fb2[:, None],
      freq.T, fwout, deltas)


def _alt_sign(rows, cols):
    r = lax.broadcasted_iota(jnp.int32, (rows, cols), 0)
    return (1 - 2 * (r & 1)).astype(F32)


_N_GROUPS = 4
_ROOT_HALF = math.sqrt(0.5)


def _radix4_tables(g):
    k = np.arange(g, dtype=np.int64)
    theta = ((k[:, None] * k[None, :]) % (2 * g)) * (math.pi / g)
    phase = k[:, None] * (math.pi / (4 * g))
    fwd = [f(theta + r * phase) for r in range(4) for f in (np.cos, np.sin)]
    as_bf16 = lambda blocks: jnp.asarray(np.concatenate(blocks, axis=0), F32).astype(BF16)
    return as_bf16(fwd), as_bf16([b.T for b in fwd])


def _cmul(ar, as_, br, bi):
    return ar * br + as_ * bi, as_ * br - ar * bi


def _table_rows(tab_ref, r, rows, g):
    count = rows.stop - rows.start
    return (tab_ref[pl.ds(2 * r * g + rows.start, count), :],
            tab_ref[pl.ds((2 * r + 1) * g + rows.start, count), :])


def _fwd4(quarters, tab_ref, rows, g):
    ts = []
    for r, q in enumerate(quarters):
        c, s = _table_rows(tab_ref, r, rows, g)
        ts.append((_dot(c, q), _dot(s, q)))
    (t0r, t0s), (t1r, t1s), (t2r, t2s), (t3r, t3s) = ts
    er, es, fr, fs = t0r + t2r, t0s + t2s, t0r - t2r, t0s - t2s
    pr, ps, dr, ds = t1r + t3r, t1s + t3s, t1r - t3r, t1s - t3s
    return ((er + pr, es + ps), (er - pr, es - ps), (fr + ds, fs - dr), (fr - ds, fs + dr))


def _mid_freqs(sums):
    s0, s1, s2, s3 = sums
    a, b = _ROOT_HALF * (s1 - s3), _ROOT_HALF * (s1 + s3)
    return (s0 + a, s2 + b), (s0 - a, b - s2)


def _alt_sums(quarters):
    alt = _alt_sign(*quarters[0].shape)
    return [jnp.sum(q * alt, axis=0, keepdims=True) for q in quarters]


def _spectrum_kernel(hf_ref, hb_ref, tab_ref, kr_ref, ki_ref, kn_ref, *, g):
    hf = hf_ref[...]
    row = lax.broadcasted_iota(jnp.int32, hf.shape, 0)
    hb = jnp.where(row == 0, 0.0, hb_ref[...])
    cos_part = hf + hb
    sin_part = hb - hf
    cq = [cos_part[r * g:(r + 1) * g] for r in range(4)]
    sq = [sin_part[r * g:(r + 1) * g] for r in range(4)]
    rows = slice(0, g)
    groups_c = _fwd4([q.astype(BF16) for q in cq], tab_ref, rows, g)
    groups_s = _fwd4([q.astype(BF16) for q in sq], tab_ref, rows, g)
    n_fft = 8 * g
    k = lax.broadcasted_iota(jnp.int32, (g, hf.shape[1]), 0)
    scale = jnp.where(k == 0, 1.0 / n_fft, 2.0 / n_fft)
    for grp in range(_N_GROUPS):
        kr_ref[grp * g:(grp + 1) * g, :] = groups_c[grp][0] * scale
        ki_ref[grp * g:(grp + 1) * g, :] = groups_s[grp][1] * scale
    (cg, _), (c3g, _) = _mid_freqs(_alt_sums(cq))
    (_, sg), (_, s3g) = _mid_freqs(_alt_sums(sq))
    mids = [v * (2.0 / n_fft) for v in (cg, sg, c3g, s3g)]
    kn_ref[...] = jnp.concatenate(mids + [jnp.zeros((B - 4, hf.shape[1]), F32)], axis=0)


def _spectrum(hf, hb, fwd_table):
    n = hf.shape[0]
    g = n // 4
    col = lambda j: (0, j)
    return pl.pallas_call(
        functools.partial(_spectrum_kernel, g=g),
        grid=(D // TC,),
        in_specs=[pl.BlockSpec((n, TC), col)] * 2 + [_resident((8 * g, g))],
        out_specs=[pl.BlockSpec((n, TC), col)] * 2 + [pl.BlockSpec((B, TC), col)],
        out_shape=[jax.ShapeDtypeStruct((n, D), F32)] * 2 + [jax.ShapeDtypeStruct((B, D), F32)],
        compiler_params=_cparams(("arbitrary",)),
        name="hyena_spectrum",
    )(hf, hb, fwd_table)


def _steps(refs, first, count):
    rows = pl.ds(first, count, stride=4)
    return jnp.concatenate([r[rows, :] for r in refs], axis=1)


def _long_conv_rows(t0, g, p_refs, x0_refs, spec, bias_ref, o_refs, scratch):
    kr_ref, ki_ref, kn_ref, fwd_ref, inv_ref = spec
    q_scr, u_scr = scratch
    tk = min(g, 512)
    r0 = t0 // 4
    seq = slice(r0, r0 + g)
    chunks = [slice(k * tk, (k + 1) * tk) for k in range(g // tk)]
    quarters = [_steps(p_refs, t0 + r, g) for r in range(4)]
    tc = quarters[0].shape[1]
    for r in range(4):
        q_scr[r, seq, :] = quarters[r].astype(BF16)
    for rows in chunks:
        dst = slice(r0 + rows.start, r0 + rows.stop)
        groups = _fwd4([q_scr[r, seq, :] for r in range(4)], fwd_ref, rows, g)
        ys = []
        for grp, (xr, xs) in enumerate(groups):
            k_rows = pl.ds(grp * g + rows.start, tk)
            ys.append(_cmul(xr, xs, kr_ref[k_rows, :], ki_ref[k_rows, :]))
        (y1r, y1s), (y2r, y2s), (y3r, y3s), (y4r, y4s) = ys
        pr, ps, mr, ms = y1r + y2r, y1s + y2s, y1r - y2r, y1s - y2s
        qr, qs, nr, ns = y3r + y4r, y3s + y4s, y3r - y4r, y3s - y4s
        us = [(pr + qr, ps + qs), (mr - ns, ms + nr), (pr - qr, ps - qs), (mr + ns, ms - nr)]
        for r, (ur, us_) in enumerate(us):
            u_scr[2 * r, dst, :] = ur.astype(BF16)
            u_scr[2 * r + 1, dst, :] = us_.astype(BF16)
    (xgr, xgs), (x3r, x3s) = _mid_freqs(_alt_sums(quarters))
    ygr, ygs = _cmul(xgr, xgs, kn_ref[0:1, :], kn_ref[1:2, :])
    y3r_, y3s_ = _cmul(x3r, x3s, kn_ref[2:3, :], kn_ref[3:4, :])
    a = _ROOT_HALF
    mids = [ygr + y3r_, a * (ygr + ygs - y3r_ + y3s_), ygs - y3s_, a * (ygs - ygr + y3r_ + y3s_)]
    alt_chunk = _alt_sign(tk, tc)
    bias = bias_ref[...]
    for rows in chunks:
        for r in range(4):
            first = t0 + 4 * rows.start + r
            c, s = _table_rows(inv_ref, r, rows, g)
            y = (_dot(c, u_scr[2 * r, seq, :]) + _dot(s, u_scr[2 * r + 1, seq, :])
                 + alt_chunk * mids[r])
            out = _steps(x0_refs, first, tk) * (y + _steps(p_refs, first, tk) * bias)
            for q, o_ref in enumerate(o_refs):
                o_ref[pl.ds(first, tk, stride=4), :] = out[:, q * LANES:(q + 1) * LANES]


_N_SPEC = 5


def _long_conv_kernel(*refs):
    p_refs, x0_refs, bias_ref = refs[0:2], refs[2:4], refs[4]
    lat, ctx = refs[5:5 + _N_SPEC], refs[5 + _N_SPEC:5 + 2 * _N_SPEC]
    o_refs, scratch = refs[5 + 2 * _N_SPEC:7 + 2 * _N_SPEC], refs[7 + 2 * _N_SPEC:]
    _long_conv_rows(0, L // 4, p_refs, x0_refs, lat, bias_ref, o_refs, scratch)
    _long_conv_rows(L, CTX // 4, p_refs, x0_refs, ctx, bias_ref, o_refs, scratch)


def _long_conv(p2, x02, bias, spec_lat, spec_ctx):
    nc = D // TC
    halves = [pl.BlockSpec((ROWS // B, LANES),
                           functools.partial(lambda q, j: (0, 2 * ((j % B) * nc + j // B) + q), q))
              for q in range(TC // LANES)]
    ch = lambda j: (0, j // B)

    def spec_specs(g):
        return ([pl.BlockSpec((4 * g, TC), ch)] * 2 + [pl.BlockSpec((B, TC), ch)]
                + [_resident((8 * g, g))] * 2)

    half_out = pl.BlockSpec((ROWS // B, LANES), lambda j: (0, (j % B) * nc + j // B))
    quarter_rows = ROWS // (4 * B)
    return pl.pallas_call(
        _long_conv_kernel,
        grid=(B * D // TC,),
        scratch_shapes=[pltpu.VMEM((4, quarter_rows, TC), BF16),
                        pltpu.VMEM((8, quarter_rows, TC), BF16)],
        in_specs=halves + halves + [pl.BlockSpec((1, TC), ch)] + spec_specs(L // 4)
        + spec_specs(CTX // 4),
        out_specs=[half_out] * (TC // LANES),
        out_shape=[jax.ShapeDtypeStruct((ROWS // B, B * D * LANES // TC), F32)] * (TC // LANES),
        compiler_params=_cparams(("arbitrary",)),
        name="hyena_long_conv",
    )(p2, p2, x02, x02, bias, *spec_lat, *spec_ctx)


def _gelu_tanh(x):
    return x * (0.5 * (1.0 + jnp.tanh(math.sqrt(2.0 / math.pi) * (x + 0.044715 * (x * x * x)))))


_TINY = 1e-30


def _rg_coeffs(xc, hd, wai_ref, bai_ref, lam_ref, a_scr, b_scr):
    sl = slice(hd * RG_BLOCK, (hd + 1) * RG_BLOCK)
    lam = lam_ref[:, sl]
    softplus_neg = jnp.maximum(-lam, 0.0) + jnp.log1p(jnp.exp(-jnp.abs(lam)))
    rate = (-RG_C * math.log2(math.e)) * softplus_neg
    pre = _dot(xc.astype(BF16), wai_ref[hd]) + bai_ref[:, 2 * hd * RG_BLOCK:2 * (hd + 1) * RG_BLOCK]
    gates = _sigmoid(pre)
    a = jnp.exp2(gates[:, 0:RG_BLOCK] * rate)
    a_scr[:, sl] = a
    v = (1.0 - a) * (1.0 + a)
    root = v * lax.rsqrt(jnp.maximum(v, _TINY))
    b_scr[:, sl] = root * gates[:, RG_BLOCK:2 * RG_BLOCK] * xc


def _scan_tile(a_scr, b_scr, h_scr, emit, reverse):
    steps = a_scr.shape[0] // B

    def body(k, h):
        t = steps - 1 - k if reverse else k
        r0 = pl.multiple_of(t * B, B)
        h = a_scr[pl.ds(r0, B), :] * h + b_scr[pl.ds(r0, B), :]
        emit(r0, h)
        return h

    h_scr[...] = lax.fori_loop(0, steps, body, h_scr[...], unroll=8)


def _rg_fwd_order(s):
    return jnp.where(s < NI_CTX, NI_LAT + s, s - NI_CTX)


def _rg_in_kernel(xp_ref, x_ref, xn_ref, mod_ref, g_ref, b_ref, cw_ref, cb_ref,
                  wai_ref, bai_ref, lam_ref, w_ref, xc_ref, gate_ref, hs_ref, a_scr, b_scr, h_scr):
    s = pl.program_id(0)
    t = _rg_fwd_order(s)
    xa = jnp.concatenate([xp_ref[...], x_ref[...], xn_ref[...]], axis=0)
    h = _modulate(_rmsnorm(xa, g_ref[2:3]), _mod(mod_ref, 3), _mod(mod_ref, 4)).astype(BF16)
    keep_lo = jnp.where(_seq_first(t), 0.0, 1.0).astype(F32)
    keep_hi = jnp.where(_seq_last(t), 0.0, 1.0).astype(F32)
    for hd in range(RG_HEADS):
        sl = slice(hd * RG_BLOCK, (hd + 1) * RG_BLOCK)
        rec = slice(D + hd * RG_BLOCK, D + (hd + 1) * RG_BLOCK)
        gate_ref[:, sl] = (_dot(h[B:TI + B], w_ref[:, sl]) + b_ref[:, sl]).astype(BF16)
        u = _dot(h, w_ref[:, rec]) + b_ref[:, rec]
        taps = (jnp.concatenate([u[0:B] * keep_lo, u[B:TI]], axis=0),
                u[B:TI + B],
                jnp.concatenate([u[2 * B:TI + B], u[TI + B:TI + 2 * B] * keep_hi], axis=0),
                jnp.concatenate([u[3 * B:TI + B], u[TI + B:TI + 3 * B] * keep_hi], axis=0))
        xc = cb_ref[:, sl]
        for k, tap in enumerate(taps):
            xc = xc + cw_ref[k:k + 1, sl] * tap
        xc_ref[:, sl] = xc
        _rg_coeffs(xc, hd, wai_ref, bai_ref, lam_ref, a_scr, b_scr)

    @pl.when(s == 0)
    def _():
        h_scr[...] = jnp.zeros((B, D), F32)

    def emit(r0, hv):
        hs_ref[pl.ds(r0, B), :] = hv

    _scan_tile(a_scr, b_scr, h_scr, emit, reverse=False)


def _rg_gate_specs():
    return [_resident((RG_HEADS, RG_BLOCK, 2 * RG_BLOCK)), _resident((1, 2 * D)), _resident((1, D))]


def _rg_in(x, mods, g, w_in, b_in, conv_w, conv_b, wai, bai, lam):
    order = _rg_fwd_order
    mod_spec = _mod_spec(1, NI_LAT, order)
    out_spec = pl.BlockSpec((TI, D), lambda s: (order(s), 0))
    return pl.pallas_call(
        _with_staged_weights(_rg_in_kernel, 12, [_first]),
        grid=(NI,),
        in_specs=_halo_specs(order, B, 2 * B) + [
            mod_spec, _resident((6, D)), _resident((1, 2 * D)),
            _resident((4, D)), _resident((1, D))] + _rg_gate_specs() + [_HBM],
        out_specs=[out_spec] * 3,
        out_shape=[jax.ShapeDtypeStruct((ROWS, D), F32), jax.ShapeDtypeStruct((ROWS, D), BF16),
                   jax.ShapeDtypeStruct((ROWS, D), F32)],
        scratch_shapes=[pltpu.VMEM((TI, D), F32), pltpu.VMEM((TI, D), F32),
                        pltpu.VMEM((B, D), F32), pltpu.VMEM((D, 2 * D), BF16)],
        compiler_params=_cparams(("arbitrary",)),
        name="rglru_in_fwd_scan",
    )(x, x, x, mods, g, b_in, conv_w, conv_b, wai, bai, lam, w_in)


def _rg_tile_coeffs(xc_ref, wai_ref, bai_ref, lam_ref, a_scr, b_scr):
    for hd in range(RG_HEADS):
        xc = xc_ref[:, hd * RG_BLOCK:(hd + 1) * RG_BLOCK]
        _rg_coeffs(xc, hd, wai_ref, bai_ref, lam_ref, a_scr, b_scr)


def _rg_ctx_bwd_kernel(xc_ref, wai_ref, bai_ref, lam_ref, h_ref, a_scr, b_scr, h_scr):
    _rg_tile_coeffs(xc_ref, wai_ref, bai_ref, lam_ref, a_scr, b_scr)

    @pl.when(pl.program_id(0) == 0)
    def _():
        h_scr[...] = jnp.zeros((B, D), F32)

    _scan_tile(a_scr, b_scr, h_scr, lambda r0, hv: None, reverse=True)
    h_ref[...] = h_scr[...]


def _rg_ctx_bwd(xc, wai, bai, lam):
    return pl.pallas_call(
        _rg_ctx_bwd_kernel,
        grid=(NT_CTX,),
        in_specs=[pl.BlockSpec((TM, D), lambda s: (NT - 1 - s, 0))] + _rg_gate_specs(),
        out_specs=pl.BlockSpec((B, D), lambda s: (0, 0)),
        out_shape=jax.ShapeDtypeStruct((B, D), F32),
        scratch_shapes=[pltpu.VMEM((TM, D), F32), pltpu.VMEM((TM, D), F32),
                        pltpu.VMEM((B, D), F32)],
        compiler_params=_cparams(("arbitrary",)),
        name="rglru_ctx_bwd_scan",
    )(xc, wai, bai, lam)


def _rg_out_ffn_kernel(xc_ref, gate_ref, hs_ref, x_ref, h0_ref, mod_ref, g_ref,
                       wai_ref, bai_ref, lam_ref, bo_ref, wo_ref, w1, w3, w2,
                       o_ref, a_scr, b_scr, hb_scr, h_scr, rows_scr):
    s = pl.program_id(0)

    def head_gates(hd):
        xc = xc_ref[:, hd * RG_BLOCK:(hd + 1) * RG_BLOCK]
        _rg_coeffs(xc, hd, wai_ref, bai_ref, lam_ref, a_scr, b_scr)

    @pl.when(s == 0)
    def _():
        h_scr[...] = h0_ref[...]
        for hd in range(RG_HEADS):
            head_gates(hd)

    @pl.when(s > 0)
    def _():
        y = (hs_ref[...] + hb_scr[...]) * _gelu_tanh(gate_ref[...].astype(F32))
        res = _mix_ffn_core(
            y, x_ref[...], mod_ref, g_ref, wo_ref, bo_ref, w1, w3, w2, chunks=F_QUARTERS,
            side_work=[functools.partial(head_gates, hd) for hd in range(RG_HEADS)])
        _put_rows(rows_scr, res)
        for b in range(B):
            o_ref[b] = _get_batch(rows_scr, b)

    @pl.when(s < NT_LAT)
    def _():
        def emit(r0, hv):
            hb_scr[pl.ds(r0, B), :] = hv

        _scan_tile(a_scr, b_scr, h_scr, emit, reverse=True)


def _rg_out_ffn(xc, gate, hs, x, h0, mods, g, wai, bai, lam, w_out, b_out, w1, w3, w2):
    scan_tile = lambda s: (jnp.maximum(NT_LAT - 1 - s, 0), 0)
    out_tile = lambda s: jnp.minimum(NT_LAT - s, NT_LAT - 1)
    prev = pl.BlockSpec((TM, D), lambda s: (out_tile(s), 0))
    return pl.pallas_call(
        _with_staged_weights(_rg_out_ffn_kernel, 15, [_first] + _ffn_views(1, 1)),
        grid=(NT_LAT + 1,),
        in_specs=[pl.BlockSpec((TM, D), scan_tile), prev, prev, prev, _resident((B, D)),
                  _mod_spec(1, NT_LAT + 1), _resident((6, D))]
        + _rg_gate_specs() + [_resident((1, D)), _HBM] + _WEIGHT_SPECS,
        out_specs=pl.BlockSpec((B, TS, D), lambda s: (0, out_tile(s), 0)),
        out_shape=jax.ShapeDtypeStruct((B, L, D), F32),
        scratch_shapes=[pltpu.VMEM((TM, D), F32), pltpu.VMEM((TM, D), F32),
                        pltpu.VMEM((TM, D), F32), pltpu.VMEM((B, D), F32), _ROWS_SCRATCH,
                        pltpu.VMEM((D, D), BF16)] + _WEIGHT_SCRATCH,
        compiler_params=_cparams(("arbitrary",)),
        name="rglru_out_ffn",
    )(xc, gate, hs, x, h0, mods, g, wai, bai, lam, b_out, w_out, w1, w3, w2)


def _grid_pos():
    rows = L // GRID_W
    quarter = D // 4
    omega = POS_BASE ** (-np.arange(quarter) / quarter)

    def emb(q):
        ang = q[:, None] * omega[None]
        return np.concatenate([np.sin(ang), np.cos(ang)], axis=-1)

    row_code = np.repeat(emb(np.arange(rows)), GRID_W, axis=0)
    col_code = np.tile(emb(np.arange(GRID_W)), (rows, 1))
    return jnp.asarray(np.concatenate([row_code, col_code], axis=-1), F32)


def kernel(x, c, ctx, c_ctx, ada_w, ada_b, norm_g, ffn_w1, ffn_w3, ffn_w2, hy_w_in, hy_b_in, hy_conv_w, hy_conv_b, hy_fw0, hy_fb0, hy_fw1, hy_fb1, hy_fw2, hy_fb2, hy_freq, hy_fwout, hy_filt_bias, hy_w_out, hy_b_out, rg_w_in, rg_b_in, rg_conv_w, rg_conv_b, rg_wa, rg_ba, rg_wi, rg_bi, rg_lam, rg_w_out, rg_b_out):
    mods = _mods(c, c_ctx, ada_w, ada_b).reshape(DEPTH, 2, B, N_MOD * D)
    w1, w3, w2 = ffn_w1, ffn_w3, ffn_w2

    g = norm_g[0]
    xs = _ffn_first(x, ctx, _grid_pos(), mods, g, w1, w3, w2)
    x0, p = _hy_in(xs, mods, g, hy_w_in, hy_b_in[0][None],
                   hy_conv_w[0], hy_conv_b[0][None])
    fparams = (hy_fw0[0], hy_fb0[0], hy_fw1[0], hy_fb1[0], hy_fw2[0], hy_fb2[0],
               hy_freq[0], hy_fwout[0])
    specs = []
    for n in (L, CTX):
        hf, hb = _filters(n, *fparams)
        fwd_table, inv_table = _radix4_tables(n // 4)
        specs.append(list(_spectrum(hf, hb, fwd_table)) + [fwd_table, inv_table])
    y_lo, y_hi = _long_conv(p, x0, hy_filt_bias[0][None], *specs)
    xs = _hy_out_ffn(y_lo, y_hi, xs, mods, g, hy_w_out, hy_b_out[0][None],
                     w1, w3, w2)

    g = norm_g[1]
    xs = _ffn_plain(xs, mods, g, w1, w3, w2, layer=1)
    wai = jnp.concatenate([rg_wa[0], rg_wi[0]], axis=-1).astype(BF16)
    per_head = lambda v: v.reshape(2, RG_HEADS, RG_BLOCK)
    bai = jnp.concatenate([per_head(rg_ba[0]), per_head(rg_bi[0])], axis=-1).reshape(2, 1, 2 * D)
    lam = rg_lam[0][:, None, :]
    xc, gate, hs = _rg_in(xs, mods, g, rg_w_in, rg_b_in[0][None],
                          rg_conv_w[0], rg_conv_b[0][None], wai[0], bai[0], lam[0])
    h_ctx = _rg_ctx_bwd(xc, wai[1], bai[1], lam[1])
    return _rg_out_ffn(xc, gate, hs, xs, h_ctx, mods, g, wai[1], bai[1], lam[1],
                       rg_w_out, rg_b_out[0][None], w1, w3, w2)
```

```python
import functools
import math

import jax
import jax.numpy as jnp
import numpy as np
from jax import lax
from jax.experimental import pallas as pl
from jax.experimental.pallas import tpu as pltpu

F32 = jnp.float32
BF16 = jnp.bfloat16

D = 1024
B = 8
LANES = 128
MXU_N = 256
L = 2048
CTX = 256
DEPTH = 2
GRID_W = 64
D_FF = 2816
N_MOD = 9
MACARON = 0.5
NORM_EPS = 1e-6
POS_BASE = 10000.0
HY_EMB = 33
HY_BANDS = 16
HY_HID = 64
HY_FAST_DECAY = 0.3
HY_SLOW_DECAY = 1.5
HY_DECAY_TARGET = 1e-2
RG_HEADS = 4
RG_BLOCK = D // RG_HEADS
RG_C = 8.0

ROWS_LAT = L * B
ROWS_CTX = CTX * B
ROWS = ROWS_LAT + ROWS_CTX
TM = 512
TS = TM // B
NT_LAT = ROWS_LAT // TM
NT_CTX = ROWS_CTX // TM
NT = NT_LAT + NT_CTX
TI = 1024
NI_LAT = ROWS_LAT // TI
NI_CTX = ROWS_CTX // TI
NI = NI_LAT + NI_CTX
F_CHUNKS = ((0, 1536), (1536, D_FF))
F_QUARTERS = ((0, 768), (768, 1536), (1536, 2304), (2304, D_FF))
TC = 256
VMEM_LIMIT = 58 * 1024 * 1024


def _cparams(sem):
    return pltpu.CompilerParams(dimension_semantics=sem, vmem_limit_bytes=VMEM_LIMIT)


def _resident(shape):
    nd = len(shape)
    return pl.BlockSpec(shape, lambda *_: (0,) * nd, pipeline_mode=pl.Buffered(1))


def _split(a):
    hi = a.astype(BF16)
    lo = (a - hi.astype(F32)).astype(BF16)
    return hi, lo


def _dot(a, b):
    return jnp.dot(a, b, preferred_element_type=F32)


def _dot3(a, b):
    ah, al = _split(a)
    bh, bl = _split(b)
    return _dot(ah, bh) + _dot(ah, bl) + _dot(al, bh)


def _rmsnorm(x, g):
    ms = jnp.mean(x * x, axis=-1, keepdims=True)
    return x * lax.rsqrt(ms + NORM_EPS) * g


def _mod(mod_ref, k):
    return mod_ref[:, k * D:(k + 1) * D]


def _per_batch(x, fn):
    rows = x.shape[0]
    return fn(x.reshape(rows // B, B, x.shape[1])).reshape(rows, x.shape[1])


def _norm_modulate(x, g, shift8, scale8):
    ms = jnp.mean(x * x, axis=-1, keepdims=True)
    factor = g * (1.0 + scale8)
    return _per_batch(x * lax.rsqrt(ms + NORM_EPS), lambda v: v * factor[None] + shift8[None])


def _gated(z, gate8):
    return _per_batch(z, lambda v: v * gate8[None])


def _sigmoid(x):
    return jax.nn.sigmoid(x)


def _ffn(x, mod_ref, k0, g_pre, g_post, w1_ref, w3_ref, w2_ref, chunks=F_CHUNKS, side_work=()):
    h = _norm_modulate(x, g_pre, _mod(mod_ref, k0), _mod(mod_ref, k0 + 1)).astype(BF16)
    y = None
    for idx, (c0, c1) in enumerate(chunks):
        a = _dot(h, w1_ref[:, c0:c1])
        b = _dot(h, w3_ref[:, c0:c1])
        if idx < len(side_work):
            side_work[idx]()
        act = (a * _sigmoid(a) * b).astype(BF16)
        part = _dot(act, w2_ref[c0:c1, :])
        y = part if y is None else y + part
    return x + MACARON * _gated(_rmsnorm(y, g_post), _mod(mod_ref, k0 + 2))


def _mods_kernel(c_ref, cctx_ref, w_ref, b_ref, o_ref):
    @pl.when(pl.program_id(1) == 0)
    def _():
        o_ref[...] = jnp.broadcast_to(b_ref[...], o_ref.shape)

    a = jnp.concatenate([c_ref[...], jnp.broadcast_to(cctx_ref[...], c_ref.shape)], axis=0)
    ah, al = _split(a * _sigmoid(a))
    wh, wl = _split(w_ref[...])
    r = _dot(jnp.concatenate([ah, al], axis=0), wh)
    o_ref[...] += r[0:2 * B] + r[2 * B:4 * B] + _dot(ah, wl)


def _mods(c, c_ctx, ada_w, ada_b):
    tk = 128
    n = N_MOD * D
    return pl.pallas_call(
        _mods_kernel,
        grid=(DEPTH, D // tk),
        in_specs=[
            pl.BlockSpec((B, tk), lambda i, k: (0, k)),
            pl.BlockSpec((1, tk), lambda i, k: (0, k)),
            pl.BlockSpec((None, tk, n), lambda i, k: (i, k, 0)),
            pl.BlockSpec((None, 1, n), lambda i, k: (i, 0, 0)),
        ],
        out_specs=pl.BlockSpec((None, 2 * B, n), lambda i, k: (i, 0, 0)),
        out_shape=jax.ShapeDtypeStruct((DEPTH, 2 * B, n), F32),
        compiler_params=_cparams(("arbitrary", "arbitrary")),
        name="ada_mods",
    )(c, c_ctx[None], ada_w, ada_b.reshape(DEPTH, 1, n))


def _mod_spec(layer, nt_lat=NT_LAT, order=lambda t: t):
    return pl.BlockSpec((None, None, B, N_MOD * D),
                        lambda t: (layer, jnp.where(order(t) < nt_lat, 0, 1), 0, 0))


def _row_spec(width=D):
    return pl.BlockSpec((TM, width), lambda t: (t, 0))


W_CHUNKS = 8
W_SLOTS = 3
_HBM = pl.BlockSpec(memory_space=pl.ANY)
_WEIGHT_SPECS = [_HBM] * 3
_WEIGHT_SCRATCH = [pltpu.VMEM((D, D_FF), BF16), pltpu.VMEM((D, D_FF), BF16),
                   pltpu.VMEM((D_FF, D), BF16)]


def _stage_weights(srcs, dsts):
    chunks = [(src, dst, r0, dst.shape[0] // W_CHUNKS) for src, dst in zip(srcs, dsts)
              for r0 in range(0, dst.shape[0], dst.shape[0] // W_CHUNKS)]
    shapes = sorted({(rows, dst.shape[1]) for _, dst, _, rows in chunks})

    def run(*scoped):
        stages, sems = dict(zip(shapes, scoped[:-1])), scoped[-1]

        def slot(k):
            _, dst, _, rows = chunks[k]
            return stages[(rows, dst.shape[1])].at[k % W_SLOTS]

        def copy(k):
            src, _, r0, rows = chunks[k]
            return pltpu.make_async_copy(src.at[pl.ds(r0, rows), :], slot(k), sems.at[k % W_SLOTS])

        for k in range(W_SLOTS - 1):
            copy(k).start()
        for k, (_, dst, r0, rows) in enumerate(chunks):
            if k + W_SLOTS - 1 < len(chunks):
                copy(k + W_SLOTS - 1).start()
            copy(k).wait()
            dst[pl.ds(r0, rows), :] = slot(k)[...].astype(BF16)

    pl.run_scoped(run, *[pltpu.VMEM((W_SLOTS,) + shape, F32) for shape in shapes],
                  pltpu.SemaphoreType.DMA((W_SLOTS,)))


def _with_staged_weights(body, n_in, views):
    n_w = len(views)

    def kernel(*refs):
        ins, rest = refs[:n_in], refs[n_in:]
        copies = rest[-n_w:]

        @pl.when(pl.program_id(0) == 0)
        def _():
            _stage_weights([view(w) for view, w in zip(views, ins[-n_w:])], copies)

        body(*ins[:-n_w], *copies, *rest[:-n_w])

    return kernel


def _ffn_views(layer, half):
    return [lambda w: w.at[layer, half]] * 3


def _first(w):
    return w.at[0]


def _rows_of_batch(b):
    return pl.ds(b, TS, stride=B)


_ROWS_SCRATCH = pltpu.VMEM((D // LANES, TM, LANES), F32)


def _put_batch(rows_scr, b, val):
    for j in range(D // LANES):
        rows_scr[j, _rows_of_batch(b), :] = val[:, j * LANES:(j + 1) * LANES]


def _get_batch(rows_scr, b):
    return jnp.concatenate([rows_scr[j, _rows_of_batch(b), :] for j in range(D // LANES)], axis=1)


def _put_rows(rows_scr, val):
    for j in range(D // LANES):
        rows_scr[j] = val[:, j * LANES:(j + 1) * LANES]


def _get_rows(rows_scr):
    return jnp.concatenate([rows_scr[j] for j in range(D // LANES)], axis=1)


def _ffn_first_kernel(x_ref, ctx_ref, pos_ref, mod_ref, g_ref, w1, w3, w2, o_ref, rows_scr):
    is_latent = pl.program_id(0) < NT_LAT
    for b in range(B):
        _put_batch(rows_scr, b, jnp.where(is_latent, x_ref[b] + pos_ref[...], ctx_ref[b]))
    o_ref[...] = _ffn(_get_rows(rows_scr), mod_ref, 0, g_ref[0:1], g_ref[1:2], w1, w3, w2)


def _ffn_kernel(x_ref, mod_ref, g_ref, w1, w3, w2, o_ref):
    for r in range(x_ref.shape[0] // TM):
        rows = slice(r * TM, (r + 1) * TM)
        o_ref[rows, :] = _ffn(x_ref[rows, :], mod_ref, 0, g_ref[0:1], g_ref[1:2], w1, w3, w2)


def _ffn_first(x, ctx, pos, mods, g, w1, w3, w2):
    lat = lambda t: jnp.minimum(t, NT_LAT - 1)
    return pl.pallas_call(
        _with_staged_weights(_ffn_first_kernel, 8, _ffn_views(0, 0)),
        grid=(NT,),
        in_specs=[pl.BlockSpec((B, TS, D), lambda t: (0, lat(t), 0)),
                  pl.BlockSpec((B, TS, D), lambda t: (0, jnp.maximum(t - NT_LAT, 0), 0)),
                  pl.BlockSpec((TS, D), lambda t: (lat(t), 0)),
                  _mod_spec(0), _resident((6, D))] + _WEIGHT_SPECS,
        out_specs=_row_spec(),
        out_shape=jax.ShapeDtypeStruct((ROWS, D), F32),
        scratch_shapes=[_ROWS_SCRATCH] + _WEIGHT_SCRATCH,
        compiler_params=_cparams(("arbitrary",)),
        name="ffn_first",
    )(x, ctx, pos, mods, g, w1, w3, w2)


def _ffn_plain(x, mods, g, w1, w3, w2, layer):
    return pl.pallas_call(
        _with_staged_weights(_ffn_kernel, 6, _ffn_views(layer, 0)),
        grid=(NT // 2,),
        in_specs=[pl.BlockSpec((2 * TM, D), lambda t: (t, 0)),
                  _mod_spec(layer, NT_LAT // 2), _resident((6, D))] + _WEIGHT_SPECS,
        out_specs=pl.BlockSpec((2 * TM, D), lambda t: (t, 0)),
        out_shape=jax.ShapeDtypeStruct((ROWS, D), F32),
        scratch_shapes=_WEIGHT_SCRATCH,
        compiler_params=_cparams(("arbitrary",)),
        name="ffn_pre",
    )(x, mods, g, w1, w3, w2)


def _mix_ffn_core(y, x, mod_ref, g_ref, wo_ref, bo_ref, w1, w3, w2, **ffn_kwargs):
    z = _dot(y.astype(BF16), wo_ref[...]) + bo_ref[...]
    x = x + _gated(_rmsnorm(z, g_ref[3:4]), _mod(mod_ref, 5))
    return _ffn(x, mod_ref, 6, g_ref[4:5], g_ref[5:6], w1, w3, w2, **ffn_kwargs)


def _hy_out_ffn_kernel(ylo_ref, yhi_ref, x_ref, mod_ref, g_ref, bo_ref, wo_ref, w1, w3, w2,
                       o_ref, rows_scr):
    halves = (ylo_ref, yhi_ref)
    for b in range(B):
        for j in range(D // LANES):
            col = (b * (D // TC) + j // 2) * LANES
            rows_scr[j, _rows_of_batch(b), :] = halves[j % 2][:, col:col + LANES]
    o_ref[...] = _mix_ffn_core(_get_rows(rows_scr), x_ref[...], mod_ref, g_ref, wo_ref, bo_ref,
                               w1, w3, w2)


def _hy_out_ffn(y_lo, y_hi, x, mods, g, w_out, b_out, w1, w3, w2):
    half = pl.BlockSpec((TS, B * D // 2), lambda t: (t, 0))
    return pl.pallas_call(
        _with_staged_weights(_hy_out_ffn_kernel, 10, [_first] + _ffn_views(0, 1)),
        grid=(NT,),
        in_specs=[half, half, _row_spec(), _mod_spec(0), _resident((6, D)), _resident((1, D)),
                  _HBM] + _WEIGHT_SPECS,
        out_specs=_row_spec(),
        out_shape=jax.ShapeDtypeStruct((ROWS, D), F32),
        scratch_shapes=[_ROWS_SCRATCH, pltpu.VMEM((D, D), BF16)] + _WEIGHT_SCRATCH,
        compiler_params=_cparams(("arbitrary",)),
        name="hyena_out_ffn",
    )(y_lo, y_hi, x, mods, g, b_out, w_out, w1, w3, w2)


def _seq_first(t):
    return jnp.logical_or(t == 0, t == NI_LAT)


def _seq_last(t):
    return jnp.logical_or(t == NI_LAT - 1, t == NI - 1)


def _halo_specs(order, lo_rows, hi_rows):
    nhi = ROWS // hi_rows
    return [
        pl.BlockSpec((lo_rows, D), lambda s: (jnp.maximum(order(s) * (TI // lo_rows) - 1, 0), 0)),
        pl.BlockSpec((TI, D), lambda s: (order(s), 0)),
        pl.BlockSpec((hi_rows, D),
                     lambda s: (jnp.minimum((order(s) + 1) * (TI // hi_rows), nhi - 1), 0)),
    ]


def _hy_in_kernel(xp_ref, x_ref, xn_ref, mod_ref, g_ref, b_ref, cw_ref, cb_ref, w_ref,
                  x0_ref, p_ref, x0_scr, p_scr):
    t = pl.program_id(0)
    xa = jnp.concatenate([xp_ref[...], x_ref[...], xn_ref[...]], axis=0)
    h = _norm_modulate(xa, g_ref[2:3], _mod(mod_ref, 3), _mod(mod_ref, 4)).astype(BF16)
    keep_lo = jnp.where(_seq_first(t), 0.0, 1.0).astype(F32)
    keep_hi = jnp.where(_seq_last(t), 0.0, 1.0).astype(F32)

    def conv_cols(c0):
        cols = slice(c0, c0 + MXU_N)
        u = _dot(h, w_ref[:, cols]) + b_ref[:, cols]
        lo = jnp.concatenate([u[0:B] * keep_lo, u[B:TI]], axis=0)
        hi = jnp.concatenate([u[2 * B:TI + B], u[TI + B:TI + 2 * B] * keep_hi], axis=0)
        return (cb_ref[:, cols] + cw_ref[0:1, cols] * lo + cw_ref[1:2, cols] * u[B:TI + B]
                + cw_ref[2:3, cols] * hi)

    slabs = MXU_N // LANES
    for c in range(D // MXU_N):
        c0 = c * MXU_N
        vals = (conv_cols(c0), conv_cols(D + c0) * conv_cols(2 * D + c0))
        for out_ref, scr, val in zip((x0_ref, p_ref), (x0_scr, p_scr), vals):
            for j in range(slabs):
                scr[c * slabs + j] = val[:, j * LANES:(j + 1) * LANES]
            for b in range(B):
                out_ref[:, b * D + c0:b * D + c0 + MXU_N] = jnp.concatenate(
                    [scr[c * slabs + j, pl.ds(b, TI // B, stride=B), :] for j in range(slabs)], axis=1)


def _hy_in(x, mods, g, w_in, b_in, conv_w, conv_b):
    wide = pl.BlockSpec((TI // B, B * D), lambda t: (t, 0))
    rows_scratch = pltpu.VMEM((D // LANES, TI, LANES), F32)
    return pl.pallas_call(
        _with_staged_weights(_hy_in_kernel, 9, [_first]),
        grid=(NI,),
        in_specs=_halo_specs(lambda s: s, B, B) + [
            _mod_spec(0, NI_LAT), _resident((6, D)), _resident((1, 3 * D)),
            _resident((3, 3 * D)), _resident((1, 3 * D)), _HBM],
        out_specs=[wide, wide],
        out_shape=[jax.ShapeDtypeStruct((ROWS // B, B * D), F32)] * 2,
        scratch_shapes=[rows_scratch, rows_scratch, pltpu.VMEM((D, 3 * D), BF16)],
        compiler_params=_cparams(("arbitrary",)),
        name="hyena_in",
    )(x, x, x, mods, g, b_in, conv_w, conv_b, w_in)


def _filter_kernel(zt_ref, t_ref, fw0t, fb0, fw1t, fb1, fw2t, fb2, freq, fwout, deltas,
                   hf_ref, hb_ref):
    h = jnp.sin(freq[:, 0:1] * (_dot3(fw0t[...], zt_ref[...]) + fb0[...]))
    h = jnp.sin(freq[:, 1:2] * (_dot3(fw1t[...], h) + fb1[...]))
    h = jnp.sin(freq[:, 2:3] * (_dot3(fw2t[...], h) + fb2[...]))
    filt = _dot3(h.T, fwout[...])
    decay = jnp.exp(-t_ref[:, 0:1] * deltas[...])
    hf_ref[...] = filt[:, 0:D] * decay
    hb_ref[...] = filt[:, D:2 * D] * decay


def _filters(n, fw0, fb0, fw1, fb1, fw2, fb2, freq, fwout):
    t = np.linspace(0.0, 1.0, n)[:, None]
    bands = np.linspace(1e-4, HY_BANDS - 1, HY_BANDS)[None]
    phase = bands * (2.0 * math.pi * np.arange(n)[:, None] / n)
    zp = np.zeros((n, LANES), np.float32)
    zp[:, :HY_EMB] = np.concatenate([t, np.cos(phase), -np.sin(phase)], axis=-1)
    zp = np.concatenate([zp[r::4] for r in range(4)], axis=0)
    fw0t = jnp.zeros((HY_HID, LANES), F32).at[:, :HY_EMB].set(fw0.T)
    max_decay = math.log(HY_DECAY_TARGET) / HY_FAST_DECAY
    min_decay = math.log(HY_DECAY_TARGET) / HY_SLOW_DECAY
    deltas = np.abs(np.linspace(min_decay, max_decay, D))[None].astype(np.float32)
    tl = 256
    row = lambda i: (i, 0)
    unit = _resident((HY_HID, 1))
    return pl.pallas_call(
        _filter_kernel,
        grid=(n // tl,),
        in_specs=[pl.BlockSpec((LANES, tl), lambda i: (0, i)), pl.BlockSpec((tl, LANES), row),
                  _resident((HY_HID, LANES)), unit, _resident((HY_HID, HY_HID)), unit,
                  _resident((HY_HID, HY_HID)), unit, _resident((HY_HID, 3)),
                  _resident((HY_HID, 2 * D)), _resident((1, D))],
        out_specs=[pl.BlockSpec((tl, D), row)] * 2,
        out_shape=[jax.ShapeDtypeStruct((n, D), F32)] * 2,
        compiler_params=_cparams(("arbitrary",)),
        name="hyena_filter",
    )(np.ascontiguousarray(zp.T), zp, fw0t, fb0[:, None], fw1.T, fb1[:, None], fw2.T, fb2[:, None],
      freq.T, fwout, deltas)


def _alt_sign(rows, cols):
    r = lax.broadcasted_iota(jnp.int32, (rows, cols), 0)
    return (1 - 2 * (r & 1)).astype(F32)


_N_GROUPS = 4
_ROOT_HALF = math.sqrt(0.5)


def _radix4_tables(g):
    k = np.arange(g, dtype=np.int64)
    theta = ((k[:, None] * k[None, :]) % (2 * g)) * (math.pi / g)
    phase = k[:, None] * (math.pi / (4 * g))
    fwd = [f(theta + r * phase) for r in range(4) for f in (np.cos, np.sin)]
    as_bf16 = lambda blocks: jnp.asarray(np.concatenate(blocks, axis=0), F32).astype(BF16)
    return as_bf16(fwd), as_bf16([b.T for b in fwd])


def _cmul(ar, as_, br, bi):
    return ar * br + as_ * bi, as_ * br - ar * bi


def _table_rows(tab_ref, r, rows, g):
    count = rows.stop - rows.start
    return (tab_ref[pl.ds(2 * r * g + rows.start, count), :],
            tab_ref[pl.ds((2 * r + 1) * g + rows.start, count), :])


def _fwd4(quarters, tab_ref, rows, g):
    ts = []
    for r, q in enumerate(quarters):
        c, s = _table_rows(tab_ref, r, rows, g)
        ts.append((_dot(c, q), _dot(s, q)))
    (t0r, t0s), (t1r, t1s), (t2r, t2s), (t3r, t3s) = ts
    er, es, fr, fs = t0r + t2r, t0s + t2s, t0r - t2r, t0s - t2s
    pr, ps, dr, ds = t1r + t3r, t1s + t3s, t1r - t3r, t1s - t3s
    return ((er + pr, es + ps), (er - pr, es - ps), (fr + ds, fs - dr), (fr - ds, fs + dr))


def _mid_freqs(sums):
    s0, s1, s2, s3 = sums
    a, b = _ROOT_HALF * (s1 - s3), _ROOT_HALF * (s1 + s3)
    return (s0 + a, s2 + b), (s0 - a, b - s2)


def _alt_sums(quarters):
    alt = _alt_sign(*quarters[0].shape)
    return [jnp.sum(q * alt, axis=0, keepdims=True) for q in quarters]


def _spectrum_kernel(hf_ref, hb_ref, tab_ref, kr_ref, ki_ref, kn_ref, *, g):
    hf = hf_ref[...]
    row = lax.broadcasted_iota(jnp.int32, hf.shape, 0)
    hb = jnp.where(row == 0, 0.0, hb_ref[...])
    cos_part = hf + hb
    sin_part = hb - hf
    cq = [cos_part[r * g:(r + 1) * g] for r in range(4)]
    sq = [sin_part[r * g:(r + 1) * g] for r in range(4)]
    rows = slice(0, g)
    groups_c = _fwd4([q.astype(BF16) for q in cq], tab_ref, rows, g)
    groups_s = _fwd4([q.astype(BF16) for q in sq], tab_ref, rows, g)
    n_fft = 8 * g
    k = lax.broadcasted_iota(jnp.int32, (g, hf.shape[1]), 0)
    scale = jnp.where(k == 0, 1.0 / n_fft, 2.0 / n_fft)
    for grp in range(_N_GROUPS):
        kr_ref[grp * g:(grp + 1) * g, :] = groups_c[grp][0] * scale
        ki_ref[grp * g:(grp + 1) * g, :] = groups_s[grp][1] * scale
    (cg, _), (c3g, _) = _mid_freqs(_alt_sums(cq))
    (_, sg), (_, s3g) = _mid_freqs(_alt_sums(sq))
    mids = [v * (2.0 / n_fft) for v in (cg, sg, c3g, s3g)]
    kn_ref[...] = jnp.concatenate(mids + [jnp.zeros((B - 4, hf.shape[1]), F32)], axis=0)


def _spectrum(hf, hb, fwd_table):
    n = hf.shape[0]
    g = n // 4
    col = lambda j: (0, j)
    return pl.pallas_call(
        functools.partial(_spectrum_kernel, g=g),
        grid=(D // TC,),
        in_specs=[pl.BlockSpec((n, TC), col)] * 2 + [_resident((8 * g, g))],
        out_specs=[pl.BlockSpec((n, TC), col)] * 2 + [pl.BlockSpec((B, TC), col)],
        out_shape=[jax.ShapeDtypeStruct((n, D), F32)] * 2 + [jax.ShapeDtypeStruct((B, D), F32)],
        compiler_params=_cparams(("arbitrary",)),
        name="hyena_spectrum",
    )(hf, hb, fwd_table)


def _steps(refs, first, count):
    rows = pl.ds(first, count, stride=4)
    return jnp.concatenate([r[rows, :] for r in refs], axis=1)


def _long_conv_rows(t0, g, p_refs, x0_refs, spec, bias_ref, o_refs, scratch):
    kr_ref, ki_ref, kn_ref, fwd_ref, inv_ref = spec
    q_scr, u_scr = scratch
    tk = min(g, 512)
    r0 = t0 // 4
    seq = slice(r0, r0 + g)
    chunks = [slice(k * tk, (k + 1) * tk) for k in range(g // tk)]
    quarters = [_steps(p_refs, t0 + r, g) for r in range(4)]
    tc = quarters[0].shape[1]
    for r in range(4):
        q_scr[r, seq, :] = quarters[r].astype(BF16)
    for rows in chunks:
        dst = slice(r0 + rows.start, r0 + rows.stop)
        groups = _fwd4([q_scr[r, seq, :] for r in range(4)], fwd_ref, rows, g)
        ys = []
        for grp, (xr, xs) in enumerate(groups):
            k_rows = pl.ds(grp * g + rows.start, tk)
            ys.append(_cmul(xr, xs, kr_ref[k_rows, :], ki_ref[k_rows, :]))
        (y1r, y1s), (y2r, y2s), (y3r, y3s), (y4r, y4s) = ys
        pr, ps, mr, ms = y1r + y2r, y1s + y2s, y1r - y2r, y1s - y2s
        qr, qs, nr, ns = y3r + y4r, y3s + y4s, y3r - y4r, y3s - y4s
        us = [(pr + qr, ps + qs), (mr - ns, ms + nr), (pr - qr, ps - qs), (mr + ns, ms - nr)]
        for r, (ur, us_) in enumerate(us):
            u_scr[2 * r, dst, :] = ur.astype(BF16)
            u_scr[2 * r + 1, dst, :] = us_.astype(BF16)
    (xgr, xgs), (x3r, x3s) = _mid_freqs(_alt_sums(quarters))
    ygr, ygs = _cmul(xgr, xgs, kn_ref[0:1, :], kn_ref[1:2, :])
    y3r_, y3s_ = _cmul(x3r, x3s, kn_ref[2:3, :], kn_ref[3:4, :])
    a = _ROOT_HALF
    mids = [ygr + y3r_, a * (ygr + ygs - y3r_ + y3s_), ygs - y3s_, a * (ygs - ygr + y3r_ + y3s_)]
    alt_chunk = _alt_sign(tk, tc)
    bias = bias_ref[...]
    for rows in chunks:
        for r in range(4):
            first = t0 + 4 * rows.start + r
            c, s = _table_rows(inv_ref, r, rows, g)
            y = (_dot(c, u_scr[2 * r, seq, :]) + _dot(s, u_scr[2 * r + 1, seq, :])
                 + alt_chunk * mids[r])
            out = _steps(x0_refs, first, tk) * (y + _steps(p_refs, first, tk) * bias)
            for q, o_ref in enumerate(o_refs):
                o_ref[pl.ds(first, tk, stride=4), :] = out[:, q * LANES:(q + 1) * LANES]


_N_SPEC = 5


def _long_conv_kernel(*refs):
    p_refs, x0_refs, bias_ref = refs[0:2], refs[2:4], refs[4]
    lat, ctx = refs[5:5 + _N_SPEC], refs[5 + _N_SPEC:5 + 2 * _N_SPEC]
    o_refs, scratch = refs[5 + 2 * _N_SPEC:7 + 2 * _N_SPEC], refs[7 + 2 * _N_SPEC:]
    _long_conv_rows(0, L // 4, p_refs, x0_refs, lat, bias_ref, o_refs, scratch)
    _long_conv_rows(L, CTX // 4, p_refs, x0_refs, ctx, bias_ref, o_refs, scratch)


def _long_conv(p2, x02, bias, spec_lat, spec_ctx):
    nc = D // TC
    halves = [pl.BlockSpec((ROWS // B, LANES),
                           functools.partial(lambda q, j: (0, 2 * ((j % B) * nc + j // B) + q), q))
              for q in range(TC // LANES)]
    ch = lambda j: (0, j // B)

    def spec_specs(g):
        return ([pl.BlockSpec((4 * g, TC), ch)] * 2 + [pl.BlockSpec((B, TC), ch)]
                + [_resident((8 * g, g))] * 2)

    half_out = pl.BlockSpec((ROWS // B, LANES), lambda j: (0, (j % B) * nc + j // B))
    quarter_rows = ROWS // (4 * B)
    return pl.pallas_call(
        _long_conv_kernel,
        grid=(B * D // TC,),
        scratch_shapes=[pltpu.VMEM((4, quarter_rows, TC), BF16),
                        pltpu.VMEM((8, quarter_rows, TC), BF16)],
        in_specs=halves + halves + [pl.BlockSpec((1, TC), ch)] + spec_specs(L // 4)
        + spec_specs(CTX // 4),
        out_specs=[half_out] * (TC // LANES),
        out_shape=[jax.ShapeDtypeStruct((ROWS // B, B * D * LANES // TC), F32)] * (TC // LANES),
        compiler_params=_cparams(("arbitrary",)),
        name="hyena_long_conv",
    )(p2, p2, x02, x02, bias, *spec_lat, *spec_ctx)


def _gelu_tanh(x):
    return x * (0.5 * (1.0 + jnp.tanh(math.sqrt(2.0 / math.pi) * (x + 0.044715 * (x * x * x)))))


_TINY = 1e-30


def _rg_coeffs(xc, hd, wai_ref, bai_ref, lam_ref, a_scr, b_scr):
    sl = slice(hd * RG_BLOCK, (hd + 1) * RG_BLOCK)
    lam = lam_ref[:, sl]
    softplus_neg = jnp.maximum(-lam, 0.0) + jnp.log1p(jnp.exp(-jnp.abs(lam)))
    rate = (-RG_C * math.log2(math.e)) * softplus_neg
    pre = _dot(xc.astype(BF16), wai_ref[hd]) + bai_ref[:, 2 * hd * RG_BLOCK:2 * (hd + 1) * RG_BLOCK]
    gates = _sigmoid(pre)
    a = jnp.exp2(gates[:, 0:RG_BLOCK] * rate)
    a_scr[:, sl] = a
    v = (1.0 - a) * (1.0 + a)
    root = v * lax.rsqrt(jnp.maximum(v, _TINY))
    b_scr[:, sl] = root * gates[:, RG_BLOCK:2 * RG_BLOCK] * xc


def _scan_tile(a_scr, b_scr, h_scr, emit, reverse):
    steps = a_scr.shape[0] // B

    def body(k, h):
        t = steps - 1 - k if reverse else k
        r0 = pl.multiple_of(t * B, B)
        h = a_scr[pl.ds(r0, B), :] * h + b_scr[pl.ds(r0, B), :]
        emit(r0, h)
        return h

    h_scr[...] = lax.fori_loop(0, steps, body, h_scr[...], unroll=8)


def _rg_fwd_order(s):
    return jnp.where(s < NI_CTX, NI_LAT + s, s - NI_CTX)


def _rg_in_kernel(xp_ref, x_ref, xn_ref, mod_ref, g_ref, b_ref, cw_ref, cb_ref,
                  wai_ref, bai_ref, lam_ref, w_ref, xc_ref, gate_ref, hs_ref, a_scr, b_scr, h_scr):
    s = pl.program_id(0)
    t = _rg_fwd_order(s)
    xa = jnp.concatenate([xp_ref[...], x_ref[...], xn_ref[...]], axis=0)
    h = _norm_modulate(xa, g_ref[2:3], _mod(mod_ref, 3), _mod(mod_ref, 4)).astype(BF16)
    keep_lo = jnp.where(_seq_first(t), 0.0, 1.0).astype(F32)
    keep_hi = jnp.where(_seq_last(t), 0.0, 1.0).astype(F32)
    for hd in range(RG_HEADS):
        sl = slice(hd * RG_BLOCK, (hd + 1) * RG_BLOCK)
        rec = slice(D + hd * RG_BLOCK, D + (hd + 1) * RG_BLOCK)
        gate_ref[:, sl] = (_dot(h[B:TI + B], w_ref[:, sl]) + b_ref[:, sl]).astype(BF16)
        u = _dot(h, w_ref[:, rec]) + b_ref[:, rec]
        taps = (jnp.concatenate([u[0:B] * keep_lo, u[B:TI]], axis=0),
                u[B:TI + B],
                jnp.concatenate([u[2 * B:TI + B], u[TI + B:TI + 2 * B] * keep_hi], axis=0),
                jnp.concatenate([u[3 * B:TI + B], u[TI + B:TI + 3 * B] * keep_hi], axis=0))
        xc = cb_ref[:, sl]
        for k, tap in enumerate(taps):
            xc = xc + cw_ref[k:k + 1, sl] * tap
        xc_ref[:, sl] = xc
        _rg_coeffs(xc, hd, wai_ref, bai_ref, lam_ref, a_scr, b_scr)

    @pl.when(s == 0)
    def _():
        h_scr[...] = jnp.zeros((B, D), F32)

    def emit(r0, hv):
        hs_ref[pl.ds(r0, B), :] = hv

    _scan_tile(a_scr, b_scr, h_scr, emit, reverse=False)


def _rg_gate_specs():
    return [_resident((RG_HEADS, RG_BLOCK, 2 * RG_BLOCK)), _resident((1, 2 * D)), _resident((1, D))]


def _rg_in(x, mods, g, w_in, b_in, conv_w, conv_b, wai, bai, lam):
    order = _rg_fwd_order
    mod_spec = _mod_spec(1, NI_LAT, order)
    out_spec = pl.BlockSpec((TI, D), lambda s: (order(s), 0))
    return pl.pallas_call(
        _with_staged_weights(_rg_in_kernel, 12, [_first]),
        grid=(NI,),
        in_specs=_halo_specs(order, B, 2 * B) + [
            mod_spec, _resident((6, D)), _resident((1, 2 * D)),
            _resident((4, D)), _resident((1, D))] + _rg_gate_specs() + [_HBM],
        out_specs=[out_spec] * 3,
        out_shape=[jax.ShapeDtypeStruct((ROWS, D), F32), jax.ShapeDtypeStruct((ROWS, D), BF16),
                   jax.ShapeDtypeStruct((ROWS, D), F32)],
        scratch_shapes=[pltpu.VMEM((TI, D), F32), pltpu.VMEM((TI, D), F32),
                        pltpu.VMEM((B, D), F32), pltpu.VMEM((D, 2 * D), BF16)],
        compiler_params=_cparams(("arbitrary",)),
        name="rglru_in_fwd_scan",
    )(x, x, x, mods, g, b_in, conv_w, conv_b, wai, bai, lam, w_in)


def _rg_tile_coeffs(xc_ref, wai_ref, bai_ref, lam_ref, a_scr, b_scr):
    for hd in range(RG_HEADS):
        xc = xc_ref[:, hd * RG_BLOCK:(hd + 1) * RG_BLOCK]
        _rg_coeffs(xc, hd, wai_ref, bai_ref, lam_ref, a_scr, b_scr)


def _rg_ctx_bwd_kernel(xc_ref, wai_ref, bai_ref, lam_ref, h_ref, a_scr, b_scr, h_scr):
    _rg_tile_coeffs(xc_ref, wai_ref, bai_ref, lam_ref, a_scr, b_scr)

    @pl.when(pl.program_id(0) == 0)
    def _():
        h_scr[...] = jnp.zeros((B, D), F32)

    _scan_tile(a_scr, b_scr, h_scr, lambda r0, hv: None, reverse=True)
    h_ref[...] = h_scr[...]


def _rg_ctx_bwd(xc, wai, bai, lam):
    return pl.pallas_call(
        _rg_ctx_bwd_kernel,
        grid=(NT_CTX,),
        in_specs=[pl.BlockSpec((TM, D), lambda s: (NT - 1 - s, 0))] + _rg_gate_specs(),
        out_specs=pl.BlockSpec((B, D), lambda s: (0, 0)),
        out_shape=jax.ShapeDtypeStruct((B, D), F32),
        scratch_shapes=[pltpu.VMEM((TM, D), F32), pltpu.VMEM((TM, D), F32),
                        pltpu.VMEM((B, D), F32)],
        compiler_params=_cparams(("arbitrary",)),
        name="rglru_ctx_bwd_scan",
    )(xc, wai, bai, lam)


def _rg_out_ffn_kernel(xc_ref, gate_ref, hs_ref, x_ref, h0_ref, mod_ref, g_ref,
                       wai_ref, bai_ref, lam_ref, bo_ref, wo_ref, w1, w3, w2,
                       o_ref, a_scr, b_scr, hb_scr, h_scr, rows_scr):
    s = pl.program_id(0)

    def head_gates(hd):
        xc = xc_ref[:, hd * RG_BLOCK:(hd + 1) * RG_BLOCK]
        _rg_coeffs(xc, hd, wai_ref, bai_ref, lam_ref, a_scr, b_scr)

    @pl.when(s == 0)
    def _():
        h_scr[...] = h0_ref[...]
        for hd in range(RG_HEADS):
            head_gates(hd)

    @pl.when(s > 0)
    def _():
        y = (hs_ref[...] + hb_scr[...]) * _gelu_tanh(gate_ref[...].astype(F32))
        res = _mix_ffn_core(
            y, x_ref[...], mod_ref, g_ref, wo_ref, bo_ref, w1, w3, w2, chunks=F_QUARTERS,
            side_work=[functools.partial(head_gates, hd) for hd in range(RG_HEADS)])
        _put_rows(rows_scr, res)
        for b in range(B):
            o_ref[b] = _get_batch(rows_scr, b)

    @pl.when(s < NT_LAT)
    def _():
        def emit(r0, hv):
            hb_scr[pl.ds(r0, B), :] = hv

        _scan_tile(a_scr, b_scr, h_scr, emit, reverse=True)


def _rg_out_ffn(xc, gate, hs, x, h0, mods, g, wai, bai, lam, w_out, b_out, w1, w3, w2):
    scan_tile = lambda s: (jnp.maximum(NT_LAT - 1 - s, 0), 0)
    out_tile = lambda s: jnp.minimum(NT_LAT - s, NT_LAT - 1)
    prev = pl.BlockSpec((TM, D), lambda s: (out_tile(s), 0))
    return pl.pallas_call(
        _with_staged_weights(_rg_out_ffn_kernel, 15, [_first] + _ffn_views(1, 1)),
        grid=(NT_LAT + 1,),
        in_specs=[pl.BlockSpec((TM, D), scan_tile), prev, prev, prev, _resident((B, D)),
                  _mod_spec(1, NT_LAT + 1), _resident((6, D))]
        + _rg_gate_specs() + [_resident((1, D)), _HBM] + _WEIGHT_SPECS,
        out_specs=pl.BlockSpec((B, TS, D), lambda s: (0, out_tile(s), 0)),
        out_shape=jax.ShapeDtypeStruct((B, L, D), F32),
        scratch_shapes=[pltpu.VMEM((TM, D), F32), pltpu.VMEM((TM, D), F32),
                        pltpu.VMEM((TM, D), F32), pltpu.VMEM((B, D), F32), _ROWS_SCRATCH,
                        pltpu.VMEM((D, D), BF16)] + _WEIGHT_SCRATCH,
        compiler_params=_cparams(("arbitrary",)),
        name="rglru_out_ffn",
    )(xc, gate, hs, x, h0, mods, g, wai, bai, lam, b_out, w_out, w1, w3, w2)


def _grid_pos():
    rows = L // GRID_W
    quarter = D // 4
    omega = POS_BASE ** (-np.arange(quarter) / quarter)

    def emb(q):
        ang = q[:, None] * omega[None]
        return np.concatenate([np.sin(ang), np.cos(ang)], axis=-1)

    row_code = np.repeat(emb(np.arange(rows)), GRID_W, axis=0)
    col_code = np.tile(emb(np.arange(GRID_W)), (rows, 1))
    return jnp.asarray(np.concatenate([row_code, col_code], axis=-1), F32)


def kernel(x, c, ctx, c_ctx, ada_w, ada_b, norm_g, ffn_w1, ffn_w3, ffn_w2, hy_w_in, hy_b_in, hy_conv_w, hy_conv_b, hy_fw0, hy_fb0, hy_fw1, hy_fb1, hy_fw2, hy_fb2, hy_freq, hy_fwout, hy_filt_bias, hy_w_out, hy_b_out, rg_w_in, rg_b_in, rg_conv_w, rg_conv_b, rg_wa, rg_ba, rg_wi, rg_bi, rg_lam, rg_w_out, rg_b_out):
    mods = _mods(c, c_ctx, ada_w, ada_b).reshape(DEPTH, 2, B, N_MOD * D)
    w1, w3, w2 = ffn_w1, ffn_w3, ffn_w2

    g = norm_g[0]
    xs = _ffn_first(x, ctx, _grid_pos(), mods, g, w1, w3, w2)
    x0, p = _hy_in(xs, mods, g, hy_w_in, hy_b_in[0][None],
                   hy_conv_w[0], hy_conv_b[0][None])
    fparams = (hy_fw0[0], hy_fb0[0], hy_fw1[0], hy_fb1[0], hy_fw2[0], hy_fb2[0],
               hy_freq[0], hy_fwout[0])
    specs = []
    for n in (L, CTX):
        hf, hb = _filters(n, *fparams)
        fwd_table, inv_table = _radix4_tables(n // 4)
        specs.append(list(_spectrum(hf, hb, fwd_table)) + [fwd_table, inv_table])
    y_lo, y_hi = _long_conv(p, x0, hy_filt_bias[0][None], *specs)
    xs = _hy_out_ffn(y_lo, y_hi, xs, mods, g, hy_w_out, hy_b_out[0][None],
                     w1, w3, w2)

    g = norm_g[1]
    xs = _ffn_plain(xs, mods, g, w1, w3, w2, layer=1)
    wai = jnp.concatenate([rg_wa[0], rg_wi[0]], axis=-1).astype(BF16)
    per_head = lambda v: v.reshape(2, RG_HEADS, RG_BLOCK)
    bai = jnp.concatenate([per_head(rg_ba[0]), per_head(rg_bi[0])], axis=-1).reshape(2, 1, 2 * D)
    lam = rg_lam[0][:, None, :]
    xc, gate, hs = _rg_in(xs, mods, g, rg_w_in, rg_b_in[0][None],
                          rg_conv_w[0], rg_conv_b[0][None], wai[0], bai[0], lam[0])
    h_ctx = _rg_ctx_bwd(xc, wai[1], bai[1], lam[1])
    return _rg_out_ffn(xc, gate, hs, xs, h_ctx, mods, g, wai[1], bai[1], lam[1],
                       rg_w_out, rg_b_out[0][None], w1, w3, w2)
```

```python
import functools
import math

import jax
import jax.numpy as jnp
import numpy as np
from jax import lax
from jax.experimental import pallas as pl
from jax.experimental.pallas import tpu as pltpu

F32 = jnp.float32
BF16 = jnp.bfloat16

D = 1024
B = 8
LANES = 128
MXU_N = 256
L = 2048
CTX = 256
DEPTH = 2
GRID_W = 64
D_FF = 2816
N_MOD = 9
MACARON = 0.5
NORM_EPS = 1e-6
POS_BASE = 10000.0
HY_EMB = 33
HY_BANDS = 16
HY_HID = 64
HY_FAST_DECAY = 0.3
HY_SLOW_DECAY = 1.5
HY_DECAY_TARGET = 1e-2
RG_HEADS = 4
RG_BLOCK = D // RG_HEADS
RG_C = 8.0

ROWS_LAT = L * B
ROWS_CTX = CTX * B
ROWS = ROWS_LAT + ROWS_CTX
TM = 512
TS = TM // B
NT_LAT = ROWS_LAT // TM
NT_CTX = ROWS_CTX // TM
NT = NT_LAT + NT_CTX
TI = 1024
NI_LAT = ROWS_LAT // TI
NI_CTX = ROWS_CTX // TI
NI = NI_LAT + NI_CTX
F_CHUNKS = ((0, 1536), (1536, D_FF))
F_QUARTERS = ((0, 768), (768, 1536), (1536, 2304), (2304, D_FF))
TC = 256
VMEM_LIMIT = 58 * 1024 * 1024


def _cparams(sem):
    return pltpu.CompilerParams(dimension_semantics=sem, vmem_limit_bytes=VMEM_LIMIT)


def _resident(shape):
    nd = len(shape)
    return pl.BlockSpec(shape, lambda *_: (0,) * nd, pipeline_mode=pl.Buffered(1))


def _split(a):
    hi = a.astype(BF16)
    lo = (a - hi.astype(F32)).astype(BF16)
    return hi, lo


def _dot(a, b):
    return jnp.dot(a, b, preferred_element_type=F32)


def _dot3(a, b):
    ah, al = _split(a)
    bh, bl = _split(b)
    return _dot(ah, bh) + _dot(ah, bl) + _dot(al, bh)


def _mod(mod_ref, k):
    return mod_ref[:, k * D:(k + 1) * D]


def _per_batch(x, fn):
    rows = x.shape[0]
    return fn(x.reshape(rows // B, B, x.shape[1])).reshape(rows, x.shape[1])


def _norm_modulate(x, g, shift8, scale8):
    ms = jnp.mean(x * x, axis=-1, keepdims=True)
    factor = g * (1.0 + scale8)
    return _per_batch(x * lax.rsqrt(ms + NORM_EPS), lambda v: v * factor[None] + shift8[None])


def _norm_gate(z, g, gate8, weight=1.0):
    ms = jnp.mean(z * z, axis=-1, keepdims=True)
    factor = (weight * g) * gate8
    return _per_batch(z * lax.rsqrt(ms + NORM_EPS), lambda v: v * factor[None])


def _sigmoid(x):
    return jax.nn.sigmoid(x)


def _ffn(x, mod_ref, k0, g_pre, g_post, w1_ref, w3_ref, w2_ref, chunks=F_CHUNKS, side_work=()):
    h = _norm_modulate(x, g_pre, _mod(mod_ref, k0), _mod(mod_ref, k0 + 1)).astype(BF16)
    y = None
    for idx, (c0, c1) in enumerate(chunks):
        a = _dot(h, w1_ref[:, c0:c1])
        b = _dot(h, w3_ref[:, c0:c1])
        if idx < len(side_work):
            side_work[idx]()
        act = (a * _sigmoid(a) * b).astype(BF16)
        part = _dot(act, w2_ref[c0:c1, :])
        y = part if y is None else y + part
    return x + _norm_gate(y, g_post, _mod(mod_ref, k0 + 2), MACARON)


def _mods_kernel(c_ref, cctx_ref, w_ref, b_ref, o_ref):
    @pl.when(pl.program_id(1) == 0)
    def _():
        o_ref[...] = jnp.broadcast_to(b_ref[...], o_ref.shape)

    a = jnp.concatenate([c_ref[...], jnp.broadcast_to(cctx_ref[...], c_ref.shape)], axis=0)
    ah, al = _split(a * _sigmoid(a))
    wh, wl = _split(w_ref[...])
    r = _dot(jnp.concatenate([ah, al], axis=0), wh)
    o_ref[...] += r[0:2 * B] + r[2 * B:4 * B] + _dot(ah, wl)


def _mods(c, c_ctx, ada_w, ada_b):
    tk = 128
    n = N_MOD * D
    return pl.pallas_call(
        _mods_kernel,
        grid=(DEPTH, D // tk),
        in_specs=[
            pl.BlockSpec((B, tk), lambda i, k: (0, k)),
            pl.BlockSpec((1, tk), lambda i, k: (0, k)),
            pl.BlockSpec((None, tk, n), lambda i, k: (i, k, 0)),
            pl.BlockSpec((None, 1, n), lambda i, k: (i, 0, 0)),
        ],
        out_specs=pl.BlockSpec((None, 2 * B, n), lambda i, k: (i, 0, 0)),
        out_shape=jax.ShapeDtypeStruct((DEPTH, 2 * B, n), F32),
        compiler_params=_cparams(("arbitrary", "arbitrary")),
        name="ada_mods",
    )(c, c_ctx[None], ada_w, ada_b.reshape(DEPTH, 1, n))


def _mod_spec(layer, nt_lat=NT_LAT, order=lambda t: t):
    return pl.BlockSpec((None, None, B, N_MOD * D),
                        lambda t: (layer, jnp.where(order(t) < nt_lat, 0, 1), 0, 0))


def _row_spec(width=D):
    return pl.BlockSpec((TM, width), lambda t: (t, 0))


W_CHUNKS = 8
W_SLOTS = 3
_HBM = pl.BlockSpec(memory_space=pl.ANY)
_WEIGHT_SPECS = [_HBM] * 3
_WEIGHT_SCRATCH = [pltpu.VMEM((D, D_FF), BF16), pltpu.VMEM((D, D_FF), BF16),
                   pltpu.VMEM((D_FF, D), BF16)]


def _stage_weights(srcs, dsts):
    chunks = [(src, dst, r0, dst.shape[0] // W_CHUNKS) for src, dst in zip(srcs, dsts)
              for r0 in range(0, dst.shape[0], dst.shape[0] // W_CHUNKS)]
    shapes = sorted({(rows, dst.shape[1]) for _, dst, _, rows in chunks})

    def run(*scoped):
        stages, sems = dict(zip(shapes, scoped[:-1])), scoped[-1]

        def slot(k):
            _, dst, _, rows = chunks[k]
            return stages[(rows, dst.shape[1])].at[k % W_SLOTS]

        def copy(k):
            src, _, r0, rows = chunks[k]
            return pltpu.make_async_copy(src.at[pl.ds(r0, rows), :], slot(k), sems.at[k % W_SLOTS])

        for k in range(W_SLOTS - 1):
            copy(k).start()
        for k, (_, dst, r0, rows) in enumerate(chunks):
            if k + W_SLOTS - 1 < len(chunks):
                copy(k + W_SLOTS - 1).start()
            copy(k).wait()
            dst[pl.ds(r0, rows), :] = slot(k)[...].astype(BF16)

    pl.run_scoped(run, *[pltpu.VMEM((W_SLOTS,) + shape, F32) for shape in shapes],
                  pltpu.SemaphoreType.DMA((W_SLOTS,)))


def _with_staged_weights(body, n_in, views):
    n_w = len(views)

    def kernel(*refs):
        ins, rest = refs[:n_in], refs[n_in:]
        copies = rest[-n_w:]

        @pl.when(pl.program_id(0) == 0)
        def _():
            _stage_weights([view(w) for view, w in zip(views, ins[-n_w:])], copies)

        body(*ins[:-n_w], *copies, *rest[:-n_w])

    return kernel


def _ffn_views(layer, half):
    return [lambda w: w.at[layer, half]] * 3


def _first(w):
    return w.at[0]


def _rows_of_batch(b):
    return pl.ds(b, TS, stride=B)


_ROWS_SCRATCH = pltpu.VMEM((D // LANES, TM, LANES), F32)


def _put_batch(rows_scr, b, val):
    for j in range(D // LANES):
        rows_scr[j, _rows_of_batch(b), :] = val[:, j * LANES:(j + 1) * LANES]


def _get_batch(rows_scr, b):
    return jnp.concatenate([rows_scr[j, _rows_of_batch(b), :] for j in range(D // LANES)], axis=1)


def _put_rows(rows_scr, val):
    for j in range(D // LANES):
        rows_scr[j] = val[:, j * LANES:(j + 1) * LANES]


def _get_rows(rows_scr):
    return jnp.concatenate([rows_scr[j] for j in range(D // LANES)], axis=1)


def _ffn_first_kernel(x_ref, ctx_ref, pos_ref, mod_ref, g_ref, w1, w3, w2, o_ref, rows_scr):
    is_latent = pl.program_id(0) < NT_LAT
    for b in range(B):
        _put_batch(rows_scr, b, jnp.where(is_latent, x_ref[b] + pos_ref[...], ctx_ref[b]))
    o_ref[...] = _ffn(_get_rows(rows_scr), mod_ref, 0, g_ref[0:1], g_ref[1:2], w1, w3, w2)


def _ffn_kernel(x_ref, mod_ref, g_ref, w1, w3, w2, o_ref):
    for r in range(x_ref.shape[0] // TM):
        rows = slice(r * TM, (r + 1) * TM)
        o_ref[rows, :] = _ffn(x_ref[rows, :], mod_ref, 0, g_ref[0:1], g_ref[1:2], w1, w3, w2)


def _ffn_first(x, ctx, pos, mods, g, w1, w3, w2):
    lat = lambda t: jnp.minimum(t, NT_LAT - 1)
    return pl.pallas_call(
        _with_staged_weights(_ffn_first_kernel, 8, _ffn_views(0, 0)),
        grid=(NT,),
        in_specs=[pl.BlockSpec((B, TS, D), lambda t: (0, lat(t), 0)),
                  pl.BlockSpec((B, TS, D), lambda t: (0, jnp.maximum(t - NT_LAT, 0), 0)),
                  pl.BlockSpec((TS, D), lambda t: (lat(t), 0)),
                  _mod_spec(0), _resident((6, D))] + _WEIGHT_SPECS,
        out_specs=_row_spec(),
        out_shape=jax.ShapeDtypeStruct((ROWS, D), F32),
        scratch_shapes=[_ROWS_SCRATCH] + _WEIGHT_SCRATCH,
        compiler_params=_cparams(("arbitrary",)),
        name="ffn_first",
    )(x, ctx, pos, mods, g, w1, w3, w2)


def _ffn_plain(x, mods, g, w1, w3, w2, layer):
    return pl.pallas_call(
        _with_staged_weights(_ffn_kernel, 6, _ffn_views(layer, 0)),
        grid=(NT // 2,),
        in_specs=[pl.BlockSpec((2 * TM, D), lambda t: (t, 0)),
                  _mod_spec(layer, NT_LAT // 2), _resident((6, D))] + _WEIGHT_SPECS,
        out_specs=pl.BlockSpec((2 * TM, D), lambda t: (t, 0)),
        out_shape=jax.ShapeDtypeStruct((ROWS, D), F32),
        scratch_shapes=_WEIGHT_SCRATCH,
        compiler_params=_cparams(("arbitrary",)),
        name="ffn_pre",
    )(x, mods, g, w1, w3, w2)


def _mix_ffn_core(y, x, mod_ref, g_ref, wo_ref, bo_ref, w1, w3, w2, **ffn_kwargs):
    z = _dot(y.astype(BF16), wo_ref[...]) + bo_ref[...]
    x = x + _norm_gate(z, g_ref[3:4], _mod(mod_ref, 5))
    return _ffn(x, mod_ref, 6, g_ref[4:5], g_ref[5:6], w1, w3, w2, **ffn_kwargs)


def _hy_out_ffn_kernel(ylo_ref, yhi_ref, x_ref, mod_ref, g_ref, bo_ref, wo_ref, w1, w3, w2,
                       o_ref, rows_scr):
    halves = (ylo_ref, yhi_ref)
    for b in range(B):
        for j in range(D // LANES):
            col = (b * (D // TC) + j // 2) * LANES
            rows_scr[j, _rows_of_batch(b), :] = halves[j % 2][:, col:col + LANES]
    o_ref[...] = _mix_ffn_core(_get_rows(rows_scr), x_ref[...], mod_ref, g_ref, wo_ref, bo_ref,
                               w1, w3, w2)


def _hy_out_ffn(y_lo, y_hi, x, mods, g, w_out, b_out, w1, w3, w2):
    half = pl.BlockSpec((TS, B * D // 2), lambda t: (t, 0))
    return pl.pallas_call(
        _with_staged_weights(_hy_out_ffn_kernel, 10, [_first] + _ffn_views(0, 1)),
        grid=(NT,),
        in_specs=[half, half, _row_spec(), _mod_spec(0), _resident((6, D)), _resident((1, D)),
                  _HBM] + _WEIGHT_SPECS,
        out_specs=_row_spec(),
        out_shape=jax.ShapeDtypeStruct((ROWS, D), F32),
        scratch_shapes=[_ROWS_SCRATCH, pltpu.VMEM((D, D), BF16)] + _WEIGHT_SCRATCH,
        compiler_params=_cparams(("arbitrary",)),
        name="hyena_out_ffn",
    )(y_lo, y_hi, x, mods, g, b_out, w_out, w1, w3, w2)


def _seq_first(t):
    return jnp.logical_or(t == 0, t == NI_LAT)


def _seq_last(t):
    return jnp.logical_or(t == NI_LAT - 1, t == NI - 1)


def _halo_specs(order, lo_rows, hi_rows):
    nhi = ROWS // hi_rows
    return [
        pl.BlockSpec((lo_rows, D), lambda s: (jnp.maximum(order(s) * (TI // lo_rows) - 1, 0), 0)),
        pl.BlockSpec((TI, D), lambda s: (order(s), 0)),
        pl.BlockSpec((hi_rows, D),
                     lambda s: (jnp.minimum((order(s) + 1) * (TI // hi_rows), nhi - 1), 0)),
    ]


def _hy_in_kernel(xp_ref, x_ref, xn_ref, mod_ref, g_ref, b_ref, cw_ref, cb_ref, w_ref,
                  x0_ref, p_ref, x0_scr, p_scr):
    t = pl.program_id(0)
    xa = jnp.concatenate([xp_ref[...], x_ref[...], xn_ref[...]], axis=0)
    h = _norm_modulate(xa, g_ref[2:3], _mod(mod_ref, 3), _mod(mod_ref, 4)).astype(BF16)
    keep_lo = jnp.where(_seq_first(t), 0.0, 1.0).astype(F32)
    keep_hi = jnp.where(_seq_last(t), 0.0, 1.0).astype(F32)

    def conv_cols(c0):
        cols = slice(c0, c0 + MXU_N)
        u = _dot(h, w_ref[:, cols]) + b_ref[:, cols]
        lo = jnp.concatenate([u[0:B] * keep_lo, u[B:TI]], axis=0)
        hi = jnp.concatenate([u[2 * B:TI + B], u[TI + B:TI + 2 * B] * keep_hi], axis=0)
        return (cb_ref[:, cols] + cw_ref[0:1, cols] * lo + cw_ref[1:2, cols] * u[B:TI + B]
                + cw_ref[2:3, cols] * hi)

    slabs = MXU_N // LANES
    for c in range(D // MXU_N):
        c0 = c * MXU_N
        vals = (conv_cols(c0), conv_cols(D + c0) * conv_cols(2 * D + c0))
        for out_ref, scr, val in zip((x0_ref, p_ref), (x0_scr, p_scr), vals):
            for j in range(slabs):
                scr[c * slabs + j] = val[:, j * LANES:(j + 1) * LANES]
            for b in range(B):
                out_ref[:, b * D + c0:b * D + c0 + MXU_N] = jnp.concatenate(
                    [scr[c * slabs + j, pl.ds(b, TI // B, stride=B), :] for j in range(slabs)], axis=1)


def _hy_in(x, mods, g, w_in, b_in, conv_w, conv_b):
    wide = pl.BlockSpec((TI // B, B * D), lambda t: (t, 0))
    rows_scratch = pltpu.VMEM((D // LANES, TI, LANES), F32)
    return pl.pallas_call(
        _with_staged_weights(_hy_in_kernel, 9, [_first]),
        grid=(NI,),
        in_specs=_halo_specs(lambda s: s, B, B) + [
            _mod_spec(0, NI_LAT), _resident((6, D)), _resident((1, 3 * D)),
            _resident((3, 3 * D)), _resident((1, 3 * D)), _HBM],
        out_specs=[wide, wide],
        out_shape=[jax.ShapeDtypeStruct((ROWS // B, B * D), F32)] * 2,
        scratch_shapes=[rows_scratch, rows_scratch, pltpu.VMEM((D, 3 * D), BF16)],
        compiler_params=_cparams(("arbitrary",)),
        name="hyena_in",
    )(x, x, x, mods, g, b_in, conv_w, conv_b, w_in)


def _filter_kernel(zt_ref, t_ref, fw0t, fb0, fw1t, fb1, fw2t, fb2, freq, fwout, deltas,
                   hf_ref, hb_ref):
    h = jnp.sin(freq[:, 0:1] * (_dot3(fw0t[...], zt_ref[...]) + fb0[...]))
    h = jnp.sin(freq[:, 1:2] * (_dot3(fw1t[...], h) + fb1[...]))
    h = jnp.sin(freq[:, 2:3] * (_dot3(fw2t[...], h) + fb2[...]))
    filt = _dot3(h.T, fwout[...])
    decay = jnp.exp(-t_ref[:, 0:1] * deltas[...])
    hf_ref[...] = filt[:, 0:D] * decay
    hb_ref[...] = filt[:, D:2 * D] * decay


def _filters(n, fw0, fb0, fw1, fb1, fw2, fb2, freq, fwout):
    t = np.linspace(0.0, 1.0, n)[:, None]
    bands = np.linspace(1e-4, HY_BANDS - 1, HY_BANDS)[None]
    phase = bands * (2.0 * math.pi * np.arange(n)[:, None] / n)
    zp = np.zeros((n, LANES), np.float32)
    zp[:, :HY_EMB] = np.concatenate([t, np.cos(phase), -np.sin(phase)], axis=-1)
    zp = np.concatenate([zp[r::4] for r in range(4)], axis=0)
    fw0t = jnp.zeros((HY_HID, LANES), F32).at[:, :HY_EMB].set(fw0.T)
    max_decay = math.log(HY_DECAY_TARGET) / HY_FAST_DECAY
    min_decay = math.log(HY_DECAY_TARGET) / HY_SLOW_DECAY
    deltas = np.abs(np.linspace(min_decay, max_decay, D))[None].astype(np.float32)
    tl = 256
    row = lambda i: (i, 0)
    unit = _resident((HY_HID, 1))
    return pl.pallas_call(
        _filter_kernel,
        grid=(n // tl,),
        in_specs=[pl.BlockSpec((LANES, tl), lambda i: (0, i)), pl.BlockSpec((tl, LANES), row),
                  _resident((HY_HID, LANES)), unit, _resident((HY_HID, HY_HID)), unit,
                  _resident((HY_HID, HY_HID)), unit, _resident((HY_HID, 3)),
                  _resident((HY_HID, 2 * D)), _resident((1, D))],
        out_specs=[pl.BlockSpec((tl, D), row)] * 2,
        out_shape=[jax.ShapeDtypeStruct((n, D), F32)] * 2,
        compiler_params=_cparams(("arbitrary",)),
        name="hyena_filter",
    )(np.ascontiguousarray(zp.T), zp, fw0t, fb0[:, None], fw1.T, fb1[:, None], fw2.T, fb2[:, None],
      freq.T, fwout, deltas)


def _alt_sign(rows, cols):
    r = lax.broadcasted_iota(jnp.int32, (rows, cols), 0)
    return (1 - 2 * (r & 1)).astype(F32)


_N_GROUPS = 4
_ROOT_HALF = math.sqrt(0.5)


def _radix4_tables(g):
    k = np.arange(g, dtype=np.int64)
    theta = ((k[:, None] * k[None, :]) % (2 * g)) * (math.pi / g)
    phase = k[:, None] * (math.pi / (4 * g))
    fwd = [f(theta + r * phase) for r in range(4) for f in (np.cos, np.sin)]
    as_bf16 = lambda blocks: jnp.asarray(np.concatenate(blocks, axis=0), F32).astype(BF16)
    return as_bf16(fwd), as_bf16([b.T for b in fwd])


def _cmul(ar, as_, br, bi):
    return ar * br + as_ * bi, as_ * br - ar * bi


def _table_rows(tab_ref, r, rows, g):
    count = rows.stop - rows.start
    return (tab_ref[pl.ds(2 * r * g + rows.start, count), :],
            tab_ref[pl.ds((2 * r + 1) * g + rows.start, count), :])


def _fwd4(quarters, tab_ref, rows, g):
    ts = []
    for r, q in enumerate(quarters):
        c, s = _table_rows(tab_ref, r, rows, g)
        ts.append((_dot(c, q), _dot(s, q)))
    (t0r, t0s), (t1r, t1s), (t2r, t2s), (t3r, t3s) = ts
    er, es, fr, fs = t0r + t2r, t0s + t2s, t0r - t2r, t0s - t2s
    pr, ps, dr, ds = t1r + t3r, t1s + t3s, t1r - t3r, t1s - t3s
    return ((er + pr, es + ps), (er - pr, es - ps), (fr + ds, fs - dr), (fr - ds, fs + dr))


def _mid_freqs(sums):
    s0, s1, s2, s3 = sums
    a, b = _ROOT_HALF * (s1 - s3), _ROOT_HALF * (s1 + s3)
    return (s0 + a, s2 + b), (s0 - a, b - s2)


def _alt_sums(quarters):
    alt = _alt_sign(*quarters[0].shape)
    return [jnp.sum(q * alt, axis=0, keepdims=True) for q in quarters]


def _spectrum_kernel(hf_ref, hb_ref, tab_ref, kr_ref, ki_ref, kn_ref, *, g):
    hf = hf_ref[...]
    row = lax.broadcasted_iota(jnp.int32, hf.shape, 0)
    hb = jnp.where(row == 0, 0.0, hb_ref[...])
    cos_part = hf + hb
    sin_part = hb - hf
    cq = [cos_part[r * g:(r + 1) * g] for r in range(4)]
    sq = [sin_part[r * g:(r + 1) * g] for r in range(4)]
    rows = slice(0, g)
    groups_c = _fwd4([q.astype(BF16) for q in cq], tab_ref, rows, g)
    groups_s = _fwd4([q.astype(BF16) for q in sq], tab_ref, rows, g)
    n_fft = 8 * g
    k = lax.broadcasted_iota(jnp.int32, (g, hf.shape[1]), 0)
    scale = jnp.where(k == 0, 1.0 / n_fft, 2.0 / n_fft)
    for grp in range(_N_GROUPS):
        kr_ref[grp * g:(grp + 1) * g, :] = groups_c[grp][0] * scale
        ki_ref[grp * g:(grp + 1) * g, :] = groups_s[grp][1] * scale
    (cg, _), (c3g, _) = _mid_freqs(_alt_sums(cq))
    (_, sg), (_, s3g) = _mid_freqs(_alt_sums(sq))
    mids = [v * (2.0 / n_fft) for v in (cg, sg, c3g, s3g)]
    kn_ref[...] = jnp.concatenate(mids + [jnp.zeros((B - 4, hf.shape[1]), F32)], axis=0)


def _spectrum(hf, hb, fwd_table):
    n = hf.shape[0]
    g = n // 4
    col = lambda j: (0, j)
    return pl.pallas_call(
        functools.partial(_spectrum_kernel, g=g),
        grid=(D // TC,),
        in_specs=[pl.BlockSpec((n, TC), col)] * 2 + [_resident((8 * g, g))],
        out_specs=[pl.BlockSpec((n, TC), col)] * 2 + [pl.BlockSpec((B, TC), col)],
        out_shape=[jax.ShapeDtypeStruct((n, D), F32)] * 2 + [jax.ShapeDtypeStruct((B, D), F32)],
        compiler_params=_cparams(("arbitrary",)),
        name="hyena_spectrum",
    )(hf, hb, fwd_table)


def _steps(refs, first, count):
    rows = pl.ds(first, count, stride=4)
    return jnp.concatenate([r[rows, :] for r in refs], axis=1)


def _long_conv_rows(t0, g, p_refs, x0_refs, spec, bias_ref, o_refs, scratch):
    kr_ref, ki_ref, kn_ref, fwd_ref, inv_ref = spec
    q_scr, u_scr = scratch
    tk = min(g, 512)
    r0 = t0 // 4
    seq = slice(r0, r0 + g)
    chunks = [slice(k * tk, (k + 1) * tk) for k in range(g // tk)]
    quarters = [_steps(p_refs, t0 + r, g) for r in range(4)]
    tc = quarters[0].shape[1]
    for r in range(4):
        q_scr[r, seq, :] = quarters[r].astype(BF16)
    for rows in chunks:
        dst = slice(r0 + rows.start, r0 + rows.stop)
        groups = _fwd4([q_scr[r, seq, :] for r in range(4)], fwd_ref, rows, g)
        ys = []
        for grp, (xr, xs) in enumerate(groups):
            k_rows = pl.ds(grp * g + rows.start, tk)
            ys.append(_cmul(xr, xs, kr_ref[k_rows, :], ki_ref[k_rows, :]))
        (y1r, y1s), (y2r, y2s), (y3r, y3s), (y4r, y4s) = ys
        pr, ps, mr, ms = y1r + y2r, y1s + y2s, y1r - y2r, y1s - y2s
        qr, qs, nr, ns = y3r + y4r, y3s + y4s, y3r - y4r, y3s - y4s
        us = [(pr + qr, ps + qs), (mr - ns, ms + nr), (pr - qr, ps - qs), (mr + ns, ms - nr)]
        for r, (ur, us_) in enumerate(us):
            u_scr[2 * r, dst, :] = ur.astype(BF16)
            u_scr[2 * r + 1, dst, :] = us_.astype(BF16)
    (xgr, xgs), (x3r, x3s) = _mid_freqs(_alt_sums(quarters))
    ygr, ygs = _cmul(xgr, xgs, kn_ref[0:1, :], kn_ref[1:2, :])
    y3r_, y3s_ = _cmul(x3r, x3s, kn_ref[2:3, :], kn_ref[3:4, :])
    a = _ROOT_HALF
    mids = [ygr + y3r_, a * (ygr + ygs - y3r_ + y3s_), ygs - y3s_, a * (ygs - ygr + y3r_ + y3s_)]
    alt_chunk = _alt_sign(tk, tc)
    bias = bias_ref[...]
    for rows in chunks:
        for r in range(4):
            first = t0 + 4 * rows.start + r
            c, s = _table_rows(inv_ref, r, rows, g)
            y = (_dot(c, u_scr[2 * r, seq, :]) + _dot(s, u_scr[2 * r + 1, seq, :])
                 + alt_chunk * mids[r])
            out = _steps(x0_refs, first, tk) * (y + _steps(p_refs, first, tk) * bias)
            for q, o_ref in enumerate(o_refs):
                o_ref[pl.ds(first, tk, stride=4), :] = out[:, q * LANES:(q + 1) * LANES]


_N_SPEC = 5


def _long_conv_kernel(*refs):
    p_refs, x0_refs, bias_ref = refs[0:2], refs[2:4], refs[4]
    lat, ctx = refs[5:5 + _N_SPEC], refs[5 + _N_SPEC:5 + 2 * _N_SPEC]
    o_refs, scratch = refs[5 + 2 * _N_SPEC:7 + 2 * _N_SPEC], refs[7 + 2 * _N_SPEC:]
    _long_conv_rows(0, L // 4, p_refs, x0_refs, lat, bias_ref, o_refs, scratch)
    _long_conv_rows(L, CTX // 4, p_refs, x0_refs, ctx, bias_ref, o_refs, scratch)


def _long_conv(p2, x02, bias, spec_lat, spec_ctx):
    nc = D // TC
    halves = [pl.BlockSpec((ROWS // B, LANES),
                           functools.partial(lambda q, j: (0, 2 * ((j % B) * nc + j // B) + q), q))
              for q in range(TC // LANES)]
    ch = lambda j: (0, j // B)

    def spec_specs(g):
        return ([pl.BlockSpec((4 * g, TC), ch)] * 2 + [pl.BlockSpec((B, TC), ch)]
                + [_resident((8 * g, g))] * 2)

    half_out = pl.BlockSpec((ROWS // B, LANES), lambda j: (0, (j % B) * nc + j // B))
    quarter_rows = ROWS // (4 * B)
    return pl.pallas_call(
        _long_conv_kernel,
        grid=(B * D // TC,),
        scratch_shapes=[pltpu.VMEM((4, quarter_rows, TC), BF16),
                        pltpu.VMEM((8, quarter_rows, TC), BF16)],
        in_specs=halves + halves + [pl.BlockSpec((1, TC), ch)] + spec_specs(L // 4)
        + spec_specs(CTX // 4),
        out_specs=[half_out] * (TC // LANES),
        out_shape=[jax.ShapeDtypeStruct((ROWS // B, B * D * LANES // TC), F32)] * (TC // LANES),
        compiler_params=_cparams(("arbitrary",)),
        name="hyena_long_conv",
    )(p2, p2, x02, x02, bias, *spec_lat, *spec_ctx)


def _gelu_tanh(x):
    return x * (0.5 * (1.0 + jnp.tanh(math.sqrt(2.0 / math.pi) * (x + 0.044715 * (x * x * x)))))


_TINY = 1e-30


def _rg_coeffs(xc, hd, wai_ref, bai_ref, lam_ref, a_scr, b_scr):
    sl = slice(hd * RG_BLOCK, (hd + 1) * RG_BLOCK)
    lam = lam_ref[:, sl]
    softplus_neg = jnp.maximum(-lam, 0.0) + jnp.log1p(jnp.exp(-jnp.abs(lam)))
    rate = (-RG_C * math.log2(math.e)) * softplus_neg
    pre = _dot(xc.astype(BF16), wai_ref[hd]) + bai_ref[:, 2 * hd * RG_BLOCK:2 * (hd + 1) * RG_BLOCK]
    gates = _sigmoid(pre)
    a = jnp.exp2(gates[:, 0:RG_BLOCK] * rate)
    a_scr[:, sl] = a
    v = (1.0 - a) * (1.0 + a)
    root = v * lax.rsqrt(jnp.maximum(v, _TINY))
    b_scr[:, sl] = root * gates[:, RG_BLOCK:2 * RG_BLOCK] * xc


def _scan_tile(a_scr, b_scr, h_scr, emit, reverse):
    steps = a_scr.shape[0] // B

    def body(k, h):
        t = steps - 1 - k if reverse else k
        r0 = pl.multiple_of(t * B, B)
        h = a_scr[pl.ds(r0, B), :] * h + b_scr[pl.ds(r0, B), :]
        emit(r0, h)
        return h

    h_scr[...] = lax.fori_loop(0, steps, body, h_scr[...], unroll=8)


def _rg_fwd_order(s):
    return jnp.where(s < NI_CTX, NI_LAT + s, s - NI_CTX)


def _rg_in_kernel(xp_ref, x_ref, xn_ref, mod_ref, g_ref, b_ref, cw_ref, cb_ref,
                  wai_ref, bai_ref, lam_ref, w_ref, xc_ref, gate_ref, hs_ref, a_scr, b_scr, h_scr):
    s = pl.program_id(0)
    t = _rg_fwd_order(s)
    xa = jnp.concatenate([xp_ref[...], x_ref[...], xn_ref[...]], axis=0)
    h = _norm_modulate(xa, g_ref[2:3], _mod(mod_ref, 3), _mod(mod_ref, 4)).astype(BF16)
    keep_lo = jnp.where(_seq_first(t), 0.0, 1.0).astype(F32)
    keep_hi = jnp.where(_seq_last(t), 0.0, 1.0).astype(F32)
    for hd in range(RG_HEADS):
        sl = slice(hd * RG_BLOCK, (hd + 1) * RG_BLOCK)
        rec = slice(D + hd * RG_BLOCK, D + (hd + 1) * RG_BLOCK)
        gate_ref[:, sl] = (_dot(h[B:TI + B], w_ref[:, sl]) + b_ref[:, sl]).astype(BF16)
        u = _dot(h, w_ref[:, rec]) + b_ref[:, rec]
        taps = (jnp.concatenate([u[0:B] * keep_lo, u[B:TI]], axis=0),
                u[B:TI + B],
                jnp.concatenate([u[2 * B:TI + B], u[TI + B:TI + 2 * B] * keep_hi], axis=0),
                jnp.concatenate([u[3 * B:TI + B], u[TI + B:TI + 3 * B] * keep_hi], axis=0))
        xc = cb_ref[:, sl]
        for k, tap in enumerate(taps):
            xc = xc + cw_ref[k:k + 1, sl] * tap
        xc_ref[:, sl] = xc
        _rg_coeffs(xc, hd, wai_ref, bai_ref, lam_ref, a_scr, b_scr)

    @pl.when(s == 0)
    def _():
        h_scr[...] = jnp.zeros((B, D), F32)

    def emit(r0, hv):
        hs_ref[pl.ds(r0, B), :] = hv

    _scan_tile(a_scr, b_scr, h_scr, emit, reverse=False)


def _rg_gate_specs():
    return [_resident((RG_HEADS, RG_BLOCK, 2 * RG_BLOCK)), _resident((1, 2 * D)), _resident((1, D))]


def _rg_in(x, mods, g, w_in, b_in, conv_w, conv_b, wai, bai, lam):
    order = _rg_fwd_order
    mod_spec = _mod_spec(1, NI_LAT, order)
    out_spec = pl.BlockSpec((TI, D), lambda s: (order(s), 0))
    return pl.pallas_call(
        _with_staged_weights(_rg_in_kernel, 12, [_first]),
        grid=(NI,),
        in_specs=_halo_specs(order, B, 2 * B) + [
            mod_spec, _resident((6, D)), _resident((1, 2 * D)),
            _resident((4, D)), _resident((1, D))] + _rg_gate_specs() + [_HBM],
        out_specs=[out_spec] * 3,
        out_shape=[jax.ShapeDtypeStruct((ROWS, D), F32), jax.ShapeDtypeStruct((ROWS, D), BF16),
                   jax.ShapeDtypeStruct((ROWS, D), F32)],
        scratch_shapes=[pltpu.VMEM((TI, D), F32), pltpu.VMEM((TI, D), F32),
                        pltpu.VMEM((B, D), F32), pltpu.VMEM((D, 2 * D), BF16)],
        compiler_params=_cparams(("arbitrary",)),
        name="rglru_in_fwd_scan",
    )(x, x, x, mods, g, b_in, conv_w, conv_b, wai, bai, lam, w_in)


def _rg_tile_coeffs(xc_ref, wai_ref, bai_ref, lam_ref, a_scr, b_scr):
    for hd in range(RG_HEADS):
        xc = xc_ref[:, hd * RG_BLOCK:(hd + 1) * RG_BLOCK]
        _rg_coeffs(xc, hd, wai_ref, bai_ref, lam_ref, a_scr, b_scr)


def _rg_ctx_bwd_kernel(xc_ref, wai_ref, bai_ref, lam_ref, h_ref, a_scr, b_scr, h_scr):
    _rg_tile_coeffs(xc_ref, wai_ref, bai_ref, lam_ref, a_scr, b_scr)

    @pl.when(pl.program_id(0) == 0)
    def _():
        h_scr[...] = jnp.zeros((B, D), F32)

    _scan_tile(a_scr, b_scr, h_scr, lambda r0, hv: None, reverse=True)
    h_ref[...] = h_scr[...]


def _rg_ctx_bwd(xc, wai, bai, lam):
    return pl.pallas_call(
        _rg_ctx_bwd_kernel,
        grid=(NT_CTX,),
        in_specs=[pl.BlockSpec((TM, D), lambda s: (NT - 1 - s, 0))] + _rg_gate_specs(),
        out_specs=pl.BlockSpec((B, D), lambda s: (0, 0)),
        out_shape=jax.ShapeDtypeStruct((B, D), F32),
        scratch_shapes=[pltpu.VMEM((TM, D), F32), pltpu.VMEM((TM, D), F32),
                        pltpu.VMEM((B, D), F32)],
        compiler_params=_cparams(("arbitrary",)),
        name="rglru_ctx_bwd_scan",
    )(xc, wai, bai, lam)


def _rg_out_ffn_kernel(xc_ref, gate_ref, hs_ref, x_ref, h0_ref, mod_ref, g_ref,
                       wai_ref, bai_ref, lam_ref, bo_ref, wo_ref, w1, w3, w2,
                       o_ref, a_scr, b_scr, hb_scr, h_scr, rows_scr):
    s = pl.program_id(0)

    def head_gates(hd):
        xc = xc_ref[:, hd * RG_BLOCK:(hd + 1) * RG_BLOCK]
        _rg_coeffs(xc, hd, wai_ref, bai_ref, lam_ref, a_scr, b_scr)

    @pl.when(s == 0)
    def _():
        h_scr[...] = h0_ref[...]
        for hd in range(RG_HEADS):
            head_gates(hd)

    @pl.when(s > 0)
    def _():
        y = (hs_ref[...] + hb_scr[...]) * _gelu_tanh(gate_ref[...].astype(F32))
        res = _mix_ffn_core(
            y, x_ref[...], mod_ref, g_ref, wo_ref, bo_ref, w1, w3, w2, chunks=F_QUARTERS,
            side_work=[functools.partial(head_gates, hd) for hd in range(RG_HEADS)])
        _put_rows(rows_scr, res)
        for b in range(B):
            o_ref[b] = _get_batch(rows_scr, b)

    @pl.when(s < NT_LAT)
    def _():
        def emit(r0, hv):
            hb_scr[pl.ds(r0, B), :] = hv

        _scan_tile(a_scr, b_scr, h_scr, emit, reverse=True)


def _rg_out_ffn(xc, gate, hs, x, h0, mods, g, wai, bai, lam, w_out, b_out, w1, w3, w2):
    scan_tile = lambda s: (jnp.maximum(NT_LAT - 1 - s, 0), 0)
    out_tile = lambda s: jnp.minimum(NT_LAT - s, NT_LAT - 1)
    prev = pl.BlockSpec((TM, D), lambda s: (out_tile(s), 0))
    return pl.pallas_call(
        _with_staged_weights(_rg_out_ffn_kernel, 15, [_first] + _ffn_views(1, 1)),
        grid=(NT_LAT + 1,),
        in_specs=[pl.BlockSpec((TM, D), scan_tile), prev, prev, prev, _resident((B, D)),
                  _mod_spec(1, NT_LAT + 1), _resident((6, D))]
        + _rg_gate_specs() + [_resident((1, D)), _HBM] + _WEIGHT_SPECS,
        out_specs=pl.BlockSpec((B, TS, D), lambda s: (0, out_tile(s), 0)),
        out_shape=jax.ShapeDtypeStruct((B, L, D), F32),
        scratch_shapes=[pltpu.VMEM((TM, D), F32), pltpu.VMEM((TM, D), F32),
                        pltpu.VMEM((TM, D), F32), pltpu.VMEM((B, D), F32), _ROWS_SCRATCH,
                        pltpu.VMEM((D, D), BF16)] + _WEIGHT_SCRATCH,
        compiler_params=_cparams(("arbitrary",)),
        name="rglru_out_ffn",
    )(xc, gate, hs, x, h0, mods, g, wai, bai, lam, b_out, w_out, w1, w3, w2)


def _grid_pos():
    rows = L // GRID_W
    quarter = D // 4
    omega = POS_BASE ** (-np.arange(quarter) / quarter)

    def emb(q):
        ang = q[:, None] * omega[None]
        return np.concatenate([np.sin(ang), np.cos(ang)], axis=-1)

    row_code = np.repeat(emb(np.arange(rows)), GRID_W, axis=0)
    col_code = np.tile(emb(np.arange(GRID_W)), (rows, 1))
    return jnp.asarray(np.concatenate([row_code, col_code], axis=-1), F32)


def kernel(x, c, ctx, c_ctx, ada_w, ada_b, norm_g, ffn_w1, ffn_w3, ffn_w2, hy_w_in, hy_b_in, hy_conv_w, hy_conv_b, hy_fw0, hy_fb0, hy_fw1, hy_fb1, hy_fw2, hy_fb2, hy_freq, hy_fwout, hy_filt_bias, hy_w_out, hy_b_out, rg_w_in, rg_b_in, rg_conv_w, rg_conv_b, rg_wa, rg_ba, rg_wi, rg_bi, rg_lam, rg_w_out, rg_b_out):
    mods = _mods(c, c_ctx, ada_w, ada_b).reshape(DEPTH, 2, B, N_MOD * D)
    w1, w3, w2 = ffn_w1, ffn_w3, ffn_w2

    g = norm_g[0]
    xs = _ffn_first(x, ctx, _grid_pos(), mods, g, w1, w3, w2)
    x0, p = _hy_in(xs, mods, g, hy_w_in, hy_b_in[0][None],
                   hy_conv_w[0], hy_conv_b[0][None])
    fparams = (hy_fw0[0], hy_fb0[0], hy_fw1[0], hy_fb1[0], hy_fw2[0], hy_fb2[0],
               hy_freq[0], hy_fwout[0])
    specs = []
    for n in (L, CTX):
        hf, hb = _filters(n, *fparams)
        fwd_table, inv_table = _radix4_tables(n // 4)
        specs.append(list(_spectrum(hf, hb, fwd_table)) + [fwd_table, inv_table])
    y_lo, y_hi = _long_conv(p, x0, hy_filt_bias[0][None], *specs)
    xs = _hy_out_ffn(y_lo, y_hi, xs, mods, g, hy_w_out, hy_b_out[0][None],
                     w1, w3, w2)

    g = norm_g[1]
    xs = _ffn_plain(xs, mods, g, w1, w3, w2, layer=1)
    wai = jnp.concatenate([rg_wa[0], rg_wi[0]], axis=-1).astype(BF16)
    per_head = lambda v: v.reshape(2, RG_HEADS, RG_BLOCK)
    bai = jnp.concatenate([per_head(rg_ba[0]), per_head(rg_bi[0])], axis=-1).reshape(2, 1, 2 * D)
    lam = rg_lam[0][:, None, :]
    xc, gate, hs = _rg_in(xs, mods, g, rg_w_in, rg_b_in[0][None],
                          rg_conv_w[0], rg_conv_b[0][None], wai[0], bai[0], lam[0])
    h_ctx = _rg_ctx_bwd(xc, wai[1], bai[1], lam[1])
    return _rg_out_ffn(xc, gate, hs, xs, h_ctx, mods, g, wai[1], bai[1], lam[1],
                       rg_w_out, rg_b_out[0][None], w1, w3, w2)
```
